```python
import jax, jax.numpy as jnp
from jax import lax
import numpy as np

D_MODEL = 2048
BATCH = 4
SEQ = 2048
DEPTH = 1
DEC_BATCH = 128
DEC_SEQ = 4
PAST_LEN = 16384
PAGE_SIZE = 128

MEM_LEN = 256
EPS = 1e-6
CONV_WIDTH = 3
D_CONV = D_MODEL // 2
D_GLA = D_MODEL - D_CONV
GLA_HEADS = 4
GLA_DV = D_GLA // GLA_HEADS
GLA_DK = GLA_DV // 2
GLA_RANK = 16
GLA_TAU = 16.0
GLA_CHUNK = 64
X_HEADS = 4
X_HD = D_MODEL // X_HEADS
D_FF = 5632
SIZES_IN = (D_CONV, D_CONV, D_CONV, GLA_HEADS * GLA_DK, GLA_HEADS * GLA_DK,
            GLA_HEADS * GLA_DV, GLA_HEADS * GLA_DV, GLA_RANK)
N_IN = sum(SIZES_IN)

kernel_name = "hymba_conv_gla_memxattn_convffn_step"


def rmsnorm(x, g):
    xf = x.astype(jnp.float32)
    y = xf * lax.rsqrt(jnp.mean(xf * xf, axis=-1, keepdims=True) + EPS)
    return (y * g.astype(jnp.float32)).astype(x.dtype)


def causal_conv(u, prev, w):
    t = u.shape[1]
    full = jnp.concatenate([prev.astype(u.dtype), u], axis=1)
    y = sum(w[i] * full[:, i:i + t] for i in range(CONV_WIDTH))
    return y, full[:, -(CONV_WIDTH - 1):]


def gla_chunked(q, k, v, logf, s0):
    b, t, h, _ = q.shape
    dv = v.shape[-1]
    c = min(GLA_CHUNK, t)
    pad = (-t) % c
    n = (t + pad) // c

    def blocks(a):
        a = jnp.pad(a.astype(jnp.float32), ((0, 0), (0, pad), (0, 0), (0, 0)))
        return a.reshape(b, n, c, h, a.shape[-1]).transpose(1, 0, 3, 2, 4)

    qb, kb, vb, gb = blocks(q), blocks(k), blocks(v), blocks(logf)
    mask = jnp.tril(jnp.ones((c, c), bool))

    def step(s, inp):
        qc, kc, vc, gc = inp
        cum = jnp.cumsum(gc, axis=2)
        diff = cum[:, :, :, None, :] - cum[:, :, None, :, :]
        decay = jnp.exp(jnp.where(mask[:, :, None], diff, -jnp.inf))
        a = jnp.einsum('bhid,bhjd,bhijd->bhij', qc, kc, decay)
        o = (jnp.einsum('bhij,bhjv->bhiv', a, vc)
             + jnp.einsum('bhid,bhdv->bhiv', qc * jnp.exp(cum), s))
        last = cum[:, :, -1:, :]
        s = (jnp.exp(last[:, :, 0, :])[..., None] * s
             + jnp.einsum('bhjd,bhjv->bhdv', kc * jnp.exp(last - cum), vc))
        return s, o

    s, o = lax.scan(step, s0.astype(jnp.float32), (qb, kb, vb, gb))
    o = o.transpose(1, 0, 3, 2, 4).reshape(b, n * c, h, dv)[:, :t]
    return o, s


def mixer(xn, conv_prev, s0, w_in, conv_w, w_gate2, b_gate, gla_norm, w_out):
    b, t, _ = xn.shape
    proj = xn @ w_in
    offs = [0]
    for sz in SIZES_IN:
        offs.append(offs[-1] + sz)
    bg, cg, vc, q, k, v, r, g1 = [proj[..., offs[i]:offs[i + 1]] for i in range(len(SIZES_IN))]
    yc, conv_new = causal_conv(cg * vc, conv_prev, conv_w)
    conv_out = bg * yc
    q = q.reshape(b, t, GLA_HEADS, GLA_DK) * (GLA_DK ** -0.5)
    k = k.reshape(b, t, GLA_HEADS, GLA_DK)
    v = v.reshape(b, t, GLA_HEADS, GLA_DV)
    z = (g1 @ w_gate2 + b_gate).astype(jnp.float32)
    logf = (jax.nn.log_sigmoid(z) / GLA_TAU).reshape(b, t, GLA_HEADS, GLA_DK)
    o, s_new = gla_chunked(q, k, v, logf, s0)
    o = rmsnorm(o.astype(xn.dtype), gla_norm.reshape(GLA_HEADS, GLA_DV)).reshape(b, t, D_GLA)
    o = o * jax.nn.silu(r)
    out = jnp.concatenate([conv_out, o], axis=-1) @ w_out
    return out, conv_new, s_new.astype(s0.dtype)


def cross_attn(hn, mk, mv, w_q, w_o):
    b, t, _ = hn.shape
    q = (hn @ w_q).reshape(b, t, X_HEADS, X_HD)
    s = jnp.einsum('bthd,bmhd->bhtm', q, mk).astype(jnp.float32) * (X_HD ** -0.5)
    p = jax.nn.softmax(s, axis=-1).astype(mv.dtype)
    o = jnp.einsum('bhtm,bmhd->bthd', p, mv).reshape(b, t, D_MODEL)
    return o @ w_o


def conv_ffn(hn, prev, w_g, w_u, c_w, c_b, w_d):
    gc, new_prev = causal_conv(hn @ w_g, prev, c_w)
    return (jax.nn.silu(gc + c_b) * (hn @ w_u)) @ w_d, new_prev


def layer(x, conv_prev, s0, ffn_prev, mk, mv, nm, w_in, conv_w, w_gate2, b_gate, gla_norm,
          w_out, nx, w_xq, w_xo, nf, w_fg, w_fu, fc_w, fc_b, w_fd):
    m, conv_new, s_new = mixer(rmsnorm(x, nm), conv_prev, s0, w_in, conv_w, w_gate2, b_gate,
                               gla_norm, w_out)
    h = x + m
    h = h + cross_attn(rmsnorm(h, nx), mk, mv, w_xq, w_xo)
    f, ffn_new = conv_ffn(rmsnorm(h, nf), ffn_prev, w_fg, w_fu, fc_w, fc_b, w_fd)
    return h + f, conv_new, s_new, ffn_new


def setup_inputs(seed: int = 0) -> dict:
    key = jax.random.key(seed)
    ks = iter(jax.random.split(key, 40))

    def nrm(shape, scale):
        return jax.random.normal(next(ks), shape, jnp.float32) * scale

    def gain(shape):
        return 1.0 + nrm(shape, 0.02)

    L = DEPTH
    return {
        "x_prompt": nrm((BATCH, SEQ, D_MODEL), 1.0),
        "x_sample": nrm((DEC_BATCH, DEC_SEQ, D_MODEL), 1.0),
        "mem_prompt": nrm((BATCH, MEM_LEN, D_MODEL), 1.0),
        "cache_conv": nrm((L, DEC_BATCH, CONV_WIDTH - 1, D_CONV), 0.5),
        "state_gla": nrm((L, DEC_BATCH, GLA_HEADS, GLA_DK, GLA_DV), 0.3),
        "cache_ffn": nrm((L, DEC_BATCH, CONV_WIDTH - 1, D_FF), 1.0),
        "cache_mem_k": nrm((L, DEC_BATCH, MEM_LEN, X_HEADS, X_HD), 1.0),
        "cache_mem_v": nrm((L, DEC_BATCH, MEM_LEN, X_HEADS, X_HD), 1.0),
        "norm_mix": gain((L, D_MODEL)),
        "w_in": nrm((L, D_MODEL, N_IN), D_MODEL ** -0.5),
        "conv_w": nrm((L, CONV_WIDTH, D_CONV), CONV_WIDTH ** -0.5),
        "w_gate2": nrm((L, GLA_RANK, GLA_HEADS * GLA_DK), GLA_RANK ** -0.5),
        "b_gate": nrm((L, GLA_HEADS * GLA_DK), 0.1),
        "gla_norm": gain((L, D_GLA)),
        "w_out": nrm((L, D_MODEL, D_MODEL), D_MODEL ** -0.5),
        "norm_x": gain((L, D_MODEL)),
        "norm_mem": gain((L, D_MODEL)),
        "w_xq": nrm((L, D_MODEL, D_MODEL), D_MODEL ** -0.5),
        "w_xk": nrm((L, D_MODEL, D_MODEL), D_MODEL ** -0.5),
        "w_xv": nrm((L, D_MODEL, D_MODEL), D_MODEL ** -0.5),
        "w_xo": nrm((L, D_MODEL, D_MODEL), D_MODEL ** -0.5),
        "norm_ffn": gain((L, D_MODEL)),
        "w_ffn_gate": nrm((L, D_MODEL, D_FF), D_MODEL ** -0.5),
        "w_ffn_up": nrm((L, D_MODEL, D_FF), D_MODEL ** -0.5),
        "ffn_conv_w": nrm((L, CONV_WIDTH, D_FF), CONV_WIDTH ** -0.5),
        "ffn_conv_b": nrm((L, D_FF), 0.02),
        "w_ffn_down": nrm((L, D_FF, D_MODEL), D_FF ** -0.5),
        "norm_final": gain((D_MODEL,)),
    }


def reference(x_prompt, x_sample, mem_prompt, cache_conv, state_gla, cache_ffn, cache_mem_k,
              cache_mem_v, norm_mix, w_in, conv_w, w_gate2, b_gate, gla_norm, w_out, norm_x,
              norm_mem, w_xq, w_xk, w_xv, w_xo, norm_ffn, w_ffn_gate, w_ffn_up, ffn_conv_w,
              ffn_conv_b, w_ffn_down, norm_final):
    hp, hs = x_prompt, x_sample
    conv_p, gla_p, ffn_p, mk_p, mv_p = [], [], [], [], []
    conv_s, gla_s, ffn_s = [], [], []
    for l in range(DEPTH):
        shared = (norm_mix[l], w_in[l], conv_w[l], w_gate2[l], b_gate[l], gla_norm[l], w_out[l],
                  norm_x[l], w_xq[l], w_xo[l], norm_ffn[l], w_ffn_gate[l], w_ffn_up[l],
                  ffn_conv_w[l], ffn_conv_b[l], w_ffn_down[l])
        memn = rmsnorm(mem_prompt, norm_mem[l])
        mk = (memn @ w_xk[l]).reshape(BATCH, MEM_LEN, X_HEADS, X_HD)
        mv = (memn @ w_xv[l]).reshape(BATCH, MEM_LEN, X_HEADS, X_HD)
        z_conv = jnp.zeros((BATCH, CONV_WIDTH - 1, D_CONV), hp.dtype)
        z_gla = jnp.zeros((BATCH, GLA_HEADS, GLA_DK, GLA_DV), hp.dtype)
        z_ffn = jnp.zeros((BATCH, CONV_WIDTH - 1, D_FF), hp.dtype)
        hp, c1, s1, f1 = layer(hp, z_conv, z_gla, z_ffn, mk, mv, *shared)
        conv_p.append(c1); gla_p.append(s1); ffn_p.append(f1); mk_p.append(mk); mv_p.append(mv)
        hs, c2, s2, f2 = layer(hs, cache_conv[l], state_gla[l], cache_ffn[l], cache_mem_k[l],
                               cache_mem_v[l], *shared)
        conv_s.append(c2); gla_s.append(s2); ffn_s.append(f2)
    y_prompt = rmsnorm(hp, norm_final)
    y_sample = rmsnorm(hs, norm_final)
    return (y_prompt, y_sample, jnp.stack(conv_p), jnp.stack(gla_p), jnp.stack(ffn_p),
            jnp.stack(mk_p), jnp.stack(mv_p), jnp.stack(conv_s), jnp.stack(gla_s),
            jnp.stack(ffn_s))
```

```python
import functools

import jax
import jax.numpy as jnp
from jax import lax
from jax.experimental import pallas as pl
from jax.experimental.pallas import tpu as pltpu

F32 = jnp.float32
BF16 = jnp.bfloat16

D_MODEL = 2048
EPS = 1e-6
CONV_WIDTH = 3
D_CONV = 1024
D_GLA = 1024
GLA_HEADS = 4
GLA_DV = 256
GLA_DK = 128
GLA_RANK = 16
GLA_TAU = 16.0
GLA_CHUNK = 64
X_HEADS = 4
X_HD = 512
MEM_LEN = 256
D_FF = 5632
N_MAIN = 3 * D_CONV + 2 * GLA_HEADS * GLA_DK + 2 * GLA_HEADS * GLA_DV

LANES = 128
SUBLANES = 8
VMEM_LIMIT_BYTES = 56 * 1024 * 1024


def _params(*sem):
    return pltpu.CompilerParams(dimension_semantics=sem, vmem_limit_bytes=VMEM_LIMIT_BYTES)


def _dot(a, b):
    return jnp.dot(a, b, preferred_element_type=F32)


def _dot_nt(a, b):
    return lax.dot_general(a, b, (((1,), (1,)), ((), ())), preferred_element_type=F32)


def _dot_tn(a, b):
    return lax.dot_general(a, b, (((0,), (0,)), ((), ())), preferred_element_type=F32)


def _rms_rows(x, g):
    ms = jnp.mean(x * x, axis=-1, keepdims=True)
    return (x * lax.rsqrt(ms + EPS)) * g


def _row_chunk(rows):
    for c in (256, 128, 64, 32, 16, 8):
        if rows % c == 0:
            return c
    return rows


def _norm_into(x_ref, g_ref, xn_ref):
    rows = x_ref.shape[0]
    ch = _row_chunk(rows)
    g = g_ref[...]

    def body(c, carry):
        r = pl.ds(pl.multiple_of(c * ch, ch), ch)
        xn_ref[r, :] = _rms_rows(x_ref[r, :], g).astype(xn_ref.dtype)
        return carry

    lax.fori_loop(0, rows // ch, body, 0)


def _norm_matmul_kernel(x_ref, g_ref, w_ref, o_ref, xn_ref):
    @pl.when(pl.program_id(1) == 0)
    def _():
        _norm_into(x_ref, g_ref, xn_ref)

    o_ref[...] = _dot(xn_ref[...], w_ref[...]).astype(o_ref.dtype)


def _in_proj_kernel(x_ref, g_ref, w_ref, wg1_ref, wg2_ref, bg_ref, o_ref, lf_ref, xn_ref):
    @pl.when(pl.program_id(1) == 0)
    def _():
        _norm_into(x_ref, g_ref, xn_ref)
        g1 = _dot(xn_ref[...], wg1_ref[...])
        z = _dot(g1.astype(BF16), wg2_ref[...]) + bg_ref[...]
        lf_ref[...] = (jnp.minimum(z, 0.0) - jnp.log1p(jnp.exp(-jnp.abs(z)))) * (1.0 / GLA_TAU)

    o_ref[...] = _dot(xn_ref[...], w_ref[...]).astype(o_ref.dtype)


def _norm_matmul(x, gain, w, out_dtype, tm, tn):
    m, d = x.shape
    n = w.shape[1]
    return pl.pallas_call(
        _norm_matmul_kernel,
        grid=(m // tm, n // tn),
        in_specs=[pl.BlockSpec((tm, d), lambda i, j: (i, 0)),
                  pl.BlockSpec((1, d), lambda i, j: (0, 0)),
                  pl.BlockSpec((d, tn), lambda i, j: (0, j))],
        out_specs=pl.BlockSpec((tm, tn), lambda i, j: (i, j)),
        out_shape=jax.ShapeDtypeStruct((m, n), out_dtype),
        scratch_shapes=[pltpu.VMEM((tm, d), BF16)],
        compiler_params=_params("arbitrary", "arbitrary"),
        name="norm_matmul",
    )(x, gain, w)


def _in_proj(x, gain, w_main, wg1, wg2, b_gate, tm, tn):
    m, d = x.shape
    n = w_main.shape[1]
    ng = wg2.shape[1]
    return pl.pallas_call(
        _in_proj_kernel,
        grid=(m // tm, n // tn),
        in_specs=[pl.BlockSpec((tm, d), lambda i, j: (i, 0)),
                  pl.BlockSpec((1, d), lambda i, j: (0, 0)),
                  pl.BlockSpec((d, tn), lambda i, j: (0, j)),
                  pl.BlockSpec((d, LANES), lambda i, j: (0, 0)),
                  pl.BlockSpec((LANES, ng), lambda i, j: (0, 0)),
                  pl.BlockSpec((1, ng), lambda i, j: (0, 0))],
        out_specs=[pl.BlockSpec((tm, tn), lambda i, j: (i, j)),
                   pl.BlockSpec((tm, ng), lambda i, j: (i, 0))],
        out_shape=[jax.ShapeDtypeStruct((m, n), F32),
                   jax.ShapeDtypeStruct((m, ng), F32)],
        scratch_shapes=[pltpu.VMEM((tm, d), BF16)],
        compiler_params=_params("arbitrary", "arbitrary"),
        name="in_proj",
    )(x, gain, w_main, wg1, wg2, b_gate)


def _matmul_kernel(a_ref, w_ref, o_ref):
    o_ref[...] = _dot(a_ref[...], w_ref[...]).astype(o_ref.dtype)


def _matmul(a, w, out_dtype, tm, tn):
    m, k = a.shape
    n = w.shape[1]
    return pl.pallas_call(
        _matmul_kernel,
        grid=(m // tm, n // tn),
        in_specs=[pl.BlockSpec((tm, k), lambda i, j: (i, 0)),
                  pl.BlockSpec((k, tn), lambda i, j: (0, j))],
        out_specs=pl.BlockSpec((tm, tn), lambda i, j: (i, j)),
        out_shape=jax.ShapeDtypeStruct((m, n), out_dtype),
        compiler_params=_params("arbitrary", "arbitrary"),
        name="matmul",
    )(a, w)


def _halo_rows(shift):
    return max(2 * shift, SUBLANES)


def _conv_stage(buf_ref, prev, shift):
    halo = _halo_rows(shift)
    buf_ref[halo - 2 * shift:halo, :] = prev


def _conv_taps(buf_ref, r0, rows, shift):
    halo = _halo_rows(shift)

    def aligned(start):
        return start if isinstance(start, int) else pl.multiple_of(start, SUBLANES)

    if shift % SUBLANES == 0:
        s1 = buf_ref[pl.ds(aligned(r0 + (halo - shift)), rows), :]
        s2 = buf_ref[pl.ds(aligned(r0 + (halo - 2 * shift)), rows), :]
        return s2, s1
    win = buf_ref[pl.ds(aligned(r0 + (halo - SUBLANES)), rows + SUBLANES), :]
    s1 = win[SUBLANES - shift:SUBLANES - shift + rows, :]
    s2 = win[SUBLANES - 2 * shift:SUBLANES - 2 * shift + rows, :]
    return s2, s1


def _conv_gate_kernel(bg_ref, cg_ref, vc_ref, prev_ref, w_ref, o_ref, new_ref, buf_ref, *, shift):
    rows = cg_ref.shape[0]
    halo = _halo_rows(shift)
    u = cg_ref[...] * vc_ref[...]
    _conv_stage(buf_ref, prev_ref[0], shift)
    buf_ref[halo:halo + rows, :] = u
    s2, s1 = _conv_taps(buf_ref, 0, rows, shift)
    y = w_ref[0:1, :] * s2 + w_ref[1:2, :] * s1 + w_ref[2:3, :] * u
    o_ref[...] = (bg_ref[...] * y).astype(o_ref.dtype)
    new_ref[0] = buf_ref[halo + rows - 2 * shift:halo + rows, :]


def _conv_gate(p, prev, conv_w, nseq, rows, shift, tn):
    nj = D_CONV // tn
    kern = functools.partial(_conv_gate_kernel, shift=shift)
    return pl.pallas_call(
        kern,
        grid=(nseq, nj),
        in_specs=[pl.BlockSpec((rows, tn), lambda b, j: (b, j)),
                  pl.BlockSpec((rows, tn), lambda b, j: (b, j + nj)),
                  pl.BlockSpec((rows, tn), lambda b, j: (b, j + 2 * nj)),
                  pl.BlockSpec((1, 2 * shift, tn), lambda b, j: (b, 0, j)),
                  pl.BlockSpec((CONV_WIDTH, tn), lambda b, j: (0, j))],
        out_specs=[pl.BlockSpec((rows, tn), lambda b, j: (b, j)),
                   pl.BlockSpec((1, 2 * shift, tn), lambda b, j: (b, 0, j))],
        out_shape=[jax.ShapeDtypeStruct((nseq * rows, D_CONV), BF16),
                   jax.ShapeDtypeStruct((nseq, 2 * shift, D_CONV), F32)],
        scratch_shapes=[pltpu.VMEM((_halo_rows(shift) + rows, tn), F32)],
        compiler_params=_params("arbitrary", "arbitrary"),
        name="conv_gate",
    )(p, p, p, prev, conv_w)


def _cumsum_rows(g):
    c = g.shape[0]
    row = lax.broadcasted_iota(jnp.int32, g.shape, 0)
    x = g
    s = 1
    while s < c:
        x = x + jnp.where(row >= s, pltpu.roll(x, s, 0), 0.0)
        s *= 2
    return x


def _bcast_block_row(x, s, k):
    c, lanes = x.shape
    if s == c:
        return jnp.broadcast_to(x[k:k + 1, :], x.shape)
    if s >= SUBLANES:
        y = x.reshape(c // s, s, lanes)
        return jnp.broadcast_to(y[:, k:k + 1, :], y.shape).reshape(c, lanes)
    y = x.reshape(c // SUBLANES, SUBLANES, lanes)
    sub = lax.broadcasted_iota(jnp.int32, y.shape, 1)
    out = None
    for blk in range(SUBLANES // s):
        src = jnp.broadcast_to(y[:, blk * s + k:blk * s + k + 1, :], y.shape)
        out = src if out is None else jnp.where(sub >= blk * s, src, out)
    return out.reshape(c, lanes)


def _gla_chunk(q, k, v, g, s_prev):
    c = q.shape[0]
    cum = _cumsum_rows(g)
    ri = lax.broadcasted_iota(jnp.int32, (c, c), 0)
    ci = lax.broadcasted_iota(jnp.int32, (c, c), 1)
    diff_bits = ri ^ ci
    a = jnp.where(diff_bits == 0, _dot_nt(q.astype(BF16), k.astype(BF16)), 0.0)
    level = 0
    while (1 << level) < c:
        half = 1 << level
        ref = _bcast_block_row(cum, 2 * half, half - 1)
        qe = q * jnp.exp(jnp.minimum(cum - ref, 0.0))
        ke = k * jnp.exp(jnp.minimum(ref - cum, 0.0))
        mask = ((diff_bits >> level) == 1) & (((ri >> level) & 1) == 1)
        a = a + jnp.where(mask, _dot_nt(qe.astype(BF16), ke.astype(BF16)), 0.0)
        level += 1
    o = _dot(a.astype(BF16), v.astype(BF16)) + _dot((q * jnp.exp(cum)).astype(BF16), s_prev.astype(BF16))
    last = cum[c - 1:c, :]
    kd = k * jnp.exp(last - cum)
    dk = last.shape[1]
    decay_t = jnp.transpose(jnp.broadcast_to(jnp.exp(last), (dk, dk)))
    decayed = jnp.concatenate([decay_t * s_prev[:, i:i + dk] for i in range(0, s_prev.shape[1], dk)], axis=1)
    s_new = decayed + _dot_tn(kd.astype(BF16), v.astype(BF16))
    return o, s_new


def _gla_kernel(q_ref, k_ref, v_ref, r_ref, g_ref, s0_ref, gn_ref, o_ref, sn_ref, s_ref, *, chunk):
    bb, rows = q_ref.shape[0], q_ref.shape[1]
    nchunk = rows // chunk

    @pl.when(pl.program_id(2) == 0)
    def _():
        s_ref[...] = s0_ref[:, 0]

    gn = gn_ref[...]

    def body(n, carry):
        b = n // nchunk
        r = pl.ds(pl.multiple_of((n % nchunk) * chunk, chunk), chunk)
        q = q_ref[b, r, :] * (GLA_DK ** -0.5)
        o, s_new = _gla_chunk(q, k_ref[b, r, :], v_ref[b, r, :], g_ref[b, r, :], s_ref[b])
        s_ref[b] = s_new
        rr = r_ref[b, r, :]
        o_ref[b, r, :] = (_rms_rows(o, gn) * (rr * jax.nn.sigmoid(rr))).astype(o_ref.dtype)
        return carry

    lax.fori_loop(0, bb * nchunk, body, 0)
    sn_ref[:, 0] = s_ref[...]


def _gla(p, logf, s0, gla_norm, *, q_blk, k_blk, v_blk, r_blk, bb, rows, chunk):
    nb, t, _ = p.shape
    kern = functools.partial(_gla_kernel, chunk=chunk)
    return pl.pallas_call(
        kern,
        grid=(nb // bb, GLA_HEADS, t // rows),
        in_specs=[pl.BlockSpec((bb, rows, GLA_DK), lambda b, h, c: (b, c, q_blk + h)),
                  pl.BlockSpec((bb, rows, GLA_DK), lambda b, h, c: (b, c, k_blk + h)),
                  pl.BlockSpec((bb, rows, GLA_DV), lambda b, h, c: (b, c, v_blk + h)),
                  pl.BlockSpec((bb, rows, GLA_DV), lambda b, h, c: (b, c, r_blk + h)),
                  pl.BlockSpec((bb, rows, GLA_DK), lambda b, h, c: (b, c, h)),
                  pl.BlockSpec((bb, 1, GLA_DK, GLA_DV), lambda b, h, c: (b, h, 0, 0)),
                  pl.BlockSpec((1, GLA_DV), lambda b, h, c: (0, h))],
        out_specs=[pl.BlockSpec((bb, rows, GLA_DV), lambda b, h, c: (b, c, h)),
                   pl.BlockSpec((bb, 1, GLA_DK, GLA_DV), lambda b, h, c: (b, h, 0, 0))],
        out_shape=[jax.ShapeDtypeStruct((nb, t, D_GLA), BF16),
                   jax.ShapeDtypeStruct((nb, GLA_HEADS, GLA_DK, GLA_DV), F32)],
        scratch_shapes=[pltpu.VMEM((bb, GLA_DK, GLA_DV), F32)],
        compiler_params=_params("arbitrary", "arbitrary", "arbitrary"),
        name="gla",
    )(p, p, p, p, logf, s0, gla_norm)


def _proj_res_norm_kernel(a_ref, w_ref, res_ref, g_ref, *refs, nk, final):
    if final:
        y_ref, acc_ref = refs
    else:
        h_ref, hn_ref, acc_ref = refs
    k = pl.program_id(1)
    part = _dot(a_ref[...], w_ref[...])

    @pl.when(k == 0)
    def _():
        acc_ref[...] = part

    @pl.when(k > 0)
    def _():
        acc_ref[...] += part

    @pl.when(k == nk - 1)
    def _():
        rows = acc_ref.shape[0]
        ch = _row_chunk(rows)
        g = g_ref[...]

        def body(c, carry):
            r = pl.ds(pl.multiple_of(c * ch, ch), ch)
            h = res_ref[r, :] + acc_ref[r, :]
            hn = _rms_rows(h, g)
            if final:
                y_ref[r, :] = hn
            else:
                h_ref[r, :] = h
                hn_ref[r, :] = hn.astype(hn_ref.dtype)
            return carry

        lax.fori_loop(0, rows // ch, body, 0)


def _proj_res_norm(a, w, res, gain, *, tm, tk, final):
    m, kdim = a.shape
    d = w.shape[1]
    nk = kdim // tk
    kern = functools.partial(_proj_res_norm_kernel, nk=nk, final=final)
    row_spec = pl.BlockSpec((tm, d), lambda i, k: (i, 0))
    if final:
        out_specs = row_spec
        out_shape = jax.ShapeDtypeStruct((m, d), F32)
    else:
        out_specs = [row_spec, row_spec]
        out_shape = [jax.ShapeDtypeStruct((m, d), F32), jax.ShapeDtypeStruct((m, d), BF16)]
    return pl.pallas_call(
        kern,
        grid=(m // tm, nk),
        in_specs=[pl.BlockSpec((tm, tk), lambda i, k: (i, k)),
                  pl.BlockSpec((tk, d), lambda i, k: (k, 0)),
                  row_spec,
                  pl.BlockSpec((1, d), lambda i, k: (0, 0))],
        out_specs=out_specs,
        out_shape=out_shape,
        scratch_shapes=[pltpu.VMEM((tm, d), F32)],
        compiler_params=_params("arbitrary", "arbitrary"),
        name="proj_res_norm",
    )(a, w, res, gain)


def _xattn_kernel(q_ref, k_ref, v_ref, o_ref):
    bb = q_ref.shape[0]

    def body(b, carry):
        q = q_ref[b].astype(BF16)
        s = _dot_nt(q, k_ref[b].astype(BF16)) * (X_HD ** -0.5)
        s = s - jnp.max(s, axis=-1, keepdims=True)
        e = jnp.exp(s)
        p = e / jnp.sum(e, axis=-1, keepdims=True)
        o_ref[b] = _dot(p.astype(BF16), v_ref[b].astype(BF16)).astype(o_ref.dtype)
        return carry

    lax.fori_loop(0, bb, body, 0)


def _xattn(q, mk, mv, *, bb, tq):
    nb, t, d = q.shape
    return pl.pallas_call(
        _xattn_kernel,
        grid=(nb // bb, X_HEADS, t // tq),
        in_specs=[pl.BlockSpec((bb, tq, X_HD), lambda b, h, i: (b, i, h)),
                  pl.BlockSpec((bb, MEM_LEN, X_HD), lambda b, h, i: (b, 0, h)),
                  pl.BlockSpec((bb, MEM_LEN, X_HD), lambda b, h, i: (b, 0, h))],
        out_specs=pl.BlockSpec((bb, tq, X_HD), lambda b, h, i: (b, i, h)),
        out_shape=jax.ShapeDtypeStruct((nb, t, d), BF16),
        compiler_params=_params("arbitrary", "arbitrary", "arbitrary"),
        name="xattn",
    )(q, mk, mv)


def _ffn_up_kernel(hn_ref, wg_ref, wu_ref, cw_ref, cb_ref, prev_ref, o_ref, new_ref, buf_ref, *, shift):
    rows = hn_ref.shape[0]
    halo = _halo_rows(shift)
    ch = _row_chunk(rows)
    _conv_stage(buf_ref, prev_ref[0], shift)

    def gate_body(c, carry):
        r0 = pl.multiple_of(c * ch, ch)
        buf_ref[pl.ds(pl.multiple_of(halo + r0, SUBLANES), ch), :] = _dot(hn_ref[pl.ds(r0, ch), :], wg_ref[...])
        return carry

    lax.fori_loop(0, rows // ch, gate_body, 0)
    w0, w1, w2, cb = cw_ref[0:1, :], cw_ref[1:2, :], cw_ref[2:3, :], cb_ref[...]

    def act_body(c, carry):
        r0 = pl.multiple_of(c * ch, ch)
        up = _dot(hn_ref[pl.ds(r0, ch), :], wu_ref[...])
        s2, s1 = _conv_taps(buf_ref, r0, ch, shift)
        gc = w0 * s2 + w1 * s1 + w2 * buf_ref[pl.ds(pl.multiple_of(halo + r0, SUBLANES), ch), :] + cb
        o_ref[pl.ds(r0, ch), :] = ((gc * jax.nn.sigmoid(gc)) * up).astype(o_ref.dtype)
        return carry

    lax.fori_loop(0, rows // ch, act_body, 0)
    new_ref[0] = buf_ref[halo + rows - 2 * shift:halo + rows, :]


def _ffn_up(hn, wg, wu, cw, cb, prev, nseq, rows, shift, tn):
    d = hn.shape[1]
    kern = functools.partial(_ffn_up_kernel, shift=shift)
    return pl.pallas_call(
        kern,
        grid=(nseq, D_FF // tn),
        in_specs=[pl.BlockSpec((rows, d), lambda b, j: (b, 0)),
                  pl.BlockSpec((d, tn), lambda b, j: (0, j)),
                  pl.BlockSpec((d, tn), lambda b, j: (0, j)),
                  pl.BlockSpec((CONV_WIDTH, tn), lambda b, j: (0, j)),
                  pl.BlockSpec((1, tn), lambda b, j: (0, j)),
                  pl.BlockSpec((1, 2 * shift, tn), lambda b, j: (b, 0, j))],
        out_specs=[pl.BlockSpec((rows, tn), lambda b, j: (b, j)),
                   pl.BlockSpec((1, 2 * shift, tn), lambda b, j: (b, 0, j))],
        out_shape=[jax.ShapeDtypeStruct((nseq * rows, D_FF), BF16),
                   jax.ShapeDtypeStruct((nseq, 2 * shift, D_FF), F32)],
        scratch_shapes=[pltpu.VMEM((_halo_rows(shift) + rows, tn), F32)],
        compiler_params=_params("arbitrary", "arbitrary"),
        name="ffn_up",
    )(hn, wg, wu, cw, cb, prev)


def _layer(x, w, *, nseq, rows, shift, prev_conv, s0, prev_ffn, mk, mv, time_major):
    m = x.shape[0]
    tm = min(m, 1024)
    p, logf = _in_proj(x, w["norm_mix"], w["w_in_main"], w["w_g1"], w["w_g2"], w["b_gate"], tm, 1024)
    conv_out, conv_new = _conv_gate(p, prev_conv, w["conv_w"], nseq, rows, shift, 256)

    if time_major:
        nt = m // shift
        pad = ((0, 0), (0, SUBLANES - nt), (0, 0))
        pg = jnp.pad(p.reshape(nt, shift, N_MAIN)[:, :, 3 * D_CONV:].transpose(1, 0, 2), pad)
        lg = jnp.pad(logf.reshape(nt, shift, -1).transpose(1, 0, 2), pad)
        o, s_new = _gla(pg, lg, s0, w["gla_norm"], q_blk=0, k_blk=GLA_HEADS, v_blk=GLA_HEADS,
                        r_blk=2 * GLA_HEADS, bb=16, rows=SUBLANES, chunk=SUBLANES)
        gla_out = o[:, :nt].transpose(1, 0, 2).reshape(m, D_GLA)
    else:
        q0 = 3 * D_CONV // GLA_DK
        v0 = (3 * D_CONV + 2 * GLA_HEADS * GLA_DK) // GLA_DV
        o, s_new = _gla(p.reshape(nseq, rows, N_MAIN), logf.reshape(nseq, rows, -1), s0, w["gla_norm"],
                        q_blk=q0, k_blk=q0 + GLA_HEADS, v_blk=v0, r_blk=v0 + GLA_HEADS,
                        bb=1, rows=512, chunk=GLA_CHUNK)
        gla_out = o.reshape(m, D_GLA)

    mix = jnp.concatenate([conv_out, gla_out], axis=-1)
    tm2 = min(m, 512)
    h, hn = _proj_res_norm(mix, w["w_out"], x, w["norm_x"], tm=tm2, tk=1024, final=False)
    qx = _matmul(hn, w["w_xq"], BF16, tm, 1024)

    if time_major:
        nt = m // shift
        qb = jnp.pad(qx.astype(F32).reshape(nt, shift, D_MODEL).transpose(1, 0, 2),
                     ((0, 0), (0, SUBLANES - nt), (0, 0)))
        ob = _xattn(qb, mk, mv, bb=8, tq=SUBLANES)
        attn = ob[:, :nt].transpose(1, 0, 2).reshape(m, D_MODEL)
    else:
        attn = _xattn(qx.reshape(nseq, rows, D_MODEL), mk, mv, bb=1, tq=rows).reshape(m, D_MODEL)

    h2, hn2 = _proj_res_norm(attn, w["w_xo"], h, w["norm_ffn"], tm=tm2, tk=1024, final=False)
    act, ffn_new = _ffn_up(hn2, w["w_fg"], w["w_fu"], w["ffn_conv_w"], w["ffn_conv_b"], prev_ffn,
                           nseq, rows, shift, 512 if time_major else 256)
    return act, h2, conv_new, s_new, ffn_new


def kernel(x_prompt, x_sample, mem_prompt, cache_conv, state_gla, cache_ffn, cache_mem_k, cache_mem_v,
           norm_mix, w_in, conv_w, w_gate2, b_gate, gla_norm, w_out, norm_x, norm_mem, w_xq, w_xk, w_xv,
           w_xo, norm_ffn, w_ffn_gate, w_ffn_up, ffn_conv_w, ffn_conv_b, w_ffn_down, norm_final):
    depth = w_in.shape[0]
    nb, seq, d = x_prompt.shape
    db, dseq, _ = x_sample.shape
    hp = x_prompt.reshape(nb * seq, d)
    hs = x_sample.transpose(1, 0, 2).reshape(dseq * db, d)
    outs = {k: [] for k in ("conv_p", "gla_p", "ffn_p", "mk", "mv", "conv_s", "gla_s", "ffn_s")}
    nfinal = norm_final.reshape(1, d)
    yp = ys = None
    for l in range(depth):
        w = {
            "norm_mix": norm_mix[l].reshape(1, d),
            "w_in_main": w_in[l][:, :N_MAIN].astype(BF16),
            "w_g1": jnp.pad(w_in[l][:, N_MAIN:], ((0, 0), (0, LANES - GLA_RANK))).astype(BF16),
            "w_g2": jnp.pad(w_gate2[l], ((0, LANES - GLA_RANK), (0, 0))).astype(BF16),
            "b_gate": b_gate[l].reshape(1, -1),
            "conv_w": conv_w[l],
            "gla_norm": gla_norm[l].reshape(1, -1),
            "w_out": w_out[l].astype(BF16),
            "norm_x": norm_x[l].reshape(1, d),
            "w_xq": w_xq[l].astype(BF16),
            "w_xo": w_xo[l].astype(BF16),
            "norm_ffn": norm_ffn[l].reshape(1, d),
            "w_fg": w_ffn_gate[l].astype(BF16),
            "w_fu": w_ffn_up[l].astype(BF16),
            "ffn_conv_w": ffn_conv_w[l],
            "ffn_conv_b": ffn_conv_b[l].reshape(1, -1),
        }
        w_fd = w_ffn_down[l].astype(BF16)
        last = l == depth - 1
        gain_next = nfinal if last else None

        mem = mem_prompt.reshape(nb * MEM_LEN, d)
        nmem = norm_mem[l].reshape(1, d)
        mk = _norm_matmul(mem, nmem, w_xk[l].astype(BF16), F32, nb * MEM_LEN, 1024)
        mv = _norm_matmul(mem, nmem, w_xv[l].astype(BF16), F32, nb * MEM_LEN, 1024)
        act, h2, c1, s1, f1 = _layer(
            hp, w, nseq=nb, rows=seq, shift=1,
            prev_conv=jnp.zeros((nb, CONV_WIDTH - 1, D_CONV), F32),
            s0=jnp.zeros((nb, GLA_HEADS, GLA_DK, GLA_DV), F32),
            prev_ffn=jnp.zeros((nb, CONV_WIDTH - 1, D_FF), F32),
            mk=mk.reshape(nb, MEM_LEN, d), mv=mv.reshape(nb, MEM_LEN, d), time_major=False)
        assert last, "only the final layer's epilogue (final rmsnorm) is implemented"
        yp = _proj_res_norm(act, w_fd, h2, gain_next, tm=512, tk=1408, final=True)
        outs["conv_p"].append(c1)
        outs["gla_p"].append(s1)
        outs["ffn_p"].append(f1)
        outs["mk"].append(mk.reshape(nb, MEM_LEN, X_HEADS, X_HD))
        outs["mv"].append(mv.reshape(nb, MEM_LEN, X_HEADS, X_HD))

        def tmajor(c):
            return c.transpose(1, 0, 2).reshape(1, (CONV_WIDTH - 1) * db, c.shape[-1])

        act, h2, c2, s2, f2 = _layer(
            hs, w, nseq=1, rows=dseq * db, shift=db,
            prev_conv=tmajor(cache_conv[l]), s0=state_gla[l], prev_ffn=tmajor(cache_ffn[l]),
            mk=cache_mem_k[l].reshape(db, MEM_LEN, d), mv=cache_mem_v[l].reshape(db, MEM_LEN, d),
            time_major=True)
        ys = _proj_res_norm(act, w_fd, h2, gain_next, tm=512, tk=1408, final=True)
        outs["conv_s"].append(c2.reshape(CONV_WIDTH - 1, db, D_CONV).transpose(1, 0, 2))
        outs["gla_s"].append(s2)
        outs["ffn_s"].append(f2.reshape(CONV_WIDTH - 1, db, D_FF).transpose(1, 0, 2))

    y_prompt = yp.reshape(nb, seq, d)
    y_sample = ys.reshape(dseq, db, d).transpose(1, 0, 2)
    st = lambda k: jnp.stack(outs[k])
    return (y_prompt, y_sample, st("conv_p"), st("gla_p"), st("ffn_p"), st("mk"), st("mv"),
            st("conv_s"), st("gla_s"), st("ffn_s"))
```

```python
import functools

import jax
import jax.numpy as jnp
from jax import lax
from jax.experimental import pallas as pl
from jax.experimental.pallas import tpu as pltpu

F32 = jnp.float32
BF16 = jnp.bfloat16

D_MODEL = 2048
EPS = 1e-6
CONV_WIDTH = 3
D_CONV = 1024
D_GLA = 1024
GLA_HEADS = 4
GLA_DV = 256
GLA_DK = 128
GLA_RANK = 16
GLA_TAU = 16.0
GLA_CHUNK = 64
X_HEADS = 4
X_HD = 512
MEM_LEN = 256
D_FF = 5632
N_MAIN = 3 * D_CONV + 2 * GLA_HEADS * GLA_DK + 2 * GLA_HEADS * GLA_DV

LANES = 128
SUBLANES = 8
VMEM_LIMIT_BYTES = 56 * 1024 * 1024


def _params(*sem):
    return pltpu.CompilerParams(dimension_semantics=sem, vmem_limit_bytes=VMEM_LIMIT_BYTES)


def _dot(a, b):
    return jnp.dot(a, b, preferred_element_type=F32)


def _dot_nt(a, b):
    return lax.dot_general(a, b, (((1,), (1,)), ((), ())), preferred_element_type=F32)


def _dot_tn(a, b):
    return lax.dot_general(a, b, (((0,), (0,)), ((), ())), preferred_element_type=F32)


def _rms_rows(x, g):
    ms = jnp.mean(x * x, axis=-1, keepdims=True)
    return (x * lax.rsqrt(ms + EPS)) * g


def _row_chunk(rows):
    for c in (256, 128, 64, 32, 16, 8):
        if rows % c == 0:
            return c
    return rows


def _norm_into(x_ref, g_ref, xn_ref):
    rows = x_ref.shape[0]
    ch = _row_chunk(rows)
    g = g_ref[...]

    def body(c, carry):
        r = pl.ds(pl.multiple_of(c * ch, ch), ch)
        xn_ref[r, :] = _rms_rows(x_ref[r, :], g).astype(xn_ref.dtype)
        return carry

    lax.fori_loop(0, rows // ch, body, 0)


def _norm_matmul_kernel(x_ref, g_ref, w_ref, o_ref, xn_ref):
    @pl.when(pl.program_id(1) == 0)
    def _():
        _norm_into(x_ref, g_ref, xn_ref)

    o_ref[...] = _dot(xn_ref[...], w_ref[...]).astype(o_ref.dtype)


def _in_proj_kernel(x_ref, g_ref, w_ref, wg1_ref, wg2_ref, bg_ref, o_ref, lf_ref, xn_ref):
    @pl.when(pl.program_id(1) == 0)
    def _():
        _norm_into(x_ref, g_ref, xn_ref)
        g1 = _dot(xn_ref[...], wg1_ref[...])
        z = _dot(g1.astype(BF16), wg2_ref[...]) + bg_ref[...]
        lf_ref[...] = (jnp.minimum(z, 0.0) - jnp.log1p(jnp.exp(-jnp.abs(z)))) * (1.0 / GLA_TAU)

    o_ref[...] = _dot(xn_ref[...], w_ref[...]).astype(o_ref.dtype)


def _norm_matmul(x, gain, w, out_dtype, tm, tn):
    m, d = x.shape
    n = w.shape[1]
    return pl.pallas_call(
        _norm_matmul_kernel,
        grid=(m // tm, n // tn),
        in_specs=[pl.BlockSpec((tm, d), lambda i, j: (i, 0)),
                  pl.BlockSpec((1, d), lambda i, j: (0, 0)),
                  pl.BlockSpec((d, tn), lambda i, j: (0, j))],
        out_specs=pl.BlockSpec((tm, tn), lambda i, j: (i, j)),
        out_shape=jax.ShapeDtypeStruct((m, n), out_dtype),
        scratch_shapes=[pltpu.VMEM((tm, d), BF16)],
        compiler_params=_params("arbitrary", "arbitrary"),
        name="norm_matmul",
    )(x, gain, w)


def _in_proj(x, gain, w_main, wg1, wg2, b_gate, tm, tn):
    m, d = x.shape
    n = w_main.shape[1]
    ng = wg2.shape[1]
    return pl.pallas_call(
        _in_proj_kernel,
        grid=(m // tm, n // tn),
        in_specs=[pl.BlockSpec((tm, d), lambda i, j: (i, 0)),
                  pl.BlockSpec((1, d), lambda i, j: (0, 0)),
                  pl.BlockSpec((d, tn), lambda i, j: (0, j)),
                  pl.BlockSpec((d, LANES), lambda i, j: (0, 0)),
                  pl.BlockSpec((LANES, ng), lambda i, j: (0, 0)),
                  pl.BlockSpec((1, ng), lambda i, j: (0, 0))],
        out_specs=[pl.BlockSpec((tm, tn), lambda i, j: (i, j)),
                   pl.BlockSpec((tm, ng), lambda i, j: (i, 0))],
        out_shape=[jax.ShapeDtypeStruct((m, n), F32),
                   jax.ShapeDtypeStruct((m, ng), F32)],
        scratch_shapes=[pltpu.VMEM((tm, d), BF16)],
        compiler_params=_params("arbitrary", "arbitrary"),
        name="in_proj",
    )(x, gain, w_main, wg1, wg2, b_gate)


def _matmul_kernel(a_ref, w_ref, o_ref):
    o_ref[...] = _dot(a_ref[...], w_ref[...]).astype(o_ref.dtype)


def _matmul(a, w, out_dtype, tm, tn):
    m, k = a.shape
    n = w.shape[1]
    return pl.pallas_call(
        _matmul_kernel,
        grid=(m // tm, n // tn),
        in_specs=[pl.BlockSpec((tm, k), lambda i, j: (i, 0)),
                  pl.BlockSpec((k, tn), lambda i, j: (0, j))],
        out_specs=pl.BlockSpec((tm, tn), lambda i, j: (i, j)),
        out_shape=jax.ShapeDtypeStruct((m, n), out_dtype),
        compiler_params=_params("arbitrary", "arbitrary"),
        name="matmul",
    )(a, w)


def _halo_rows(shift):
    return max(2 * shift, SUBLANES)


def _conv_stage(buf_ref, prev, shift):
    halo = _halo_rows(shift)
    buf_ref[halo - 2 * shift:halo, :] = prev


def _conv_taps(buf_ref, r0, rows, shift):
    halo = _halo_rows(shift)

    def aligned(start):
        return start if isinstance(start, int) else pl.multiple_of(start, SUBLANES)

    if shift % SUBLANES == 0:
        s1 = buf_ref[pl.ds(aligned(r0 + (halo - shift)), rows), :]
        s2 = buf_ref[pl.ds(aligned(r0 + (halo - 2 * shift)), rows), :]
        return s2, s1
    win = buf_ref[pl.ds(aligned(r0 + (halo - SUBLANES)), rows + SUBLANES), :]
    s1 = win[SUBLANES - shift:SUBLANES - shift + rows, :]
    s2 = win[SUBLANES - 2 * shift:SUBLANES - 2 * shift + rows, :]
    return s2, s1


def _conv_gate_kernel(bg_ref, cg_ref, vc_ref, prev_ref, w_ref, o_ref, new_ref, buf_ref, *, shift):
    rows = cg_ref.shape[0]
    halo = _halo_rows(shift)
    u = cg_ref[...] * vc_ref[...]
    _conv_stage(buf_ref, prev_ref[0], shift)
    buf_ref[halo:halo + rows, :] = u
    s2, s1 = _conv_taps(buf_ref, 0, rows, shift)
    y = w_ref[0:1, :] * s2 + w_ref[1:2, :] * s1 + w_ref[2:3, :] * u
    o_ref[...] = (bg_ref[...] * y).astype(o_ref.dtype)
    new_ref[0] = buf_ref[halo + rows - 2 * shift:halo + rows, :]


def _conv_gate(p, prev, conv_w, nseq, rows, shift, tn):
    nj = D_CONV // tn
    kern = functools.partial(_conv_gate_kernel, shift=shift)
    return pl.pallas_call(
        kern,
        grid=(nseq, nj),
        in_specs=[pl.BlockSpec((rows, tn), lambda b, j: (b, j)),
                  pl.BlockSpec((rows, tn), lambda b, j: (b, j + nj)),
                  pl.BlockSpec((rows, tn), lambda b, j: (b, j + 2 * nj)),
                  pl.BlockSpec((1, 2 * shift, tn), lambda b, j: (b, 0, j)),
                  pl.BlockSpec((CONV_WIDTH, tn), lambda b, j: (0, j))],
        out_specs=[pl.BlockSpec((rows, tn), lambda b, j: (b, j)),
                   pl.BlockSpec((1, 2 * shift, tn), lambda b, j: (b, 0, j))],
        out_shape=[jax.ShapeDtypeStruct((nseq * rows, D_CONV), BF16),
                   jax.ShapeDtypeStruct((nseq, 2 * shift, D_CONV), F32)],
        scratch_shapes=[pltpu.VMEM((_halo_rows(shift) + rows, tn), F32)],
        compiler_params=_params("arbitrary", "arbitrary"),
        name="conv_gate",
    )(p, p, p, prev, conv_w)


def _cumsum_rows(g):
    c = g.shape[0]
    row = lax.broadcasted_iota(jnp.int32, g.shape, 0)
    x = g
    s = 1
    while s < c:
        x = x + jnp.where(row >= s, pltpu.roll(x, s, 0), 0.0)
        s *= 2
    return x


def _bcast_block_row(x, s, k):
    c, lanes = x.shape
    if s == c:
        return jnp.broadcast_to(x[k:k + 1, :], x.shape)
    if s >= SUBLANES:
        y = x.reshape(c // s, s, lanes)
        return jnp.broadcast_to(y[:, k:k + 1, :], y.shape).reshape(c, lanes)
    y = x.reshape(c // SUBLANES, SUBLANES, lanes)
    sub = lax.broadcasted_iota(jnp.int32, y.shape, 1)
    out = None
    for blk in range(SUBLANES // s):
        src = jnp.broadcast_to(y[:, blk * s + k:blk * s + k + 1, :], y.shape)
        out = src if out is None else jnp.where(sub >= blk * s, src, out)
    return out.reshape(c, lanes)


def _gla_chunk(q, k, v, g, s_prev):
    c = q.shape[0]
    cum = _cumsum_rows(g)
    ri = lax.broadcasted_iota(jnp.int32, (c, c), 0)
    ci = lax.broadcasted_iota(jnp.int32, (c, c), 1)
    diff_bits = ri ^ ci
    a = jnp.where(diff_bits == 0, _dot_nt(q.astype(BF16), k.astype(BF16)), 0.0)
    level = 0
    while (1 << level) < c:
        half = 1 << level
        ref = _bcast_block_row(cum, 2 * half, half - 1)
        qe = q * jnp.exp(jnp.minimum(cum - ref, 0.0))
        ke = k * jnp.exp(jnp.minimum(ref - cum, 0.0))
        mask = ((diff_bits >> level) == 1) & (((ri >> level) & 1) == 1)
        a = a + jnp.where(mask, _dot_nt(qe.astype(BF16), ke.astype(BF16)), 0.0)
        level += 1
    o = _dot(a.astype(BF16), v.astype(BF16)) + _dot((q * jnp.exp(cum)).astype(BF16), s_prev.astype(BF16))
    last = cum[c - 1:c, :]
    kd = k * jnp.exp(last - cum)
    dk = last.shape[1]
    decay_t = jnp.transpose(jnp.broadcast_to(jnp.exp(last), (dk, dk)))
    decayed = jnp.concatenate([decay_t * s_prev[:, i:i + dk] for i in range(0, s_prev.shape[1], dk)], axis=1)
    s_new = decayed + _dot_tn(kd.astype(BF16), v.astype(BF16))
    return o, s_new


def _gla_kernel(q_ref, k_ref, v_ref, r_ref, g_ref, s0_ref, gn_ref, o_ref, sn_ref, s_ref, *, chunk):
    bb, rows = q_ref.shape[0], q_ref.shape[1]
    nchunk = rows // chunk

    @pl.when(pl.program_id(2) == 0)
    def _():
        s_ref[...] = s0_ref[:, 0]

    gn = gn_ref[...]

    def body(n, carry):
        b = n // nchunk
        r = pl.ds(pl.multiple_of((n % nchunk) * chunk, chunk), chunk)
        q = q_ref[b, r, :] * (GLA_DK ** -0.5)
        o, s_new = _gla_chunk(q, k_ref[b, r, :], v_ref[b, r, :], g_ref[b, r, :], s_ref[b])
        s_ref[b] = s_new
        rr = r_ref[b, r, :]
        o_ref[b, r, :] = (_rms_rows(o, gn) * (rr * jax.nn.sigmoid(rr))).astype(o_ref.dtype)
        return carry

    lax.fori_loop(0, bb * nchunk, body, 0)
    sn_ref[:, 0] = s_ref[...]


def _gla(p, logf, s0, gla_norm, *, q_blk, k_blk, v_blk, r_blk, bb, rows, chunk):
    nb, t, _ = p.shape
    kern = functools.partial(_gla_kernel, chunk=chunk)
    return pl.pallas_call(
        kern,
        grid=(nb // bb, GLA_HEADS, t // rows),
        in_specs=[pl.BlockSpec((bb, rows, GLA_DK), lambda b, h, c: (b, c, q_blk + h)),
                  pl.BlockSpec((bb, rows, GLA_DK), lambda b, h, c: (b, c, k_blk + h)),
                  pl.BlockSpec((bb, rows, GLA_DV), lambda b, h, c: (b, c, v_blk + h)),
                  pl.BlockSpec((bb, rows, GLA_DV), lambda b, h, c: (b, c, r_blk + h)),
                  pl.BlockSpec((bb, rows, GLA_DK), lambda b, h, c: (b, c, h)),
                  pl.BlockSpec((bb, 1, GLA_DK, GLA_DV), lambda b, h, c: (b, h, 0, 0)),
                  pl.BlockSpec((1, GLA_DV), lambda b, h, c: (0, h))],
        out_specs=[pl.BlockSpec((bb, rows, GLA_DV), lambda b, h, c: (b, c, h)),
                   pl.BlockSpec((bb, 1, GLA_DK, GLA_DV), lambda b, h, c: (b, h, 0, 0))],
        out_shape=[jax.ShapeDtypeStruct((nb, t, D_GLA), BF16),
                   jax.ShapeDtypeStruct((nb, GLA_HEADS, GLA_DK, GLA_DV), F32)],
        scratch_shapes=[pltpu.VMEM((bb, GLA_DK, GLA_DV), F32)],
        compiler_params=_params("arbitrary", "arbitrary", "arbitrary"),
        name="gla",
    )(p, p, p, p, logf, s0, gla_norm)


def _proj_res_norm_kernel(a_ref, w_ref, res_ref, g_ref, *refs, nk, final):
    out_refs, acc_ref = refs[:-1], refs[-1]
    k = pl.program_id(1)
    part = _dot(a_ref[...], w_ref[...])

    @pl.when(k == 0)
    def _():
        acc_ref[...] = part

    if nk > 1:
        @pl.when(k > 0)
        def _():
            acc_ref[...] += part

    @pl.when(k == nk - 1)
    def _():
        rows = acc_ref.shape[0]
        ch = _row_chunk(rows)
        g = g_ref[...]

        def body(c, carry):
            r = pl.ds(pl.multiple_of(c * ch, ch), ch)
            h = res_ref[r, :] + acc_ref[r, :]
            hn = _rms_rows(h, g)
            if final:
                out_refs[0][r, :] = hn
            else:
                out_refs[0][r, :] = h
                out_refs[1][r, :] = hn.astype(out_refs[1].dtype)
            return carry

        lax.fori_loop(0, rows // ch, body, 0)


def _proj_res_norm(a, w, res, gain, *, tm, tk, final):
    m, kdim = a.shape
    d = w.shape[1]
    nk = kdim // tk
    kern = functools.partial(_proj_res_norm_kernel, nk=nk, final=final)
    row_spec = pl.BlockSpec((tm, d), lambda i, k: (i, 0))
    if final:
        out_specs = row_spec
        out_shape = jax.ShapeDtypeStruct((m, d), F32)
    else:
        out_specs = [row_spec, row_spec]
        out_shape = [jax.ShapeDtypeStruct((m, d), F32), jax.ShapeDtypeStruct((m, d), BF16)]
    return pl.pallas_call(
        kern,
        grid=(m // tm, nk),
        in_specs=[pl.BlockSpec((tm, tk), lambda i, k: (i, k)),
                  pl.BlockSpec((tk, d), lambda i, k: (k, 0)),
                  row_spec,
                  pl.BlockSpec((1, d), lambda i, k: (0, 0))],
        out_specs=out_specs,
        out_shape=out_shape,
        scratch_shapes=[pltpu.VMEM((tm, d), F32)],
        compiler_params=_params("arbitrary", "arbitrary"),
        name="proj_res_norm",
    )(a, w, res, gain)


def _xattn_kernel(q_ref, k_ref, v_ref, o_ref):
    bb = q_ref.shape[0]

    def body(b, carry):
        q = q_ref[b].astype(BF16)
        p = _softmax_rows(_dot_nt(q, k_ref[b].astype(BF16)) * (X_HD ** -0.5))
        o_ref[b] = _dot(p.astype(BF16), v_ref[b].astype(BF16)).astype(o_ref.dtype)
        return carry

    lax.fori_loop(0, bb, body, 0)


def _xattn(q, mk, mv, *, bb, tq):
    nb, t, d = q.shape
    return pl.pallas_call(
        _xattn_kernel,
        grid=(nb // bb, X_HEADS, t // tq),
        in_specs=[pl.BlockSpec((bb, tq, X_HD), lambda b, h, i: (b, i, h)),
                  pl.BlockSpec((bb, MEM_LEN, X_HD), lambda b, h, i: (b, 0, h)),
                  pl.BlockSpec((bb, MEM_LEN, X_HD), lambda b, h, i: (b, 0, h))],
        out_specs=pl.BlockSpec((bb, tq, X_HD), lambda b, h, i: (b, i, h)),
        out_shape=jax.ShapeDtypeStruct((nb, t, d), BF16),
        compiler_params=_params("arbitrary", "arbitrary", "arbitrary"),
        name="xattn",
    )(q, mk, mv)


def _softmax_rows(s):
    s = s - jnp.max(s, axis=-1, keepdims=True)
    e = jnp.exp(s)
    return e / jnp.sum(e, axis=-1, keepdims=True)


def _xattn_cache_kernel(q_ref, k_ref, v_ref, o_ref):
    bb = q_ref.shape[0]
    nchunk = X_HD // LANES
    pitch = nchunk * X_HEADS

    def gather(ref, b, h):
        parts = [ref[b, pl.ds(c * X_HEADS + h, MEM_LEN, stride=pitch), :] for c in range(nchunk)]
        return jnp.concatenate(parts, axis=1).astype(BF16)

    def body(b, carry):
        for h in range(X_HEADS):
            q = q_ref[b, :, h * X_HD:(h + 1) * X_HD].astype(BF16)
            p = _softmax_rows(_dot_nt(q, gather(k_ref, b, h)) * (X_HD ** -0.5))
            o_ref[b, :, h * X_HD:(h + 1) * X_HD] = _dot(p.astype(BF16), gather(v_ref, b, h)).astype(o_ref.dtype)
        return carry

    lax.fori_loop(0, bb, body, 0)


def _xattn_cache(q, ck, cv, *, bb):
    nb, tq, d = q.shape
    nchunk = X_HD // LANES

    def stored_order(c):
        c = c.reshape(nb, MEM_LEN, X_HEADS, nchunk, LANES).transpose(0, 1, 3, 2, 4)
        return c.reshape(nb, MEM_LEN * nchunk * X_HEADS, LANES)

    rows = MEM_LEN * nchunk * X_HEADS
    return pl.pallas_call(
        _xattn_cache_kernel,
        grid=(nb // bb,),
        in_specs=[pl.BlockSpec((bb, tq, d), lambda b: (b, 0, 0)),
                  pl.BlockSpec((bb, rows, LANES), lambda b: (b, 0, 0)),
                  pl.BlockSpec((bb, rows, LANES), lambda b: (b, 0, 0))],
        out_specs=pl.BlockSpec((bb, tq, d), lambda b: (b, 0, 0)),
        out_shape=jax.ShapeDtypeStruct((nb, tq, d), BF16),
        compiler_params=_params("arbitrary"),
        name="xattn_cache",
    )(q, stored_order(ck), stored_order(cv))


def _ffn_up_kernel(hn_ref, wg_ref, wu_ref, cw_ref, cb_ref, prev_ref, o_ref, new_ref, buf_ref, up_ref, *,
                   shift):
    rows = hn_ref.shape[0]
    halo = _halo_rows(shift)
    ch = _row_chunk(rows)
    _conv_stage(buf_ref, prev_ref[0], shift)
    buf_ref[halo:halo + rows, :] = _dot(hn_ref[...], wg_ref[...])
    up_ref[...] = _dot(hn_ref[...], wu_ref[...])
    w0, w1, w2, cb = cw_ref[0:1, :], cw_ref[1:2, :], cw_ref[2:3, :], cb_ref[...]

    def act_body(c, carry):
        r0 = pl.multiple_of(c * ch, ch)
        s2, s1 = _conv_taps(buf_ref, r0, ch, shift)
        gc = w0 * s2 + w1 * s1 + w2 * buf_ref[pl.ds(pl.multiple_of(halo + r0, SUBLANES), ch), :] + cb
        o_ref[pl.ds(r0, ch), :] = ((gc * jax.nn.sigmoid(gc)) * up_ref[pl.ds(r0, ch), :]).astype(o_ref.dtype)
        return carry

    lax.fori_loop(0, rows // ch, act_body, 0)
    new_ref[0] = buf_ref[halo + rows - 2 * shift:halo + rows, :]


def _ffn_up(hn, wg, wu, cw, cb, prev, nseq, rows, shift, tn):
    d = hn.shape[1]
    kern = functools.partial(_ffn_up_kernel, shift=shift)
    return pl.pallas_call(
        kern,
        grid=(nseq, D_FF // tn),
        in_specs=[pl.BlockSpec((rows, d), lambda b, j: (b, 0)),
                  pl.BlockSpec((d, tn), lambda b, j: (0, j)),
                  pl.BlockSpec((d, tn), lambda b, j: (0, j)),
                  pl.BlockSpec((CONV_WIDTH, tn), lambda b, j: (0, j)),
                  pl.BlockSpec((1, tn), lambda b, j: (0, j)),
                  pl.BlockSpec((1, 2 * shift, tn), lambda b, j: (b, 0, j))],
        out_specs=[pl.BlockSpec((rows, tn), lambda b, j: (b, j)),
                   pl.BlockSpec((1, 2 * shift, tn), lambda b, j: (b, 0, j))],
        out_shape=[jax.ShapeDtypeStruct((nseq * rows, D_FF), BF16),
                   jax.ShapeDtypeStruct((nseq, 2 * shift, D_FF), F32)],
        scratch_shapes=[pltpu.VMEM((_halo_rows(shift) + rows, tn), F32), pltpu.VMEM((rows, tn), F32)],
        compiler_params=_params("arbitrary", "arbitrary"),
        name="ffn_up",
    )(hn, wg, wu, cw, cb, prev)


def _layer(x, w, *, nseq, rows, shift, prev_conv, s0, prev_ffn, mk, mv, time_major):
    m = x.shape[0]
    tm = min(m, 1024)
    p, logf = _in_proj(x, w["norm_mix"], w["w_in_main"], w["w_g1"], w["w_g2"], w["b_gate"], tm, 1024)
    conv_out, conv_new = _conv_gate(p, prev_conv, w["conv_w"], nseq, rows, shift, 256)

    if time_major:
        nt = m // shift
        pad = ((0, 0), (0, SUBLANES - nt), (0, 0))
        pg = jnp.pad(p.reshape(nt, shift, N_MAIN)[:, :, 3 * D_CONV:].transpose(1, 0, 2), pad)
        lg = jnp.pad(logf.reshape(nt, shift, -1).transpose(1, 0, 2), pad)
        o, s_new = _gla(pg, lg, s0, w["gla_norm"], q_blk=0, k_blk=GLA_HEADS, v_blk=GLA_HEADS,
                        r_blk=2 * GLA_HEADS, bb=16, rows=SUBLANES, chunk=SUBLANES)
        gla_out = o[:, :nt].transpose(1, 0, 2).reshape(m, D_GLA)
    else:
        q0 = 3 * D_CONV // GLA_DK
        v0 = (3 * D_CONV + 2 * GLA_HEADS * GLA_DK) // GLA_DV
        o, s_new = _gla(p.reshape(nseq, rows, N_MAIN), logf.reshape(nseq, rows, -1), s0, w["gla_norm"],
                        q_blk=q0, k_blk=q0 + GLA_HEADS, v_blk=v0, r_blk=v0 + GLA_HEADS,
                        bb=1, rows=512, chunk=GLA_CHUNK)
        gla_out = o.reshape(m, D_GLA)

    mix = jnp.concatenate([conv_out, gla_out], axis=-1)
    tm2 = min(m, 512)
    h, hn = _proj_res_norm(mix, w["w_out"], x, w["norm_x"], tm=tm2, tk=D_MODEL, final=False)
    qx = _matmul(hn, w["w_xq"], BF16, tm, 1024)

    if time_major:
        nt = m // shift
        qb = jnp.pad(qx.astype(F32).reshape(nt, shift, D_MODEL).transpose(1, 0, 2),
                     ((0, 0), (0, SUBLANES - nt), (0, 0)))
        ob = _xattn_cache(qb, mk, mv, bb=2)
        attn = ob[:, :nt].transpose(1, 0, 2).reshape(m, D_MODEL)
    else:
        attn = _xattn(qx.reshape(nseq, rows, D_MODEL), mk, mv, bb=1, tq=rows).reshape(m, D_MODEL)

    h2, hn2 = _proj_res_norm(attn, w["w_xo"], h, w["norm_ffn"], tm=tm2, tk=D_MODEL, final=False)
    act, ffn_new = _ffn_up(hn2, w["w_fg"], w["w_fu"], w["ffn_conv_w"], w["ffn_conv_b"], prev_ffn,
                           nseq, rows, shift, 512)
    return act, h2, conv_new, s_new, ffn_new


def kernel(x_prompt, x_sample, mem_prompt, cache_conv, state_gla, cache_ffn, cache_mem_k, cache_mem_v,
           norm_mix, w_in, conv_w, w_gate2, b_gate, gla_norm, w_out, norm_x, norm_mem, w_xq, w_xk, w_xv,
           w_xo, norm_ffn, w_ffn_gate, w_ffn_up, ffn_conv_w, ffn_conv_b, w_ffn_down, norm_final):
    depth = w_in.shape[0]
    nb, seq, d = x_prompt.shape
    db, dseq, _ = x_sample.shape
    hp = x_prompt.reshape(nb * seq, d)
    hs = x_sample.transpose(1, 0, 2).reshape(dseq * db, d)
    outs = {k: [] for k in ("conv_p", "gla_p", "ffn_p", "mk", "mv", "conv_s", "gla_s", "ffn_s")}
    nfinal = norm_final.reshape(1, d)
    yp = ys = None
    for l in range(depth):
        w = {
            "norm_mix": norm_mix[l].reshape(1, d),
            "w_in_main": w_in[l][:, :N_MAIN].astype(BF16),
            "w_g1": jnp.pad(w_in[l][:, N_MAIN:], ((0, 0), (0, LANES - GLA_RANK))).astype(BF16),
            "w_g2": jnp.pad(w_gate2[l], ((0, LANES - GLA_RANK), (0, 0))).astype(BF16),
            "b_gate": b_gate[l].reshape(1, -1),
            "conv_w": conv_w[l],
            "gla_norm": gla_norm[l].reshape(1, -1),
            "w_out": w_out[l].astype(BF16),
            "norm_x": norm_x[l].reshape(1, d),
            "w_xq": w_xq[l].astype(BF16),
            "w_xo": w_xo[l].astype(BF16),
            "norm_ffn": norm_ffn[l].reshape(1, d),
            "w_fg": w_ffn_gate[l].astype(BF16),
            "w_fu": w_ffn_up[l].astype(BF16),
            "ffn_conv_w": ffn_conv_w[l],
            "ffn_conv_b": ffn_conv_b[l].reshape(1, -1),
        }
        w_fd = w_ffn_down[l].astype(BF16)
        last = l == depth - 1
        gain_next = nfinal if last else None

        mem = mem_prompt.reshape(nb * MEM_LEN, d)
        nmem = norm_mem[l].reshape(1, d)
        mk = _norm_matmul(mem, nmem, w_xk[l].astype(BF16), F32, nb * MEM_LEN, 1024)
        mv = _norm_matmul(mem, nmem, w_xv[l].astype(BF16), F32, nb * MEM_LEN, 1024)
        act, h2, c1, s1, f1 = _layer(
            hp, w, nseq=nb, rows=seq, shift=1,
            prev_conv=jnp.zeros((nb, CONV_WIDTH - 1, D_CONV), F32),
            s0=jnp.zeros((nb, GLA_HEADS, GLA_DK, GLA_DV), F32),
            prev_ffn=jnp.zeros((nb, CONV_WIDTH - 1, D_FF), F32),
            mk=mk.reshape(nb, MEM_LEN, d), mv=mv.reshape(nb, MEM_LEN, d), time_major=False)
        assert last, "only the final layer's epilogue (final rmsnorm) is implemented"
        yp = _proj_res_norm(act, w_fd, h2, gain_next, tm=512, tk=1408, final=True)
        outs["conv_p"].append(c1)
        outs["gla_p"].append(s1)
        outs["ffn_p"].append(f1)
        outs["mk"].append(mk.reshape(nb, MEM_LEN, X_HEADS, X_HD))
        outs["mv"].append(mv.reshape(nb, MEM_LEN, X_HEADS, X_HD))

        def tmajor(c):
            return c.transpose(1, 0, 2).reshape(1, (CONV_WIDTH - 1) * db, c.shape[-1])

        act, h2, c2, s2, f2 = _layer(
            hs, w, nseq=1, rows=dseq * db, shift=db,
            prev_conv=tmajor(cache_conv[l]), s0=state_gla[l], prev_ffn=tmajor(cache_ffn[l]),
            mk=cache_mem_k[l], mv=cache_mem_v[l], time_major=True)
        ys = _proj_res_norm(act, w_fd, h2, gain_next, tm=512, tk=1408, final=True)
        outs["conv_s"].append(c2.reshape(CONV_WIDTH - 1, db, D_CONV).transpose(1, 0, 2))
        outs["gla_s"].append(s2)
        outs["ffn_s"].append(f2.reshape(CONV_WIDTH - 1, db, D_FF).transpose(1, 0, 2))

    y_prompt = yp.reshape(nb, seq, d)
    y_sample = ys.reshape(dseq, db, d).transpose(1, 0, 2)
    st = lambda k: jnp.stack(outs[k])
    return (y_prompt, y_sample, st("conv_p"), st("gla_p"), st("ffn_p"), st("mk"), st("mv"),
            st("conv_s"), st("gla_s"), st("ffn_s"))
```

```python
import functools

import jax
import jax.numpy as jnp
from jax import lax
from jax.experimental import pallas as pl
from jax.experimental.pallas import tpu as pltpu

F32 = jnp.float32
BF16 = jnp.bfloat16

D_MODEL = 2048
EPS = 1e-6
CONV_WIDTH = 3
D_CONV = 1024
D_GLA = 1024
GLA_HEADS = 4
GLA_DV = 256
GLA_DK = 128
GLA_RANK = 16
GLA_TAU = 16.0
GLA_CHUNK = 64
X_HEADS = 4
X_HD = 512
MEM_LEN = 256
D_FF = 5632
N_MAIN = 3 * D_CONV + 2 * GLA_HEADS * GLA_DK + 2 * GLA_HEADS * GLA_DV

LANES = 128
SUBLANES = 8
VMEM_LIMIT_BYTES = 56 * 1024 * 1024


def _params(*sem):
    return pltpu.CompilerParams(dimension_semantics=sem, vmem_limit_bytes=VMEM_LIMIT_BYTES)


def _dot(a, b):
    return jnp.dot(a, b, preferred_element_type=F32)


def _dot_nt(a, b):
    return lax.dot_general(a, b, (((1,), (1,)), ((), ())), preferred_element_type=F32)


def _dot_tn(a, b):
    return lax.dot_general(a, b, (((0,), (0,)), ((), ())), preferred_element_type=F32)


def _rms_rows(x, g):
    ms = jnp.mean(x * x, axis=-1, keepdims=True)
    return (x * lax.rsqrt(ms + EPS)) * g


def _row_chunk(rows):
    for c in (256, 128, 64, 32, 16, 8):
        if rows % c == 0:
            return c
    return rows


def _norm_into(x_ref, g_ref, xn_ref):
    rows = x_ref.shape[0]
    ch = _row_chunk(rows)
    g = g_ref[...]

    def body(c, carry):
        r = pl.ds(pl.multiple_of(c * ch, ch), ch)
        xn_ref[r, :] = _rms_rows(x_ref[r, :], g).astype(xn_ref.dtype)
        return carry

    lax.fori_loop(0, rows // ch, body, 0)


def _norm_matmul_kernel(x_ref, g_ref, w_ref, o_ref, xn_ref):
    @pl.when(pl.program_id(1) == 0)
    def _():
        _norm_into(x_ref, g_ref, xn_ref)

    o_ref[...] = _dot(xn_ref[...], w_ref[...]).astype(o_ref.dtype)


def _in_proj_kernel(x_ref, g_ref, w_ref, wg1_ref, wg2_ref, bg_ref, o_ref, lf_ref, xn_ref):
    @pl.when(pl.program_id(1) == 0)
    def _():
        _norm_into(x_ref, g_ref, xn_ref)
        g1 = _dot(xn_ref[...], wg1_ref[...])
        z = _dot(g1.astype(BF16), wg2_ref[...]) + bg_ref[...]
        lf_ref[...] = (jnp.minimum(z, 0.0) - jnp.log1p(jnp.exp(-jnp.abs(z)))) * (1.0 / GLA_TAU)

    o_ref[...] = _dot(xn_ref[...], w_ref[...]).astype(o_ref.dtype)


def _norm_matmul(x, gain, w, out_dtype, tm, tn):
    m, d = x.shape
    n = w.shape[1]
    return pl.pallas_call(
        _norm_matmul_kernel,
        grid=(m // tm, n // tn),
        in_specs=[pl.BlockSpec((tm, d), lambda i, j: (i, 0)),
                  pl.BlockSpec((1, d), lambda i, j: (0, 0)),
                  pl.BlockSpec((d, tn), lambda i, j: (0, j))],
        out_specs=pl.BlockSpec((tm, tn), lambda i, j: (i, j)),
        out_shape=jax.ShapeDtypeStruct((m, n), out_dtype),
        scratch_shapes=[pltpu.VMEM((tm, d), BF16)],
        compiler_params=_params("arbitrary", "arbitrary"),
        name="norm_matmul",
    )(x, gain, w)


def _in_proj(x, gain, w_main, wg1, wg2, b_gate, tm, tn):
    m, d = x.shape
    n = w_main.shape[1]
    ng = wg2.shape[1]
    return pl.pallas_call(
        _in_proj_kernel,
        grid=(m // tm, n // tn),
        in_specs=[pl.BlockSpec((tm, d), lambda i, j: (i, 0)),
                  pl.BlockSpec((1, d), lambda i, j: (0, 0)),
                  pl.BlockSpec((d, tn), lambda i, j: (0, j)),
                  pl.BlockSpec((d, LANES), lambda i, j: (0, 0)),
                  pl.BlockSpec((LANES, ng), lambda i, j: (0, 0)),
                  pl.BlockSpec((1, ng), lambda i, j: (0, 0))],
        out_specs=[pl.BlockSpec((tm, tn), lambda i, j: (i, j)),
                   pl.BlockSpec((tm, ng), lambda i, j: (i, 0))],
        out_shape=[jax.ShapeDtypeStruct((m, n), F32),
                   jax.ShapeDtypeStruct((m, ng), F32)],
        scratch_shapes=[pltpu.VMEM((tm, d), BF16)],
        compiler_params=_params("arbitrary", "arbitrary"),
        name="in_proj",
    )(x, gain, w_main, wg1, wg2, b_gate)


def _matmul_kernel(a_ref, w_ref, o_ref):
    o_ref[...] = _dot(a_ref[...], w_ref[...]).astype(o_ref.dtype)


def _matmul(a, w, out_dtype, tm, tn):
    m, k = a.shape
    n = w.shape[1]
    return pl.pallas_call(
        _matmul_kernel,
        grid=(m // tm, n // tn),
        in_specs=[pl.BlockSpec((tm, k), lambda i, j: (i, 0)),
                  pl.BlockSpec((k, tn), lambda i, j: (0, j))],
        out_specs=pl.BlockSpec((tm, tn), lambda i, j: (i, j)),
        out_shape=jax.ShapeDtypeStruct((m, n), out_dtype),
        compiler_params=_params("arbitrary", "arbitrary"),
        name="matmul",
    )(a, w)


def _halo_rows(shift):
    return max(2 * shift, SUBLANES)


def _conv_stage(buf_ref, prev, shift):
    halo = _halo_rows(shift)
    buf_ref[halo - 2 * shift:halo, :] = prev


def _conv_taps(buf_ref, r0, rows, shift):
    halo = _halo_rows(shift)

    def aligned(start):
        return start if isinstance(start, int) else pl.multiple_of(start, SUBLANES)

    if shift % SUBLANES == 0:
        s1 = buf_ref[pl.ds(aligned(r0 + (halo - shift)), rows), :]
        s2 = buf_ref[pl.ds(aligned(r0 + (halo - 2 * shift)), rows), :]
        return s2, s1
    win = buf_ref[pl.ds(aligned(r0 + (halo - SUBLANES)), rows + SUBLANES), :]
    s1 = win[SUBLANES - shift:SUBLANES - shift + rows, :]
    s2 = win[SUBLANES - 2 * shift:SUBLANES - 2 * shift + rows, :]
    return s2, s1


def _conv_gate_kernel(bg_ref, cg_ref, vc_ref, prev_ref, w_ref, o_ref, new_ref, buf_ref, *, shift):
    rows = cg_ref.shape[0]
    halo = _halo_rows(shift)
    u = cg_ref[...] * vc_ref[...]
    _conv_stage(buf_ref, prev_ref[0], shift)
    buf_ref[halo:halo + rows, :] = u
    s2, s1 = _conv_taps(buf_ref, 0, rows, shift)
    y = w_ref[0:1, :] * s2 + w_ref[1:2, :] * s1 + w_ref[2:3, :] * u
    o_ref[...] = (bg_ref[...] * y).astype(o_ref.dtype)
    new_ref[0] = buf_ref[halo + rows - 2 * shift:halo + rows, :]


def _conv_gate(p, prev, conv_w, nseq, rows, shift, tn):
    nj = D_CONV // tn
    kern = functools.partial(_conv_gate_kernel, shift=shift)
    return pl.pallas_call(
        kern,
        grid=(nseq, nj),
        in_specs=[pl.BlockSpec((rows, tn), lambda b, j: (b, j)),
                  pl.BlockSpec((rows, tn), lambda b, j: (b, j + nj)),
                  pl.BlockSpec((rows, tn), lambda b, j: (b, j + 2 * nj)),
                  pl.BlockSpec((1, 2 * shift, tn), lambda b, j: (b, 0, j)),
                  pl.BlockSpec((CONV_WIDTH, tn), lambda b, j: (0, j))],
        out_specs=[pl.BlockSpec((rows, tn), lambda b, j: (b, j)),
                   pl.BlockSpec((1, 2 * shift, tn), lambda b, j: (b, 0, j))],
        out_shape=[jax.ShapeDtypeStruct((nseq * rows, D_CONV), BF16),
                   jax.ShapeDtypeStruct((nseq, 2 * shift, D_CONV), F32)],
        scratch_shapes=[pltpu.VMEM((_halo_rows(shift) + rows, tn), F32)],
        compiler_params=_params("arbitrary", "arbitrary"),
        name="conv_gate",
    )(p, p, p, prev, conv_w)


def _cumsum_rows(g):
    c = g.shape[0]
    row = lax.broadcasted_iota(jnp.int32, g.shape, 0)
    x = g
    s = 1
    while s < c:
        x = x + jnp.where(row >= s, pltpu.roll(x, s, 0), 0.0)
        s *= 2
    return x


def _bcast_block_row(x, s, k):
    c, lanes = x.shape
    if s == c:
        return jnp.broadcast_to(x[k:k + 1, :], x.shape)
    if s >= SUBLANES:
        y = x.reshape(c // s, s, lanes)
        return jnp.broadcast_to(y[:, k:k + 1, :], y.shape).reshape(c, lanes)
    y = x.reshape(c // SUBLANES, SUBLANES, lanes)
    sub = lax.broadcasted_iota(jnp.int32, y.shape, 1)
    out = None
    for blk in range(SUBLANES // s):
        src = jnp.broadcast_to(y[:, blk * s + k:blk * s + k + 1, :], y.shape)
        out = src if out is None else jnp.where(sub >= blk * s, src, out)
    return out.reshape(c, lanes)


def _gla_pair_masks(c):
    ri = lax.broadcasted_iota(jnp.int32, (c, c), 0)
    ci = lax.broadcasted_iota(jnp.int32, (c, c), 1)
    diff_bits = ri ^ ci
    masks = [diff_bits == 0]
    level = 0
    while (1 << level) < c:
        masks.append(((diff_bits >> level) == 1) & (((ri >> level) & 1) == 1))
        level += 1
    return masks


def _gla_chunk(q, k, v, g, s_prev, masks):
    c = q.shape[0]
    cum = _cumsum_rows(g)
    a = jnp.where(masks[0], _dot_nt(q.astype(BF16), k.astype(BF16)), 0.0)
    for level in range(len(masks) - 1):
        half = 1 << level
        ref = _bcast_block_row(cum, 2 * half, half - 1)
        qe = q * jnp.exp(jnp.minimum(cum - ref, 0.0))
        ke = k * jnp.exp(jnp.minimum(ref - cum, 0.0))
        a = a + jnp.where(masks[1 + level], _dot_nt(qe.astype(BF16), ke.astype(BF16)), 0.0)
    o = _dot(a.astype(BF16), v.astype(BF16)) + _dot((q * jnp.exp(cum)).astype(BF16), s_prev.astype(BF16))
    last = cum[c - 1:c, :]
    kd = k * jnp.exp(last - cum)
    dk = last.shape[1]
    decay_t = jnp.transpose(jnp.broadcast_to(jnp.exp(last), (dk, dk)))
    decayed = jnp.concatenate([decay_t * s_prev[:, i:i + dk] for i in range(0, s_prev.shape[1], dk)], axis=1)
    s_new = decayed + _dot_tn(kd.astype(BF16), v.astype(BF16))
    return o, s_new


def _gla_kernel(q_ref, k_ref, v_ref, r_ref, g_ref, s0_ref, gn_ref, o_ref, sn_ref, *, chunk, single_chunk):
    bb, rows = q_ref.shape[0], q_ref.shape[1]
    nchunk = rows // chunk
    state_in = s0_ref if single_chunk else sn_ref
    seq_per_iter = 2 if single_chunk and bb % 2 == 0 else 1

    if not single_chunk:
        @pl.when(pl.program_id(1) == 0)
        def _():
            sn_ref[...] = s0_ref[...]

    masks = _gla_pair_masks(chunk)

    def one(b, r):
        for h in range(GLA_HEADS):
            kc = slice(h * GLA_DK, (h + 1) * GLA_DK)
            vc = slice(h * GLA_DV, (h + 1) * GLA_DV)
            q = q_ref[b, r, kc] * (GLA_DK ** -0.5)
            o, s_new = _gla_chunk(q, k_ref[b, r, kc], v_ref[b, r, vc], g_ref[b, r, kc], state_in[b, h], masks)
            sn_ref[b, h] = s_new
            rr = r_ref[b, r, vc]
            o_ref[b, r, vc] = (_rms_rows(o, gn_ref[:, vc]) * (rr * jax.nn.sigmoid(rr))).astype(o_ref.dtype)

    def body(n, carry):
        for u in range(seq_per_iter):
            i = n * seq_per_iter + u
            one(i // nchunk, pl.ds(pl.multiple_of((i % nchunk) * chunk, chunk), chunk))
        return carry

    lax.fori_loop(0, bb * nchunk // seq_per_iter, body, 0)


def _gla(p, logf, s0, gla_norm, *, q_blk, k_blk, v_blk, r_blk, bb, rows, chunk):
    nb, t, _ = p.shape
    nk, nv = GLA_HEADS * GLA_DK, GLA_HEADS * GLA_DV
    kern = functools.partial(_gla_kernel, chunk=chunk, single_chunk=(t == chunk))
    state_spec = pl.BlockSpec((bb, GLA_HEADS, GLA_DK, GLA_DV), lambda b, c: (b, 0, 0, 0))
    return pl.pallas_call(
        kern,
        grid=(nb // bb, t // rows),
        in_specs=[pl.BlockSpec((bb, rows, nk), lambda b, c: (b, c, q_blk)),
                  pl.BlockSpec((bb, rows, nk), lambda b, c: (b, c, k_blk)),
                  pl.BlockSpec((bb, rows, nv), lambda b, c: (b, c, v_blk)),
                  pl.BlockSpec((bb, rows, nv), lambda b, c: (b, c, r_blk)),
                  pl.BlockSpec((bb, rows, nk), lambda b, c: (b, c, 0)),
                  state_spec,
                  pl.BlockSpec((1, nv), lambda b, c: (0, 0))],
        out_specs=[pl.BlockSpec((bb, rows, nv), lambda b, c: (b, c, 0)),
                   state_spec],
        out_shape=[jax.ShapeDtypeStruct((nb, t, D_GLA), BF16),
                   jax.ShapeDtypeStruct((nb, GLA_HEADS, GLA_DK, GLA_DV), F32)],
        compiler_params=_params("arbitrary", "arbitrary"),
        name="gla",
    )(p, p, p, p, logf, s0, gla_norm)


def _proj_res_norm_kernel(*refs, n_a, nk, final):
    a_refs, (w_ref, res_ref, g_ref) = refs[:n_a], refs[n_a:n_a + 3]
    out_refs, acc_ref = refs[n_a + 3:-1], refs[-1]
    k = pl.program_id(1)
    part, r0 = None, 0
    for a_ref in a_refs:
        kw = a_ref.shape[1]
        term = _dot(a_ref[...], w_ref[r0:r0 + kw, :])
        part = term if part is None else part + term
        r0 += kw

    @pl.when(k == 0)
    def _():
        acc_ref[...] = part

    if nk > 1:
        @pl.when(k > 0)
        def _():
            acc_ref[...] += part

    @pl.when(k == nk - 1)
    def _():
        rows = acc_ref.shape[0]
        ch = _row_chunk(rows)
        g = g_ref[...]

        def body(c, carry):
            r = pl.ds(pl.multiple_of(c * ch, ch), ch)
            h = res_ref[r, :] + acc_ref[r, :]
            hn = _rms_rows(h, g)
            if final:
                out_refs[0][r, :] = hn
            else:
                out_refs[0][r, :] = h
                out_refs[1][r, :] = hn.astype(out_refs[1].dtype)
            return carry

        lax.fori_loop(0, rows // ch, body, 0)


def _proj_res_norm(a_list, w, res, gain, *, tm, tk, final):
    m = a_list[0].shape[0]
    kdim, d = w.shape
    nk = kdim // tk
    assert len(a_list) == 1 or nk == 1
    kern = functools.partial(_proj_res_norm_kernel, n_a=len(a_list), nk=nk, final=final)
    a_specs = ([pl.BlockSpec((tm, tk), lambda i, k: (i, k))] if len(a_list) == 1 else
               [pl.BlockSpec((tm, a.shape[1]), lambda i, k: (i, 0)) for a in a_list])
    row_spec = pl.BlockSpec((tm, d), lambda i, k: (i, 0))
    if final:
        out_specs = row_spec
        out_shape = jax.ShapeDtypeStruct((m, d), F32)
    else:
        out_specs = [row_spec, row_spec]
        out_shape = [jax.ShapeDtypeStruct((m, d), F32), jax.ShapeDtypeStruct((m, d), BF16)]
    return pl.pallas_call(
        kern,
        grid=(m // tm, nk),
        in_specs=a_specs + [pl.BlockSpec((tk, d), lambda i, k: (k, 0)),
                            row_spec,
                            pl.BlockSpec((1, d), lambda i, k: (0, 0))],
        out_specs=out_specs,
        out_shape=out_shape,
        scratch_shapes=[pltpu.VMEM((tm, d), F32)],
        compiler_params=_params("arbitrary", "arbitrary"),
        name="proj_res_norm",
    )(*a_list, w, res, gain)


def _xattn_kernel(q_ref, k_ref, v_ref, o_ref):
    bb = q_ref.shape[0]

    def body(b, carry):
        q = q_ref[b].astype(BF16)
        p = _softmax_rows(_dot_nt(q, k_ref[b].astype(BF16)) * (X_HD ** -0.5))
        o_ref[b] = _dot(p.astype(BF16), v_ref[b].astype(BF16)).astype(o_ref.dtype)
        return carry

    lax.fori_loop(0, bb, body, 0)


def _xattn(q, mk, mv, *, bb, tq):
    nb, t, d = q.shape
    return pl.pallas_call(
        _xattn_kernel,
        grid=(nb // bb, X_HEADS, t // tq),
        in_specs=[pl.BlockSpec((bb, tq, X_HD), lambda b, h, i: (b, i, h)),
                  pl.BlockSpec((bb, MEM_LEN, X_HD), lambda b, h, i: (b, 0, h)),
                  pl.BlockSpec((bb, MEM_LEN, X_HD), lambda b, h, i: (b, 0, h))],
        out_specs=pl.BlockSpec((bb, tq, X_HD), lambda b, h, i: (b, i, h)),
        out_shape=jax.ShapeDtypeStruct((nb, t, d), BF16),
        compiler_params=_params("arbitrary", "arbitrary", "arbitrary"),
        name="xattn",
    )(q, mk, mv)


def _softmax_rows(s):
    s = s - jnp.max(s, axis=-1, keepdims=True)
    e = jnp.exp(s)
    return e / jnp.sum(e, axis=-1, keepdims=True)


def _xattn_cache_kernel(q_ref, k_ref, v_ref, o_ref):
    bb = q_ref.shape[0]
    nchunk = X_HD // LANES
    pitch = nchunk * X_HEADS

    def gather(ref, b, h):
        parts = [ref[b, pl.ds(c * X_HEADS + h, MEM_LEN, stride=pitch), :] for c in range(nchunk)]
        return jnp.concatenate(parts, axis=1).astype(BF16)

    def body(b, carry):
        for h in range(X_HEADS):
            q = q_ref[b, :, h * X_HD:(h + 1) * X_HD].astype(BF16)
            p = _softmax_rows(_dot_nt(q, gather(k_ref, b, h)) * (X_HD ** -0.5))
            o_ref[b, :, h * X_HD:(h + 1) * X_HD] = _dot(p.astype(BF16), gather(v_ref, b, h)).astype(o_ref.dtype)
        return carry

    lax.fori_loop(0, bb, body, 0)


def _xattn_cache(q, ck, cv, *, bb):
    nb, tq, d = q.shape
    nchunk = X_HD // LANES

    def stored_order(c):
        c = c.reshape(nb, MEM_LEN, X_HEADS, nchunk, LANES).transpose(0, 1, 3, 2, 4)
        return c.reshape(nb, MEM_LEN * nchunk * X_HEADS, LANES)

    rows = MEM_LEN * nchunk * X_HEADS
    return pl.pallas_call(
        _xattn_cache_kernel,
        grid=(nb // bb,),
        in_specs=[pl.BlockSpec((bb, tq, d), lambda b: (b, 0, 0)),
                  pl.BlockSpec((bb, rows, LANES), lambda b: (b, 0, 0)),
                  pl.BlockSpec((bb, rows, LANES), lambda b: (b, 0, 0))],
        out_specs=pl.BlockSpec((bb, tq, d), lambda b: (b, 0, 0)),
        out_shape=jax.ShapeDtypeStruct((nb, tq, d), BF16),
        compiler_params=_params("arbitrary"),
        name="xattn_cache",
    )(q, stored_order(ck), stored_order(cv))


def _ffn_up_kernel(hn_ref, wg_ref, wu_ref, cw_ref, cb_ref, prev_ref, o_ref, new_ref, buf_ref, up_ref, *,
                   shift):
    rows = hn_ref.shape[0]
    halo = _halo_rows(shift)
    ch = _row_chunk(rows)
    _conv_stage(buf_ref, prev_ref[0], shift)
    buf_ref[halo:halo + rows, :] = _dot(hn_ref[...], wg_ref[...])
    up_ref[...] = _dot(hn_ref[...], wu_ref[...])
    w0, w1, w2, cb = cw_ref[0:1, :], cw_ref[1:2, :], cw_ref[2:3, :], cb_ref[...]

    def act_body(c, carry):
        r0 = pl.multiple_of(c * ch, ch)
        s2, s1 = _conv_taps(buf_ref, r0, ch, shift)
        gc = w0 * s2 + w1 * s1 + w2 * buf_ref[pl.ds(pl.multiple_of(halo + r0, SUBLANES), ch), :] + cb
        o_ref[pl.ds(r0, ch), :] = ((gc * jax.nn.sigmoid(gc)) * up_ref[pl.ds(r0, ch), :]).astype(o_ref.dtype)
        return carry

    lax.fori_loop(0, rows // ch, act_body, 0)
    new_ref[0] = buf_ref[halo + rows - 2 * shift:halo + rows, :]


def _ffn_up(hn, wg, wu, cw, cb, prev, nseq, rows, shift, tn):
    d = hn.shape[1]
    kern = functools.partial(_ffn_up_kernel, shift=shift)
    return pl.pallas_call(
        kern,
        grid=(nseq, D_FF // tn),
        in_specs=[pl.BlockSpec((rows, d), lambda b, j: (b, 0)),
                  pl.BlockSpec((d, tn), lambda b, j: (0, j)),
                  pl.BlockSpec((d, tn), lambda b, j: (0, j)),
                  pl.BlockSpec((CONV_WIDTH, tn), lambda b, j: (0, j)),
                  pl.BlockSpec((1, tn), lambda b, j: (0, j)),
                  pl.BlockSpec((1, 2 * shift, tn), lambda b, j: (b, 0, j))],
        out_specs=[pl.BlockSpec((rows, tn), lambda b, j: (b, j)),
                   pl.BlockSpec((1, 2 * shift, tn), lambda b, j: (b, 0, j))],
        out_shape=[jax.ShapeDtypeStruct((nseq * rows, D_FF), BF16),
                   jax.ShapeDtypeStruct((nseq, 2 * shift, D_FF), F32)],
        scratch_shapes=[pltpu.VMEM((_halo_rows(shift) + rows, tn), F32), pltpu.VMEM((rows, tn), F32)],
        compiler_params=_params("arbitrary", "arbitrary"),
        name="ffn_up",
    )(hn, wg, wu, cw, cb, prev)


def _layer(x, w, *, nseq, rows, shift, prev_conv, s0, prev_ffn, mk, mv, time_major):
    m = x.shape[0]
    tm = min(m, 1024)
    p, logf = _in_proj(x, w["norm_mix"], w["w_in_main"], w["w_g1"], w["w_g2"], w["b_gate"], tm, 1024)
    conv_out, conv_new = _conv_gate(p, prev_conv, w["conv_w"], nseq, rows, shift, 256)

    if time_major:
        nt = m // shift
        pad = ((0, 0), (0, SUBLANES - nt), (0, 0))
        pg = jnp.pad(p.reshape(nt, shift, N_MAIN)[:, :, 3 * D_CONV:].transpose(1, 0, 2), pad)
        lg = jnp.pad(logf.reshape(nt, shift, -1).transpose(1, 0, 2), pad)
        o, s_new = _gla(pg, lg, s0, w["gla_norm"], q_blk=0, k_blk=1, v_blk=1, r_blk=2,
                        bb=8, rows=SUBLANES, chunk=SUBLANES)
        gla_out = o[:, :nt].transpose(1, 0, 2).reshape(m, D_GLA)
    else:
        q0 = 3 * D_CONV // (GLA_HEADS * GLA_DK)
        v0 = (3 * D_CONV + 2 * GLA_HEADS * GLA_DK) // (GLA_HEADS * GLA_DV)
        o, s_new = _gla(p.reshape(nseq, rows, N_MAIN), logf.reshape(nseq, rows, -1), s0, w["gla_norm"],
                        q_blk=q0, k_blk=q0 + 1, v_blk=v0, r_blk=v0 + 1,
                        bb=1, rows=512, chunk=GLA_CHUNK)
        gla_out = o.reshape(m, D_GLA)

    tm2 = min(m, 512)
    h, hn = _proj_res_norm([conv_out, gla_out], w["w_out"], x, w["norm_x"], tm=tm2, tk=D_MODEL, final=False)
    qx = _matmul(hn, w["w_xq"], BF16, tm, 1024)

    if time_major:
        nt = m // shift
        qb = jnp.pad(qx.astype(F32).reshape(nt, shift, D_MODEL).transpose(1, 0, 2),
                     ((0, 0), (0, SUBLANES - nt), (0, 0)))
        ob = _xattn_cache(qb, mk, mv, bb=2)
        attn = ob[:, :nt].transpose(1, 0, 2).reshape(m, D_MODEL)
    else:
        attn = _xattn(qx.reshape(nseq, rows, D_MODEL), mk, mv, bb=1, tq=rows).reshape(m, D_MODEL)

    h2, hn2 = _proj_res_norm([attn], w["w_xo"], h, w["norm_ffn"], tm=tm2, tk=D_MODEL, final=False)
    act, ffn_new = _ffn_up(hn2, w["w_fg"], w["w_fu"], w["ffn_conv_w"], w["ffn_conv_b"], prev_ffn,
                           nseq, rows, shift, 512)
    return act, h2, conv_new, s_new, ffn_new


def kernel(x_prompt, x_sample, mem_prompt, cache_conv, state_gla, cache_ffn, cache_mem_k, cache_mem_v,
           norm_mix, w_in, conv_w, w_gate2, b_gate, gla_norm, w_out, norm_x, norm_mem, w_xq, w_xk, w_xv,
           w_xo, norm_ffn, w_ffn_gate, w_ffn_up, ffn_conv_w, ffn_conv_b, w_ffn_down, norm_final):
    depth = w_in.shape[0]
    nb, seq, d = x_prompt.shape
    db, dseq, _ = x_sample.shape
    hp = x_prompt.reshape(nb * seq, d)
    hs = x_sample.transpose(1, 0, 2).reshape(dseq * db, d)
    outs = {k: [] for k in ("conv_p", "gla_p", "ffn_p", "mk", "mv", "conv_s", "gla_s", "ffn_s")}
    nfinal = norm_final.reshape(1, d)
    yp = ys = None
    for l in range(depth):
        w = {
            "norm_mix": norm_mix[l].reshape(1, d),
            "w_in_main": w_in[l][:, :N_MAIN].astype(BF16),
            "w_g1": jnp.pad(w_in[l][:, N_MAIN:], ((0, 0), (0, LANES - GLA_RANK))).astype(BF16),
            "w_g2": jnp.pad(w_gate2[l], ((0, LANES - GLA_RANK), (0, 0))).astype(BF16),
            "b_gate": b_gate[l].reshape(1, -1),
            "conv_w": conv_w[l],
            "gla_norm": gla_norm[l].reshape(1, -1),
            "w_out": w_out[l].astype(BF16),
            "norm_x": norm_x[l].reshape(1, d),
            "w_xq": w_xq[l].astype(BF16),
            "w_xo": w_xo[l].astype(BF16),
            "norm_ffn": norm_ffn[l].reshape(1, d),
            "w_fg": w_ffn_gate[l].astype(BF16),
            "w_fu": w_ffn_up[l].astype(BF16),
            "ffn_conv_w": ffn_conv_w[l],
            "ffn_conv_b": ffn_conv_b[l].reshape(1, -1),
        }
        w_fd = w_ffn_down[l].astype(BF16)
        last = l == depth - 1
        gain_next = nfinal if last else None

        mem = mem_prompt.reshape(nb * MEM_LEN, d)
        nmem = norm_mem[l].reshape(1, d)
        mk = _norm_matmul(mem, nmem, w_xk[l].astype(BF16), F32, nb * MEM_LEN, 1024)
        mv = _norm_matmul(mem, nmem, w_xv[l].astype(BF16), F32, nb * MEM_LEN, 1024)
        act, h2, c1, s1, f1 = _layer(
            hp, w, nseq=nb, rows=seq, shift=1,
            prev_conv=jnp.zeros((nb, CONV_WIDTH - 1, D_CONV), F32),
            s0=jnp.zeros((nb, GLA_HEADS, GLA_DK, GLA_DV), F32),
            prev_ffn=jnp.zeros((nb, CONV_WIDTH - 1, D_FF), F32),
            mk=mk.reshape(nb, MEM_LEN, d), mv=mv.reshape(nb, MEM_LEN, d), time_major=False)
        assert last, "only the final layer's epilogue (final rmsnorm) is implemented"
        yp = _proj_res_norm([act], w_fd, h2, gain_next, tm=512, tk=1408, final=True)
        outs["conv_p"].append(c1)
        outs["gla_p"].append(s1)
        outs["ffn_p"].append(f1)
        outs["mk"].append(mk.reshape(nb, MEM_LEN, X_HEADS, X_HD))
        outs["mv"].append(mv.reshape(nb, MEM_LEN, X_HEADS, X_HD))

        def tmajor(c):
            return c.transpose(1, 0, 2).reshape(1, (CONV_WIDTH - 1) * db, c.shape[-1])

        act, h2, c2, s2, f2 = _layer(
            hs, w, nseq=1, rows=dseq * db, shift=db,
            prev_conv=tmajor(cache_conv[l]), s0=state_gla[l], prev_ffn=tmajor(cache_ffn[l]),
            mk=cache_mem_k[l], mv=cache_mem_v[l], time_major=True)
        ys = _proj_res_norm([act], w_fd, h2, gain_next, tm=512, tk=1408, final=True)
        outs["conv_s"].append(c2.reshape(CONV_WIDTH - 1, db, D_CONV).transpose(1, 0, 2))
        outs["gla_s"].append(s2)
        outs["ffn_s"].append(f2.reshape(CONV_WIDTH - 1, db, D_FF).transpose(1, 0, 2))

    y_prompt = yp.reshape(nb, seq, d)
    y_sample = ys.reshape(dseq, db, d).transpose(1, 0, 2)
    st = lambda k: jnp.stack(outs[k])
    return (y_prompt, y_sample, st("conv_p"), st("gla_p"), st("ffn_p"), st("mk"), st("mv"),
            st("conv_s"), st("gla_s"), st("ffn_s"))
```

```python
import functools

import jax
import jax.numpy as jnp
from jax import lax
from jax.experimental import pallas as pl
from jax.experimental.pallas import tpu as pltpu

F32 = jnp.float32
BF16 = jnp.bfloat16

D_MODEL = 2048
EPS = 1e-6
CONV_WIDTH = 3
D_CONV = 1024
D_GLA = 1024
GLA_HEADS = 4
GLA_DV = 256
GLA_DK = 128
GLA_RANK = 16
GLA_TAU = 16.0
GLA_CHUNK = 64
X_HEADS = 4
X_HD = 512
MEM_LEN = 256
D_FF = 5632
N_MAIN = 3 * D_CONV + 2 * GLA_HEADS * GLA_DK + 2 * GLA_HEADS * GLA_DV

LANES = 128
SUBLANES = 8
VMEM_LIMIT_BYTES = 56 * 1024 * 1024


def _params(*sem):
    return pltpu.CompilerParams(dimension_semantics=sem, vmem_limit_bytes=VMEM_LIMIT_BYTES)


def _dot(a, b):
    return jnp.dot(a, b, preferred_element_type=F32)


def _dot_nt(a, b):
    return lax.dot_general(a, b, (((1,), (1,)), ((), ())), preferred_element_type=F32)


def _dot_tn(a, b):
    return lax.dot_general(a, b, (((0,), (0,)), ((), ())), preferred_element_type=F32)


def _rms_rows(x, g):
    ms = jnp.mean(x * x, axis=-1, keepdims=True)
    return (x * lax.rsqrt(ms + EPS)) * g


def _row_chunk(rows):
    for c in (256, 128, 64, 32, 16, 8):
        if rows % c == 0:
            return c
    return rows


def _norm_into(x_ref, g_ref, xn_ref):
    rows = x_ref.shape[0]
    ch = _row_chunk(rows)
    g = g_ref[...]

    def body(c, carry):
        r = pl.ds(pl.multiple_of(c * ch, ch), ch)
        xn_ref[r, :] = _rms_rows(x_ref[r, :], g).astype(xn_ref.dtype)
        return carry

    lax.fori_loop(0, rows // ch, body, 0)


def _norm_matmul_kernel(x_ref, g_ref, w_ref, o_ref, xn_ref):
    @pl.when(pl.program_id(1) == 0)
    def _():
        _norm_into(x_ref, g_ref, xn_ref)

    o_ref[...] = _dot(xn_ref[...], w_ref[...]).astype(o_ref.dtype)


def _in_proj_kernel(x_ref, g_ref, w_ref, wg1_ref, wg2_ref, bg_ref, o_ref, lf_ref, xn_ref):
    @pl.when(pl.program_id(1) == 0)
    def _():
        _norm_into(x_ref, g_ref, xn_ref)
        g1 = _dot(xn_ref[...], wg1_ref[...])
        z = _dot(g1.astype(BF16), wg2_ref[...]) + bg_ref[...]
        lf_ref[...] = (jnp.minimum(z, 0.0) - jnp.log1p(jnp.exp(-jnp.abs(z)))) * (1.0 / GLA_TAU)

    o_ref[...] = _dot(xn_ref[...], w_ref[...]).astype(o_ref.dtype)


def _norm_matmul(x, gain, w, out_dtype, tm, tn):
    m, d = x.shape
    n = w.shape[1]
    return pl.pallas_call(
        _norm_matmul_kernel,
        grid=(m // tm, n // tn),
        in_specs=[pl.BlockSpec((tm, d), lambda i, j: (i, 0)),
                  pl.BlockSpec((1, d), lambda i, j: (0, 0)),
                  pl.BlockSpec((d, tn), lambda i, j: (0, j))],
        out_specs=pl.BlockSpec((tm, tn), lambda i, j: (i, j)),
        out_shape=jax.ShapeDtypeStruct((m, n), out_dtype),
        scratch_shapes=[pltpu.VMEM((tm, d), BF16)],
        compiler_params=_params("arbitrary", "arbitrary"),
        name="norm_matmul",
    )(x, gain, w)


def _in_proj(x, gain, w_in, wg1, wg2, b_gate, tm, tn):
    m, d = x.shape
    n = N_MAIN
    ng = wg2.shape[1]
    return pl.pallas_call(
        _in_proj_kernel,
        grid=(m // tm, n // tn),
        in_specs=[pl.BlockSpec((tm, d), lambda i, j: (i, 0)),
                  pl.BlockSpec((1, d), lambda i, j: (0, 0)),
                  pl.BlockSpec((d, tn), lambda i, j: (0, j)),
                  pl.BlockSpec((d, LANES), lambda i, j: (0, 0)),
                  pl.BlockSpec((LANES, ng), lambda i, j: (0, 0)),
                  pl.BlockSpec((1, ng), lambda i, j: (0, 0))],
        out_specs=[pl.BlockSpec((tm, tn), lambda i, j: (i, j)),
                   pl.BlockSpec((tm, ng), lambda i, j: (i, 0))],
        out_shape=[jax.ShapeDtypeStruct((m, n), F32),
                   jax.ShapeDtypeStruct((m, ng), F32)],
        scratch_shapes=[pltpu.VMEM((tm, d), BF16)],
        compiler_params=_params("arbitrary", "arbitrary"),
        name="in_proj",
    )(x, gain, w_in, wg1, wg2, b_gate)


def _matmul_kernel(a_ref, w_ref, o_ref):
    o_ref[...] = _dot(a_ref[...], w_ref[...]).astype(o_ref.dtype)


def _matmul(a, w, out_dtype, tm, tn):
    m, k = a.shape
    n = w.shape[1]
    return pl.pallas_call(
        _matmul_kernel,
        grid=(m // tm, n // tn),
        in_specs=[pl.BlockSpec((tm, k), lambda i, j: (i, 0)),
                  pl.BlockSpec((k, tn), lambda i, j: (0, j))],
        out_specs=pl.BlockSpec((tm, tn), lambda i, j: (i, j)),
        out_shape=jax.ShapeDtypeStruct((m, n), out_dtype),
        compiler_params=_params("arbitrary", "arbitrary"),
        name="matmul",
    )(a, w)


def _halo_rows(shift):
    return max(2 * shift, SUBLANES)


def _conv_stage(buf_ref, prev, shift):
    halo = _halo_rows(shift)
    buf_ref[halo - 2 * shift:halo, :] = prev


def _conv_taps(buf_ref, r0, rows, shift):
    halo = _halo_rows(shift)

    def aligned(start):
        return start if isinstance(start, int) else pl.multiple_of(start, SUBLANES)

    if shift % SUBLANES == 0:
        s1 = buf_ref[pl.ds(aligned(r0 + (halo - shift)), rows), :]
        s2 = buf_ref[pl.ds(aligned(r0 + (halo - 2 * shift)), rows), :]
        return s2, s1
    win = buf_ref[pl.ds(aligned(r0 + (halo - SUBLANES)), rows + SUBLANES), :]
    s1 = pltpu.roll(win, shift, 0)[SUBLANES:, :]
    s2 = pltpu.roll(win, 2 * shift, 0)[SUBLANES:, :]
    return s2, s1


def _conv_gate_kernel(bg_ref, cg_ref, vc_ref, prev_ref, w_ref, o_ref, new_ref, buf_ref, *, shift):
    rows = cg_ref.shape[0]
    halo = _halo_rows(shift)
    u = cg_ref[...] * vc_ref[...]
    _conv_stage(buf_ref, prev_ref[0], shift)
    buf_ref[halo:halo + rows, :] = u
    s2, s1 = _conv_taps(buf_ref, 0, rows, shift)
    y = w_ref[0:1, :] * s2 + w_ref[1:2, :] * s1 + w_ref[2:3, :] * u
    o_ref[...] = (bg_ref[...] * y).astype(o_ref.dtype)
    new_ref[0] = buf_ref[halo + rows - 2 * shift:halo + rows, :]


def _conv_gate(p, prev, conv_w, nseq, rows, shift, tn):
    nj = D_CONV // tn
    kern = functools.partial(_conv_gate_kernel, shift=shift)
    return pl.pallas_call(
        kern,
        grid=(nseq, nj),
        in_specs=[pl.BlockSpec((rows, tn), lambda b, j: (b, j)),
                  pl.BlockSpec((rows, tn), lambda b, j: (b, j + nj)),
                  pl.BlockSpec((rows, tn), lambda b, j: (b, j + 2 * nj)),
                  pl.BlockSpec((1, 2 * shift, tn), lambda b, j: (b, 0, j)),
                  pl.BlockSpec((CONV_WIDTH, tn), lambda b, j: (0, j))],
        out_specs=[pl.BlockSpec((rows, tn), lambda b, j: (b, j)),
                   pl.BlockSpec((1, 2 * shift, tn), lambda b, j: (b, 0, j))],
        out_shape=[jax.ShapeDtypeStruct((nseq * rows, D_CONV), BF16),
                   jax.ShapeDtypeStruct((nseq, 2 * shift, D_CONV), F32)],
        scratch_shapes=[pltpu.VMEM((_halo_rows(shift) + rows, tn), F32)],
        compiler_params=_params("arbitrary", "arbitrary"),
        name="conv_gate",
    )(p, p, p, prev, conv_w)


def _cumsum_rows(g):
    c = g.shape[0]
    row = lax.broadcasted_iota(jnp.int32, g.shape, 0)
    x = g
    s = 1
    while s < c:
        x = x + jnp.where(row >= s, pltpu.roll(x, s, 0), 0.0)
        s *= 2
    return x


def _bcast_block_row(x, s, k):
    c, lanes = x.shape
    if s == c:
        return jnp.broadcast_to(x[k:k + 1, :], x.shape)
    if s >= SUBLANES:
        y = x.reshape(c // s, s, lanes)
        return jnp.broadcast_to(y[:, k:k + 1, :], y.shape).reshape(c, lanes)
    y = x.reshape(c // SUBLANES, SUBLANES, lanes)
    sub = lax.broadcasted_iota(jnp.int32, y.shape, 1)
    out = None
    for blk in range(SUBLANES // s):
        src = jnp.broadcast_to(y[:, blk * s + k:blk * s + k + 1, :], y.shape)
        out = src if out is None else jnp.where(sub >= blk * s, src, out)
    return out.reshape(c, lanes)


def _gla_pair_masks(c):
    ri = lax.broadcasted_iota(jnp.int32, (c, c), 0)
    ci = lax.broadcasted_iota(jnp.int32, (c, c), 1)
    diff_bits = ri ^ ci
    masks = [diff_bits == 0]
    level = 0
    while (1 << level) < c:
        masks.append(((diff_bits >> level) == 1) & (((ri >> level) & 1) == 1))
        level += 1
    return masks


def _gla_chunk(q, k, v, g, s_prev, masks):
    c = q.shape[0]
    cum = _cumsum_rows(g)
    a = jnp.where(masks[0], _dot_nt(q.astype(BF16), k.astype(BF16)), 0.0)
    for level in range(len(masks) - 1):
        half = 1 << level
        ref = _bcast_block_row(cum, 2 * half, half - 1)
        qe = q * jnp.exp(jnp.minimum(cum - ref, 0.0))
        ke = k * jnp.exp(jnp.minimum(ref - cum, 0.0))
        a = a + jnp.where(masks[1 + level], _dot_nt(qe.astype(BF16), ke.astype(BF16)), 0.0)
    o = _dot(a.astype(BF16), v.astype(BF16)) + _dot((q * jnp.exp(cum)).astype(BF16), s_prev.astype(BF16))
    last = cum[c - 1:c, :]
    kd = k * jnp.exp(last - cum)
    dk = last.shape[1]
    decay_t = jnp.transpose(jnp.broadcast_to(jnp.exp(last), (dk, dk)))
    decayed = jnp.concatenate([decay_t * s_prev[:, i:i + dk] for i in range(0, s_prev.shape[1], dk)], axis=1)
    s_new = decayed + _dot_tn(kd.astype(BF16), v.astype(BF16))
    return o, s_new


def _gla_kernel(q_ref, k_ref, v_ref, r_ref, g_ref, s0_ref, gn_ref, o_ref, sn_ref, *, chunk, single_chunk):
    bb, rows = q_ref.shape[0], q_ref.shape[1]
    nchunk = rows // chunk
    state_in = s0_ref if single_chunk else sn_ref
    seq_per_iter = 2 if single_chunk and bb % 2 == 0 else 1

    if not single_chunk:
        @pl.when(pl.program_id(1) == 0)
        def _():
            sn_ref[...] = s0_ref[...]

    masks = _gla_pair_masks(chunk)

    def one(b, r):
        for h in range(GLA_HEADS):
            kc = slice(h * GLA_DK, (h + 1) * GLA_DK)
            vc = slice(h * GLA_DV, (h + 1) * GLA_DV)
            q = q_ref[b, r, kc] * (GLA_DK ** -0.5)
            o, s_new = _gla_chunk(q, k_ref[b, r, kc], v_ref[b, r, vc], g_ref[b, r, kc], state_in[b, h], masks)
            sn_ref[b, h] = s_new
            rr = r_ref[b, r, vc]
            o_ref[b, r, vc] = (_rms_rows(o, gn_ref[:, vc]) * (rr * jax.nn.sigmoid(rr))).astype(o_ref.dtype)

    def body(n, carry):
        for u in range(seq_per_iter):
            i = n * seq_per_iter + u
            one(i // nchunk, pl.ds(pl.multiple_of((i % nchunk) * chunk, chunk), chunk))
        return carry

    lax.fori_loop(0, bb * nchunk // seq_per_iter, body, 0)


def _gla(p, logf, s0, gla_norm, *, q_blk, k_blk, v_blk, r_blk, bb, rows, chunk):
    nb, t, _ = p.shape
    nk, nv = GLA_HEADS * GLA_DK, GLA_HEADS * GLA_DV
    kern = functools.partial(_gla_kernel, chunk=chunk, single_chunk=(t == chunk))
    state_spec = pl.BlockSpec((bb, GLA_HEADS, GLA_DK, GLA_DV), lambda b, c: (b, 0, 0, 0))
    return pl.pallas_call(
        kern,
        grid=(nb // bb, t // rows),
        in_specs=[pl.BlockSpec((bb, rows, nk), lambda b, c: (b, c, q_blk)),
                  pl.BlockSpec((bb, rows, nk), lambda b, c: (b, c, k_blk)),
                  pl.BlockSpec((bb, rows, nv), lambda b, c: (b, c, v_blk)),
                  pl.BlockSpec((bb, rows, nv), lambda b, c: (b, c, r_blk)),
                  pl.BlockSpec((bb, rows, nk), lambda b, c: (b, c, 0)),
                  state_spec,
                  pl.BlockSpec((1, nv), lambda b, c: (0, 0))],
        out_specs=[pl.BlockSpec((bb, rows, nv), lambda b, c: (b, c, 0)),
                   state_spec],
        out_shape=[jax.ShapeDtypeStruct((nb, t, D_GLA), BF16),
                   jax.ShapeDtypeStruct((nb, GLA_HEADS, GLA_DK, GLA_DV), F32)],
        compiler_params=_params("arbitrary", "arbitrary"),
        name="gla",
    )(p, p, p, p, logf, s0, gla_norm)


def _proj_res_norm_kernel(*refs, n_a, nk, final):
    a_refs, (w_ref, res_ref, g_ref) = refs[:n_a], refs[n_a:n_a + 3]
    out_refs, acc_ref = refs[n_a + 3:-1], refs[-1]
    k = pl.program_id(1)
    part, r0 = None, 0
    for a_ref in a_refs:
        kw = a_ref.shape[1]
        term = _dot(a_ref[...], w_ref[r0:r0 + kw, :])
        part = term if part is None else part + term
        r0 += kw

    @pl.when(k == 0)
    def _():
        acc_ref[...] = part

    if nk > 1:
        @pl.when(k > 0)
        def _():
            acc_ref[...] += part

    @pl.when(k == nk - 1)
    def _():
        rows = acc_ref.shape[0]
        ch = _row_chunk(rows)
        g = g_ref[...]

        def body(c, carry):
            r = pl.ds(pl.multiple_of(c * ch, ch), ch)
            h = res_ref[r, :] + acc_ref[r, :]
            hn = _rms_rows(h, g)
            if final:
                out_refs[0][r, :] = hn
            else:
                out_refs[0][r, :] = h
                out_refs[1][r, :] = hn.astype(out_refs[1].dtype)
            return carry

        lax.fori_loop(0, rows // ch, body, 0)


def _proj_res_norm(a_list, w, res, gain, *, tm, tk, final):
    m = a_list[0].shape[0]
    kdim, d = w.shape
    nk = kdim // tk
    assert len(a_list) == 1 or nk == 1
    kern = functools.partial(_proj_res_norm_kernel, n_a=len(a_list), nk=nk, final=final)
    a_specs = ([pl.BlockSpec((tm, tk), lambda i, k: (i, k))] if len(a_list) == 1 else
               [pl.BlockSpec((tm, a.shape[1]), lambda i, k: (i, 0)) for a in a_list])
    row_spec = pl.BlockSpec((tm, d), lambda i, k: (i, 0))
    if final:
        out_specs = row_spec
        out_shape = jax.ShapeDtypeStruct((m, d), F32)
    else:
        out_specs = [row_spec, row_spec]
        out_shape = [jax.ShapeDtypeStruct((m, d), F32), jax.ShapeDtypeStruct((m, d), BF16)]
    return pl.pallas_call(
        kern,
        grid=(m // tm, nk),
        in_specs=a_specs + [pl.BlockSpec((tk, d), lambda i, k: (k, 0)),
                            row_spec,
                            pl.BlockSpec((1, d), lambda i, k: (0, 0))],
        out_specs=out_specs,
        out_shape=out_shape,
        scratch_shapes=[pltpu.VMEM((tm, d), F32)],
        compiler_params=_params("arbitrary", "arbitrary"),
        name="proj_res_norm",
    )(*a_list, w, res, gain)


def _xattn_kernel(q_ref, k_ref, v_ref, o_ref):
    bb = q_ref.shape[0]

    def body(b, carry):
        q = q_ref[b].astype(BF16)
        p = _softmax_rows(_dot_nt(q, k_ref[b].astype(BF16)) * (X_HD ** -0.5))
        o_ref[b] = _dot(p.astype(BF16), v_ref[b].astype(BF16)).astype(o_ref.dtype)
        return carry

    lax.fori_loop(0, bb, body, 0)


def _xattn(q, mk, mv, *, bb, tq):
    nb, t, d = q.shape
    return pl.pallas_call(
        _xattn_kernel,
        grid=(nb // bb, X_HEADS, t // tq),
        in_specs=[pl.BlockSpec((bb, tq, X_HD), lambda b, h, i: (b, i, h)),
                  pl.BlockSpec((bb, MEM_LEN, X_HD), lambda b, h, i: (b, 0, h)),
                  pl.BlockSpec((bb, MEM_LEN, X_HD), lambda b, h, i: (b, 0, h))],
        out_specs=pl.BlockSpec((bb, tq, X_HD), lambda b, h, i: (b, i, h)),
        out_shape=jax.ShapeDtypeStruct((nb, t, d), BF16),
        compiler_params=_params("arbitrary", "arbitrary", "arbitrary"),
        name="xattn",
    )(q, mk, mv)


def _softmax_rows(s):
    s = s - jnp.max(s, axis=-1, keepdims=True)
    e = jnp.exp(s)
    return e / jnp.sum(e, axis=-1, keepdims=True)


def _xattn_cache_kernel(q_ref, k_ref, v_ref, o_ref):
    bb = q_ref.shape[0]
    nchunk = X_HD // LANES
    pitch = nchunk * X_HEADS

    def gather(ref, b, h):
        parts = [ref[b, pl.ds(c * X_HEADS + h, MEM_LEN, stride=pitch), :] for c in range(nchunk)]
        return jnp.concatenate(parts, axis=1).astype(BF16)

    for b in range(bb):
        for h in range(X_HEADS):
            q = q_ref[b, :, h * X_HD:(h + 1) * X_HD].astype(BF16)
            p = _softmax_rows(_dot_nt(q, gather(k_ref, b, h)) * (X_HD ** -0.5))
            o_ref[b, :, h * X_HD:(h + 1) * X_HD] = _dot(p.astype(BF16), gather(v_ref, b, h)).astype(o_ref.dtype)


def _xattn_cache(q, ck, cv, *, bb):
    nb, tq, d = q.shape
    nchunk = X_HD // LANES

    def stored_order(c):
        c = c.reshape(nb, MEM_LEN, X_HEADS, nchunk, LANES).transpose(0, 1, 3, 2, 4)
        return c.reshape(nb, MEM_LEN * nchunk * X_HEADS, LANES)

    rows = MEM_LEN * nchunk * X_HEADS
    return pl.pallas_call(
        _xattn_cache_kernel,
        grid=(nb // bb,),
        in_specs=[pl.BlockSpec((bb, tq, d), lambda b: (b, 0, 0)),
                  pl.BlockSpec((bb, rows, LANES), lambda b: (b, 0, 0)),
                  pl.BlockSpec((bb, rows, LANES), lambda b: (b, 0, 0))],
        out_specs=pl.BlockSpec((bb, tq, d), lambda b: (b, 0, 0)),
        out_shape=jax.ShapeDtypeStruct((nb, tq, d), BF16),
        compiler_params=_params("arbitrary"),
        name="xattn_cache",
    )(q, stored_order(ck), stored_order(cv))


def _ffn_up_kernel(hn_ref, wg_ref, wu_ref, cw_ref, cb_ref, prev_ref, o_ref, new_ref, buf_ref, up_ref, *,
                   shift):
    rows = hn_ref.shape[0]
    halo = _halo_rows(shift)
    ch = _row_chunk(rows)
    _conv_stage(buf_ref, prev_ref[0], shift)
    buf_ref[halo:halo + rows, :] = _dot(hn_ref[...], wg_ref[...])
    up_ref[...] = _dot(hn_ref[...], wu_ref[...])
    w0, w1, w2, cb = cw_ref[0:1, :], cw_ref[1:2, :], cw_ref[2:3, :], cb_ref[...]

    def act_body(c, carry):
        r0 = pl.multiple_of(c * ch, ch)
        s2, s1 = _conv_taps(buf_ref, r0, ch, shift)
        gc = w0 * s2 + w1 * s1 + w2 * buf_ref[pl.ds(pl.multiple_of(halo + r0, SUBLANES), ch), :] + cb
        o_ref[pl.ds(r0, ch), :] = ((gc * jax.nn.sigmoid(gc)) * up_ref[pl.ds(r0, ch), :]).astype(o_ref.dtype)
        return carry

    lax.fori_loop(0, rows // ch, act_body, 0)
    new_ref[0] = buf_ref[halo + rows - 2 * shift:halo + rows, :]


def _ffn_up(hn, wg, wu, cw, cb, prev, nseq, rows, shift, tn):
    d = hn.shape[1]
    kern = functools.partial(_ffn_up_kernel, shift=shift)
    return pl.pallas_call(
        kern,
        grid=(nseq, D_FF // tn),
        in_specs=[pl.BlockSpec((rows, d), lambda b, j: (b, 0)),
                  pl.BlockSpec((d, tn), lambda b, j: (0, j)),
                  pl.BlockSpec((d, tn), lambda b, j: (0, j)),
                  pl.BlockSpec((CONV_WIDTH, tn), lambda b, j: (0, j)),
                  pl.BlockSpec((1, tn), lambda b, j: (0, j)),
                  pl.BlockSpec((1, 2 * shift, tn), lambda b, j: (b, 0, j))],
        out_specs=[pl.BlockSpec((rows, tn), lambda b, j: (b, j)),
                   pl.BlockSpec((1, 2 * shift, tn), lambda b, j: (b, 0, j))],
        out_shape=[jax.ShapeDtypeStruct((nseq * rows, D_FF), BF16),
                   jax.ShapeDtypeStruct((nseq, 2 * shift, D_FF), F32)],
        scratch_shapes=[pltpu.VMEM((_halo_rows(shift) + rows, tn), F32), pltpu.VMEM((rows, tn), F32)],
        compiler_params=_params("arbitrary", "arbitrary"),
        name="ffn_up",
    )(hn, wg, wu, cw, cb, prev)


def _layer(x, w, *, nseq, rows, shift, prev_conv, s0, prev_ffn, mk, mv, time_major):
    m = x.shape[0]
    tm = min(m, 1024)
    p, logf = _in_proj(x, w["norm_mix"], w["w_in"], w["w_g1"], w["w_g2"], w["b_gate"], tm, 1024)
    conv_out, conv_new = _conv_gate(p, prev_conv, w["conv_w"], nseq, rows, shift, 256)

    if time_major:
        nt = m // shift
        pad = ((0, 0), (0, SUBLANES - nt), (0, 0))
        pg = jnp.pad(p.reshape(nt, shift, N_MAIN)[:, :, 3 * D_CONV:].transpose(1, 0, 2), pad)
        lg = jnp.pad(logf.reshape(nt, shift, -1).transpose(1, 0, 2), pad)
        o, s_new = _gla(pg, lg, s0, w["gla_norm"], q_blk=0, k_blk=1, v_blk=1, r_blk=2,
                        bb=8, rows=SUBLANES, chunk=SUBLANES)
        gla_out = o[:, :nt].transpose(1, 0, 2).reshape(m, D_GLA)
    else:
        q0 = 3 * D_CONV // (GLA_HEADS * GLA_DK)
        v0 = (3 * D_CONV + 2 * GLA_HEADS * GLA_DK) // (GLA_HEADS * GLA_DV)
        o, s_new = _gla(p.reshape(nseq, rows, N_MAIN), logf.reshape(nseq, rows, -1), s0, w["gla_norm"],
                        q_blk=q0, k_blk=q0 + 1, v_blk=v0, r_blk=v0 + 1,
                        bb=1, rows=512, chunk=GLA_CHUNK)
        gla_out = o.reshape(m, D_GLA)

    tm2 = min(m, 512)
    h, hn = _proj_res_norm([conv_out, gla_out], w["w_out"], x, w["norm_x"], tm=tm2, tk=D_MODEL, final=False)
    qx = _matmul(hn, w["w_xq"], BF16, tm, D_MODEL)

    if time_major:
        nt = m // shift
        qb = jnp.pad(qx.astype(F32).reshape(nt, shift, D_MODEL).transpose(1, 0, 2),
                     ((0, 0), (0, SUBLANES - nt), (0, 0)))
        ob = _xattn_cache(qb, mk, mv, bb=2)
        attn = ob[:, :nt].transpose(1, 0, 2).reshape(m, D_MODEL)
    else:
        attn = _xattn(qx.reshape(nseq, rows, D_MODEL), mk, mv, bb=1, tq=rows).reshape(m, D_MODEL)

    h2, hn2 = _proj_res_norm([attn], w["w_xo"], h, w["norm_ffn"], tm=tm2, tk=D_MODEL, final=False)
    act, ffn_new = _ffn_up(hn2, w["w_fg"], w["w_fu"], w["ffn_conv_w"], w["ffn_conv_b"], prev_ffn,
                           nseq, rows, shift, 512)
    return act, h2, conv_new, s_new, ffn_new


def kernel(x_prompt, x_sample, mem_prompt, cache_conv, state_gla, cache_ffn, cache_mem_k, cache_mem_v,
           norm_mix, w_in, conv_w, w_gate2, b_gate, gla_norm, w_out, norm_x, norm_mem, w_xq, w_xk, w_xv,
           w_xo, norm_ffn, w_ffn_gate, w_ffn_up, ffn_conv_w, ffn_conv_b, w_ffn_down, norm_final):
    depth = w_in.shape[0]
    nb, seq, d = x_prompt.shape
    db, dseq, _ = x_sample.shape
    hp = x_prompt.reshape(nb * seq, d)
    hs = x_sample.transpose(1, 0, 2).reshape(dseq * db, d)
    outs = {k: [] for k in ("conv_p", "gla_p", "ffn_p", "mk", "mv", "conv_s", "gla_s", "ffn_s")}
    nfinal = norm_final.reshape(1, d)
    yp = ys = None
    for l in range(depth):
        w = {
            "norm_mix": norm_mix[l].reshape(1, d),
            "w_in": w_in[l].astype(BF16),
            "w_g1": jnp.pad(w_in[l][:, N_MAIN:], ((0, 0), (0, LANES - GLA_RANK))).astype(BF16),
            "w_g2": jnp.pad(w_gate2[l], ((0, LANES - GLA_RANK), (0, 0))).astype(BF16),
            "b_gate": b_gate[l].reshape(1, -1),
            "conv_w": conv_w[l],
            "gla_norm": gla_norm[l].reshape(1, -1),
            "w_out": w_out[l].astype(BF16),
            "norm_x": norm_x[l].reshape(1, d),
            "w_xq": w_xq[l].astype(BF16),
            "w_xo": w_xo[l].astype(BF16),
            "norm_ffn": norm_ffn[l].reshape(1, d),
            "w_fg": w_ffn_gate[l].astype(BF16),
            "w_fu": w_ffn_up[l].astype(BF16),
            "ffn_conv_w": ffn_conv_w[l],
            "ffn_conv_b": ffn_conv_b[l].reshape(1, -1),
        }
        w_fd = w_ffn_down[l].astype(BF16)
        last = l == depth - 1
        gain_next = nfinal if last else None

        mem = mem_prompt.reshape(nb * MEM_LEN, d)
        nmem = norm_mem[l].reshape(1, d)
        mk = _norm_matmul(mem, nmem, w_xk[l].astype(BF16), F32, nb * MEM_LEN, 1024)
        mv = _norm_matmul(mem, nmem, w_xv[l].astype(BF16), F32, nb * MEM_LEN, 1024)
        act, h2, c1, s1, f1 = _layer(
            hp, w, nseq=nb, rows=seq, shift=1,
            prev_conv=jnp.zeros((nb, CONV_WIDTH - 1, D_CONV), F32),
            s0=jnp.zeros((nb, GLA_HEADS, GLA_DK, GLA_DV), F32),
            prev_ffn=jnp.zeros((nb, CONV_WIDTH - 1, D_FF), F32),
            mk=mk.reshape(nb, MEM_LEN, d), mv=mv.reshape(nb, MEM_LEN, d), time_major=False)
        assert last, "only the final layer's epilogue (final rmsnorm) is implemented"
        yp = _proj_res_norm([act], w_fd, h2, gain_next, tm=512, tk=D_FF // 2, final=True)
        outs["conv_p"].append(c1)
        outs["gla_p"].append(s1)
        outs["ffn_p"].append(f1)
        outs["mk"].append(mk.reshape(nb, MEM_LEN, X_HEADS, X_HD))
        outs["mv"].append(mv.reshape(nb, MEM_LEN, X_HEADS, X_HD))

        def tmajor(c):
            return c.transpose(1, 0, 2).reshape(1, (CONV_WIDTH - 1) * db, c.shape[-1])

        act, h2, c2, s2, f2 = _layer(
            hs, w, nseq=1, rows=dseq * db, shift=db,
            prev_conv=tmajor(cache_conv[l]), s0=state_gla[l], prev_ffn=tmajor(cache_ffn[l]),
            mk=cache_mem_k[l], mv=cache_mem_v[l], time_major=True)
        ys = _proj_res_norm([act], w_fd, h2, gain_next, tm=512, tk=D_FF // 2, final=True)
        outs["conv_s"].append(c2.reshape(CONV_WIDTH - 1, db, D_CONV).transpose(1, 0, 2))
        outs["gla_s"].append(s2)
        outs["ffn_s"].append(f2.reshape(CONV_WIDTH - 1, db, D_FF).transpose(1, 0, 2))

    y_prompt = yp.reshape(nb, seq, d)
    y_sample = ys.reshape(dseq, db, d).transpose(1, 0, 2)
    st = lambda k: jnp.stack(outs[k])
    return (y_prompt, y_sample, st("conv_p"), st("gla_p"), st("ffn_p"), st("mk"), st("mv"),
            st("conv_s"), st("gla_s"), st("ffn_s"))
```

```python
import functools

import jax
import jax.numpy as jnp
from jax import lax
from jax.experimental import pallas as pl
from jax.experimental.pallas import tpu as pltpu

F32 = jnp.float32
BF16 = jnp.bfloat16

D_MODEL = 2048
EPS = 1e-6
CONV_WIDTH = 3
D_CONV = 1024
D_GLA = 1024
GLA_HEADS = 4
GLA_DV = 256
GLA_DK = 128
GLA_RANK = 16
GLA_TAU = 16.0
GLA_CHUNK = 64
X_HEADS = 4
X_HD = 512
MEM_LEN = 256
D_FF = 5632
N_MAIN = 3 * D_CONV + 2 * GLA_HEADS * GLA_DK + 2 * GLA_HEADS * GLA_DV

LANES = 128
SUBLANES = 8
VMEM_LIMIT_BYTES = 56 * 1024 * 1024


def _params(*sem):
    return pltpu.CompilerParams(dimension_semantics=sem, vmem_limit_bytes=VMEM_LIMIT_BYTES)


def _dot(a, b):
    return jnp.dot(a, b, preferred_element_type=F32)


def _dot_nt(a, b):
    return lax.dot_general(a, b, (((1,), (1,)), ((), ())), preferred_element_type=F32)


def _dot_tn(a, b):
    return lax.dot_general(a, b, (((0,), (0,)), ((), ())), preferred_element_type=F32)


def _rms_rows(x, g):
    ms = jnp.mean(x * x, axis=-1, keepdims=True)
    return (x * lax.rsqrt(ms + EPS)) * g


def _row_chunk(rows, limit=256):
    for c in (256, 128, 64, 32, 16, 8):
        if c <= limit and rows % c == 0:
            return c
    return rows


NORM_ROWS = 128


def _norm_into(x_ref, g_ref, xn_ref):
    rows = x_ref.shape[0]
    ch = _row_chunk(rows)
    g = g_ref[...]

    def body(c, carry):
        r = pl.ds(pl.multiple_of(c * ch, ch), ch)
        xn_ref[r, :] = _rms_rows(x_ref[r, :], g).astype(xn_ref.dtype)
        return carry

    lax.fori_loop(0, rows // ch, body, 0)


def _norm_matmul_kernel(x_ref, g_ref, w_ref, o_ref, xn_ref):
    @pl.when(pl.program_id(1) == 0)
    def _():
        _norm_into(x_ref, g_ref, xn_ref)

    o_ref[...] = _dot(xn_ref[...], w_ref[...]).astype(o_ref.dtype)


def _in_proj_kernel(x_ref, g_ref, w_ref, wg1_ref, wg2_ref, bg_ref, o_ref, lf_ref, xn_ref):
    @pl.when(pl.program_id(1) == 0)
    def _():
        _norm_into(x_ref, g_ref, xn_ref)
        g1 = _dot(xn_ref[...], wg1_ref[...])
        z = _dot(g1.astype(BF16), wg2_ref[...]) + bg_ref[...]
        lf_ref[...] = (jnp.minimum(z, 0.0) - jnp.log1p(jnp.exp(-jnp.abs(z)))) * (1.0 / GLA_TAU)

    o_ref[...] = _dot(xn_ref[...], w_ref[...]).astype(o_ref.dtype)


def _norm_matmul(x, gain, w, out_dtype, tm, tn):
    m, d = x.shape
    n = w.shape[1]
    return pl.pallas_call(
        _norm_matmul_kernel,
        grid=(m // tm, n // tn),
        in_specs=[pl.BlockSpec((tm, d), lambda i, j: (i, 0)),
                  pl.BlockSpec((1, d), lambda i, j: (0, 0)),
                  pl.BlockSpec((d, tn), lambda i, j: (0, j))],
        out_specs=pl.BlockSpec((tm, tn), lambda i, j: (i, j)),
        out_shape=jax.ShapeDtypeStruct((m, n), out_dtype),
        scratch_shapes=[pltpu.VMEM((tm, d), BF16)],
        compiler_params=_params("arbitrary", "arbitrary"),
        name="norm_matmul",
    )(x, gain, w)


def _in_proj(x, gain, w_in, wg1, wg2, b_gate, tm, tn):
    m, d = x.shape
    n = N_MAIN
    ng = wg2.shape[1]
    return pl.pallas_call(
        _in_proj_kernel,
        grid=(m // tm, n // tn),
        in_specs=[pl.BlockSpec((tm, d), lambda i, j: (i, 0)),
                  pl.BlockSpec((1, d), lambda i, j: (0, 0)),
                  pl.BlockSpec((d, tn), lambda i, j: (0, j)),
                  pl.BlockSpec((d, LANES), lambda i, j: (0, 0)),
                  pl.BlockSpec((LANES, ng), lambda i, j: (0, 0)),
                  pl.BlockSpec((1, ng), lambda i, j: (0, 0))],
        out_specs=[pl.BlockSpec((tm, tn), lambda i, j: (i, j)),
                   pl.BlockSpec((tm, ng), lambda i, j: (i, 0))],
        out_shape=[jax.ShapeDtypeStruct((m, n), F32),
                   jax.ShapeDtypeStruct((m, ng), F32)],
        scratch_shapes=[pltpu.VMEM((tm, d), BF16)],
        compiler_params=_params("arbitrary", "arbitrary"),
        name="in_proj",
    )(x, gain, w_in, wg1, wg2, b_gate)


def _matmul_kernel(a_ref, w_ref, o_ref):
    o_ref[...] = _dot(a_ref[...], w_ref[...]).astype(o_ref.dtype)


def _matmul(a, w, out_dtype, tm, tn):
    m, k = a.shape
    n = w.shape[1]
    return pl.pallas_call(
        _matmul_kernel,
        grid=(m // tm, n // tn),
        in_specs=[pl.BlockSpec((tm, k), lambda i, j: (i, 0)),
                  pl.BlockSpec((k, tn), lambda i, j: (0, j))],
        out_specs=pl.BlockSpec((tm, tn), lambda i, j: (i, j)),
        out_shape=jax.ShapeDtypeStruct((m, n), out_dtype),
        compiler_params=_params("arbitrary", "arbitrary"),
        name="matmul",
    )(a, w)


def _halo_rows(shift):
    return max(2 * shift, SUBLANES)


def _conv_stage(buf_ref, prev, shift):
    halo = _halo_rows(shift)
    buf_ref[halo - 2 * shift:halo, :] = prev


def _conv_taps(buf_ref, r0, rows, shift):
    halo = _halo_rows(shift)

    def aligned(start):
        return start if isinstance(start, int) else pl.multiple_of(start, SUBLANES)

    if shift % SUBLANES == 0:
        s1 = buf_ref[pl.ds(aligned(r0 + (halo - shift)), rows), :]
        s2 = buf_ref[pl.ds(aligned(r0 + (halo - 2 * shift)), rows), :]
        return s2, s1
    win = buf_ref[pl.ds(aligned(r0 + (halo - SUBLANES)), rows + SUBLANES), :]
    s1 = pltpu.roll(win, shift, 0)[SUBLANES:, :]
    s2 = pltpu.roll(win, 2 * shift, 0)[SUBLANES:, :]
    return s2, s1


def _conv_gate_kernel(bg_ref, cg_ref, vc_ref, prev_ref, w_ref, o_ref, new_ref, buf_ref, *, shift):
    rows = cg_ref.shape[0]
    halo = _halo_rows(shift)
    u = cg_ref[...] * vc_ref[...]
    _conv_stage(buf_ref, prev_ref[0], shift)
    buf_ref[halo:halo + rows, :] = u
    s2, s1 = _conv_taps(buf_ref, 0, rows, shift)
    y = w_ref[0:1, :] * s2 + w_ref[1:2, :] * s1 + w_ref[2:3, :] * u
    o_ref[...] = (bg_ref[...] * y).astype(o_ref.dtype)
    new_ref[0] = buf_ref[halo + rows - 2 * shift:halo + rows, :]


def _conv_gate(p, prev, conv_w, nseq, rows, shift, tn):
    nj = D_CONV // tn
    kern = functools.partial(_conv_gate_kernel, shift=shift)
    return pl.pallas_call(
        kern,
        grid=(nseq, nj),
        in_specs=[pl.BlockSpec((rows, tn), lambda b, j: (b, j)),
                  pl.BlockSpec((rows, tn), lambda b, j: (b, j + nj)),
                  pl.BlockSpec((rows, tn), lambda b, j: (b, j + 2 * nj)),
                  pl.BlockSpec((1, 2 * shift, tn), lambda b, j: (b, 0, j)),
                  pl.BlockSpec((CONV_WIDTH, tn), lambda b, j: (0, j))],
        out_specs=[pl.BlockSpec((rows, tn), lambda b, j: (b, j)),
                   pl.BlockSpec((1, 2 * shift, tn), lambda b, j: (b, 0, j))],
        out_shape=[jax.ShapeDtypeStruct((nseq * rows, D_CONV), BF16),
                   jax.ShapeDtypeStruct((nseq, 2 * shift, D_CONV), F32)],
        scratch_shapes=[pltpu.VMEM((_halo_rows(shift) + rows, tn), F32)],
        compiler_params=_params("arbitrary", "arbitrary"),
        name="conv_gate",
    )(p, p, p, prev, conv_w)


def _cumsum_rows(g):
    c = g.shape[0]
    row = lax.broadcasted_iota(jnp.int32, g.shape, 0)
    x = g
    s = 1
    while s < c:
        x = x + jnp.where(row >= s, pltpu.roll(x, s, 0), 0.0)
        s *= 2
    return x


def _bcast_block_row(x, s, k):
    c, lanes = x.shape
    if s == c:
        return jnp.broadcast_to(x[k:k + 1, :], x.shape)
    if s >= SUBLANES:
        y = x.reshape(c // s, s, lanes)
        return jnp.broadcast_to(y[:, k:k + 1, :], y.shape).reshape(c, lanes)
    y = x.reshape(c // SUBLANES, SUBLANES, lanes)
    sub = lax.broadcasted_iota(jnp.int32, y.shape, 1)
    out = None
    for blk in range(SUBLANES // s):
        src = jnp.broadcast_to(y[:, blk * s + k:blk * s + k + 1, :], y.shape)
        out = src if out is None else jnp.where(sub >= blk * s, src, out)
    return out.reshape(c, lanes)


def _gla_pair_masks(c):
    ri = lax.broadcasted_iota(jnp.int32, (c, c), 0)
    ci = lax.broadcasted_iota(jnp.int32, (c, c), 1)
    diff_bits = ri ^ ci
    masks = [diff_bits == 0]
    level = 0
    while (1 << level) < c:
        masks.append(((diff_bits >> level) == 1) & (((ri >> level) & 1) == 1))
        level += 1
    return masks


def _gla_chunk(q, k, v, g, s_prev, masks):
    c = q.shape[0]
    cum = _cumsum_rows(g)
    a = jnp.where(masks[0], _dot_nt(q.astype(BF16), k.astype(BF16)), 0.0)
    for level in range(len(masks) - 1):
        half = 1 << level
        ref = _bcast_block_row(cum, 2 * half, half - 1)
        qe = q * jnp.exp(jnp.minimum(cum - ref, 0.0))
        ke = k * jnp.exp(jnp.minimum(ref - cum, 0.0))
        a = a + jnp.where(masks[1 + level], _dot_nt(qe.astype(BF16), ke.astype(BF16)), 0.0)
    o = _dot(a.astype(BF16), v.astype(BF16)) + _dot((q * jnp.exp(cum)).astype(BF16), s_prev.astype(BF16))
    last = cum[c - 1:c, :]
    kd = k * jnp.exp(last - cum)
    dk = last.shape[1]
    decay_t = jnp.transpose(jnp.broadcast_to(jnp.exp(last), (dk, dk)))
    decayed = jnp.concatenate([decay_t * s_prev[:, i:i + dk] for i in range(0, s_prev.shape[1], dk)], axis=1)
    s_new = decayed + _dot_tn(kd.astype(BF16), v.astype(BF16))
    return o, s_new


def _gla_kernel(q_ref, k_ref, v_ref, r_ref, g_ref, s0_ref, gn_ref, o_ref, sn_ref, *, chunk, single_chunk):
    bb, rows = q_ref.shape[0], q_ref.shape[1]
    nchunk = rows // chunk
    state_in = s0_ref if single_chunk else sn_ref
    seq_per_iter = 2 if single_chunk and bb % 2 == 0 else 1

    if not single_chunk:
        @pl.when(pl.program_id(1) == 0)
        def _():
            sn_ref[...] = s0_ref[...]

    masks = _gla_pair_masks(chunk)

    def one(b, r):
        for h in range(GLA_HEADS):
            kc = slice(h * GLA_DK, (h + 1) * GLA_DK)
            vc = slice(h * GLA_DV, (h + 1) * GLA_DV)
            q = q_ref[b, r, kc] * (GLA_DK ** -0.5)
            o, s_new = _gla_chunk(q, k_ref[b, r, kc], v_ref[b, r, vc], g_ref[b, r, kc], state_in[b, h], masks)
            sn_ref[b, h] = s_new
            rr = r_ref[b, r, vc]
            o_ref[b, r, vc] = (_rms_rows(o, gn_ref[:, vc]) * (rr * jax.nn.sigmoid(rr))).astype(o_ref.dtype)

    def body(n, carry):
        for u in range(seq_per_iter):
            i = n * seq_per_iter + u
            one(i // nchunk, pl.ds(pl.multiple_of((i % nchunk) * chunk, chunk), chunk))
        return carry

    lax.fori_loop(0, bb * nchunk // seq_per_iter, body, 0)


def _gla(p, logf, s0, gla_norm, *, q_blk, k_blk, v_blk, r_blk, bb, rows, chunk):
    nb, t, _ = p.shape
    nk, nv = GLA_HEADS * GLA_DK, GLA_HEADS * GLA_DV
    kern = functools.partial(_gla_kernel, chunk=chunk, single_chunk=(t == chunk))
    state_spec = pl.BlockSpec((bb, GLA_HEADS, GLA_DK, GLA_DV), lambda b, c: (b, 0, 0, 0))
    return pl.pallas_call(
        kern,
        grid=(nb // bb, t // rows),
        in_specs=[pl.BlockSpec((bb, rows, nk), lambda b, c: (b, c, q_blk)),
                  pl.BlockSpec((bb, rows, nk), lambda b, c: (b, c, k_blk)),
                  pl.BlockSpec((bb, rows, nv), lambda b, c: (b, c, v_blk)),
                  pl.BlockSpec((bb, rows, nv), lambda b, c: (b, c, r_blk)),
                  pl.BlockSpec((bb, rows, nk), lambda b, c: (b, c, 0)),
                  state_spec,
                  pl.BlockSpec((1, nv), lambda b, c: (0, 0))],
        out_specs=[pl.BlockSpec((bb, rows, nv), lambda b, c: (b, c, 0)),
                   state_spec],
        out_shape=[jax.ShapeDtypeStruct((nb, t, D_GLA), BF16),
                   jax.ShapeDtypeStruct((nb, GLA_HEADS, GLA_DK, GLA_DV), F32)],
        compiler_params=_params("arbitrary", "arbitrary"),
        name="gla",
    )(p, p, p, p, logf, s0, gla_norm)


def _proj_res_norm_kernel(*refs, n_a, nk, final):
    a_refs, (w_ref, res_ref, g_ref) = refs[:n_a], refs[n_a:n_a + 3]
    out_refs, acc_ref = refs[n_a + 3:-1], refs[-1]
    k = pl.program_id(1)
    part, r0 = None, 0
    for a_ref in a_refs:
        kw = a_ref.shape[1]
        term = _dot(a_ref[...], w_ref[r0:r0 + kw, :])
        part = term if part is None else part + term
        r0 += kw

    @pl.when(k == 0)
    def _():
        acc_ref[...] = part

    if nk > 1:
        @pl.when(k > 0)
        def _():
            acc_ref[...] += part

    @pl.when(k == nk - 1)
    def _():
        rows = acc_ref.shape[0]
        ch = _row_chunk(rows, NORM_ROWS)
        g = g_ref[...]

        def body(c, carry):
            r = pl.ds(pl.multiple_of(c * ch, ch), ch)
            h = res_ref[r, :] + acc_ref[r, :]
            hn = _rms_rows(h, g)
            if final:
                out_refs[0][r, :] = hn
            else:
                out_refs[0][r, :] = h
                out_refs[1][r, :] = hn.astype(out_refs[1].dtype)
            return carry

        lax.fori_loop(0, rows // ch, body, 0)


def _proj_res_norm(a_list, w, res, gain, *, tm, tk, final):
    m = a_list[0].shape[0]
    kdim, d = w.shape
    nk = kdim // tk
    assert len(a_list) == 1 or nk == 1
    kern = functools.partial(_proj_res_norm_kernel, n_a=len(a_list), nk=nk, final=final)
    a_specs = ([pl.BlockSpec((tm, tk), lambda i, k: (i, k))] if len(a_list) == 1 else
               [pl.BlockSpec((tm, a.shape[1]), lambda i, k: (i, 0)) for a in a_list])
    row_spec = pl.BlockSpec((tm, d), lambda i, k: (i, 0))
    if final:
        out_specs = row_spec
        out_shape = jax.ShapeDtypeStruct((m, d), F32)
    else:
        out_specs = [row_spec, row_spec]
        out_shape = [jax.ShapeDtypeStruct((m, d), F32), jax.ShapeDtypeStruct((m, d), BF16)]
    return pl.pallas_call(
        kern,
        grid=(m // tm, nk),
        in_specs=a_specs + [pl.BlockSpec((tk, d), lambda i, k: (k, 0)),
                            row_spec,
                            pl.BlockSpec((1, d), lambda i, k: (0, 0))],
        out_specs=out_specs,
        out_shape=out_shape,
        scratch_shapes=[pltpu.VMEM((tm, d), F32)],
        compiler_params=_params("arbitrary", "arbitrary"),
        name="proj_res_norm",
    )(*a_list, w, res, gain)


def _xattn_kernel(q_ref, k_ref, v_ref, o_ref):
    bb = q_ref.shape[0]

    def body(b, carry):
        q = q_ref[b].astype(BF16)
        p = _softmax_rows(_dot_nt(q, k_ref[b].astype(BF16)) * (X_HD ** -0.5))
        o_ref[b] = _dot(p.astype(BF16), v_ref[b].astype(BF16)).astype(o_ref.dtype)
        return carry

    lax.fori_loop(0, bb, body, 0)


def _xattn(q, mk, mv, *, bb, tq):
    nb, t, d = q.shape
    return pl.pallas_call(
        _xattn_kernel,
        grid=(nb // bb, X_HEADS, t // tq),
        in_specs=[pl.BlockSpec((bb, tq, X_HD), lambda b, h, i: (b, i, h)),
                  pl.BlockSpec((bb, MEM_LEN, X_HD), lambda b, h, i: (b, 0, h)),
                  pl.BlockSpec((bb, MEM_LEN, X_HD), lambda b, h, i: (b, 0, h))],
        out_specs=pl.BlockSpec((bb, tq, X_HD), lambda b, h, i: (b, i, h)),
        out_shape=jax.ShapeDtypeStruct((nb, t, d), BF16),
        compiler_params=_params("arbitrary", "arbitrary", "arbitrary"),
        name="xattn",
    )(q, mk, mv)


def _softmax_rows(s):
    s = s - jnp.max(s, axis=-1, keepdims=True)
    e = jnp.exp(s)
    return e / jnp.sum(e, axis=-1, keepdims=True)


def _xattn_cache_kernel(q_ref, k_hbm, v_hbm, o_ref, kbuf, vbuf, sem, *, nsteps):
    bb = q_ref.shape[0]
    nchunk = X_HD // LANES
    pitch = nchunk * X_HEADS
    i = pl.program_id(0)

    def copies(step, slot):
        out = []
        for t, (src, dst) in enumerate(((k_hbm, kbuf), (v_hbm, vbuf))):
            for b in range(bb):
                for r in range(pitch):
                    out.append(pltpu.make_async_copy(src.at[step * bb + b, :, r, :], dst.at[slot, b, r],
                                                     sem.at[slot, t]))
        return out

    @pl.when(i == 0)
    def _():
        for cp in copies(0, 0):
            cp.start()

    @pl.when(i + 1 < nsteps)
    def _():
        for cp in copies(i + 1, (i + 1) % 2):
            cp.start()

    slot = i % 2
    for cp in copies(i, slot):
        cp.wait()

    def head(buf, b, h):
        parts = [buf[slot, b, c * X_HEADS + h] for c in range(nchunk)]
        return jnp.concatenate(parts, axis=1).astype(BF16)

    for b in range(bb):
        for h in range(X_HEADS):
            q = q_ref[b, :, h * X_HD:(h + 1) * X_HD].astype(BF16)
            p = _softmax_rows(_dot_nt(q, head(kbuf, b, h)) * (X_HD ** -0.5))
            o_ref[b, :, h * X_HD:(h + 1) * X_HD] = _dot(p.astype(BF16), head(vbuf, b, h)).astype(o_ref.dtype)


def _xattn_cache(q, ck, cv, *, bb):
    nb, tq, d = q.shape
    nchunk = X_HD // LANES
    pitch = nchunk * X_HEADS

    def stored_order(c):
        c = c.reshape(nb, MEM_LEN, X_HEADS, nchunk, LANES).transpose(0, 1, 3, 2, 4)
        return c.reshape(nb, MEM_LEN, pitch, LANES)

    nsteps = nb // bb
    buf = pltpu.VMEM((2, bb, pitch, MEM_LEN, LANES), F32)
    kern = functools.partial(_xattn_cache_kernel, nsteps=nsteps)
    return pl.pallas_call(
        kern,
        grid=(nsteps,),
        in_specs=[pl.BlockSpec((bb, tq, d), lambda b: (b, 0, 0)),
                  pl.BlockSpec(memory_space=pl.ANY),
                  pl.BlockSpec(memory_space=pl.ANY)],
        out_specs=pl.BlockSpec((bb, tq, d), lambda b: (b, 0, 0)),
        out_shape=jax.ShapeDtypeStruct((nb, tq, d), BF16),
        scratch_shapes=[buf, buf, pltpu.SemaphoreType.DMA((2, 2))],
        compiler_params=_params("arbitrary"),
        name="xattn_cache",
    )(q, stored_order(ck), stored_order(cv))


def _ffn_up_kernel(hn_ref, wg_ref, wu_ref, cw_ref, cb_ref, prev_ref, o_ref, new_ref, buf_ref, up_ref, *,
                   shift):
    rows = hn_ref.shape[0]
    halo = _halo_rows(shift)
    ch = _row_chunk(rows)
    _conv_stage(buf_ref, prev_ref[0], shift)
    buf_ref[halo:halo + rows, :] = _dot(hn_ref[...], wg_ref[...])
    up_ref[...] = _dot(hn_ref[...], wu_ref[...])
    w0, w1, w2, cb = cw_ref[0:1, :], cw_ref[1:2, :], cw_ref[2:3, :], cb_ref[...]

    def act_body(c, carry):
        r0 = pl.multiple_of(c * ch, ch)
        s2, s1 = _conv_taps(buf_ref, r0, ch, shift)
        gc = w0 * s2 + w1 * s1 + w2 * buf_ref[pl.ds(pl.multiple_of(halo + r0, SUBLANES), ch), :] + cb
        o_ref[pl.ds(r0, ch), :] = ((gc * jax.nn.sigmoid(gc)) * up_ref[pl.ds(r0, ch), :]).astype(o_ref.dtype)
        return carry

    lax.fori_loop(0, rows // ch, act_body, 0)
    new_ref[0] = buf_ref[halo + rows - 2 * shift:halo + rows, :]


def _ffn_up(hn, wg, wu, cw, cb, prev, nseq, rows, shift, tn):
    d = hn.shape[1]
    kern = functools.partial(_ffn_up_kernel, shift=shift)
    return pl.pallas_call(
        kern,
        grid=(nseq, D_FF // tn),
        in_specs=[pl.BlockSpec((rows, d), lambda b, j: (b, 0)),
                  pl.BlockSpec((d, tn), lambda b, j: (0, j)),
                  pl.BlockSpec((d, tn), lambda b, j: (0, j)),
                  pl.BlockSpec((CONV_WIDTH, tn), lambda b, j: (0, j)),
                  pl.BlockSpec((1, tn), lambda b, j: (0, j)),
                  pl.BlockSpec((1, 2 * shift, tn), lambda b, j: (b, 0, j))],
        out_specs=[pl.BlockSpec((rows, tn), lambda b, j: (b, j)),
                   pl.BlockSpec((1, 2 * shift, tn), lambda b, j: (b, 0, j))],
        out_shape=[jax.ShapeDtypeStruct((nseq * rows, D_FF), BF16),
                   jax.ShapeDtypeStruct((nseq, 2 * shift, D_FF), F32)],
        scratch_shapes=[pltpu.VMEM((_halo_rows(shift) + rows, tn), F32), pltpu.VMEM((rows, tn), F32)],
        compiler_params=_params("arbitrary", "arbitrary"),
        name="ffn_up",
    )(hn, wg, wu, cw, cb, prev)


def _layer(x, w, *, nseq, rows, shift, prev_conv, s0, prev_ffn, mk, mv, time_major):
    m = x.shape[0]
    tm = min(m, 1024)
    p, logf = _in_proj(x, w["norm_mix"], w["w_in"], w["w_g1"], w["w_g2"], w["b_gate"], tm, 1024)
    conv_out, conv_new = _conv_gate(p, prev_conv, w["conv_w"], nseq, rows, shift, 256)

    if time_major:
        nt = m // shift
        pad = ((0, 0), (0, SUBLANES - nt), (0, 0))
        pg = jnp.pad(p.reshape(nt, shift, N_MAIN)[:, :, 3 * D_CONV:].transpose(1, 0, 2), pad)
        lg = jnp.pad(logf.reshape(nt, shift, -1).transpose(1, 0, 2), pad)
        o, s_new = _gla(pg, lg, s0, w["gla_norm"], q_blk=0, k_blk=1, v_blk=1, r_blk=2,
                        bb=8, rows=SUBLANES, chunk=SUBLANES)
        gla_out = o[:, :nt].transpose(1, 0, 2).reshape(m, D_GLA)
    else:
        q0 = 3 * D_CONV // (GLA_HEADS * GLA_DK)
        v0 = (3 * D_CONV + 2 * GLA_HEADS * GLA_DK) // (GLA_HEADS * GLA_DV)
        o, s_new = _gla(p.reshape(nseq, rows, N_MAIN), logf.reshape(nseq, rows, -1), s0, w["gla_norm"],
                        q_blk=q0, k_blk=q0 + 1, v_blk=v0, r_blk=v0 + 1,
                        bb=1, rows=512, chunk=GLA_CHUNK)
        gla_out = o.reshape(m, D_GLA)

    tm2 = min(m, 512)
    h, hn = _proj_res_norm([conv_out, gla_out], w["w_out"], x, w["norm_x"], tm=tm2, tk=D_MODEL, final=False)
    qx = _matmul(hn, w["w_xq"], BF16, tm, D_MODEL)

    if time_major:
        nt = m // shift
        qb = jnp.pad(qx.astype(F32).reshape(nt, shift, D_MODEL).transpose(1, 0, 2),
                     ((0, 0), (0, SUBLANES - nt), (0, 0)))
        ob = _xattn_cache(qb, mk, mv, bb=2)
        attn = ob[:, :nt].transpose(1, 0, 2).reshape(m, D_MODEL)
    else:
        attn = _xattn(qx.reshape(nseq, rows, D_MODEL), mk, mv, bb=1, tq=rows).reshape(m, D_MODEL)

    h2, hn2 = _proj_res_norm([attn], w["w_xo"], h, w["norm_ffn"], tm=tm2, tk=D_MODEL, final=False)
    act, ffn_new = _ffn_up(hn2, w["w_fg"], w["w_fu"], w["ffn_conv_w"], w["ffn_conv_b"], prev_ffn,
                           nseq, rows, shift, 512)
    return act, h2, conv_new, s_new, ffn_new


def kernel(x_prompt, x_sample, mem_prompt, cache_conv, state_gla, cache_ffn, cache_mem_k, cache_mem_v,
           norm_mix, w_in, conv_w, w_gate2, b_gate, gla_norm, w_out, norm_x, norm_mem, w_xq, w_xk, w_xv,
           w_xo, norm_ffn, w_ffn_gate, w_ffn_up, ffn_conv_w, ffn_conv_b, w_ffn_down, norm_final):
    depth = w_in.shape[0]
    nb, seq, d = x_prompt.shape
    db, dseq, _ = x_sample.shape
    hp = x_prompt.reshape(nb * seq, d)
    hs = x_sample.transpose(1, 0, 2).reshape(dseq * db, d)
    outs = {k: [] for k in ("conv_p", "gla_p", "ffn_p", "mk", "mv", "conv_s", "gla_s", "ffn_s")}
    nfinal = norm_final.reshape(1, d)
    yp = ys = None
    for l in range(depth):
        w = {
            "norm_mix": norm_mix[l].reshape(1, d),
            "w_in": w_in[l].astype(BF16),
            "w_g1": jnp.pad(w_in[l][:, N_MAIN:], ((0, 0), (0, LANES - GLA_RANK))).astype(BF16),
            "w_g2": jnp.pad(w_gate2[l], ((0, LANES - GLA_RANK), (0, 0))).astype(BF16),
            "b_gate": b_gate[l].reshape(1, -1),
            "conv_w": conv_w[l],
            "gla_norm": gla_norm[l].reshape(1, -1),
            "w_out": w_out[l].astype(BF16),
            "norm_x": norm_x[l].reshape(1, d),
            "w_xq": w_xq[l].astype(BF16),
            "w_xo": w_xo[l].astype(BF16),
            "norm_ffn": norm_ffn[l].reshape(1, d),
            "w_fg": w_ffn_gate[l].astype(BF16),
            "w_fu": w_ffn_up[l].astype(BF16),
            "ffn_conv_w": ffn_conv_w[l],
            "ffn_conv_b": ffn_conv_b[l].reshape(1, -1),
        }
        w_fd = w_ffn_down[l].astype(BF16)
        last = l == depth - 1
        gain_next = nfinal if last else None

        mem = mem_prompt.reshape(nb * MEM_LEN, d)
        nmem = norm_mem[l].reshape(1, d)
        mk = _norm_matmul(mem, nmem, w_xk[l].astype(BF16), F32, nb * MEM_LEN, 1024)
        mv = _norm_matmul(mem, nmem, w_xv[l].astype(BF16), F32, nb * MEM_LEN, 1024)
        act, h2, c1, s1, f1 = _layer(
            hp, w, nseq=nb, rows=seq, shift=1,
            prev_conv=jnp.zeros((nb, CONV_WIDTH - 1, D_CONV), F32),
            s0=jnp.zeros((nb, GLA_HEADS, GLA_DK, GLA_DV), F32),
            prev_ffn=jnp.zeros((nb, CONV_WIDTH - 1, D_FF), F32),
            mk=mk.reshape(nb, MEM_LEN, d), mv=mv.reshape(nb, MEM_LEN, d), time_major=False)
        assert last, "only the final layer's epilogue (final rmsnorm) is implemented"
        yp = _proj_res_norm([act], w_fd, h2, gain_next, tm=512, tk=D_FF // 2, final=True)
        outs["conv_p"].append(c1)
        outs["gla_p"].append(s1)
        outs["ffn_p"].append(f1)
        outs["mk"].append(mk.reshape(nb, MEM_LEN, X_HEADS, X_HD))
        outs["mv"].append(mv.reshape(nb, MEM_LEN, X_HEADS, X_HD))

        def tmajor(c):
            return c.transpose(1, 0, 2).reshape(1, (CONV_WIDTH - 1) * db, c.shape[-1])

        act, h2, c2, s2, f2 = _layer(
            hs, w, nseq=1, rows=dseq * db, shift=db,
            prev_conv=tmajor(cache_conv[l]), s0=state_gla[l], prev_ffn=tmajor(cache_ffn[l]),
            mk=cache_mem_k[l], mv=cache_mem_v[l], time_major=True)
        ys = _proj_res_norm([act], w_fd, h2, gain_next, tm=512, tk=D_FF // 2, final=True)
        outs["conv_s"].append(c2.reshape(CONV_WIDTH - 1, db, D_CONV).transpose(1, 0, 2))
        outs["gla_s"].append(s2)
        outs["ffn_s"].append(f2.reshape(CONV_WIDTH - 1, db, D_FF).transpose(1, 0, 2))

    y_prompt = yp.reshape(nb, seq, d)
    y_sample = ys.reshape(dseq, db, d).transpose(1, 0, 2)
    st = lambda k: jnp.stack(outs[k])
    return (y_prompt, y_sample, st("conv_p"), st("gla_p"), st("ffn_p"), st("mk"), st("mv"),
            st("conv_s"), st("gla_s"), st("ffn_s"))
```

```python
import functools

import jax
import jax.numpy as jnp
from jax import lax
from jax.experimental import pallas as pl
from jax.experimental.pallas import tpu as pltpu

F32 = jnp.float32
BF16 = jnp.bfloat16

D_MODEL = 2048
EPS = 1e-6
CONV_WIDTH = 3
D_CONV = 1024
D_GLA = 1024
GLA_HEADS = 4
GLA_DV = 256
GLA_DK = 128
GLA_RANK = 16
GLA_TAU = 16.0
GLA_CHUNK = 64
X_HEADS = 4
X_HD = 512
MEM_LEN = 256
D_FF = 5632
N_MAIN = 3 * D_CONV + 2 * GLA_HEADS * GLA_DK + 2 * GLA_HEADS * GLA_DV

LANES = 128
SUBLANES = 8
VMEM_LIMIT_BYTES = 56 * 1024 * 1024


def _params(*sem):
    return pltpu.CompilerParams(dimension_semantics=sem, vmem_limit_bytes=VMEM_LIMIT_BYTES)


def _dot(a, b):
    return jnp.dot(a, b, preferred_element_type=F32)


def _dot_nt(a, b):
    return lax.dot_general(a, b, (((1,), (1,)), ((), ())), preferred_element_type=F32)


def _dot_tn(a, b):
    return lax.dot_general(a, b, (((0,), (0,)), ((), ())), preferred_element_type=F32)


def _rms_rows(x, g):
    ms = jnp.mean(x * x, axis=-1, keepdims=True)
    return (x * lax.rsqrt(ms + EPS)) * g


def _row_chunk(rows, limit=256):
    for c in (256, 128, 64, 32, 16, 8):
        if c <= limit and rows % c == 0:
            return c
    return rows


NORM_ROWS = 128


def _norm_into(x_ref, g_ref, xn_ref):
    rows = x_ref.shape[0]
    ch = _row_chunk(rows)
    g = g_ref[...]

    def body(c, carry):
        r = pl.ds(pl.multiple_of(c * ch, ch), ch)
        xn_ref[r, :] = _rms_rows(x_ref[r, :], g).astype(xn_ref.dtype)
        return carry

    lax.fori_loop(0, rows // ch, body, 0)


def _norm_matmul_kernel(x_ref, g_ref, w_ref, o_ref, xn_ref):
    @pl.when(pl.program_id(1) == 0)
    def _():
        _norm_into(x_ref, g_ref, xn_ref)

    o_ref[...] = _dot(xn_ref[...], w_ref[...].astype(BF16)).astype(o_ref.dtype)


def _in_proj_kernel(x_ref, g_ref, w_ref, wg1_ref, wg2_ref, bg_ref, o_ref, lf_ref, xn_ref):
    @pl.when(pl.program_id(1) == 0)
    def _():
        _norm_into(x_ref, g_ref, xn_ref)
        g1 = _dot(xn_ref[...], wg1_ref[...])
        z = _dot(g1.astype(BF16), wg2_ref[...]) + bg_ref[...]
        lf_ref[...] = (jnp.minimum(z, 0.0) - jnp.log1p(jnp.exp(-jnp.abs(z)))) * (1.0 / GLA_TAU)

    o_ref[...] = _dot(xn_ref[...], w_ref[...]).astype(o_ref.dtype)


def _norm_matmul(x, gain, w, out_dtype, tm, tn):
    m, d = x.shape
    n = w.shape[1]
    return pl.pallas_call(
        _norm_matmul_kernel,
        grid=(m // tm, n // tn),
        in_specs=[pl.BlockSpec((tm, d), lambda i, j: (i, 0)),
                  pl.BlockSpec((1, d), lambda i, j: (0, 0)),
                  pl.BlockSpec((d, tn), lambda i, j: (0, j))],
        out_specs=pl.BlockSpec((tm, tn), lambda i, j: (i, j)),
        out_shape=jax.ShapeDtypeStruct((m, n), out_dtype),
        scratch_shapes=[pltpu.VMEM((tm, d), BF16)],
        compiler_params=_params("arbitrary", "arbitrary"),
        name="norm_matmul",
    )(x, gain, w)


def _in_proj(x, gain, w_in, wg1, wg2, b_gate, tm, tn):
    m, d = x.shape
    n = N_MAIN
    ng = wg2.shape[1]
    return pl.pallas_call(
        _in_proj_kernel,
        grid=(m // tm, n // tn),
        in_specs=[pl.BlockSpec((tm, d), lambda i, j: (i, 0)),
                  pl.BlockSpec((1, d), lambda i, j: (0, 0)),
                  pl.BlockSpec((d, tn), lambda i, j: (0, j)),
                  pl.BlockSpec((d, LANES), lambda i, j: (0, 0)),
                  pl.BlockSpec((LANES, ng), lambda i, j: (0, 0)),
                  pl.BlockSpec((1, ng), lambda i, j: (0, 0))],
        out_specs=[pl.BlockSpec((tm, tn), lambda i, j: (i, j)),
                   pl.BlockSpec((tm, ng), lambda i, j: (i, 0))],
        out_shape=[jax.ShapeDtypeStruct((m, n), F32),
                   jax.ShapeDtypeStruct((m, ng), F32)],
        scratch_shapes=[pltpu.VMEM((tm, d), BF16)],
        compiler_params=_params("arbitrary", "arbitrary"),
        name="in_proj",
    )(x, gain, w_in, wg1, wg2, b_gate)


def _cast_into(src_ref, dst_ref):
    rows = src_ref.shape[0]
    ch = _row_chunk(rows)

    def body(c, carry):
        r = pl.ds(pl.multiple_of(c * ch, ch), ch)
        dst_ref[r, :] = src_ref[r, :].astype(dst_ref.dtype)
        return carry

    lax.fori_loop(0, rows // ch, body, 0)


def _matmul_kernel(a_ref, w_ref, o_ref, wb_ref):
    @pl.when(pl.program_id(0) == 0)
    def _():
        _cast_into(w_ref, wb_ref)

    o_ref[...] = _dot(a_ref[...], wb_ref[...]).astype(o_ref.dtype)


def _matmul(a, w, out_dtype, tm):
    m, k = a.shape
    n = w.shape[1]
    return pl.pallas_call(
        _matmul_kernel,
        grid=(m // tm,),
        in_specs=[pl.BlockSpec((tm, k), lambda i: (i, 0)),
                  pl.BlockSpec((k, n), lambda i: (0, 0), pipeline_mode=pl.Buffered(1))],
        out_specs=pl.BlockSpec((tm, n), lambda i: (i, 0)),
        out_shape=jax.ShapeDtypeStruct((m, n), out_dtype),
        scratch_shapes=[pltpu.VMEM((k, n), BF16)],
        compiler_params=_params("arbitrary"),
        name="matmul",
    )(a, w)


def _halo_rows(shift):
    return max(2 * shift, SUBLANES)


def _conv_stage(buf_ref, prev, shift):
    halo = _halo_rows(shift)
    buf_ref[halo - 2 * shift:halo, :] = prev


def _conv_taps(buf_ref, r0, rows, shift):
    halo = _halo_rows(shift)

    def aligned(start):
        return start if isinstance(start, int) else pl.multiple_of(start, SUBLANES)

    if shift % SUBLANES == 0:
        s1 = buf_ref[pl.ds(aligned(r0 + (halo - shift)), rows), :]
        s2 = buf_ref[pl.ds(aligned(r0 + (halo - 2 * shift)), rows), :]
        return s2, s1
    win = buf_ref[pl.ds(aligned(r0 + (halo - SUBLANES)), rows + SUBLANES), :]
    s1 = pltpu.roll(win, shift, 0)[SUBLANES:, :]
    s2 = pltpu.roll(win, 2 * shift, 0)[SUBLANES:, :]
    return s2, s1


def _conv_gate_kernel(bg_ref, cg_ref, vc_ref, prev_ref, w_ref, o_ref, new_ref, buf_ref, *, shift):
    rows = cg_ref.shape[0]
    halo = _halo_rows(shift)
    u = cg_ref[...] * vc_ref[...]
    _conv_stage(buf_ref, prev_ref[0], shift)
    buf_ref[halo:halo + rows, :] = u
    s2, s1 = _conv_taps(buf_ref, 0, rows, shift)
    y = w_ref[0:1, :] * s2 + w_ref[1:2, :] * s1 + w_ref[2:3, :] * u
    o_ref[...] = (bg_ref[...] * y).astype(o_ref.dtype)
    new_ref[0] = buf_ref[halo + rows - 2 * shift:halo + rows, :]


def _conv_gate(p, prev, conv_w, nseq, rows, shift, tn):
    nj = D_CONV // tn
    kern = functools.partial(_conv_gate_kernel, shift=shift)
    return pl.pallas_call(
        kern,
        grid=(nseq, nj),
        in_specs=[pl.BlockSpec((rows, tn), lambda b, j: (b, j)),
                  pl.BlockSpec((rows, tn), lambda b, j: (b, j + nj)),
                  pl.BlockSpec((rows, tn), lambda b, j: (b, j + 2 * nj)),
                  pl.BlockSpec((1, 2 * shift, tn), lambda b, j: (b, 0, j)),
                  pl.BlockSpec((CONV_WIDTH, tn), lambda b, j: (0, j))],
        out_specs=[pl.BlockSpec((rows, tn), lambda b, j: (b, j)),
                   pl.BlockSpec((1, 2 * shift, tn), lambda b, j: (b, 0, j))],
        out_shape=[jax.ShapeDtypeStruct((nseq * rows, D_CONV), BF16),
                   jax.ShapeDtypeStruct((nseq, 2 * shift, D_CONV), F32)],
        scratch_shapes=[pltpu.VMEM((_halo_rows(shift) + rows, tn), F32)],
        compiler_params=_params("arbitrary", "arbitrary"),
        name="conv_gate",
    )(p, p, p, prev, conv_w)


def _cumsum_rows(g):
    c = g.shape[0]
    row = lax.broadcasted_iota(jnp.int32, g.shape, 0)
    x = g
    s = 1
    while s < c:
        x = x + jnp.where(row >= s, pltpu.roll(x, s, 0), 0.0)
        s *= 2
    return x


def _bcast_block_row(x, s, k):
    c, lanes = x.shape
    if s == c:
        return jnp.broadcast_to(x[k:k + 1, :], x.shape)
    if s >= SUBLANES:
        y = x.reshape(c // s, s, lanes)
        return jnp.broadcast_to(y[:, k:k + 1, :], y.shape).reshape(c, lanes)
    y = x.reshape(c // SUBLANES, SUBLANES, lanes)
    sub = lax.broadcasted_iota(jnp.int32, y.shape, 1)
    out = None
    for blk in range(SUBLANES // s):
        src = jnp.broadcast_to(y[:, blk * s + k:blk * s + k + 1, :], y.shape)
        out = src if out is None else jnp.where(sub >= blk * s, src, out)
    return out.reshape(c, lanes)


def _gla_pair_masks(c):
    ri = lax.broadcasted_iota(jnp.int32, (c, c), 0)
    ci = lax.broadcasted_iota(jnp.int32, (c, c), 1)
    diff_bits = ri ^ ci
    masks = [diff_bits == 0]
    level = 0
    while (1 << level) < c:
        masks.append(((diff_bits >> level) == 1) & (((ri >> level) & 1) == 1))
        level += 1
    return masks


def _gla_chunk(q, k, v, g, s_prev, masks):
    c = q.shape[0]
    cum = _cumsum_rows(g)
    a = jnp.where(masks[0], _dot_nt(q.astype(BF16), k.astype(BF16)), 0.0)
    for level in range(len(masks) - 1):
        half = 1 << level
        ref = _bcast_block_row(cum, 2 * half, half - 1)
        qe = q * jnp.exp(jnp.minimum(cum - ref, 0.0))
        ke = k * jnp.exp(jnp.minimum(ref - cum, 0.0))
        a = a + jnp.where(masks[1 + level], _dot_nt(qe.astype(BF16), ke.astype(BF16)), 0.0)
    o = _dot(a.astype(BF16), v.astype(BF16)) + _dot((q * jnp.exp(cum)).astype(BF16), s_prev.astype(BF16))
    last = cum[c - 1:c, :]
    kd = k * jnp.exp(last - cum)
    dk = last.shape[1]
    decay_t = jnp.transpose(jnp.broadcast_to(jnp.exp(last), (dk, dk)))
    decayed = jnp.concatenate([decay_t * s_prev[:, i:i + dk] for i in range(0, s_prev.shape[1], dk)], axis=1)
    s_new = decayed + _dot_tn(kd.astype(BF16), v.astype(BF16))
    return o, s_new


def _gla_kernel(q_ref, k_ref, v_ref, r_ref, g_ref, s0_ref, gn_ref, o_ref, sn_ref, *, chunk, single_chunk):
    bb, rows = q_ref.shape[0], q_ref.shape[1]
    nchunk = rows // chunk
    state_in = s0_ref if single_chunk else sn_ref
    seq_per_iter = 2 if single_chunk and bb % 2 == 0 else 1

    if not single_chunk:
        @pl.when(pl.program_id(1) == 0)
        def _():
            sn_ref[...] = s0_ref[...]

    masks = _gla_pair_masks(chunk)

    def one(b, r):
        for h in range(GLA_HEADS):
            kc = slice(h * GLA_DK, (h + 1) * GLA_DK)
            vc = slice(h * GLA_DV, (h + 1) * GLA_DV)
            q = q_ref[b, r, kc] * (GLA_DK ** -0.5)
            o, s_new = _gla_chunk(q, k_ref[b, r, kc], v_ref[b, r, vc], g_ref[b, r, kc], state_in[b, h], masks)
            sn_ref[b, h] = s_new
            rr = r_ref[b, r, vc]
            o_ref[b, r, vc] = (_rms_rows(o, gn_ref[:, vc]) * (rr * jax.nn.sigmoid(rr))).astype(o_ref.dtype)

    def body(n, carry):
        for u in range(seq_per_iter):
            i = n * seq_per_iter + u
            one(i // nchunk, pl.ds(pl.multiple_of((i % nchunk) * chunk, chunk), chunk))
        return carry

    lax.fori_loop(0, bb * nchunk // seq_per_iter, body, 0)


def _gla(p, logf, s0, gla_norm, *, q_blk, k_blk, v_blk, r_blk, bb, rows, chunk):
    nb, t, _ = p.shape
    nk, nv = GLA_HEADS * GLA_DK, GLA_HEADS * GLA_DV
    kern = functools.partial(_gla_kernel, chunk=chunk, single_chunk=(t == chunk))
    state_spec = pl.BlockSpec((bb, GLA_HEADS, GLA_DK, GLA_DV), lambda b, c: (b, 0, 0, 0))
    return pl.pallas_call(
        kern,
        grid=(nb // bb, t // rows),
        in_specs=[pl.BlockSpec((bb, rows, nk), lambda b, c: (b, c, q_blk)),
                  pl.BlockSpec((bb, rows, nk), lambda b, c: (b, c, k_blk)),
                  pl.BlockSpec((bb, rows, nv), lambda b, c: (b, c, v_blk)),
                  pl.BlockSpec((bb, rows, nv), lambda b, c: (b, c, r_blk)),
                  pl.BlockSpec((bb, rows, nk), lambda b, c: (b, c, 0)),
                  state_spec,
                  pl.BlockSpec((1, nv), lambda b, c: (0, 0))],
        out_specs=[pl.BlockSpec((bb, rows, nv), lambda b, c: (b, c, 0)),
                   state_spec],
        out_shape=[jax.ShapeDtypeStruct((nb, t, D_GLA), BF16),
                   jax.ShapeDtypeStruct((nb, GLA_HEADS, GLA_DK, GLA_DV), F32)],
        compiler_params=_params("arbitrary", "arbitrary"),
        name="gla",
    )(p, p, p, p, logf, s0, gla_norm)


def _proj_res_norm_kernel(*refs, n_a, nk, final):
    a_refs, (w_ref, res_ref, g_ref) = refs[:n_a], refs[n_a:n_a + 3]
    out_refs, acc_ref = refs[n_a + 3:-1], refs[-1]
    k = pl.program_id(1)
    part, r0 = None, 0
    for a_ref in a_refs:
        kw = a_ref.shape[1]
        term = _dot(a_ref[...], w_ref[r0:r0 + kw, :])
        part = term if part is None else part + term
        r0 += kw

    @pl.when(k == 0)
    def _():
        acc_ref[...] = part

    if nk > 1:
        @pl.when(k > 0)
        def _():
            acc_ref[...] += part

    @pl.when(k == nk - 1)
    def _():
        rows = acc_ref.shape[0]
        ch = _row_chunk(rows, NORM_ROWS)
        g = g_ref[...]

        def body(c, carry):
            r = pl.ds(pl.multiple_of(c * ch, ch), ch)
            h = res_ref[r, :] + acc_ref[r, :]
            hn = _rms_rows(h, g)
            if final:
                out_refs[0][r, :] = hn
            else:
                out_refs[0][r, :] = h
                out_refs[1][r, :] = hn.astype(out_refs[1].dtype)
            return carry

        lax.fori_loop(0, rows // ch, body, 0)


def _proj_res_norm(a_list, w, res, gain, *, tm, tk, final):
    m = a_list[0].shape[0]
    kdim, d = w.shape
    nk = kdim // tk
    assert len(a_list) == 1 or nk == 1
    kern = functools.partial(_proj_res_norm_kernel, n_a=len(a_list), nk=nk, final=final)
    a_specs = ([pl.BlockSpec((tm, tk), lambda i, k: (i, k))] if len(a_list) == 1 else
               [pl.BlockSpec((tm, a.shape[1]), lambda i, k: (i, 0)) for a in a_list])
    row_spec = pl.BlockSpec((tm, d), lambda i, k: (i, 0))
    if final:
        out_specs = row_spec
        out_shape = jax.ShapeDtypeStruct((m, d), F32)
    else:
        out_specs = [row_spec, row_spec]
        out_shape = [jax.ShapeDtypeStruct((m, d), F32), jax.ShapeDtypeStruct((m, d), BF16)]
    return pl.pallas_call(
        kern,
        grid=(m // tm, nk),
        in_specs=a_specs + [pl.BlockSpec((tk, d), lambda i, k: (k, 0)),
                            row_spec,
                            pl.BlockSpec((1, d), lambda i, k: (0, 0))],
        out_specs=out_specs,
        out_shape=out_shape,
        scratch_shapes=[pltpu.VMEM((tm, d), F32)],
        compiler_params=_params("arbitrary", "arbitrary"),
        name="proj_res_norm",
    )(*a_list, w, res, gain)


def _xattn_kernel(q_ref, k_ref, v_ref, o_ref):
    bb = q_ref.shape[0]

    def body(b, carry):
        q = q_ref[b].astype(BF16)
        p = _softmax_rows(_dot_nt(q, k_ref[b].astype(BF16)) * (X_HD ** -0.5))
        o_ref[b] = _dot(p.astype(BF16), v_ref[b].astype(BF16)).astype(o_ref.dtype)
        return carry

    lax.fori_loop(0, bb, body, 0)


def _xattn(q, mk, mv, *, bb, tq):
    nb, t, d = q.shape
    return pl.pallas_call(
        _xattn_kernel,
        grid=(nb // bb, X_HEADS, t // tq),
        in_specs=[pl.BlockSpec((bb, tq, X_HD), lambda b, h, i: (b, i, h)),
                  pl.BlockSpec((bb, MEM_LEN, X_HD), lambda b, h, i: (b, 0, h)),
                  pl.BlockSpec((bb, MEM_LEN, X_HD), lambda b, h, i: (b, 0, h))],
        out_specs=pl.BlockSpec((bb, tq, X_HD), lambda b, h, i: (b, i, h)),
        out_shape=jax.ShapeDtypeStruct((nb, t, d), BF16),
        compiler_params=_params("arbitrary", "arbitrary", "arbitrary"),
        name="xattn",
    )(q, mk, mv)


def _softmax_rows(s):
    s = s - jnp.max(s, axis=-1, keepdims=True)
    e = jnp.exp(s)
    return e / jnp.sum(e, axis=-1, keepdims=True)


def _xattn_cache_kernel(q_ref, k_hbm, v_hbm, o_ref, kbuf, vbuf, sem, *, nsteps):
    bb = q_ref.shape[0]
    nchunk = X_HD // LANES
    pitch = nchunk * X_HEADS
    i = pl.program_id(0)

    def copies(step, slot):
        out = []
        for t, (src, dst) in enumerate(((k_hbm, kbuf), (v_hbm, vbuf))):
            for b in range(bb):
                for r in range(pitch):
                    out.append(pltpu.make_async_copy(src.at[step * bb + b, :, r, :], dst.at[slot, b, r],
                                                     sem.at[slot, t]))
        return out

    def start_all(cps):
        for n, cp in enumerate(cps):
            cp.start(priority=n % 2)

    @pl.when(i == 0)
    def _():
        start_all(copies(0, 0))

    @pl.when(i + 1 < nsteps)
    def _():
        start_all(copies(i + 1, (i + 1) % 2))

    slot = i % 2
    for cp in copies(i, slot):
        cp.wait()

    def head(buf, b, h):
        parts = [buf[slot, b, c * X_HEADS + h] for c in range(nchunk)]
        return jnp.concatenate(parts, axis=1).astype(BF16)

    for b in range(bb):
        for h in range(X_HEADS):
            q = q_ref[b, :, h * X_HD:(h + 1) * X_HD].astype(BF16)
            p = _softmax_rows(_dot_nt(q, head(kbuf, b, h)) * (X_HD ** -0.5))
            o_ref[b, :, h * X_HD:(h + 1) * X_HD] = _dot(p.astype(BF16), head(vbuf, b, h)).astype(o_ref.dtype)


def _xattn_cache(q, ck, cv, *, bb):
    nb, tq, d = q.shape
    nchunk = X_HD // LANES
    pitch = nchunk * X_HEADS

    def stored_order(c):
        c = c.reshape(nb, MEM_LEN, X_HEADS, nchunk, LANES).transpose(0, 1, 3, 2, 4)
        return c.reshape(nb, MEM_LEN, pitch, LANES)

    nsteps = nb // bb
    buf = pltpu.VMEM((2, bb, pitch, MEM_LEN, LANES), F32)
    kern = functools.partial(_xattn_cache_kernel, nsteps=nsteps)
    return pl.pallas_call(
        kern,
        grid=(nsteps,),
        in_specs=[pl.BlockSpec((bb, tq, d), lambda b: (b, 0, 0)),
                  pl.BlockSpec(memory_space=pl.ANY),
                  pl.BlockSpec(memory_space=pl.ANY)],
        out_specs=pl.BlockSpec((bb, tq, d), lambda b: (b, 0, 0)),
        out_shape=jax.ShapeDtypeStruct((nb, tq, d), BF16),
        scratch_shapes=[buf, buf, pltpu.SemaphoreType.DMA((2, 2))],
        compiler_params=_params("arbitrary"),
        name="xattn_cache",
    )(q, stored_order(ck), stored_order(cv))


def _ffn_up_kernel(hn_ref, wg_ref, wu_ref, cw_ref, cb_ref, prev_ref, o_ref, new_ref, buf_ref, up_ref,
                   wgb_ref, wub_ref, *, shift):
    rows = hn_ref.shape[0]
    halo = _halo_rows(shift)
    ch = _row_chunk(rows)

    @pl.when(pl.program_id(1) == 0)
    def _():
        _cast_into(wg_ref, wgb_ref)
        _cast_into(wu_ref, wub_ref)

    _conv_stage(buf_ref, prev_ref[0], shift)
    buf_ref[halo:halo + rows, :] = _dot(hn_ref[...], wgb_ref[...])
    up_ref[...] = _dot(hn_ref[...], wub_ref[...])
    w0, w1, w2, cb = cw_ref[0:1, :], cw_ref[1:2, :], cw_ref[2:3, :], cb_ref[...]

    def act_body(c, carry):
        r0 = pl.multiple_of(c * ch, ch)
        s2, s1 = _conv_taps(buf_ref, r0, ch, shift)
        gc = w0 * s2 + w1 * s1 + w2 * buf_ref[pl.ds(pl.multiple_of(halo + r0, SUBLANES), ch), :] + cb
        o_ref[pl.ds(r0, ch), :] = ((gc * jax.nn.sigmoid(gc)) * up_ref[pl.ds(r0, ch), :]).astype(o_ref.dtype)
        return carry

    lax.fori_loop(0, rows // ch, act_body, 0)
    new_ref[0] = buf_ref[halo + rows - 2 * shift:halo + rows, :]


def _ffn_up(hn, wg, wu, cw, cb, prev, nseq, rows, shift, tn):
    d = hn.shape[1]
    kern = functools.partial(_ffn_up_kernel, shift=shift)
    return pl.pallas_call(
        kern,
        grid=(D_FF // tn, nseq),
        in_specs=[pl.BlockSpec((rows, d), lambda j, b: (b, 0)),
                  pl.BlockSpec((d, tn), lambda j, b: (0, j)),
                  pl.BlockSpec((d, tn), lambda j, b: (0, j)),
                  pl.BlockSpec((CONV_WIDTH, tn), lambda j, b: (0, j)),
                  pl.BlockSpec((1, tn), lambda j, b: (0, j)),
                  pl.BlockSpec((1, 2 * shift, tn), lambda j, b: (b, 0, j))],
        out_specs=[pl.BlockSpec((rows, tn), lambda j, b: (b, j)),
                   pl.BlockSpec((1, 2 * shift, tn), lambda j, b: (b, 0, j))],
        out_shape=[jax.ShapeDtypeStruct((nseq * rows, D_FF), BF16),
                   jax.ShapeDtypeStruct((nseq, 2 * shift, D_FF), F32)],
        scratch_shapes=[pltpu.VMEM((_halo_rows(shift) + rows, tn), F32), pltpu.VMEM((rows, tn), F32),
                        pltpu.VMEM((d, tn), BF16), pltpu.VMEM((d, tn), BF16)],
        compiler_params=_params("arbitrary", "arbitrary"),
        name="ffn_up",
    )(hn, wg, wu, cw, cb, prev)


def _layer(x, w, *, nseq, rows, shift, prev_conv, s0, prev_ffn, mk, mv, time_major):
    m = x.shape[0]
    tm = min(m, 1024)
    p, logf = _in_proj(x, w["norm_mix"], w["w_in"], w["w_g1"], w["w_g2"], w["b_gate"], tm, 1024)
    conv_out, conv_new = _conv_gate(p, prev_conv, w["conv_w"], nseq, rows, shift, 256)

    if time_major:
        nt = m // shift
        pad = ((0, 0), (0, SUBLANES - nt), (0, 0))
        pg = jnp.pad(p.reshape(nt, shift, N_MAIN)[:, :, 3 * D_CONV:].transpose(1, 0, 2), pad)
        lg = jnp.pad(logf.reshape(nt, shift, -1).transpose(1, 0, 2), pad)
        o, s_new = _gla(pg, lg, s0, w["gla_norm"], q_blk=0, k_blk=1, v_blk=1, r_blk=2,
                        bb=8, rows=SUBLANES, chunk=SUBLANES)
        gla_out = o[:, :nt].transpose(1, 0, 2).reshape(m, D_GLA)
    else:
        q0 = 3 * D_CONV // (GLA_HEADS * GLA_DK)
        v0 = (3 * D_CONV + 2 * GLA_HEADS * GLA_DK) // (GLA_HEADS * GLA_DV)
        o, s_new = _gla(p.reshape(nseq, rows, N_MAIN), logf.reshape(nseq, rows, -1), s0, w["gla_norm"],
                        q_blk=q0, k_blk=q0 + 1, v_blk=v0, r_blk=v0 + 1,
                        bb=1, rows=512, chunk=GLA_CHUNK)
        gla_out = o.reshape(m, D_GLA)

    tm2 = min(m, 512)
    h, hn = _proj_res_norm([conv_out, gla_out], w["w_out"], x, w["norm_x"], tm=tm2, tk=D_MODEL, final=False)
    qx = _matmul(hn, w["w_xq"], BF16, tm)

    if time_major:
        nt = m // shift
        qb = jnp.pad(qx.astype(F32).reshape(nt, shift, D_MODEL).transpose(1, 0, 2),
                     ((0, 0), (0, SUBLANES - nt), (0, 0)))
        ob = _xattn_cache(qb, mk, mv, bb=2)
        attn = ob[:, :nt].transpose(1, 0, 2).reshape(m, D_MODEL)
    else:
        attn = _xattn(qx.reshape(nseq, rows, D_MODEL), mk, mv, bb=1, tq=rows).reshape(m, D_MODEL)

    h2, hn2 = _proj_res_norm([attn], w["w_xo"], h, w["norm_ffn"], tm=tm2, tk=D_MODEL, final=False)
    act, ffn_new = _ffn_up(hn2, w["w_fg"], w["w_fu"], w["ffn_conv_w"], w["ffn_conv_b"], prev_ffn,
                           nseq, rows, shift, 512)
    return act, h2, conv_new, s_new, ffn_new


def kernel(x_prompt, x_sample, mem_prompt, cache_conv, state_gla, cache_ffn, cache_mem_k, cache_mem_v,
           norm_mix, w_in, conv_w, w_gate2, b_gate, gla_norm, w_out, norm_x, norm_mem, w_xq, w_xk, w_xv,
           w_xo, norm_ffn, w_ffn_gate, w_ffn_up, ffn_conv_w, ffn_conv_b, w_ffn_down, norm_final):
    depth = w_in.shape[0]
    nb, seq, d = x_prompt.shape
    db, dseq, _ = x_sample.shape
    hp = x_prompt.reshape(nb * seq, d)
    hs = x_sample.transpose(1, 0, 2).reshape(dseq * db, d)
    outs = {k: [] for k in ("conv_p", "gla_p", "ffn_p", "mk", "mv", "conv_s", "gla_s", "ffn_s")}
    nfinal = norm_final.reshape(1, d)
    yp = ys = None
    for l in range(depth):
        w = {
            "norm_mix": norm_mix[l].reshape(1, d),
            "w_in": w_in[l].astype(BF16),
            "w_g1": jnp.pad(w_in[l][:, N_MAIN:], ((0, 0), (0, LANES - GLA_RANK))).astype(BF16),
            "w_g2": jnp.pad(w_gate2[l], ((0, LANES - GLA_RANK), (0, 0))).astype(BF16),
            "b_gate": b_gate[l].reshape(1, -1),
            "conv_w": conv_w[l],
            "gla_norm": gla_norm[l].reshape(1, -1),
            "w_out": w_out[l].astype(BF16),
            "norm_x": norm_x[l].reshape(1, d),
            "w_xq": w_xq[l],
            "w_xo": w_xo[l].astype(BF16),
            "norm_ffn": norm_ffn[l].reshape(1, d),
            "w_fg": w_ffn_gate[l],
            "w_fu": w_ffn_up[l],
            "ffn_conv_w": ffn_conv_w[l],
            "ffn_conv_b": ffn_conv_b[l].reshape(1, -1),
        }
        w_fd = w_ffn_down[l].astype(BF16)
        last = l == depth - 1
        gain_next = nfinal if last else None

        mem = mem_prompt.reshape(nb * MEM_LEN, d)
        nmem = norm_mem[l].reshape(1, d)
        mk = _norm_matmul(mem, nmem, w_xk[l], F32, nb * MEM_LEN, 1024)
        mv = _norm_matmul(mem, nmem, w_xv[l], F32, nb * MEM_LEN, 1024)
        act, h2, c1, s1, f1 = _layer(
            hp, w, nseq=nb, rows=seq, shift=1,
            prev_conv=jnp.zeros((nb, CONV_WIDTH - 1, D_CONV), F32),
            s0=jnp.zeros((nb, GLA_HEADS, GLA_DK, GLA_DV), F32),
            prev_ffn=jnp.zeros((nb, CONV_WIDTH - 1, D_FF), F32),
            mk=mk.reshape(nb, MEM_LEN, d), mv=mv.reshape(nb, MEM_LEN, d), time_major=False)
        assert last, "only the final layer's epilogue (final rmsnorm) is implemented"
        yp = _proj_res_norm([act], w_fd, h2, gain_next, tm=512, tk=D_FF // 2, final=True)
        outs["conv_p"].append(c1)
        outs["gla_p"].append(s1)
        outs["ffn_p"].append(f1)
        outs["mk"].append(mk.reshape(nb, MEM_LEN, X_HEADS, X_HD))
        outs["mv"].append(mv.reshape(nb, MEM_LEN, X_HEADS, X_HD))

        def tmajor(c):
            return c.transpose(1, 0, 2).reshape(1, (CONV_WIDTH - 1) * db, c.shape[-1])

        act, h2, c2, s2, f2 = _layer(
            hs, w, nseq=1, rows=dseq * db, shift=db,
            prev_conv=tmajor(cache_conv[l]), s0=state_gla[l], prev_ffn=tmajor(cache_ffn[l]),
            mk=cache_mem_k[l], mv=cache_mem_v[l], time_major=True)
        ys = _proj_res_norm([act], w_fd, h2, gain_next, tm=512, tk=D_FF // 2, final=True)
        outs["conv_s"].append(c2.reshape(CONV_WIDTH - 1, db, D_CONV).transpose(1, 0, 2))
        outs["gla_s"].append(s2)
        outs["ffn_s"].append(f2.reshape(CONV_WIDTH - 1, db, D_FF).transpose(1, 0, 2))

    y_prompt = yp.reshape(nb, seq, d)
    y_sample = ys.reshape(dseq, db, d).transpose(1, 0, 2)
    st = lambda k: jnp.stack(outs[k])
    return (y_prompt, y_sample, st("conv_p"), st("gla_p"), st("ffn_p"), st("mk"), st("mv"),
            st("conv_s"), st("gla_s"), st("ffn_s"))
```

```python
import functools

import jax
import jax.numpy as jnp
from jax import lax
from jax.experimental import pallas as pl
from jax.experimental.pallas import tpu as pltpu

F32 = jnp.float32
BF16 = jnp.bfloat16

D_MODEL = 2048
EPS = 1e-6
CONV_WIDTH = 3
D_CONV = 1024
D_GLA = 1024
GLA_HEADS = 4
GLA_DV = 256
GLA_DK = 128
GLA_RANK = 16
GLA_TAU = 16.0
GLA_CHUNK = 64
X_HEADS = 4
X_HD = 512
MEM_LEN = 256
D_FF = 5632
N_MAIN = 3 * D_CONV + 2 * GLA_HEADS * GLA_DK + 2 * GLA_HEADS * GLA_DV

LANES = 128
SUBLANES = 8
VMEM_LIMIT_BYTES = 56 * 1024 * 1024


def _params(*sem):
    return pltpu.CompilerParams(dimension_semantics=sem, vmem_limit_bytes=VMEM_LIMIT_BYTES)


def _dot(a, b):
    return jnp.dot(a, b, preferred_element_type=F32)


def _dot_nt(a, b):
    return lax.dot_general(a, b, (((1,), (1,)), ((), ())), preferred_element_type=F32)


def _dot_tn(a, b):
    return lax.dot_general(a, b, (((0,), (0,)), ((), ())), preferred_element_type=F32)


def _rms_rows(x, g):
    ms = jnp.mean(x * x, axis=-1, keepdims=True)
    return (x * lax.rsqrt(ms + EPS)) * g


def _row_chunk(rows, limit=256):
    for c in (256, 128, 64, 32, 16, 8):
        if c <= limit and rows % c == 0:
            return c
    return rows


NORM_ROWS = 128


def _norm_into(x_ref, g_ref, xn_ref):
    rows = x_ref.shape[0]
    ch = _row_chunk(rows)
    g = g_ref[...]

    def body(c, carry):
        r = pl.ds(pl.multiple_of(c * ch, ch), ch)
        xn_ref[r, :] = _rms_rows(x_ref[r, :], g).astype(xn_ref.dtype)
        return carry

    lax.fori_loop(0, rows // ch, body, 0)


def _norm_matmul_kernel(x_ref, g_ref, w_ref, o_ref, oc_ref, xn_ref):
    j = pl.program_id(1)

    @pl.when(j == 0)
    def _():
        _norm_into(x_ref, g_ref, xn_ref)

    o_ref[...] = _dot(xn_ref[...], w_ref[...].astype(BF16)).astype(o_ref.dtype)
    nchunk = X_HD // LANES
    pitch = nchunk * X_HEADS
    heads_per_tile = o_ref.shape[1] // X_HD
    for b in range(oc_ref.shape[0]):
        for hh in range(heads_per_tile):
            for c in range(nchunk):
                col = hh * X_HD + c * LANES
                row = c * X_HEADS + j * heads_per_tile + hh
                oc_ref[b, pl.ds(row, MEM_LEN, stride=pitch), :] = (
                    o_ref[b * MEM_LEN:(b + 1) * MEM_LEN, col:col + LANES])


def _in_proj_kernel(x_ref, g_ref, w_ref, wg1_ref, wg2_ref, bg_ref, o_ref, lf_ref, xn_ref):
    @pl.when(pl.program_id(1) == 0)
    def _():
        _norm_into(x_ref, g_ref, xn_ref)
        g1 = _dot(xn_ref[...], wg1_ref[...])
        z = _dot(g1.astype(BF16), wg2_ref[...]) + bg_ref[...]
        lf_ref[...] = (jnp.minimum(z, 0.0) - jnp.log1p(jnp.exp(-jnp.abs(z)))) * (1.0 / GLA_TAU)

    o_ref[...] = _dot(xn_ref[...], w_ref[...]).astype(o_ref.dtype)


def _mem_proj(mem, gain, w, tn):
    m, d = mem.shape
    n = w.shape[1]
    nb = m // MEM_LEN
    nchunk = X_HD // LANES
    rows = MEM_LEN * nchunk * X_HEADS
    flat, stored = pl.pallas_call(
        _norm_matmul_kernel,
        grid=(1, n // tn),
        in_specs=[pl.BlockSpec((m, d), lambda i, j: (0, 0)),
                  pl.BlockSpec((1, d), lambda i, j: (0, 0)),
                  pl.BlockSpec((d, tn), lambda i, j: (0, j))],
        out_specs=[pl.BlockSpec((m, tn), lambda i, j: (0, j)),
                   pl.BlockSpec((nb, rows, LANES), lambda i, j: (0, 0, 0))],
        out_shape=[jax.ShapeDtypeStruct((m, n), F32),
                   jax.ShapeDtypeStruct((nb, rows, LANES), F32)],
        scratch_shapes=[pltpu.VMEM((m, d), BF16)],
        compiler_params=_params("arbitrary", "arbitrary"),
        name="norm_matmul",
    )(mem, gain, w)
    cache = stored.reshape(nb, MEM_LEN, nchunk, X_HEADS, LANES).transpose(0, 1, 3, 2, 4)
    return flat, cache.reshape(nb, MEM_LEN, X_HEADS, X_HD)


def _in_proj(x, gain, w_in, wg1, wg2, b_gate, tm, tn):
    m, d = x.shape
    n = N_MAIN
    ng = wg2.shape[1]
    return pl.pallas_call(
        _in_proj_kernel,
        grid=(m // tm, n // tn),
        in_specs=[pl.BlockSpec((tm, d), lambda i, j: (i, 0)),
                  pl.BlockSpec((1, d), lambda i, j: (0, 0)),
                  pl.BlockSpec((d, tn), lambda i, j: (0, j)),
                  pl.BlockSpec((d, LANES), lambda i, j: (0, 0)),
                  pl.BlockSpec((LANES, ng), lambda i, j: (0, 0)),
                  pl.BlockSpec((1, ng), lambda i, j: (0, 0))],
        out_specs=[pl.BlockSpec((tm, tn), lambda i, j: (i, j)),
                   pl.BlockSpec((tm, ng), lambda i, j: (i, 0))],
        out_shape=[jax.ShapeDtypeStruct((m, n), F32),
                   jax.ShapeDtypeStruct((m, ng), F32)],
        scratch_shapes=[pltpu.VMEM((tm, d), BF16)],
        compiler_params=_params("arbitrary", "arbitrary"),
        name="in_proj",
    )(x, gain, w_in, wg1, wg2, b_gate)


def _cast_into(src_ref, dst_ref):
    rows = src_ref.shape[0]
    ch = _row_chunk(rows)

    def body(c, carry):
        r = pl.ds(pl.multiple_of(c * ch, ch), ch)
        dst_ref[r, :] = src_ref[r, :].astype(dst_ref.dtype)
        return carry

    lax.fori_loop(0, rows // ch, body, 0)


def _matmul_kernel(a_ref, w_ref, o_ref, wb_ref):
    @pl.when(pl.program_id(0) == 0)
    def _():
        _cast_into(w_ref, wb_ref)

    o_ref[...] = _dot(a_ref[...], wb_ref[...]).astype(o_ref.dtype)


def _matmul(a, w, out_dtype, tm):
    m, k = a.shape
    n = w.shape[1]
    return pl.pallas_call(
        _matmul_kernel,
        grid=(m // tm,),
        in_specs=[pl.BlockSpec((tm, k), lambda i: (i, 0)),
                  pl.BlockSpec((k, n), lambda i: (0, 0), pipeline_mode=pl.Buffered(1))],
        out_specs=pl.BlockSpec((tm, n), lambda i: (i, 0)),
        out_shape=jax.ShapeDtypeStruct((m, n), out_dtype),
        scratch_shapes=[pltpu.VMEM((k, n), BF16)],
        compiler_params=_params("arbitrary"),
        name="matmul",
    )(a, w)


def _halo_rows(shift):
    return max(2 * shift, SUBLANES)


def _conv_stage(buf_ref, prev, shift):
    halo = _halo_rows(shift)
    buf_ref[halo - 2 * shift:halo, :] = prev


def _conv_taps(buf_ref, r0, rows, shift):
    halo = _halo_rows(shift)

    def aligned(start):
        return start if isinstance(start, int) else pl.multiple_of(start, SUBLANES)

    if shift % SUBLANES == 0:
        s1 = buf_ref[pl.ds(aligned(r0 + (halo - shift)), rows), :]
        s2 = buf_ref[pl.ds(aligned(r0 + (halo - 2 * shift)), rows), :]
        return s2, s1
    win = buf_ref[pl.ds(aligned(r0 + (halo - SUBLANES)), rows + SUBLANES), :]
    s1 = pltpu.roll(win, shift, 0)[SUBLANES:, :]
    s2 = pltpu.roll(win, 2 * shift, 0)[SUBLANES:, :]
    return s2, s1


def _conv_gate_kernel(bg_ref, cg_ref, vc_ref, prev_ref, w_ref, o_ref, new_ref, buf_ref, *, shift):
    rows = cg_ref.shape[0]
    halo = _halo_rows(shift)
    u = cg_ref[...] * vc_ref[...]
    _conv_stage(buf_ref, prev_ref[0], shift)
    buf_ref[halo:halo + rows, :] = u
    s2, s1 = _conv_taps(buf_ref, 0, rows, shift)
    y = w_ref[0:1, :] * s2 + w_ref[1:2, :] * s1 + w_ref[2:3, :] * u
    o_ref[...] = (bg_ref[...] * y).astype(o_ref.dtype)
    new_ref[0] = buf_ref[halo + rows - 2 * shift:halo + rows, :]


def _conv_gate(p, prev, conv_w, nseq, rows, shift, tn):
    nj = D_CONV // tn
    kern = functools.partial(_conv_gate_kernel, shift=shift)
    return pl.pallas_call(
        kern,
        grid=(nseq, nj),
        in_specs=[pl.BlockSpec((rows, tn), lambda b, j: (b, j)),
                  pl.BlockSpec((rows, tn), lambda b, j: (b, j + nj)),
                  pl.BlockSpec((rows, tn), lambda b, j: (b, j + 2 * nj)),
                  pl.BlockSpec((1, 2 * shift, tn), lambda b, j: (b, 0, j)),
                  pl.BlockSpec((CONV_WIDTH, tn), lambda b, j: (0, j))],
        out_specs=[pl.BlockSpec((rows, tn), lambda b, j: (b, j)),
                   pl.BlockSpec((1, 2 * shift, tn), lambda b, j: (b, 0, j))],
        out_shape=[jax.ShapeDtypeStruct((nseq * rows, D_CONV), BF16),
                   jax.ShapeDtypeStruct((nseq, 2 * shift, D_CONV), F32)],
        scratch_shapes=[pltpu.VMEM((_halo_rows(shift) + rows, tn), F32)],
        compiler_params=_params("arbitrary", "arbitrary"),
        name="conv_gate",
    )(p, p, p, prev, conv_w)


def _cumsum_rows(g):
    c = g.shape[0]
    row = lax.broadcasted_iota(jnp.int32, g.shape, 0)
    x = g
    s = 1
    while s < c:
        x = x + jnp.where(row >= s, pltpu.roll(x, s, 0), 0.0)
        s *= 2
    return x


def _bcast_block_row(x, s, k):
    c, lanes = x.shape
    if s == c:
        return jnp.broadcast_to(x[k:k + 1, :], x.shape)
    if s >= SUBLANES:
        y = x.reshape(c // s, s, lanes)
        return jnp.broadcast_to(y[:, k:k + 1, :], y.shape).reshape(c, lanes)
    y = x.reshape(c // SUBLANES, SUBLANES, lanes)
    sub = lax.broadcasted_iota(jnp.int32, y.shape, 1)
    out = None
    for blk in range(SUBLANES // s):
        src = jnp.broadcast_to(y[:, blk * s + k:blk * s + k + 1, :], y.shape)
        out = src if out is None else jnp.where(sub >= blk * s, src, out)
    return out.reshape(c, lanes)


def _gla_pair_masks(c):
    ri = lax.broadcasted_iota(jnp.int32, (c, c), 0)
    ci = lax.broadcasted_iota(jnp.int32, (c, c), 1)
    diff_bits = ri ^ ci
    masks = [diff_bits == 0]
    level = 0
    while (1 << level) < c:
        masks.append(((diff_bits >> level) == 1) & (((ri >> level) & 1) == 1))
        level += 1
    return masks


def _gla_chunk(q, k, v, g, s_prev, masks):
    c = q.shape[0]
    cum = _cumsum_rows(g)
    a = jnp.where(masks[0], _dot_nt(q.astype(BF16), k.astype(BF16)), 0.0)
    for level in range(len(masks) - 1):
        half = 1 << level
        ref = _bcast_block_row(cum, 2 * half, half - 1)
        qe = q * jnp.exp(jnp.minimum(cum - ref, 0.0))
        ke = k * jnp.exp(jnp.minimum(ref - cum, 0.0))
        a = a + jnp.where(masks[1 + level], _dot_nt(qe.astype(BF16), ke.astype(BF16)), 0.0)
    o = _dot(a.astype(BF16), v.astype(BF16)) + _dot((q * jnp.exp(cum)).astype(BF16), s_prev.astype(BF16))
    last = cum[c - 1:c, :]
    kd = k * jnp.exp(last - cum)
    dk = last.shape[1]
    decay_t = jnp.transpose(jnp.broadcast_to(jnp.exp(last), (dk, dk)))
    decayed = jnp.concatenate([decay_t * s_prev[:, i:i + dk] for i in range(0, s_prev.shape[1], dk)], axis=1)
    s_new = decayed + _dot_tn(kd.astype(BF16), v.astype(BF16))
    return o, s_new


def _gla_kernel(q_ref, k_ref, v_ref, r_ref, g_ref, s0_ref, gn_ref, o_ref, sn_ref, *, chunk, single_chunk, group):
    bb, rows = q_ref.shape[0], q_ref.shape[1]
    nchunk = rows // chunk
    state_in = s0_ref if single_chunk else sn_ref

    if not single_chunk:
        @pl.when(pl.program_id(1) == 0)
        def _():
            sn_ref[...] = s0_ref[...]

    masks = _gla_pair_masks(chunk)

    def one(b, r):
        for h in range(GLA_HEADS):
            kc = slice(h * GLA_DK, (h + 1) * GLA_DK)
            vc = slice(h * GLA_DV, (h + 1) * GLA_DV)
            q = q_ref[b, r, kc] * (GLA_DK ** -0.5)
            o, s_new = _gla_chunk(q, k_ref[b, r, kc], v_ref[b, r, vc], g_ref[b, r, kc], state_in[b, h], masks)
            sn_ref[b, h] = s_new
            rr = r_ref[b, r, vc]
            o_ref[b, r, vc] = (_rms_rows(o, gn_ref[:, vc]) * (rr * jax.nn.sigmoid(rr))).astype(o_ref.dtype)

    def body(n, carry):
        r = pl.ds(pl.multiple_of((n % nchunk) * chunk, chunk), chunk)
        for u in range(group):
            one((n // nchunk) * group + u, r)
        return carry

    lax.fori_loop(0, (bb // group) * nchunk, body, 0)


def _gla(p, logf, s0, gla_norm, *, q_blk, k_blk, v_blk, r_blk, bb, rows, chunk, group):
    nb, t, _ = p.shape
    nk, nv = GLA_HEADS * GLA_DK, GLA_HEADS * GLA_DV
    kern = functools.partial(_gla_kernel, chunk=chunk, single_chunk=(t == chunk), group=group)
    state_spec = pl.BlockSpec((bb, GLA_HEADS, GLA_DK, GLA_DV), lambda b, c: (b, 0, 0, 0))
    return pl.pallas_call(
        kern,
        grid=(nb // bb, t // rows),
        in_specs=[pl.BlockSpec((bb, rows, nk), lambda b, c: (b, c, q_blk)),
                  pl.BlockSpec((bb, rows, nk), lambda b, c: (b, c, k_blk)),
                  pl.BlockSpec((bb, rows, nv), lambda b, c: (b, c, v_blk)),
                  pl.BlockSpec((bb, rows, nv), lambda b, c: (b, c, r_blk)),
                  pl.BlockSpec((bb, rows, nk), lambda b, c: (b, c, 0)),
                  state_spec,
                  pl.BlockSpec((1, nv), lambda b, c: (0, 0))],
        out_specs=[pl.BlockSpec((bb, rows, nv), lambda b, c: (b, c, 0)),
                   state_spec],
        out_shape=[jax.ShapeDtypeStruct((nb, t, D_GLA), BF16),
                   jax.ShapeDtypeStruct((nb, GLA_HEADS, GLA_DK, GLA_DV), F32)],
        compiler_params=_params("arbitrary", "arbitrary"),
        name="gla",
    )(p, p, p, p, logf, s0, gla_norm)


def _proj_res_norm_kernel(*refs, n_a, nk, final):
    a_refs, (w_ref, res_ref, g_ref) = refs[:n_a], refs[n_a:n_a + 3]
    out_refs, acc_ref = refs[n_a + 3:-1], refs[-1]
    k = pl.program_id(1)
    part, r0 = None, 0
    for a_ref in a_refs:
        kw = a_ref.shape[1]
        term = _dot(a_ref[...], w_ref[r0:r0 + kw, :])
        part = term if part is None else part + term
        r0 += kw

    @pl.when(k == 0)
    def _():
        acc_ref[...] = part

    if nk > 1:
        @pl.when(k > 0)
        def _():
            acc_ref[...] += part

    @pl.when(k == nk - 1)
    def _():
        rows = acc_ref.shape[0]
        ch = _row_chunk(rows, NORM_ROWS)
        g = g_ref[...]

        def body(c, carry):
            r = pl.ds(pl.multiple_of(c * ch, ch), ch)
            h = res_ref[r, :] + acc_ref[r, :]
            hn = _rms_rows(h, g)
            if final:
                out_refs[0][r, :] = hn
            else:
                out_refs[0][r, :] = h
                out_refs[1][r, :] = hn.astype(out_refs[1].dtype)
            return carry

        lax.fori_loop(0, rows // ch, body, 0)


def _proj_res_norm(a_list, w, res, gain, *, tm, tk, final):
    m = a_list[0].shape[0]
    kdim, d = w.shape
    nk = kdim // tk
    assert len(a_list) == 1 or nk == 1
    kern = functools.partial(_proj_res_norm_kernel, n_a=len(a_list), nk=nk, final=final)
    a_specs = ([pl.BlockSpec((tm, tk), lambda i, k: (i, k))] if len(a_list) == 1 else
               [pl.BlockSpec((tm, a.shape[1]), lambda i, k: (i, 0)) for a in a_list])
    row_spec = pl.BlockSpec((tm, d), lambda i, k: (i, 0))
    if final:
        out_specs = row_spec
        out_shape = jax.ShapeDtypeStruct((m, d), F32)
    else:
        out_specs = [row_spec, row_spec]
        out_shape = [jax.ShapeDtypeStruct((m, d), F32), jax.ShapeDtypeStruct((m, d), BF16)]
    return pl.pallas_call(
        kern,
        grid=(m // tm, nk),
        in_specs=a_specs + [pl.BlockSpec((tk, d), lambda i, k: (k, 0)),
                            row_spec,
                            pl.BlockSpec((1, d), lambda i, k: (0, 0))],
        out_specs=out_specs,
        out_shape=out_shape,
        scratch_shapes=[pltpu.VMEM((tm, d), F32)],
        compiler_params=_params("arbitrary", "arbitrary"),
        name="proj_res_norm",
    )(*a_list, w, res, gain)


def _xattn_kernel(q_ref, k_ref, v_ref, o_ref):
    bb = q_ref.shape[0]

    def body(b, carry):
        q = q_ref[b].astype(BF16)
        p = _softmax_rows(_dot_nt(q, k_ref[b].astype(BF16)) * (X_HD ** -0.5))
        o_ref[b] = _dot(p.astype(BF16), v_ref[b].astype(BF16)).astype(o_ref.dtype)
        return carry

    lax.fori_loop(0, bb, body, 0)


def _xattn(q, mk, mv, *, bb, tq):
    nb, t, d = q.shape
    return pl.pallas_call(
        _xattn_kernel,
        grid=(nb // bb, X_HEADS, t // tq),
        in_specs=[pl.BlockSpec((bb, tq, X_HD), lambda b, h, i: (b, i, h)),
                  pl.BlockSpec((bb, MEM_LEN, X_HD), lambda b, h, i: (b, 0, h)),
                  pl.BlockSpec((bb, MEM_LEN, X_HD), lambda b, h, i: (b, 0, h))],
        out_specs=pl.BlockSpec((bb, tq, X_HD), lambda b, h, i: (b, i, h)),
        out_shape=jax.ShapeDtypeStruct((nb, t, d), BF16),
        compiler_params=_params("arbitrary", "arbitrary", "arbitrary"),
        name="xattn",
    )(q, mk, mv)


def _softmax_rows(s):
    s = s - jnp.max(s, axis=-1, keepdims=True)
    e = jnp.exp(s)
    return e / jnp.sum(e, axis=-1, keepdims=True)


def _xattn_cache_kernel(q_ref, k_ref, v_ref, o_ref):
    bb = q_ref.shape[0]
    nchunk = X_HD // LANES
    pitch = nchunk * X_HEADS

    def gather(ref, b, h):
        parts = [ref[b, pl.ds(c * X_HEADS + h, MEM_LEN, stride=pitch), :] for c in range(nchunk)]
        return jnp.concatenate(parts, axis=1).astype(BF16)

    for b in range(bb):
        for h in range(X_HEADS):
            q = q_ref[b, :, h * X_HD:(h + 1) * X_HD].astype(BF16)
            p = _softmax_rows(_dot_nt(q, gather(k_ref, b, h)) * (X_HD ** -0.5))
            o_ref[b, :, h * X_HD:(h + 1) * X_HD] = _dot(p.astype(BF16), gather(v_ref, b, h)).astype(o_ref.dtype)


def _xattn_cache(q, ck, cv, *, bb):
    nb, tq, d = q.shape
    nchunk = X_HD // LANES

    def stored_order(c):
        c = c.reshape(nb, MEM_LEN, X_HEADS, nchunk, LANES).transpose(0, 1, 3, 2, 4)
        return c.reshape(nb, MEM_LEN * nchunk * X_HEADS, LANES)

    rows = MEM_LEN * nchunk * X_HEADS
    return pl.pallas_call(
        _xattn_cache_kernel,
        grid=(nb // bb,),
        in_specs=[pl.BlockSpec((bb, tq, d), lambda b: (b, 0, 0)),
                  pl.BlockSpec((bb, rows, LANES), lambda b: (b, 0, 0)),
                  pl.BlockSpec((bb, rows, LANES), lambda b: (b, 0, 0))],
        out_specs=pl.BlockSpec((bb, tq, d), lambda b: (b, 0, 0)),
        out_shape=jax.ShapeDtypeStruct((nb, tq, d), BF16),
        compiler_params=_params("arbitrary"),
        name="xattn_cache",
    )(q, stored_order(ck), stored_order(cv))


def _ffn_up_kernel(hn_ref, wg_ref, wu_ref, cw_ref, cb_ref, prev_ref, o_ref, new_ref, buf_ref, up_ref,
                   wgb_ref, wub_ref, *, shift):
    rows = hn_ref.shape[0]
    halo = _halo_rows(shift)
    ch = _row_chunk(rows)

    @pl.when(pl.program_id(1) == 0)
    def _():
        _cast_into(wg_ref, wgb_ref)
        _cast_into(wu_ref, wub_ref)

    _conv_stage(buf_ref, prev_ref[0], shift)
    buf_ref[halo:halo + rows, :] = _dot(hn_ref[...], wgb_ref[...])
    up_ref[...] = _dot(hn_ref[...], wub_ref[...])
    w0, w1, w2, cb = cw_ref[0:1, :], cw_ref[1:2, :], cw_ref[2:3, :], cb_ref[...]

    def act_body(c, carry):
        r0 = pl.multiple_of(c * ch, ch)
        s2, s1 = _conv_taps(buf_ref, r0, ch, shift)
        gc = w0 * s2 + w1 * s1 + w2 * buf_ref[pl.ds(pl.multiple_of(halo + r0, SUBLANES), ch), :] + cb
        o_ref[pl.ds(r0, ch), :] = ((gc * jax.nn.sigmoid(gc)) * up_ref[pl.ds(r0, ch), :]).astype(o_ref.dtype)
        return carry

    lax.fori_loop(0, rows // ch, act_body, 0)
    new_ref[0] = buf_ref[halo + rows - 2 * shift:halo + rows, :]


def _ffn_up(hn, wg, wu, cw, cb, prev, nseq, rows, shift, tn):
    d = hn.shape[1]
    kern = functools.partial(_ffn_up_kernel, shift=shift)
    return pl.pallas_call(
        kern,
        grid=(D_FF // tn, nseq),
        in_specs=[pl.BlockSpec((rows, d), lambda j, b: (b, 0)),
                  pl.BlockSpec((d, tn), lambda j, b: (0, j)),
                  pl.BlockSpec((d, tn), lambda j, b: (0, j)),
                  pl.BlockSpec((CONV_WIDTH, tn), lambda j, b: (0, j)),
                  pl.BlockSpec((1, tn), lambda j, b: (0, j)),
                  pl.BlockSpec((1, 2 * shift, tn), lambda j, b: (b, 0, j))],
        out_specs=[pl.BlockSpec((rows, tn), lambda j, b: (b, j)),
                   pl.BlockSpec((1, 2 * shift, tn), lambda j, b: (b, 0, j))],
        out_shape=[jax.ShapeDtypeStruct((nseq * rows, D_FF), BF16),
                   jax.ShapeDtypeStruct((nseq, 2 * shift, D_FF), F32)],
        scratch_shapes=[pltpu.VMEM((_halo_rows(shift) + rows, tn), F32), pltpu.VMEM((rows, tn), F32),
                        pltpu.VMEM((d, tn), BF16), pltpu.VMEM((d, tn), BF16)],
        compiler_params=_params("arbitrary", "arbitrary"),
        name="ffn_up",
    )(hn, wg, wu, cw, cb, prev)


def _layer(x, w, *, nseq, rows, shift, prev_conv, s0, prev_ffn, mk, mv, time_major):
    m = x.shape[0]
    tm = min(m, 1024)
    p, logf = _in_proj(x, w["norm_mix"], w["w_in"], w["w_g1"], w["w_g2"], w["b_gate"], tm, 1024)
    conv_out, conv_new = _conv_gate(p, prev_conv, w["conv_w"], nseq, rows, shift, 256)

    if time_major:
        nt = m // shift
        pad = ((0, 0), (0, SUBLANES - nt), (0, 0))
        pg = jnp.pad(p.reshape(nt, shift, N_MAIN)[:, :, 3 * D_CONV:].transpose(1, 0, 2), pad)
        lg = jnp.pad(logf.reshape(nt, shift, -1).transpose(1, 0, 2), pad)
        o, s_new = _gla(pg, lg, s0, w["gla_norm"], q_blk=0, k_blk=1, v_blk=1, r_blk=2,
                        bb=8, rows=SUBLANES, chunk=SUBLANES, group=4)
        gla_out = o[:, :nt].transpose(1, 0, 2).reshape(m, D_GLA)
    else:
        q0 = 3 * D_CONV // (GLA_HEADS * GLA_DK)
        v0 = (3 * D_CONV + 2 * GLA_HEADS * GLA_DK) // (GLA_HEADS * GLA_DV)
        o, s_new = _gla(p.reshape(nseq, rows, N_MAIN), logf.reshape(nseq, rows, -1), s0, w["gla_norm"],
                        q_blk=q0, k_blk=q0 + 1, v_blk=v0, r_blk=v0 + 1,
                        bb=2, rows=512, chunk=GLA_CHUNK, group=2)
        gla_out = o.reshape(m, D_GLA)

    tm2 = min(m, 512)
    h, hn = _proj_res_norm([conv_out, gla_out], w["w_out"], x, w["norm_x"], tm=tm2, tk=D_MODEL, final=False)
    qx = _matmul(hn, w["w_xq"], BF16, tm)

    if time_major:
        nt = m // shift
        qb = jnp.pad(qx.astype(F32).reshape(nt, shift, D_MODEL).transpose(1, 0, 2),
                     ((0, 0), (0, SUBLANES - nt), (0, 0)))
        ob = _xattn_cache(qb, mk, mv, bb=2)
        attn = ob[:, :nt].transpose(1, 0, 2).reshape(m, D_MODEL)
    else:
        attn = _xattn(qx.reshape(nseq, rows, D_MODEL), mk, mv, bb=1, tq=rows).reshape(m, D_MODEL)

    h2, hn2 = _proj_res_norm([attn], w["w_xo"], h, w["norm_ffn"], tm=tm2, tk=D_MODEL, final=False)
    act, ffn_new = _ffn_up(hn2, w["w_fg"], w["w_fu"], w["ffn_conv_w"], w["ffn_conv_b"], prev_ffn,
                           nseq, rows, shift, 512)
    return act, h2, conv_new, s_new, ffn_new


def kernel(x_prompt, x_sample, mem_prompt, cache_conv, state_gla, cache_ffn, cache_mem_k, cache_mem_v,
           norm_mix, w_in, conv_w, w_gate2, b_gate, gla_norm, w_out, norm_x, norm_mem, w_xq, w_xk, w_xv,
           w_xo, norm_ffn, w_ffn_gate, w_ffn_up, ffn_conv_w, ffn_conv_b, w_ffn_down, norm_final):
    depth = w_in.shape[0]
    nb, seq, d = x_prompt.shape
    db, dseq, _ = x_sample.shape
    hp = x_prompt.reshape(nb * seq, d)
    hs = x_sample.transpose(1, 0, 2).reshape(dseq * db, d)
    outs = {k: [] for k in ("conv_p", "gla_p", "ffn_p", "mk", "mv", "conv_s", "gla_s", "ffn_s")}
    nfinal = norm_final.reshape(1, d)
    yp = ys = None
    for l in range(depth):
        w = {
            "norm_mix": norm_mix[l].reshape(1, d),
            "w_in": w_in[l].astype(BF16),
            "w_g1": jnp.pad(w_in[l][:, N_MAIN:], ((0, 0), (0, LANES - GLA_RANK))).astype(BF16),
            "w_g2": jnp.pad(w_gate2[l], ((0, LANES - GLA_RANK), (0, 0))).astype(BF16),
            "b_gate": b_gate[l].reshape(1, -1),
            "conv_w": conv_w[l],
            "gla_norm": gla_norm[l].reshape(1, -1),
            "w_out": w_out[l].astype(BF16),
            "norm_x": norm_x[l].reshape(1, d),
            "w_xq": w_xq[l],
            "w_xo": w_xo[l].astype(BF16),
            "norm_ffn": norm_ffn[l].reshape(1, d),
            "w_fg": w_ffn_gate[l],
            "w_fu": w_ffn_up[l],
            "ffn_conv_w": ffn_conv_w[l],
            "ffn_conv_b": ffn_conv_b[l].reshape(1, -1),
        }
        w_fd = w_ffn_down[l].astype(BF16)
        last = l == depth - 1
        gain_next = nfinal if last else None

        mem = mem_prompt.reshape(nb * MEM_LEN, d)
        nmem = norm_mem[l].reshape(1, d)
        mk, mk_cache = _mem_proj(mem, nmem, w_xk[l], 1024)
        mv, mv_cache = _mem_proj(mem, nmem, w_xv[l], 1024)
        act, h2, c1, s1, f1 = _layer(
            hp, w, nseq=nb, rows=seq, shift=1,
            prev_conv=jnp.zeros((nb, CONV_WIDTH - 1, D_CONV), F32),
            s0=jnp.zeros((nb, GLA_HEADS, GLA_DK, GLA_DV), F32),
            prev_ffn=jnp.zeros((nb, CONV_WIDTH - 1, D_FF), F32),
            mk=mk.reshape(nb, MEM_LEN, d), mv=mv.reshape(nb, MEM_LEN, d), time_major=False)
        assert last, "only the final layer's epilogue (final rmsnorm) is implemented"
        yp = _proj_res_norm([act], w_fd, h2, gain_next, tm=512, tk=D_FF // 2, final=True)
        outs["conv_p"].append(c1)
        outs["gla_p"].append(s1)
        outs["ffn_p"].append(f1)
        outs["mk"].append(mk_cache)
        outs["mv"].append(mv_cache)

        def tmajor(c):
            return c.transpose(1, 0, 2).reshape(1, (CONV_WIDTH - 1) * db, c.shape[-1])

        act, h2, c2, s2, f2 = _layer(
            hs, w, nseq=1, rows=dseq * db, shift=db,
            prev_conv=tmajor(cache_conv[l]), s0=state_gla[l], prev_ffn=tmajor(cache_ffn[l]),
            mk=cache_mem_k[l], mv=cache_mem_v[l], time_major=True)
        ys = _proj_res_norm([act], w_fd, h2, gain_next, tm=512, tk=D_FF // 2, final=True)
        outs["conv_s"].append(c2.reshape(CONV_WIDTH - 1, db, D_CONV).transpose(1, 0, 2))
        outs["gla_s"].append(s2)
        outs["ffn_s"].append(f2.reshape(CONV_WIDTH - 1, db, D_FF).transpose(1, 0, 2))

    y_prompt = yp.reshape(nb, seq, d)
    y_sample = ys.reshape(dseq, db, d).transpose(1, 0, 2)
    st = lambda k: jnp.stack(outs[k])
    return (y_prompt, y_sample, st("conv_p"), st("gla_p"), st("ffn_p"), st("mk"), st("mv"),
            st("conv_s"), st("gla_s"), st("ffn_s"))
```

```python
import functools

import jax
import jax.numpy as jnp
from jax import lax
from jax.experimental import pallas as pl
from jax.experimental.pallas import tpu as pltpu

F32 = jnp.float32
BF16 = jnp.bfloat16

D_MODEL = 2048
EPS = 1e-6
CONV_WIDTH = 3
D_CONV = 1024
D_GLA = 1024
GLA_HEADS = 4
GLA_DV = 256
GLA_DK = 128
GLA_RANK = 16
GLA_TAU = 16.0
GLA_CHUNK = 64
X_HEADS = 4
X_HD = 512
MEM_LEN = 256
D_FF = 5632
N_MAIN = 3 * D_CONV + 2 * GLA_HEADS * GLA_DK + 2 * GLA_HEADS * GLA_DV

LANES = 128
SUBLANES = 8
VMEM_LIMIT_BYTES = 56 * 1024 * 1024


def _params(*sem):
    return pltpu.CompilerParams(dimension_semantics=sem, vmem_limit_bytes=VMEM_LIMIT_BYTES)


def _dot(a, b):
    return jnp.dot(a, b, preferred_element_type=F32)


def _dot_nt(a, b):
    return lax.dot_general(a, b, (((1,), (1,)), ((), ())), preferred_element_type=F32)


def _dot_tn(a, b):
    return lax.dot_general(a, b, (((0,), (0,)), ((), ())), preferred_element_type=F32)


def _rms_rows(x, g):
    ms = jnp.mean(x * x, axis=-1, keepdims=True)
    return (x * lax.rsqrt(ms + EPS)) * g


def _row_chunk(rows, limit=256):
    for c in (256, 128, 64, 32, 16, 8):
        if c <= limit and rows % c == 0:
            return c
    return rows


NORM_ROWS = 128


def _norm_into(x_ref, g_ref, xn_ref):
    rows = x_ref.shape[0]
    ch = _row_chunk(rows)
    g = g_ref[...]

    def body(c, carry):
        r = pl.ds(pl.multiple_of(c * ch, ch), ch)
        xn_ref[r, :] = _rms_rows(x_ref[r, :], g).astype(xn_ref.dtype)
        return carry

    lax.fori_loop(0, rows // ch, body, 0)


def _norm_matmul_kernel(x_ref, g_ref, w_ref, o_ref, oc_ref, xn_ref):
    j = pl.program_id(1)

    @pl.when(j == 0)
    def _():
        _norm_into(x_ref, g_ref, xn_ref)

    o_ref[...] = _dot(xn_ref[...], w_ref[...].astype(BF16)).astype(o_ref.dtype)
    nchunk = X_HD // LANES
    pitch = nchunk * X_HEADS
    heads_per_tile = o_ref.shape[1] // X_HD
    for b in range(oc_ref.shape[0]):
        for hh in range(heads_per_tile):
            for c in range(nchunk):
                col = hh * X_HD + c * LANES
                row = c * X_HEADS + j * heads_per_tile + hh
                oc_ref[b, pl.ds(row, MEM_LEN, stride=pitch), :] = (
                    o_ref[b * MEM_LEN:(b + 1) * MEM_LEN, col:col + LANES])


def _in_proj_kernel(x_ref, g_ref, w_ref, wg1_ref, wg2_ref, bg_ref, o_ref, lf_ref, xn_ref):
    @pl.when(pl.program_id(1) == 0)
    def _():
        _norm_into(x_ref, g_ref, xn_ref)
        g1 = _dot(xn_ref[...], wg1_ref[...])
        z = _dot(g1.astype(BF16), wg2_ref[...]) + bg_ref[...]
        lf_ref[...] = (jnp.minimum(z, 0.0) - jnp.log1p(jnp.exp(-jnp.abs(z)))) * (1.0 / GLA_TAU)

    o_ref[...] = _dot(xn_ref[...], w_ref[...]).astype(o_ref.dtype)


def _mem_proj(mem, gain, w, tn):
    m, d = mem.shape
    n = w.shape[1]
    nb = m // MEM_LEN
    nchunk = X_HD // LANES
    rows = MEM_LEN * nchunk * X_HEADS
    flat, stored = pl.pallas_call(
        _norm_matmul_kernel,
        grid=(1, n // tn),
        in_specs=[pl.BlockSpec((m, d), lambda i, j: (0, 0)),
                  pl.BlockSpec((1, d), lambda i, j: (0, 0)),
                  pl.BlockSpec((d, tn), lambda i, j: (0, j))],
        out_specs=[pl.BlockSpec((m, tn), lambda i, j: (0, j)),
                   pl.BlockSpec((nb, rows, LANES), lambda i, j: (0, 0, 0))],
        out_shape=[jax.ShapeDtypeStruct((m, n), F32),
                   jax.ShapeDtypeStruct((nb, rows, LANES), F32)],
        scratch_shapes=[pltpu.VMEM((m, d), BF16)],
        compiler_params=_params("arbitrary", "arbitrary"),
        name="norm_matmul",
    )(mem, gain, w)
    cache = stored.reshape(nb, MEM_LEN, nchunk, X_HEADS, LANES).transpose(0, 1, 3, 2, 4)
    return flat, cache.reshape(nb, MEM_LEN, X_HEADS, X_HD)


def _in_proj(x, gain, w_in, wg1, wg2, b_gate, tm, tn):
    m, d = x.shape
    n = N_MAIN
    ng = wg2.shape[1]
    return pl.pallas_call(
        _in_proj_kernel,
        grid=(m // tm, n // tn),
        in_specs=[pl.BlockSpec((tm, d), lambda i, j: (i, 0)),
                  pl.BlockSpec((1, d), lambda i, j: (0, 0)),
                  pl.BlockSpec((d, tn), lambda i, j: (0, j)),
                  pl.BlockSpec((d, LANES), lambda i, j: (0, 0)),
                  pl.BlockSpec((LANES, ng), lambda i, j: (0, 0)),
                  pl.BlockSpec((1, ng), lambda i, j: (0, 0))],
        out_specs=[pl.BlockSpec((tm, tn), lambda i, j: (i, j)),
                   pl.BlockSpec((tm, ng), lambda i, j: (i, 0))],
        out_shape=[jax.ShapeDtypeStruct((m, n), F32),
                   jax.ShapeDtypeStruct((m, ng), F32)],
        scratch_shapes=[pltpu.VMEM((tm, d), BF16)],
        compiler_params=_params("arbitrary", "arbitrary"),
        name="in_proj",
    )(x, gain, w_in, wg1, wg2, b_gate)


def _cast_into(src_ref, dst_ref):
    rows = src_ref.shape[0]
    ch = _row_chunk(rows)

    def body(c, carry):
        r = pl.ds(pl.multiple_of(c * ch, ch), ch)
        dst_ref[r, :] = src_ref[r, :].astype(dst_ref.dtype)
        return carry

    lax.fori_loop(0, rows // ch, body, 0)


def _matmul_kernel(a_ref, w_ref, o_ref, wb_ref):
    @pl.when(pl.program_id(0) == 0)
    def _():
        _cast_into(w_ref, wb_ref)

    o_ref[...] = _dot(a_ref[...], wb_ref[...]).astype(o_ref.dtype)


def _matmul(a, w, out_dtype, tm):
    m, k = a.shape
    n = w.shape[1]
    return pl.pallas_call(
        _matmul_kernel,
        grid=(m // tm,),
        in_specs=[pl.BlockSpec((tm, k), lambda i: (i, 0)),
                  pl.BlockSpec((k, n), lambda i: (0, 0), pipeline_mode=pl.Buffered(1))],
        out_specs=pl.BlockSpec((tm, n), lambda i: (i, 0)),
        out_shape=jax.ShapeDtypeStruct((m, n), out_dtype),
        scratch_shapes=[pltpu.VMEM((k, n), BF16)],
        compiler_params=_params("arbitrary"),
        name="matmul",
    )(a, w)


def _halo_rows(shift):
    return max(2 * shift, SUBLANES)


def _conv_stage(buf_ref, prev, shift):
    halo = _halo_rows(shift)
    buf_ref[halo - 2 * shift:halo, :] = prev


def _conv_taps(buf_ref, r0, rows, shift):
    halo = _halo_rows(shift)

    def aligned(start):
        return start if isinstance(start, int) else pl.multiple_of(start, SUBLANES)

    if shift % SUBLANES == 0:
        s1 = buf_ref[pl.ds(aligned(r0 + (halo - shift)), rows), :]
        s2 = buf_ref[pl.ds(aligned(r0 + (halo - 2 * shift)), rows), :]
        return s2, s1
    win = buf_ref[pl.ds(aligned(r0 + (halo - SUBLANES)), rows + SUBLANES), :]
    s1 = pltpu.roll(win, shift, 0)[SUBLANES:, :]
    s2 = pltpu.roll(win, 2 * shift, 0)[SUBLANES:, :]
    return s2, s1


def _conv_gate_kernel(bg_ref, cg_ref, vc_ref, prev_ref, w_ref, o_ref, new_ref, buf_ref, *, shift):
    rows = cg_ref.shape[0]
    halo = _halo_rows(shift)
    u = cg_ref[...] * vc_ref[...]
    _conv_stage(buf_ref, prev_ref[0], shift)
    buf_ref[halo:halo + rows, :] = u
    s2, s1 = _conv_taps(buf_ref, 0, rows, shift)
    y = w_ref[0:1, :] * s2 + w_ref[1:2, :] * s1 + w_ref[2:3, :] * u
    o_ref[...] = (bg_ref[...] * y).astype(o_ref.dtype)
    new_ref[0] = buf_ref[halo + rows - 2 * shift:halo + rows, :]


def _conv_gate(p, prev, conv_w, nseq, rows, shift, tn):
    nj = D_CONV // tn
    kern = functools.partial(_conv_gate_kernel, shift=shift)
    return pl.pallas_call(
        kern,
        grid=(nseq, nj),
        in_specs=[pl.BlockSpec((rows, tn), lambda b, j: (b, j)),
                  pl.BlockSpec((rows, tn), lambda b, j: (b, j + nj)),
                  pl.BlockSpec((rows, tn), lambda b, j: (b, j + 2 * nj)),
                  pl.BlockSpec((1, 2 * shift, tn), lambda b, j: (b, 0, j)),
                  pl.BlockSpec((CONV_WIDTH, tn), lambda b, j: (0, j))],
        out_specs=[pl.BlockSpec((rows, tn), lambda b, j: (b, j)),
                   pl.BlockSpec((1, 2 * shift, tn), lambda b, j: (b, 0, j))],
        out_shape=[jax.ShapeDtypeStruct((nseq * rows, D_CONV), BF16),
                   jax.ShapeDtypeStruct((nseq, 2 * shift, D_CONV), F32)],
        scratch_shapes=[pltpu.VMEM((_halo_rows(shift) + rows, tn), F32)],
        compiler_params=_params("arbitrary", "arbitrary"),
        name="conv_gate",
    )(p, p, p, prev, conv_w)


def _cumsum_rows(g):
    c = g.shape[0]
    row = lax.broadcasted_iota(jnp.int32, g.shape, 0)
    x = g
    s = 1
    while s < c:
        x = x + jnp.where(row >= s, pltpu.roll(x, s, 0), 0.0)
        s *= 2
    return x


def _bcast_block_row(x, s, k):
    c, lanes = x.shape
    if s == c:
        return jnp.broadcast_to(x[k:k + 1, :], x.shape)
    if s >= SUBLANES:
        y = x.reshape(c // s, s, lanes)
        return jnp.broadcast_to(y[:, k:k + 1, :], y.shape).reshape(c, lanes)
    y = x.reshape(c // SUBLANES, SUBLANES, lanes)
    sub = lax.broadcasted_iota(jnp.int32, y.shape, 1)
    out = None
    for blk in range(SUBLANES // s):
        src = jnp.broadcast_to(y[:, blk * s + k:blk * s + k + 1, :], y.shape)
        out = src if out is None else jnp.where(sub >= blk * s, src, out)
    return out.reshape(c, lanes)


def _gla_pair_masks(c):
    ri = lax.broadcasted_iota(jnp.int32, (c, c), 0)
    ci = lax.broadcasted_iota(jnp.int32, (c, c), 1)
    diff_bits = ri ^ ci
    masks = [diff_bits == 0]
    level = 0
    while (1 << level) < c:
        masks.append(((diff_bits >> level) == 1) & (((ri >> level) & 1) == 1))
        level += 1
    return masks


def _gla_chunk(q, k, v, g, s_prev, masks):
    c = q.shape[0]
    cum = _cumsum_rows(g)
    a = jnp.where(masks[0], _dot_nt(q.astype(BF16), k.astype(BF16)), 0.0)
    for level in range(len(masks) - 1):
        half = 1 << level
        ref = _bcast_block_row(cum, 2 * half, half - 1)
        qe = q * jnp.exp(jnp.minimum(cum - ref, 0.0))
        ke = k * jnp.exp(jnp.minimum(ref - cum, 0.0))
        a = a + jnp.where(masks[1 + level], _dot_nt(qe.astype(BF16), ke.astype(BF16)), 0.0)
    o = _dot(a.astype(BF16), v.astype(BF16)) + _dot((q * jnp.exp(cum)).astype(BF16), s_prev.astype(BF16))
    last = cum[c - 1:c, :]
    kd = k * jnp.exp(last - cum)
    dk = last.shape[1]
    decay_t = jnp.transpose(jnp.broadcast_to(jnp.exp(last), (dk, dk)))
    decayed = jnp.concatenate([decay_t * s_prev[:, i:i + dk] for i in range(0, s_prev.shape[1], dk)], axis=1)
    s_new = decayed + _dot_tn(kd.astype(BF16), v.astype(BF16))
    return o, s_new


def _gla_kernel(q_ref, k_ref, v_ref, r_ref, g_ref, s0_ref, gn_ref, o_ref, sn_ref, *, chunk, single_chunk, group):
    bb, rows = q_ref.shape[0], q_ref.shape[1]
    nchunk = rows // chunk
    state_in = s0_ref if single_chunk else sn_ref

    if not single_chunk:
        @pl.when(pl.program_id(1) == 0)
        def _():
            sn_ref[...] = s0_ref[...]

    masks = _gla_pair_masks(chunk)

    def one(b, r):
        for h in range(GLA_HEADS):
            kc = slice(h * GLA_DK, (h + 1) * GLA_DK)
            vc = slice(h * GLA_DV, (h + 1) * GLA_DV)
            q = q_ref[b, r, kc] * (GLA_DK ** -0.5)
            o, s_new = _gla_chunk(q, k_ref[b, r, kc], v_ref[b, r, vc], g_ref[b, r, kc], state_in[b, h], masks)
            sn_ref[b, h] = s_new
            rr = r_ref[b, r, vc]
            o_ref[b, r, vc] = (_rms_rows(o, gn_ref[:, vc]) * (rr * jax.nn.sigmoid(rr))).astype(o_ref.dtype)

    def body(n, carry):
        r = pl.ds(pl.multiple_of((n % nchunk) * chunk, chunk), chunk)
        for u in range(group):
            one((n // nchunk) * group + u, r)
        return carry

    lax.fori_loop(0, (bb // group) * nchunk, body, 0)


def _gla(p, logf, s0, gla_norm, *, q_blk, k_blk, v_blk, r_blk, bb, rows, chunk, group):
    nb, t, _ = p.shape
    nk, nv = GLA_HEADS * GLA_DK, GLA_HEADS * GLA_DV
    kern = functools.partial(_gla_kernel, chunk=chunk, single_chunk=(t == chunk), group=group)
    state_spec = pl.BlockSpec((bb, GLA_HEADS, GLA_DK, GLA_DV), lambda b, c: (b, 0, 0, 0))
    return pl.pallas_call(
        kern,
        grid=(nb // bb, t // rows),
        in_specs=[pl.BlockSpec((bb, rows, nk), lambda b, c: (b, c, q_blk)),
                  pl.BlockSpec((bb, rows, nk), lambda b, c: (b, c, k_blk)),
                  pl.BlockSpec((bb, rows, nv), lambda b, c: (b, c, v_blk)),
                  pl.BlockSpec((bb, rows, nv), lambda b, c: (b, c, r_blk)),
                  pl.BlockSpec((bb, rows, nk), lambda b, c: (b, c, 0)),
                  state_spec,
                  pl.BlockSpec((1, nv), lambda b, c: (0, 0))],
        out_specs=[pl.BlockSpec((bb, rows, nv), lambda b, c: (b, c, 0)),
                   state_spec],
        out_shape=[jax.ShapeDtypeStruct((nb, t, D_GLA), BF16),
                   jax.ShapeDtypeStruct((nb, GLA_HEADS, GLA_DK, GLA_DV), F32)],
        compiler_params=_params("arbitrary", "arbitrary"),
        name="gla",
    )(p, p, p, p, logf, s0, gla_norm)


def _proj_res_norm_kernel(*refs, n_a, nk, final):
    a_refs, (w_ref, res_ref, g_ref) = refs[:n_a], refs[n_a:n_a + 3]
    out_refs, acc_ref = refs[n_a + 3:-1], refs[-1]
    k = pl.program_id(1)
    part, r0 = None, 0
    for a_ref in a_refs:
        kw = a_ref.shape[1]
        term = _dot(a_ref[...], w_ref[r0:r0 + kw, :])
        part = term if part is None else part + term
        r0 += kw

    @pl.when(k == 0)
    def _():
        acc_ref[...] = part

    if nk > 1:
        @pl.when(k > 0)
        def _():
            acc_ref[...] += part

    @pl.when(k == nk - 1)
    def _():
        rows = acc_ref.shape[0]
        ch = _row_chunk(rows, NORM_ROWS)
        g = g_ref[...]

        def body(c, carry):
            r = pl.ds(pl.multiple_of(c * ch, ch), ch)
            h = res_ref[r, :] + acc_ref[r, :]
            hn = _rms_rows(h, g)
            if final:
                out_refs[0][r, :] = hn
            else:
                out_refs[0][r, :] = h
                out_refs[1][r, :] = hn.astype(out_refs[1].dtype)
            return carry

        lax.fori_loop(0, rows // ch, body, 0)


def _proj_res_norm(a_list, w, res, gain, *, tm, tk, final):
    m = a_list[0].shape[0]
    kdim, d = w.shape
    nk = kdim // tk
    assert len(a_list) == 1 or nk == 1
    kern = functools.partial(_proj_res_norm_kernel, n_a=len(a_list), nk=nk, final=final)
    a_specs = ([pl.BlockSpec((tm, tk), lambda i, k: (i, k))] if len(a_list) == 1 else
               [pl.BlockSpec((tm, a.shape[1]), lambda i, k: (i, 0)) for a in a_list])
    row_spec = pl.BlockSpec((tm, d), lambda i, k: (i, 0))
    if final:
        out_specs = row_spec
        out_shape = jax.ShapeDtypeStruct((m, d), F32)
    else:
        out_specs = [row_spec, row_spec]
        out_shape = [jax.ShapeDtypeStruct((m, d), F32), jax.ShapeDtypeStruct((m, d), BF16)]
    return pl.pallas_call(
        kern,
        grid=(m // tm, nk),
        in_specs=a_specs + [pl.BlockSpec((tk, d), lambda i, k: (k, 0)),
                            row_spec,
                            pl.BlockSpec((1, d), lambda i, k: (0, 0))],
        out_specs=out_specs,
        out_shape=out_shape,
        scratch_shapes=[pltpu.VMEM((tm, d), F32)],
        compiler_params=_params("arbitrary", "arbitrary"),
        name="proj_res_norm",
    )(*a_list, w, res, gain)


def _xattn_kernel(q_ref, k_ref, v_ref, o_ref):
    bb = q_ref.shape[0]

    def body(b, carry):
        q = q_ref[b].astype(BF16)
        p = _softmax_rows(_dot_nt(q, k_ref[b].astype(BF16)) * (X_HD ** -0.5))
        o_ref[b] = _dot(p.astype(BF16), v_ref[b].astype(BF16)).astype(o_ref.dtype)
        return carry

    lax.fori_loop(0, bb, body, 0)


def _xattn(q, mk, mv, *, bb, tq):
    nb, t, d = q.shape
    return pl.pallas_call(
        _xattn_kernel,
        grid=(nb // bb, X_HEADS, t // tq),
        in_specs=[pl.BlockSpec((bb, tq, X_HD), lambda b, h, i: (b, i, h)),
                  pl.BlockSpec((bb, MEM_LEN, X_HD), lambda b, h, i: (b, 0, h)),
                  pl.BlockSpec((bb, MEM_LEN, X_HD), lambda b, h, i: (b, 0, h))],
        out_specs=pl.BlockSpec((bb, tq, X_HD), lambda b, h, i: (b, i, h)),
        out_shape=jax.ShapeDtypeStruct((nb, t, d), BF16),
        compiler_params=_params("arbitrary", "arbitrary", "arbitrary"),
        name="xattn",
    )(q, mk, mv)


def _softmax_rows(s):
    s = s - jnp.max(s, axis=-1, keepdims=True)
    e = jnp.exp(s)
    return e / jnp.sum(e, axis=-1, keepdims=True)


def _xattn_cache_kernel(q_ref, k_ref, v_ref, o_ref):
    bb, tq = q_ref.shape[0], q_ref.shape[1]
    nchunk = X_HD // LANES
    pitch = nchunk * X_HEADS

    def gather(ref, b, h):
        parts = [ref[b, pl.ds(c * X_HEADS + h, MEM_LEN, stride=pitch), :] for c in range(nchunk)]
        return jnp.concatenate(parts, axis=1).astype(BF16)

    pairs = [(b, h) for b in range(bb) for h in range(X_HEADS)]
    scores = [_dot_nt(q_ref[b, :, h * X_HD:(h + 1) * X_HD].astype(BF16), gather(k_ref, b, h)) for b, h in pairs]
    p = _softmax_rows(jnp.concatenate(scores, axis=0) * (X_HD ** -0.5)).astype(BF16)
    for n, (b, h) in enumerate(pairs):
        o_ref[b, :, h * X_HD:(h + 1) * X_HD] = _dot(p[n * tq:(n + 1) * tq], gather(v_ref, b, h)).astype(o_ref.dtype)


def _xattn_cache(q, ck, cv, *, bb):
    nb, tq, d = q.shape
    nchunk = X_HD // LANES

    def stored_order(c):
        c = c.reshape(nb, MEM_LEN, X_HEADS, nchunk, LANES).transpose(0, 1, 3, 2, 4)
        return c.reshape(nb, MEM_LEN * nchunk * X_HEADS, LANES)

    rows = MEM_LEN * nchunk * X_HEADS
    return pl.pallas_call(
        _xattn_cache_kernel,
        grid=(nb // bb,),
        in_specs=[pl.BlockSpec((bb, tq, d), lambda b: (b, 0, 0)),
                  pl.BlockSpec((bb, rows, LANES), lambda b: (b, 0, 0)),
                  pl.BlockSpec((bb, rows, LANES), lambda b: (b, 0, 0))],
        out_specs=pl.BlockSpec((bb, tq, d), lambda b: (b, 0, 0)),
        out_shape=jax.ShapeDtypeStruct((nb, tq, d), BF16),
        compiler_params=_params("arbitrary"),
        name="xattn_cache",
    )(q, stored_order(ck), stored_order(cv))


def _ffn_up_kernel(hn_ref, wg_ref, wu_ref, cw_ref, cb_ref, prev_ref, o_ref, new_ref, buf_ref, up_ref,
                   wgb_ref, wub_ref, *, shift):
    rows = hn_ref.shape[0]
    halo = _halo_rows(shift)
    ch = _row_chunk(rows)

    @pl.when(pl.program_id(1) == 0)
    def _():
        _cast_into(wg_ref, wgb_ref)
        _cast_into(wu_ref, wub_ref)

    _conv_stage(buf_ref, prev_ref[0], shift)
    buf_ref[halo:halo + rows, :] = _dot(hn_ref[...], wgb_ref[...])
    up_ref[...] = _dot(hn_ref[...], wub_ref[...])
    w0, w1, w2, cb = cw_ref[0:1, :], cw_ref[1:2, :], cw_ref[2:3, :], cb_ref[...]

    def act_body(c, carry):
        r0 = pl.multiple_of(c * ch, ch)
        s2, s1 = _conv_taps(buf_ref, r0, ch, shift)
        gc = w0 * s2 + w1 * s1 + w2 * buf_ref[pl.ds(pl.multiple_of(halo + r0, SUBLANES), ch), :] + cb
        o_ref[pl.ds(r0, ch), :] = ((gc * jax.nn.sigmoid(gc)) * up_ref[pl.ds(r0, ch), :]).astype(o_ref.dtype)
        return carry

    lax.fori_loop(0, rows // ch, act_body, 0)
    new_ref[0] = buf_ref[halo + rows - 2 * shift:halo + rows, :]


def _ffn_up(hn, wg, wu, cw, cb, prev, nseq, rows, shift, tn):
    d = hn.shape[1]
    kern = functools.partial(_ffn_up_kernel, shift=shift)
    return pl.pallas_call(
        kern,
        grid=(D_FF // tn, nseq),
        in_specs=[pl.BlockSpec((rows, d), lambda j, b: (b, 0)),
                  pl.BlockSpec((d, tn), lambda j, b: (0, j)),
                  pl.BlockSpec((d, tn), lambda j, b: (0, j)),
                  pl.BlockSpec((CONV_WIDTH, tn), lambda j, b: (0, j)),
                  pl.BlockSpec((1, tn), lambda j, b: (0, j)),
                  pl.BlockSpec((1, 2 * shift, tn), lambda j, b: (b, 0, j))],
        out_specs=[pl.BlockSpec((rows, tn), lambda j, b: (b, j)),
                   pl.BlockSpec((1, 2 * shift, tn), lambda j, b: (b, 0, j))],
        out_shape=[jax.ShapeDtypeStruct((nseq * rows, D_FF), BF16),
                   jax.ShapeDtypeStruct((nseq, 2 * shift, D_FF), F32)],
        scratch_shapes=[pltpu.VMEM((_halo_rows(shift) + rows, tn), F32), pltpu.VMEM((rows, tn), F32),
                        pltpu.VMEM((d, tn), BF16), pltpu.VMEM((d, tn), BF16)],
        compiler_params=_params("arbitrary", "arbitrary"),
        name="ffn_up",
    )(hn, wg, wu, cw, cb, prev)


def _layer(x, w, *, nseq, rows, shift, prev_conv, s0, prev_ffn, mk, mv, time_major):
    m = x.shape[0]
    tm = min(m, 1024)
    p, logf = _in_proj(x, w["norm_mix"], w["w_in"], w["w_g1"], w["w_g2"], w["b_gate"], tm, 1024)
    conv_out, conv_new = _conv_gate(p, prev_conv, w["conv_w"], nseq, rows, shift, 256)

    if time_major:
        nt = m // shift
        pad = ((0, 0), (0, SUBLANES - nt), (0, 0))
        pg = jnp.pad(p.reshape(nt, shift, N_MAIN)[:, :, 3 * D_CONV:].transpose(1, 0, 2), pad)
        lg = jnp.pad(logf.reshape(nt, shift, -1).transpose(1, 0, 2), pad)
        o, s_new = _gla(pg, lg, s0, w["gla_norm"], q_blk=0, k_blk=1, v_blk=1, r_blk=2,
                        bb=8, rows=SUBLANES, chunk=SUBLANES, group=4)
        gla_out = o[:, :nt].transpose(1, 0, 2).reshape(m, D_GLA)
    else:
        q0 = 3 * D_CONV // (GLA_HEADS * GLA_DK)
        v0 = (3 * D_CONV + 2 * GLA_HEADS * GLA_DK) // (GLA_HEADS * GLA_DV)
        o, s_new = _gla(p.reshape(nseq, rows, N_MAIN), logf.reshape(nseq, rows, -1), s0, w["gla_norm"],
                        q_blk=q0, k_blk=q0 + 1, v_blk=v0, r_blk=v0 + 1,
                        bb=2, rows=512, chunk=GLA_CHUNK, group=2)
        gla_out = o.reshape(m, D_GLA)

    tm2 = min(m, 512)
    h, hn = _proj_res_norm([conv_out, gla_out], w["w_out"], x, w["norm_x"], tm=tm2, tk=D_MODEL, final=False)
    qx = _matmul(hn, w["w_xq"], BF16, tm)

    if time_major:
        nt = m // shift
        qb = jnp.pad(qx.astype(F32).reshape(nt, shift, D_MODEL).transpose(1, 0, 2),
                     ((0, 0), (0, SUBLANES - nt), (0, 0)))
        ob = _xattn_cache(qb, mk, mv, bb=2)
        attn = ob[:, :nt].transpose(1, 0, 2).reshape(m, D_MODEL)
    else:
        attn = _xattn(qx.reshape(nseq, rows, D_MODEL), mk, mv, bb=1, tq=rows).reshape(m, D_MODEL)

    h2, hn2 = _proj_res_norm([attn], w["w_xo"], h, w["norm_ffn"], tm=tm2, tk=D_MODEL, final=False)
    act, ffn_new = _ffn_up(hn2, w["w_fg"], w["w_fu"], w["ffn_conv_w"], w["ffn_conv_b"], prev_ffn,
                           nseq, rows, shift, 512)
    return act, h2, conv_new, s_new, ffn_new


def kernel(x_prompt, x_sample, mem_prompt, cache_conv, state_gla, cache_ffn, cache_mem_k, cache_mem_v,
           norm_mix, w_in, conv_w, w_gate2, b_gate, gla_norm, w_out, norm_x, norm_mem, w_xq, w_xk, w_xv,
           w_xo, norm_ffn, w_ffn_gate, w_ffn_up, ffn_conv_w, ffn_conv_b, w_ffn_down, norm_final):
    depth = w_in.shape[0]
    nb, seq, d = x_prompt.shape
    db, dseq, _ = x_sample.shape
    hp = x_prompt.reshape(nb * seq, d)
    hs = x_sample.transpose(1, 0, 2).reshape(dseq * db, d)
    outs = {k: [] for k in ("conv_p", "gla_p", "ffn_p", "mk", "mv", "conv_s", "gla_s", "ffn_s")}
    nfinal = norm_final.reshape(1, d)
    yp = ys = None
    for l in range(depth):
        w = {
            "norm_mix": norm_mix[l].reshape(1, d),
            "w_in": w_in[l].astype(BF16),
            "w_g1": jnp.pad(w_in[l][:, N_MAIN:], ((0, 0), (0, LANES - GLA_RANK))).astype(BF16),
            "w_g2": jnp.pad(w_gate2[l], ((0, LANES - GLA_RANK), (0, 0))).astype(BF16),
            "b_gate": b_gate[l].reshape(1, -1),
            "conv_w": conv_w[l],
            "gla_norm": gla_norm[l].reshape(1, -1),
            "w_out": w_out[l].astype(BF16),
            "norm_x": norm_x[l].reshape(1, d),
            "w_xq": w_xq[l],
            "w_xo": w_xo[l].astype(BF16),
            "norm_ffn": norm_ffn[l].reshape(1, d),
            "w_fg": w_ffn_gate[l],
            "w_fu": w_ffn_up[l],
            "ffn_conv_w": ffn_conv_w[l],
            "ffn_conv_b": ffn_conv_b[l].reshape(1, -1),
        }
        w_fd = w_ffn_down[l].astype(BF16)
        last = l == depth - 1
        gain_next = nfinal if last else None

        mem = mem_prompt.reshape(nb * MEM_LEN, d)
        nmem = norm_mem[l].reshape(1, d)
        mk, mk_cache = _mem_proj(mem, nmem, w_xk[l], 1024)
        mv, mv_cache = _mem_proj(mem, nmem, w_xv[l], 1024)
        act, h2, c1, s1, f1 = _layer(
            hp, w, nseq=nb, rows=seq, shift=1,
            prev_conv=jnp.zeros((nb, CONV_WIDTH - 1, D_CONV), F32),
            s0=jnp.zeros((nb, GLA_HEADS, GLA_DK, GLA_DV), F32),
            prev_ffn=jnp.zeros((nb, CONV_WIDTH - 1, D_FF), F32),
            mk=mk.reshape(nb, MEM_LEN, d), mv=mv.reshape(nb, MEM_LEN, d), time_major=False)
        assert last, "only the final layer's epilogue (final rmsnorm) is implemented"
        yp = _proj_res_norm([act], w_fd, h2, gain_next, tm=512, tk=D_FF // 2, final=True)
        outs["conv_p"].append(c1)
        outs["gla_p"].append(s1)
        outs["ffn_p"].append(f1)
        outs["mk"].append(mk_cache)
        outs["mv"].append(mv_cache)

        def tmajor(c):
            return c.transpose(1, 0, 2).reshape(1, (CONV_WIDTH - 1) * db, c.shape[-1])

        act, h2, c2, s2, f2 = _layer(
            hs, w, nseq=1, rows=dseq * db, shift=db,
            prev_conv=tmajor(cache_conv[l]), s0=state_gla[l], prev_ffn=tmajor(cache_ffn[l]),
            mk=cache_mem_k[l], mv=cache_mem_v[l], time_major=True)
        ys = _proj_res_norm([act], w_fd, h2, gain_next, tm=512, tk=D_FF // 2, final=True)
        outs["conv_s"].append(c2.reshape(CONV_WIDTH - 1, db, D_CONV).transpose(1, 0, 2))
        outs["gla_s"].append(s2)
        outs["ffn_s"].append(f2.reshape(CONV_WIDTH - 1, db, D_FF).transpose(1, 0, 2))

    y_prompt = yp.reshape(nb, seq, d)
    y_sample = ys.reshape(dseq, db, d).transpose(1, 0, 2)
    st = lambda k: jnp.stack(outs[k])
    return (y_prompt, y_sample, st("conv_p"), st("gla_p"), st("ffn_p"), st("mk"), st("mv"),
            st("conv_s"), st("gla_s"), st("ffn_s"))
```

```python
import functools

import jax
import jax.numpy as jnp
from jax import lax
from jax.experimental import pallas as pl
from jax.experimental.pallas import tpu as pltpu

F32 = jnp.float32
BF16 = jnp.bfloat16

D_MODEL = 2048
EPS = 1e-6
CONV_WIDTH = 3
D_CONV = 1024
D_GLA = 1024
GLA_HEADS = 4
GLA_DV = 256
GLA_DK = 128
GLA_RANK = 16
GLA_TAU = 16.0
GLA_CHUNK = 64
X_HEADS = 4
X_HD = 512
MEM_LEN = 256
D_FF = 5632
N_MAIN = 3 * D_CONV + 2 * GLA_HEADS * GLA_DK + 2 * GLA_HEADS * GLA_DV

LANES = 128
SUBLANES = 8
VMEM_LIMIT_BYTES = 56 * 1024 * 1024


def _params(*sem):
    return pltpu.CompilerParams(dimension_semantics=sem, vmem_limit_bytes=VMEM_LIMIT_BYTES)


def _dot(a, b):
    return jnp.dot(a, b, preferred_element_type=F32)


def _dot_nt(a, b):
    return lax.dot_general(a, b, (((1,), (1,)), ((), ())), preferred_element_type=F32)


def _dot_tn(a, b):
    return lax.dot_general(a, b, (((0,), (0,)), ((), ())), preferred_element_type=F32)


def _rms_rows(x, g):
    ms = jnp.mean(x * x, axis=-1, keepdims=True)
    return (x * lax.rsqrt(ms + EPS)) * g


def _row_chunk(rows, limit=256):
    for c in (256, 128, 64, 32, 16, 8):
        if c <= limit and rows % c == 0:
            return c
    return rows


NORM_ROWS = 128


def _norm_into(x_ref, g_ref, xn_ref):
    rows = x_ref.shape[0]
    ch = _row_chunk(rows)
    g = g_ref[...]

    def body(c, carry):
        r = pl.ds(pl.multiple_of(c * ch, ch), ch)
        xn_ref[r, :] = _rms_rows(x_ref[r, :], g).astype(xn_ref.dtype)
        return carry

    lax.fori_loop(0, rows // ch, body, 0)


def _norm_matmul_kernel(x_ref, g_ref, w_ref, o_ref, oc_ref, xn_ref):
    j = pl.program_id(1)

    @pl.when(j == 0)
    def _():
        _norm_into(x_ref, g_ref, xn_ref)

    o_ref[...] = _dot(xn_ref[...], w_ref[...].astype(BF16)).astype(o_ref.dtype)
    nchunk = X_HD // LANES
    pitch = nchunk * X_HEADS
    heads_per_tile = o_ref.shape[1] // X_HD
    for b in range(oc_ref.shape[0]):
        for hh in range(heads_per_tile):
            for c in range(nchunk):
                col = hh * X_HD + c * LANES
                row = c * X_HEADS + j * heads_per_tile + hh
                oc_ref[b, pl.ds(row, MEM_LEN, stride=pitch), :] = (
                    o_ref[b * MEM_LEN:(b + 1) * MEM_LEN, col:col + LANES])


def _in_proj_kernel(x_ref, g_ref, w_ref, wg1_ref, wg2_ref, bg_ref, o_ref, lf_ref, xn_ref):
    @pl.when(pl.program_id(1) == 0)
    def _():
        _norm_into(x_ref, g_ref, xn_ref)
        g1 = _dot(xn_ref[...], wg1_ref[...])
        z = _dot(g1.astype(BF16), wg2_ref[...]) + bg_ref[...]
        lf_ref[...] = (jnp.minimum(z, 0.0) - jnp.log1p(jnp.exp(-jnp.abs(z)))) * (1.0 / GLA_TAU)

    o_ref[...] = _dot(xn_ref[...], w_ref[...]).astype(o_ref.dtype)


def _mem_proj(mem, gain, w, tn):
    m, d = mem.shape
    n = w.shape[1]
    nb = m // MEM_LEN
    nchunk = X_HD // LANES
    rows = MEM_LEN * nchunk * X_HEADS
    flat, stored = pl.pallas_call(
        _norm_matmul_kernel,
        grid=(1, n // tn),
        in_specs=[pl.BlockSpec((m, d), lambda i, j: (0, 0)),
                  pl.BlockSpec((1, d), lambda i, j: (0, 0)),
                  pl.BlockSpec((d, tn), lambda i, j: (0, j))],
        out_specs=[pl.BlockSpec((m, tn), lambda i, j: (0, j)),
                   pl.BlockSpec((nb, rows, LANES), lambda i, j: (0, 0, 0))],
        out_shape=[jax.ShapeDtypeStruct((m, n), F32),
                   jax.ShapeDtypeStruct((nb, rows, LANES), F32)],
        scratch_shapes=[pltpu.VMEM((m, d), BF16)],
        compiler_params=_params("arbitrary", "arbitrary"),
        name="norm_matmul",
    )(mem, gain, w)
    cache = stored.reshape(nb, MEM_LEN, nchunk, X_HEADS, LANES).transpose(0, 1, 3, 2, 4)
    return flat, cache.reshape(nb, MEM_LEN, X_HEADS, X_HD)


def _in_proj(x, gain, w_in, wg1, wg2, b_gate, tm, tn):
    m, d = x.shape
    n = N_MAIN
    ng = wg2.shape[1]
    return pl.pallas_call(
        _in_proj_kernel,
        grid=(m // tm, n // tn),
        in_specs=[pl.BlockSpec((tm, d), lambda i, j: (i, 0)),
                  pl.BlockSpec((1, d), lambda i, j: (0, 0)),
                  pl.BlockSpec((d, tn), lambda i, j: (0, j)),
                  pl.BlockSpec((d, LANES), lambda i, j: (0, 0)),
                  pl.BlockSpec((LANES, ng), lambda i, j: (0, 0)),
                  pl.BlockSpec((1, ng), lambda i, j: (0, 0))],
        out_specs=[pl.BlockSpec((tm, tn), lambda i, j: (i, j)),
                   pl.BlockSpec((tm, ng), lambda i, j: (i, 0))],
        out_shape=[jax.ShapeDtypeStruct((m, n), F32),
                   jax.ShapeDtypeStruct((m, ng), F32)],
        scratch_shapes=[pltpu.VMEM((tm, d), BF16)],
        compiler_params=_params("arbitrary", "arbitrary"),
        name="in_proj",
    )(x, gain, w_in, wg1, wg2, b_gate)


def _cast_into(src_ref, dst_ref):
    rows = src_ref.shape[0]
    ch = _row_chunk(rows)

    def body(c, carry):
        r = pl.ds(pl.multiple_of(c * ch, ch), ch)
        dst_ref[r, :] = src_ref[r, :].astype(dst_ref.dtype)
        return carry

    lax.fori_loop(0, rows // ch, body, 0)


def _matmul_kernel(a_ref, w_ref, o_ref, wb_ref):
    @pl.when(pl.program_id(0) == 0)
    def _():
        _cast_into(w_ref, wb_ref)

    o_ref[...] = _dot(a_ref[...], wb_ref[...]).astype(o_ref.dtype)


def _matmul(a, w, out_dtype, tm):
    m, k = a.shape
    n = w.shape[1]
    return pl.pallas_call(
        _matmul_kernel,
        grid=(m // tm,),
        in_specs=[pl.BlockSpec((tm, k), lambda i: (i, 0)),
                  pl.BlockSpec((k, n), lambda i: (0, 0), pipeline_mode=pl.Buffered(1))],
        out_specs=pl.BlockSpec((tm, n), lambda i: (i, 0)),
        out_shape=jax.ShapeDtypeStruct((m, n), out_dtype),
        scratch_shapes=[pltpu.VMEM((k, n), BF16)],
        compiler_params=_params("arbitrary"),
        name="matmul",
    )(a, w)


def _halo_rows(shift):
    return max(2 * shift, SUBLANES)


def _conv_stage(buf_ref, prev, shift):
    halo = _halo_rows(shift)
    buf_ref[halo - 2 * shift:halo, :] = prev


def _conv_taps(buf_ref, r0, rows, shift):
    halo = _halo_rows(shift)

    def aligned(start):
        return start if isinstance(start, int) else pl.multiple_of(start, SUBLANES)

    if shift % SUBLANES == 0:
        s1 = buf_ref[pl.ds(aligned(r0 + (halo - shift)), rows), :]
        s2 = buf_ref[pl.ds(aligned(r0 + (halo - 2 * shift)), rows), :]
        return s2, s1
    win = buf_ref[pl.ds(aligned(r0 + (halo - SUBLANES)), rows + SUBLANES), :]
    s1 = pltpu.roll(win, shift, 0)[SUBLANES:, :]
    s2 = pltpu.roll(win, 2 * shift, 0)[SUBLANES:, :]
    return s2, s1


def _conv_gate_kernel(bg_ref, cg_ref, vc_ref, prev_ref, w_ref, o_ref, new_ref, buf_ref, *, shift):
    rows = cg_ref.shape[0]
    halo = _halo_rows(shift)
    u = cg_ref[...] * vc_ref[...]
    _conv_stage(buf_ref, prev_ref[0], shift)
    buf_ref[halo:halo + rows, :] = u
    s2, s1 = _conv_taps(buf_ref, 0, rows, shift)
    y = w_ref[0:1, :] * s2 + w_ref[1:2, :] * s1 + w_ref[2:3, :] * u
    o_ref[...] = (bg_ref[...] * y).astype(o_ref.dtype)
    new_ref[0] = buf_ref[halo + rows - 2 * shift:halo + rows, :]


def _conv_gate(p, prev, conv_w, nseq, rows, shift, tn):
    nj = D_CONV // tn
    kern = functools.partial(_conv_gate_kernel, shift=shift)
    return pl.pallas_call(
        kern,
        grid=(nseq, nj),
        in_specs=[pl.BlockSpec((rows, tn), lambda b, j: (b, j)),
                  pl.BlockSpec((rows, tn), lambda b, j: (b, j + nj)),
                  pl.BlockSpec((rows, tn), lambda b, j: (b, j + 2 * nj)),
                  pl.BlockSpec((1, 2 * shift, tn), lambda b, j: (b, 0, j)),
                  pl.BlockSpec((CONV_WIDTH, tn), lambda b, j: (0, j))],
        out_specs=[pl.BlockSpec((rows, tn), lambda b, j: (b, j)),
                   pl.BlockSpec((1, 2 * shift, tn), lambda b, j: (b, 0, j))],
        out_shape=[jax.ShapeDtypeStruct((nseq * rows, D_CONV), BF16),
                   jax.ShapeDtypeStruct((nseq, 2 * shift, D_CONV), F32)],
        scratch_shapes=[pltpu.VMEM((_halo_rows(shift) + rows, tn), F32)],
        compiler_params=_params("arbitrary", "arbitrary"),
        name="conv_gate",
    )(p, p, p, prev, conv_w)


def _cumsum_rows(g):
    c = g.shape[0]
    row = lax.broadcasted_iota(jnp.int32, g.shape, 0)
    x = g
    s = 1
    while s < c:
        x = x + jnp.where(row >= s, pltpu.roll(x, s, 0), 0.0)
        s *= 2
    return x


def _bcast_block_row(x, s, k):
    c, lanes = x.shape
    if s == c:
        return jnp.broadcast_to(x[k:k + 1, :], x.shape)
    if s >= SUBLANES:
        y = x.reshape(c // s, s, lanes)
        return jnp.broadcast_to(y[:, k:k + 1, :], y.shape).reshape(c, lanes)
    y = x.reshape(c // SUBLANES, SUBLANES, lanes)
    sub = lax.broadcasted_iota(jnp.int32, y.shape, 1)
    out = None
    for blk in range(SUBLANES // s):
        src = jnp.broadcast_to(y[:, blk * s + k:blk * s + k + 1, :], y.shape)
        out = src if out is None else jnp.where(sub >= blk * s, src, out)
    return out.reshape(c, lanes)


def _gla_pair_masks(c):
    ri = lax.broadcasted_iota(jnp.int32, (c, c), 0)
    ci = lax.broadcasted_iota(jnp.int32, (c, c), 1)
    diff_bits = ri ^ ci
    masks = [diff_bits == 0]
    level = 0
    while (1 << level) < c:
        masks.append(((diff_bits >> level) == 1) & (((ri >> level) & 1) == 1))
        level += 1
    return masks


def _gla_chunk(q, k, v, g, s_prev, masks):
    c = q.shape[0]
    cum = _cumsum_rows(g)
    a = jnp.where(masks[0], _dot_nt(q.astype(BF16), k.astype(BF16)), 0.0)
    for level in range(len(masks) - 1):
        half = 1 << level
        ref = _bcast_block_row(cum, 2 * half, half - 1)
        qe = q * jnp.exp(jnp.minimum(cum - ref, 0.0))
        ke = k * jnp.exp(jnp.minimum(ref - cum, 0.0))
        a = a + jnp.where(masks[1 + level], _dot_nt(qe.astype(BF16), ke.astype(BF16)), 0.0)
    o = _dot(a.astype(BF16), v.astype(BF16)) + _dot((q * jnp.exp(cum)).astype(BF16), s_prev.astype(BF16))
    last = cum[c - 1:c, :]
    kd = k * jnp.exp(last - cum)
    dk = last.shape[1]
    decay_t = jnp.transpose(jnp.broadcast_to(jnp.exp(last), (dk, dk)))
    decayed = jnp.concatenate([decay_t * s_prev[:, i:i + dk] for i in range(0, s_prev.shape[1], dk)], axis=1)
    s_new = decayed + _dot_tn(kd.astype(BF16), v.astype(BF16))
    return o, s_new


def _gla_kernel(q_ref, k_ref, v_ref, r_ref, g_ref, s0_ref, gn_ref, o_ref, sn_ref, *, chunk, single_chunk, group):
    bb, rows = q_ref.shape[0], q_ref.shape[1]
    nchunk = rows // chunk
    state_in = s0_ref if single_chunk else sn_ref

    if not single_chunk:
        @pl.when(pl.program_id(1) == 0)
        def _():
            sn_ref[...] = s0_ref[...]

    masks = _gla_pair_masks(chunk)

    def one(b, r):
        for h in range(GLA_HEADS):
            kc = slice(h * GLA_DK, (h + 1) * GLA_DK)
            vc = slice(h * GLA_DV, (h + 1) * GLA_DV)
            q = q_ref[b, r, kc] * (GLA_DK ** -0.5)
            o, s_new = _gla_chunk(q, k_ref[b, r, kc], v_ref[b, r, vc], g_ref[b, r, kc], state_in[b, h], masks)
            sn_ref[b, h] = s_new
            rr = r_ref[b, r, vc]
            o_ref[b, r, vc] = (_rms_rows(o, gn_ref[:, vc]) * (rr * jax.nn.sigmoid(rr))).astype(o_ref.dtype)

    def body(n, carry):
        r = pl.ds(pl.multiple_of((n % nchunk) * chunk, chunk), chunk)
        for u in range(group):
            one((n // nchunk) * group + u, r)
        return carry

    lax.fori_loop(0, (bb // group) * nchunk, body, 0)


def _gla(p, logf, s0, gla_norm, *, q_blk, k_blk, v_blk, r_blk, bb, rows, chunk, group):
    nb, t, _ = p.shape
    nk, nv = GLA_HEADS * GLA_DK, GLA_HEADS * GLA_DV
    kern = functools.partial(_gla_kernel, chunk=chunk, single_chunk=(t == chunk), group=group)
    state_spec = pl.BlockSpec((bb, GLA_HEADS, GLA_DK, GLA_DV), lambda b, c: (b, 0, 0, 0))
    return pl.pallas_call(
        kern,
        grid=(nb // bb, t // rows),
        in_specs=[pl.BlockSpec((bb, rows, nk), lambda b, c: (b, c, q_blk)),
                  pl.BlockSpec((bb, rows, nk), lambda b, c: (b, c, k_blk)),
                  pl.BlockSpec((bb, rows, nv), lambda b, c: (b, c, v_blk)),
                  pl.BlockSpec((bb, rows, nv), lambda b, c: (b, c, r_blk)),
                  pl.BlockSpec((bb, rows, nk), lambda b, c: (b, c, 0)),
                  state_spec,
                  pl.BlockSpec((1, nv), lambda b, c: (0, 0))],
        out_specs=[pl.BlockSpec((bb, rows, nv), lambda b, c: (b, c, 0)),
                   state_spec],
        out_shape=[jax.ShapeDtypeStruct((nb, t, D_GLA), BF16),
                   jax.ShapeDtypeStruct((nb, GLA_HEADS, GLA_DK, GLA_DV), F32)],
        compiler_params=_params("arbitrary", "arbitrary"),
        name="gla",
    )(p, p, p, p, logf, s0, gla_norm)


def _proj_res_norm_kernel(*refs, n_a, nk, final):
    a_refs, (w_ref, res_ref, g_ref) = refs[:n_a], refs[n_a:n_a + 3]
    out_refs, acc_ref = refs[n_a + 3:-1], refs[-1]
    k = pl.program_id(1)
    part, r0 = None, 0
    for a_ref in a_refs:
        kw = a_ref.shape[1]
        term = _dot(a_ref[...], w_ref[r0:r0 + kw, :])
        part = term if part is None else part + term
        r0 += kw

    @pl.when(k == 0)
    def _():
        acc_ref[...] = part

    if nk > 1:
        @pl.when(k > 0)
        def _():
            acc_ref[...] += part

    @pl.when(k == nk - 1)
    def _():
        rows = acc_ref.shape[0]
        ch = _row_chunk(rows, NORM_ROWS)
        g = g_ref[...]

        def body(c, carry):
            r = pl.ds(pl.multiple_of(c * ch, ch), ch)
            h = res_ref[r, :] + acc_ref[r, :]
            hn = _rms_rows(h, g)
            if final:
                out_refs[0][r, :] = hn
            else:
                out_refs[0][r, :] = h
                out_refs[1][r, :] = hn.astype(out_refs[1].dtype)
            return carry

        lax.fori_loop(0, rows // ch, body, 0)


def _proj_res_norm(a_list, w, res, gain, *, tm, tk, final):
    m = a_list[0].shape[0]
    kdim, d = w.shape
    nk = kdim // tk
    assert len(a_list) == 1 or nk == 1
    kern = functools.partial(_proj_res_norm_kernel, n_a=len(a_list), nk=nk, final=final)
    a_specs = ([pl.BlockSpec((tm, tk), lambda i, k: (i, k))] if len(a_list) == 1 else
               [pl.BlockSpec((tm, a.shape[1]), lambda i, k: (i, 0)) for a in a_list])
    row_spec = pl.BlockSpec((tm, d), lambda i, k: (i, 0))
    if final:
        out_specs = row_spec
        out_shape = jax.ShapeDtypeStruct((m, d), F32)
    else:
        out_specs = [row_spec, row_spec]
        out_shape = [jax.ShapeDtypeStruct((m, d), F32), jax.ShapeDtypeStruct((m, d), BF16)]
    return pl.pallas_call(
        kern,
        grid=(m // tm, nk),
        in_specs=a_specs + [pl.BlockSpec((tk, d), lambda i, k: (k, 0),
                                         pipeline_mode=pl.Buffered(1 if nk == 1 else 2)),
                            row_spec,
                            pl.BlockSpec((1, d), lambda i, k: (0, 0))],
        out_specs=out_specs,
        out_shape=out_shape,
        scratch_shapes=[pltpu.VMEM((tm, d), F32)],
        compiler_params=_params("arbitrary", "arbitrary"),
        name="proj_res_norm",
    )(*a_list, w, res, gain)


def _xattn_kernel(q_ref, k_ref, v_ref, o_ref):
    bb = q_ref.shape[0]

    def body(b, carry):
        q = q_ref[b].astype(BF16)
        p = _softmax_rows(_dot_nt(q, k_ref[b].astype(BF16)) * (X_HD ** -0.5))
        o_ref[b] = _dot(p.astype(BF16), v_ref[b].astype(BF16)).astype(o_ref.dtype)
        return carry

    lax.fori_loop(0, bb, body, 0)


def _xattn(q, mk, mv, *, bb, tq):
    nb, t, d = q.shape
    return pl.pallas_call(
        _xattn_kernel,
        grid=(nb // bb, X_HEADS, t // tq),
        in_specs=[pl.BlockSpec((bb, tq, X_HD), lambda b, h, i: (b, i, h)),
                  pl.BlockSpec((bb, MEM_LEN, X_HD), lambda b, h, i: (b, 0, h)),
                  pl.BlockSpec((bb, MEM_LEN, X_HD), lambda b, h, i: (b, 0, h))],
        out_specs=pl.BlockSpec((bb, tq, X_HD), lambda b, h, i: (b, i, h)),
        out_shape=jax.ShapeDtypeStruct((nb, t, d), BF16),
        compiler_params=_params("arbitrary", "arbitrary", "arbitrary"),
        name="xattn",
    )(q, mk, mv)


def _softmax_rows(s):
    s = s - jnp.max(s, axis=-1, keepdims=True)
    e = jnp.exp(s)
    return e / jnp.sum(e, axis=-1, keepdims=True)


def _xattn_cache_kernel(q_ref, k_ref, v_ref, o_ref):
    bb, tq = q_ref.shape[0], q_ref.shape[1]
    nchunk = X_HD // LANES
    pitch = nchunk * X_HEADS

    def gather(ref, b, h):
        parts = [ref[b, pl.ds(c * X_HEADS + h, MEM_LEN, stride=pitch), :] for c in range(nchunk)]
        return jnp.concatenate(parts, axis=1).astype(BF16)

    pairs = [(b, h) for b in range(bb) for h in range(X_HEADS)]
    scores = [_dot_nt(q_ref[b, :, h * X_HD:(h + 1) * X_HD].astype(BF16), gather(k_ref, b, h)) for b, h in pairs]
    p = _softmax_rows(jnp.concatenate(scores, axis=0) * (X_HD ** -0.5)).astype(BF16)
    for n, (b, h) in enumerate(pairs):
        o_ref[b, :, h * X_HD:(h + 1) * X_HD] = _dot(p[n * tq:(n + 1) * tq], gather(v_ref, b, h)).astype(o_ref.dtype)


def _xattn_cache(q, ck, cv, *, bb):
    nb, tq, d = q.shape
    nchunk = X_HD // LANES

    def stored_order(c):
        c = c.reshape(nb, MEM_LEN, X_HEADS, nchunk, LANES).transpose(0, 1, 3, 2, 4)
        return c.reshape(nb, MEM_LEN * nchunk * X_HEADS, LANES)

    rows = MEM_LEN * nchunk * X_HEADS
    return pl.pallas_call(
        _xattn_cache_kernel,
        grid=(nb // bb,),
        in_specs=[pl.BlockSpec((bb, tq, d), lambda b: (b, 0, 0)),
                  pl.BlockSpec((bb, rows, LANES), lambda b: (b, 0, 0)),
                  pl.BlockSpec((bb, rows, LANES), lambda b: (b, 0, 0))],
        out_specs=pl.BlockSpec((bb, tq, d), lambda b: (b, 0, 0)),
        out_shape=jax.ShapeDtypeStruct((nb, tq, d), BF16),
        compiler_params=_params("arbitrary"),
        name="xattn_cache",
    )(q, stored_order(ck), stored_order(cv))


def _ffn_up_kernel(hn_ref, wg_ref, wu_ref, cw_ref, cb_ref, prev_ref, o_ref, new_ref, buf_ref, up_ref,
                   wgb_ref, wub_ref, *, shift):
    rows = hn_ref.shape[0]
    halo = _halo_rows(shift)
    ch = _row_chunk(rows)

    @pl.when(pl.program_id(1) == 0)
    def _():
        _cast_into(wg_ref, wgb_ref)
        _cast_into(wu_ref, wub_ref)

    _conv_stage(buf_ref, prev_ref[0], shift)
    buf_ref[halo:halo + rows, :] = _dot(hn_ref[...], wgb_ref[...])
    up_ref[...] = _dot(hn_ref[...], wub_ref[...])
    w0, w1, w2, cb = cw_ref[0:1, :], cw_ref[1:2, :], cw_ref[2:3, :], cb_ref[...]

    def act_body(c, carry):
        r0 = pl.multiple_of(c * ch, ch)
        s2, s1 = _conv_taps(buf_ref, r0, ch, shift)
        gc = w0 * s2 + w1 * s1 + w2 * buf_ref[pl.ds(pl.multiple_of(halo + r0, SUBLANES), ch), :] + cb
        o_ref[pl.ds(r0, ch), :] = ((gc * jax.nn.sigmoid(gc)) * up_ref[pl.ds(r0, ch), :]).astype(o_ref.dtype)
        return carry

    lax.fori_loop(0, rows // ch, act_body, 0)
    new_ref[0] = buf_ref[halo + rows - 2 * shift:halo + rows, :]


def _ffn_up(hn, wg, wu, cw, cb, prev, nseq, rows, shift, tn):
    d = hn.shape[1]
    kern = functools.partial(_ffn_up_kernel, shift=shift)
    return pl.pallas_call(
        kern,
        grid=(D_FF // tn, nseq),
        in_specs=[pl.BlockSpec((rows, d), lambda j, b: (b, 0)),
                  pl.BlockSpec((d, tn), lambda j, b: (0, j)),
                  pl.BlockSpec((d, tn), lambda j, b: (0, j)),
                  pl.BlockSpec((CONV_WIDTH, tn), lambda j, b: (0, j)),
                  pl.BlockSpec((1, tn), lambda j, b: (0, j)),
                  pl.BlockSpec((1, 2 * shift, tn), lambda j, b: (b, 0, j))],
        out_specs=[pl.BlockSpec((rows, tn), lambda j, b: (b, j)),
                   pl.BlockSpec((1, 2 * shift, tn), lambda j, b: (b, 0, j))],
        out_shape=[jax.ShapeDtypeStruct((nseq * rows, D_FF), BF16),
                   jax.ShapeDtypeStruct((nseq, 2 * shift, D_FF), F32)],
        scratch_shapes=[pltpu.VMEM((_halo_rows(shift) + rows, tn), F32), pltpu.VMEM((rows, tn), F32),
                        pltpu.VMEM((d, tn), BF16), pltpu.VMEM((d, tn), BF16)],
        compiler_params=_params("arbitrary", "arbitrary"),
        name="ffn_up",
    )(hn, wg, wu, cw, cb, prev)


def _layer(x, w, *, nseq, rows, shift, prev_conv, s0, prev_ffn, mk, mv, time_major):
    m = x.shape[0]
    tm = min(m, 1024)
    p, logf = _in_proj(x, w["norm_mix"], w["w_in"], w["w_g1"], w["w_g2"], w["b_gate"], tm, 1024)
    conv_out, conv_new = _conv_gate(p, prev_conv, w["conv_w"], nseq, rows, shift, 256)

    if time_major:
        nt = m // shift
        pad = ((0, 0), (0, SUBLANES - nt), (0, 0))
        pg = jnp.pad(p.reshape(nt, shift, N_MAIN)[:, :, 3 * D_CONV:].transpose(1, 0, 2), pad)
        lg = jnp.pad(logf.reshape(nt, shift, -1).transpose(1, 0, 2), pad)
        o, s_new = _gla(pg, lg, s0, w["gla_norm"], q_blk=0, k_blk=1, v_blk=1, r_blk=2,
                        bb=8, rows=SUBLANES, chunk=SUBLANES, group=4)
        gla_out = o[:, :nt].transpose(1, 0, 2).reshape(m, D_GLA)
    else:
        q0 = 3 * D_CONV // (GLA_HEADS * GLA_DK)
        v0 = (3 * D_CONV + 2 * GLA_HEADS * GLA_DK) // (GLA_HEADS * GLA_DV)
        o, s_new = _gla(p.reshape(nseq, rows, N_MAIN), logf.reshape(nseq, rows, -1), s0, w["gla_norm"],
                        q_blk=q0, k_blk=q0 + 1, v_blk=v0, r_blk=v0 + 1,
                        bb=2, rows=512, chunk=GLA_CHUNK, group=2)
        gla_out = o.reshape(m, D_GLA)

    tm2 = min(m, 512)
    h, hn = _proj_res_norm([conv_out, gla_out], w["w_out"], x, w["norm_x"], tm=tm2, tk=D_MODEL, final=False)
    qx = _matmul(hn, w["w_xq"], BF16, tm)

    if time_major:
        nt = m // shift
        qb = jnp.pad(qx.astype(F32).reshape(nt, shift, D_MODEL).transpose(1, 0, 2),
                     ((0, 0), (0, SUBLANES - nt), (0, 0)))
        ob = _xattn_cache(qb, mk, mv, bb=4)
        attn = ob[:, :nt].transpose(1, 0, 2).reshape(m, D_MODEL)
    else:
        attn = _xattn(qx.reshape(nseq, rows, D_MODEL), mk, mv, bb=1, tq=rows).reshape(m, D_MODEL)

    h2, hn2 = _proj_res_norm([attn], w["w_xo"], h, w["norm_ffn"], tm=tm2, tk=D_MODEL, final=False)
    act, ffn_new = _ffn_up(hn2, w["w_fg"], w["w_fu"], w["ffn_conv_w"], w["ffn_conv_b"], prev_ffn,
                           nseq, rows, shift, 512)
    return act, h2, conv_new, s_new, ffn_new


def kernel(x_prompt, x_sample, mem_prompt, cache_conv, state_gla, cache_ffn, cache_mem_k, cache_mem_v,
           norm_mix, w_in, conv_w, w_gate2, b_gate, gla_norm, w_out, norm_x, norm_mem, w_xq, w_xk, w_xv,
           w_xo, norm_ffn, w_ffn_gate, w_ffn_up, ffn_conv_w, ffn_conv_b, w_ffn_down, norm_final):
    depth = w_in.shape[0]
    nb, seq, d = x_prompt.shape
    db, dseq, _ = x_sample.shape
    hp = x_prompt.reshape(nb * seq, d)
    hs = x_sample.transpose(1, 0, 2).reshape(dseq * db, d)
    outs = {k: [] for k in ("conv_p", "gla_p", "ffn_p", "mk", "mv", "conv_s", "gla_s", "ffn_s")}
    nfinal = norm_final.reshape(1, d)
    yp = ys = None
    for l in range(depth):
        w = {
            "norm_mix": norm_mix[l].reshape(1, d),
            "w_in": w_in[l].astype(BF16),
            "w_g1": jnp.pad(w_in[l][:, N_MAIN:], ((0, 0), (0, LANES - GLA_RANK))).astype(BF16),
            "w_g2": jnp.pad(w_gate2[l], ((0, LANES - GLA_RANK), (0, 0))).astype(BF16),
            "b_gate": b_gate[l].reshape(1, -1),
            "conv_w": conv_w[l],
            "gla_norm": gla_norm[l].reshape(1, -1),
            "w_out": w_out[l].astype(BF16),
            "norm_x": norm_x[l].reshape(1, d),
            "w_xq": w_xq[l],
            "w_xo": w_xo[l].astype(BF16),
            "norm_ffn": norm_ffn[l].reshape(1, d),
            "w_fg": w_ffn_gate[l],
            "w_fu": w_ffn_up[l],
            "ffn_conv_w": ffn_conv_w[l],
            "ffn_conv_b": ffn_conv_b[l].reshape(1, -1),
        }
        w_fd = w_ffn_down[l].astype(BF16)
        last = l == depth - 1
        gain_next = nfinal if last else None

        mem = mem_prompt.reshape(nb * MEM_LEN, d)
        nmem = norm_mem[l].reshape(1, d)
        mk, mk_cache = _mem_proj(mem, nmem, w_xk[l], 1024)
        mv, mv_cache = _mem_proj(mem, nmem, w_xv[l], 1024)
        act, h2, c1, s1, f1 = _layer(
            hp, w, nseq=nb, rows=seq, shift=1,
            prev_conv=jnp.zeros((nb, CONV_WIDTH - 1, D_CONV), F32),
            s0=jnp.zeros((nb, GLA_HEADS, GLA_DK, GLA_DV), F32),
            prev_ffn=jnp.zeros((nb, CONV_WIDTH - 1, D_FF), F32),
            mk=mk.reshape(nb, MEM_LEN, d), mv=mv.reshape(nb, MEM_LEN, d), time_major=False)
        assert last, "only the final layer's epilogue (final rmsnorm) is implemented"
        yp = _proj_res_norm([act], w_fd, h2, gain_next, tm=256, tk=D_FF, final=True)
        outs["conv_p"].append(c1)
        outs["gla_p"].append(s1)
        outs["ffn_p"].append(f1)
        outs["mk"].append(mk_cache)
        outs["mv"].append(mv_cache)

        def tmajor(c):
            return c.transpose(1, 0, 2).reshape(1, (CONV_WIDTH - 1) * db, c.shape[-1])

        act, h2, c2, s2, f2 = _layer(
            hs, w, nseq=1, rows=dseq * db, shift=db,
            prev_conv=tmajor(cache_conv[l]), s0=state_gla[l], prev_ffn=tmajor(cache_ffn[l]),
            mk=cache_mem_k[l], mv=cache_mem_v[l], time_major=True)
        ys = _proj_res_norm([act], w_fd, h2, gain_next, tm=256, tk=D_FF, final=True)
        outs["conv_s"].append(c2.reshape(CONV_WIDTH - 1, db, D_CONV).transpose(1, 0, 2))
        outs["gla_s"].append(s2)
        outs["ffn_s"].append(f2.reshape(CONV_WIDTH - 1, db, D_FF).transpose(1, 0, 2))

    y_prompt = yp.reshape(nb, seq, d)
    y_sample = ys.reshape(dseq, db, d).transpose(1, 0, 2)
    st = lambda k: jnp.stack(outs[k])
    return (y_prompt, y_sample, st("conv_p"), st("gla_p"), st("ffn_p"), st("mk"), st("mv"),
            st("conv_s"), st("gla_s"), st("ffn_s"))
```

```python
import functools

import jax
import jax.numpy as jnp
from jax import lax
from jax.experimental import pallas as pl
from jax.experimental.pallas import tpu as pltpu

F32 = jnp.float32
BF16 = jnp.bfloat16

D_MODEL = 2048
EPS = 1e-6
CONV_WIDTH = 3
D_CONV = 1024
D_GLA = 1024
GLA_HEADS = 4
GLA_DV = 256
GLA_DK = 128
GLA_RANK = 16
GLA_TAU = 16.0
GLA_CHUNK = 64
X_HEADS = 4
X_HD = 512
MEM_LEN = 256
D_FF = 5632
N_MAIN = 3 * D_CONV + 2 * GLA_HEADS * GLA_DK + 2 * GLA_HEADS * GLA_DV

LANES = 128
SUBLANES = 8
VMEM_LIMIT_BYTES = 56 * 1024 * 1024


def _params(*sem):
    return pltpu.CompilerParams(dimension_semantics=sem, vmem_limit_bytes=VMEM_LIMIT_BYTES)


def _dot(a, b):
    return jnp.dot(a, b, preferred_element_type=F32)


def _dot_nt(a, b):
    return lax.dot_general(a, b, (((1,), (1,)), ((), ())), preferred_element_type=F32)


def _dot_tn(a, b):
    return lax.dot_general(a, b, (((0,), (0,)), ((), ())), preferred_element_type=F32)


def _rms_rows(x, g):
    ms = jnp.mean(x * x, axis=-1, keepdims=True)
    return (x * lax.rsqrt(ms + EPS)) * g


def _row_chunk(rows, limit=256):
    for c in (256, 128, 64, 32, 16, 8):
        if c <= limit and rows % c == 0:
            return c
    return rows


NORM_ROWS = 128


def _norm_into(x_ref, g_ref, xn_ref):
    rows = x_ref.shape[0]
    ch = _row_chunk(rows)
    g = g_ref[...]

    def body(c, carry):
        r = pl.ds(pl.multiple_of(c * ch, ch), ch)
        xn_ref[r, :] = _rms_rows(x_ref[r, :], g).astype(xn_ref.dtype)
        return carry

    lax.fori_loop(0, rows // ch, body, 0)


def _norm_matmul_kernel(x_ref, g_ref, w_ref, o_ref, oc_ref, xn_ref):
    j = pl.program_id(1)

    @pl.when(j == 0)
    def _():
        _norm_into(x_ref, g_ref, xn_ref)

    o_ref[...] = _dot(xn_ref[...], w_ref[...].astype(BF16)).astype(o_ref.dtype)
    nchunk = X_HD // LANES
    pitch = nchunk * X_HEADS
    heads_per_tile = o_ref.shape[1] // X_HD
    for b in range(oc_ref.shape[0]):
        for hh in range(heads_per_tile):
            for c in range(nchunk):
                col = hh * X_HD + c * LANES
                row = c * X_HEADS + j * heads_per_tile + hh
                oc_ref[b, pl.ds(row, MEM_LEN, stride=pitch), :] = (
                    o_ref[b * MEM_LEN:(b + 1) * MEM_LEN, col:col + LANES])


def _in_proj_kernel(x_ref, g_ref, w_ref, wg1_ref, wg2_ref, bg_ref, o_ref, lf_ref, xn_ref):
    @pl.when(pl.program_id(1) == 0)
    def _():
        _norm_into(x_ref, g_ref, xn_ref)
        g1 = _dot(xn_ref[...], wg1_ref[...])
        z = _dot(g1.astype(BF16), wg2_ref[...]) + bg_ref[...]
        lf_ref[...] = (jnp.minimum(z, 0.0) - jnp.log1p(jnp.exp(-jnp.abs(z)))) * (1.0 / GLA_TAU)

    o_ref[...] = _dot(xn_ref[...], w_ref[...]).astype(o_ref.dtype)


def _mem_proj(mem, gain, w, tn):
    m, d = mem.shape
    n = w.shape[1]
    nb = m // MEM_LEN
    nchunk = X_HD // LANES
    rows = MEM_LEN * nchunk * X_HEADS
    flat, stored = pl.pallas_call(
        _norm_matmul_kernel,
        grid=(1, n // tn),
        in_specs=[pl.BlockSpec((m, d), lambda i, j: (0, 0)),
                  pl.BlockSpec((1, d), lambda i, j: (0, 0)),
                  pl.BlockSpec((d, tn), lambda i, j: (0, j))],
        out_specs=[pl.BlockSpec((m, tn), lambda i, j: (0, j)),
                   pl.BlockSpec((nb, rows, LANES), lambda i, j: (0, 0, 0))],
        out_shape=[jax.ShapeDtypeStruct((m, n), F32),
                   jax.ShapeDtypeStruct((nb, rows, LANES), F32)],
        scratch_shapes=[pltpu.VMEM((m, d), BF16)],
        compiler_params=_params("arbitrary", "arbitrary"),
        name="norm_matmul",
    )(mem, gain, w)
    cache = stored.reshape(nb, MEM_LEN, nchunk, X_HEADS, LANES).transpose(0, 1, 3, 2, 4)
    return flat, cache.reshape(nb, MEM_LEN, X_HEADS, X_HD)


def _in_proj(x, gain, w_in, wg1, wg2, b_gate, tm, tn):
    m, d = x.shape
    n = N_MAIN
    ng = wg2.shape[1]
    return pl.pallas_call(
        _in_proj_kernel,
        grid=(m // tm, n // tn),
        in_specs=[pl.BlockSpec((tm, d), lambda i, j: (i, 0)),
                  pl.BlockSpec((1, d), lambda i, j: (0, 0)),
                  pl.BlockSpec((d, tn), lambda i, j: (0, j)),
                  pl.BlockSpec((d, LANES), lambda i, j: (0, 0)),
                  pl.BlockSpec((LANES, ng), lambda i, j: (0, 0)),
                  pl.BlockSpec((1, ng), lambda i, j: (0, 0))],
        out_specs=[pl.BlockSpec((tm, tn), lambda i, j: (i, j)),
                   pl.BlockSpec((tm, ng), lambda i, j: (i, 0))],
        out_shape=[jax.ShapeDtypeStruct((m, n), F32),
                   jax.ShapeDtypeStruct((m, ng), F32)],
        scratch_shapes=[pltpu.VMEM((tm, d), BF16)],
        compiler_params=_params("arbitrary", "arbitrary"),
        name="in_proj",
    )(x, gain, w_in, wg1, wg2, b_gate)


def _cast_into(src_ref, dst_ref):
    rows = src_ref.shape[0]
    ch = _row_chunk(rows)

    def body(c, carry):
        r = pl.ds(pl.multiple_of(c * ch, ch), ch)
        dst_ref[r, :] = src_ref[r, :].astype(dst_ref.dtype)
        return carry

    lax.fori_loop(0, rows // ch, body, 0)


def _matmul_kernel(a_ref, w_ref, o_ref, wb_ref):
    @pl.when(pl.program_id(0) == 0)
    def _():
        _cast_into(w_ref, wb_ref)

    o_ref[...] = _dot(a_ref[...], wb_ref[...]).astype(o_ref.dtype)


def _matmul(a, w, out_dtype, tm):
    m, k = a.shape
    n = w.shape[1]
    return pl.pallas_call(
        _matmul_kernel,
        grid=(m // tm,),
        in_specs=[pl.BlockSpec((tm, k), lambda i: (i, 0)),
                  pl.BlockSpec((k, n), lambda i: (0, 0), pipeline_mode=pl.Buffered(1))],
        out_specs=pl.BlockSpec((tm, n), lambda i: (i, 0)),
        out_shape=jax.ShapeDtypeStruct((m, n), out_dtype),
        scratch_shapes=[pltpu.VMEM((k, n), BF16)],
        compiler_params=_params("arbitrary"),
        name="matmul",
    )(a, w)


def _halo_rows(shift):
    return max(2 * shift, SUBLANES)


def _conv_buf(rows, tn, shift):
    return pltpu.VMEM((tn // LANES, _halo_rows(shift) + rows, LANES), F32)


def _slab(s):
    return slice(s * LANES, (s + 1) * LANES)


def _conv_stage(buf_ref, row0, x):
    for s in range(buf_ref.shape[0]):
        buf_ref[s, row0:row0 + x.shape[0], :] = x[:, _slab(s)]


def _conv_taps(buf_ref, s, r0, rows, shift):
    halo = _halo_rows(shift)

    def back(steps):
        start = r0 + (halo - steps * shift)
        if (steps * shift) % SUBLANES == 0:
            return pl.ds(start if isinstance(start, int) else pl.multiple_of(start, SUBLANES), rows)
        return pl.ds(start, rows, stride=1)

    return buf_ref[s, back(2), :], buf_ref[s, back(1), :], buf_ref[s, back(0), :]


def _conv_last(buf_ref, new_ref, rows, shift):
    halo = _halo_rows(shift)
    for s in range(buf_ref.shape[0]):
        new_ref[0, :, _slab(s)] = buf_ref[s, halo + rows - 2 * shift:halo + rows, :]


def _conv_gate_kernel(bg_ref, cg_ref, vc_ref, prev_ref, w_ref, o_ref, new_ref, buf_ref, *, shift):
    rows = cg_ref.shape[0]
    halo = _halo_rows(shift)
    _conv_stage(buf_ref, halo - 2 * shift, prev_ref[0])
    _conv_stage(buf_ref, halo, cg_ref[...] * vc_ref[...])
    for s in range(buf_ref.shape[0]):
        u2, u1, u0 = _conv_taps(buf_ref, s, 0, rows, shift)
        y = w_ref[0:1, _slab(s)] * u2 + w_ref[1:2, _slab(s)] * u1 + w_ref[2:3, _slab(s)] * u0
        o_ref[:, _slab(s)] = (bg_ref[:, _slab(s)] * y).astype(o_ref.dtype)
    _conv_last(buf_ref, new_ref, rows, shift)


def _conv_gate(p, prev, conv_w, nseq, rows, shift, tn):
    nj = D_CONV // tn
    kern = functools.partial(_conv_gate_kernel, shift=shift)
    return pl.pallas_call(
        kern,
        grid=(nseq, nj),
        in_specs=[pl.BlockSpec((rows, tn), lambda b, j: (b, j)),
                  pl.BlockSpec((rows, tn), lambda b, j: (b, j + nj)),
                  pl.BlockSpec((rows, tn), lambda b, j: (b, j + 2 * nj)),
                  pl.BlockSpec((1, 2 * shift, tn), lambda b, j: (b, 0, j)),
                  pl.BlockSpec((CONV_WIDTH, tn), lambda b, j: (0, j))],
        out_specs=[pl.BlockSpec((rows, tn), lambda b, j: (b, j)),
                   pl.BlockSpec((1, 2 * shift, tn), lambda b, j: (b, 0, j))],
        out_shape=[jax.ShapeDtypeStruct((nseq * rows, D_CONV), BF16),
                   jax.ShapeDtypeStruct((nseq, 2 * shift, D_CONV), F32)],
        scratch_shapes=[_conv_buf(rows, tn, shift)],
        compiler_params=_params("arbitrary", "arbitrary"),
        name="conv_gate",
    )(p, p, p, prev, conv_w)


def _cumsum_rows(g):
    c = g.shape[0]
    row = lax.broadcasted_iota(jnp.int32, g.shape, 0)
    x = g
    s = 1
    while s < c:
        x = x + jnp.where(row >= s, pltpu.roll(x, s, 0), 0.0)
        s *= 2
    return x


def _bcast_block_row(x, s, k):
    c, lanes = x.shape
    if s == c:
        return jnp.broadcast_to(x[k:k + 1, :], x.shape)
    if s >= SUBLANES:
        y = x.reshape(c // s, s, lanes)
        return jnp.broadcast_to(y[:, k:k + 1, :], y.shape).reshape(c, lanes)
    y = x.reshape(c // SUBLANES, SUBLANES, lanes)
    sub = lax.broadcasted_iota(jnp.int32, y.shape, 1)
    out = None
    for blk in range(SUBLANES // s):
        src = jnp.broadcast_to(y[:, blk * s + k:blk * s + k + 1, :], y.shape)
        out = src if out is None else jnp.where(sub >= blk * s, src, out)
    return out.reshape(c, lanes)


def _gla_pair_masks(c):
    ri = lax.broadcasted_iota(jnp.int32, (c, c), 0)
    ci = lax.broadcasted_iota(jnp.int32, (c, c), 1)
    diff_bits = ri ^ ci
    masks = [diff_bits == 0]
    level = 0
    while (1 << level) < c:
        masks.append(((diff_bits >> level) == 1) & (((ri >> level) & 1) == 1))
        level += 1
    return masks


def _gla_chunk(q, k, v, g, s_prev, masks):
    c = q.shape[0]
    cum = _cumsum_rows(g)
    a = jnp.where(masks[0], _dot_nt(q.astype(BF16), k.astype(BF16)), 0.0)
    for level in range(len(masks) - 1):
        half = 1 << level
        ref = _bcast_block_row(cum, 2 * half, half - 1)
        qe = q * jnp.exp(jnp.minimum(cum - ref, 0.0))
        ke = k * jnp.exp(jnp.minimum(ref - cum, 0.0))
        a = a + jnp.where(masks[1 + level], _dot_nt(qe.astype(BF16), ke.astype(BF16)), 0.0)
    o = _dot(a.astype(BF16), v.astype(BF16)) + _dot((q * jnp.exp(cum)).astype(BF16), s_prev.astype(BF16))
    last = cum[c - 1:c, :]
    kd = k * jnp.exp(last - cum)
    dk = last.shape[1]
    decay_t = jnp.transpose(jnp.broadcast_to(jnp.exp(last), (dk, dk)))
    decayed = jnp.concatenate([decay_t * s_prev[:, i:i + dk] for i in range(0, s_prev.shape[1], dk)], axis=1)
    s_new = decayed + _dot_tn(kd.astype(BF16), v.astype(BF16))
    return o, s_new


def _gla_kernel(q_ref, k_ref, v_ref, r_ref, g_ref, s0_ref, gn_ref, o_ref, sn_ref, *, chunk, single_chunk, group):
    bb, rows = q_ref.shape[0], q_ref.shape[1]
    nchunk = rows // chunk
    state_in = s0_ref if single_chunk else sn_ref

    if not single_chunk:
        @pl.when(pl.program_id(1) == 0)
        def _():
            sn_ref[...] = s0_ref[...]

    masks = _gla_pair_masks(chunk)

    def one(b, r):
        for h in range(GLA_HEADS):
            kc = slice(h * GLA_DK, (h + 1) * GLA_DK)
            vc = slice(h * GLA_DV, (h + 1) * GLA_DV)
            q = q_ref[b, r, kc] * (GLA_DK ** -0.5)
            o, s_new = _gla_chunk(q, k_ref[b, r, kc], v_ref[b, r, vc], g_ref[b, r, kc], state_in[b, h], masks)
            sn_ref[b, h] = s_new
            rr = r_ref[b, r, vc]
            o_ref[b, r, vc] = (_rms_rows(o, gn_ref[:, vc]) * (rr * jax.nn.sigmoid(rr))).astype(o_ref.dtype)

    def body(n, carry):
        r = pl.ds(pl.multiple_of((n % nchunk) * chunk, chunk), chunk)
        for u in range(group):
            one((n // nchunk) * group + u, r)
        return carry

    lax.fori_loop(0, (bb // group) * nchunk, body, 0)


def _gla(p, logf, s0, gla_norm, *, q_blk, k_blk, v_blk, r_blk, bb, rows, chunk, group):
    nb, t, _ = p.shape
    nk, nv = GLA_HEADS * GLA_DK, GLA_HEADS * GLA_DV
    kern = functools.partial(_gla_kernel, chunk=chunk, single_chunk=(t == chunk), group=group)
    state_spec = pl.BlockSpec((bb, GLA_HEADS, GLA_DK, GLA_DV), lambda b, c: (b, 0, 0, 0))
    return pl.pallas_call(
        kern,
        grid=(nb // bb, t // rows),
        in_specs=[pl.BlockSpec((bb, rows, nk), lambda b, c: (b, c, q_blk)),
                  pl.BlockSpec((bb, rows, nk), lambda b, c: (b, c, k_blk)),
                  pl.BlockSpec((bb, rows, nv), lambda b, c: (b, c, v_blk)),
                  pl.BlockSpec((bb, rows, nv), lambda b, c: (b, c, r_blk)),
                  pl.BlockSpec((bb, rows, nk), lambda b, c: (b, c, 0)),
                  state_spec,
                  pl.BlockSpec((1, nv), lambda b, c: (0, 0))],
        out_specs=[pl.BlockSpec((bb, rows, nv), lambda b, c: (b, c, 0)),
                   state_spec],
        out_shape=[jax.ShapeDtypeStruct((nb, t, D_GLA), BF16),
                   jax.ShapeDtypeStruct((nb, GLA_HEADS, GLA_DK, GLA_DV), F32)],
        compiler_params=_params("arbitrary", "arbitrary"),
        name="gla",
    )(p, p, p, p, logf, s0, gla_norm)


def _proj_res_norm_kernel(*refs, n_a, nk, final):
    a_refs, (w_ref, res_ref, g_ref) = refs[:n_a], refs[n_a:n_a + 3]
    out_refs, acc_ref = refs[n_a + 3:-1], refs[-1]
    k = pl.program_id(1)
    part, r0 = None, 0
    for a_ref in a_refs:
        kw = a_ref.shape[1]
        term = _dot(a_ref[...], w_ref[r0:r0 + kw, :])
        part = term if part is None else part + term
        r0 += kw

    @pl.when(k == 0)
    def _():
        acc_ref[...] = part

    if nk > 1:
        @pl.when(k > 0)
        def _():
            acc_ref[...] += part

    @pl.when(k == nk - 1)
    def _():
        rows = acc_ref.shape[0]
        ch = _row_chunk(rows, NORM_ROWS)
        g = g_ref[...]

        def body(c, carry):
            r = pl.ds(pl.multiple_of(c * ch, ch), ch)
            h = res_ref[r, :] + acc_ref[r, :]
            hn = _rms_rows(h, g)
            if final:
                out_refs[0][r, :] = hn
            else:
                out_refs[0][r, :] = h
                out_refs[1][r, :] = hn.astype(out_refs[1].dtype)
            return carry

        lax.fori_loop(0, rows // ch, body, 0)


def _proj_res_norm(a_list, w, res, gain, *, tm, tk, final):
    m = a_list[0].shape[0]
    kdim, d = w.shape
    nk = kdim // tk
    assert len(a_list) == 1 or nk == 1
    kern = functools.partial(_proj_res_norm_kernel, n_a=len(a_list), nk=nk, final=final)
    a_specs = ([pl.BlockSpec((tm, tk), lambda i, k: (i, k))] if len(a_list) == 1 else
               [pl.BlockSpec((tm, a.shape[1]), lambda i, k: (i, 0)) for a in a_list])
    row_spec = pl.BlockSpec((tm, d), lambda i, k: (i, 0))
    if final:
        out_specs = row_spec
        out_shape = jax.ShapeDtypeStruct((m, d), F32)
    else:
        out_specs = [row_spec, row_spec]
        out_shape = [jax.ShapeDtypeStruct((m, d), F32), jax.ShapeDtypeStruct((m, d), BF16)]
    return pl.pallas_call(
        kern,
        grid=(m // tm, nk),
        in_specs=a_specs + [pl.BlockSpec((tk, d), lambda i, k: (k, 0),
                                         pipeline_mode=pl.Buffered(1 if nk == 1 else 2)),
                            row_spec,
                            pl.BlockSpec((1, d), lambda i, k: (0, 0))],
        out_specs=out_specs,
        out_shape=out_shape,
        scratch_shapes=[pltpu.VMEM((tm, d), F32)],
        compiler_params=_params("arbitrary", "arbitrary"),
        name="proj_res_norm",
    )(*a_list, w, res, gain)


def _xattn_kernel(q_ref, k_ref, v_ref, o_ref):
    bb = q_ref.shape[0]

    def body(b, carry):
        q = q_ref[b].astype(BF16)
        p = _softmax_rows(_dot_nt(q, k_ref[b].astype(BF16)) * (X_HD ** -0.5))
        o_ref[b] = _dot(p.astype(BF16), v_ref[b].astype(BF16)).astype(o_ref.dtype)
        return carry

    lax.fori_loop(0, bb, body, 0)


def _xattn(q, mk, mv, *, bb, tq):
    nb, t, d = q.shape
    return pl.pallas_call(
        _xattn_kernel,
        grid=(nb // bb, X_HEADS, t // tq),
        in_specs=[pl.BlockSpec((bb, tq, X_HD), lambda b, h, i: (b, i, h)),
                  pl.BlockSpec((bb, MEM_LEN, X_HD), lambda b, h, i: (b, 0, h)),
                  pl.BlockSpec((bb, MEM_LEN, X_HD), lambda b, h, i: (b, 0, h))],
        out_specs=pl.BlockSpec((bb, tq, X_HD), lambda b, h, i: (b, i, h)),
        out_shape=jax.ShapeDtypeStruct((nb, t, d), BF16),
        compiler_params=_params("arbitrary", "arbitrary", "arbitrary"),
        name="xattn",
    )(q, mk, mv)


def _softmax_rows(s):
    s = s - jnp.max(s, axis=-1, keepdims=True)
    e = jnp.exp(s)
    return e / jnp.sum(e, axis=-1, keepdims=True)


def _xattn_cache_kernel(q_ref, k_ref, v_ref, o_ref):
    bb, tq = q_ref.shape[0], q_ref.shape[1]
    nchunk = X_HD // LANES
    pitch = nchunk * X_HEADS

    def gather(ref, b, h):
        parts = [ref[b, pl.ds(c * X_HEADS + h, MEM_LEN, stride=pitch), :] for c in range(nchunk)]
        return jnp.concatenate(parts, axis=1).astype(BF16)

    pairs = [(b, h) for b in range(bb) for h in range(X_HEADS)]
    scores = [_dot_nt(q_ref[b, :, h * X_HD:(h + 1) * X_HD].astype(BF16), gather(k_ref, b, h)) for b, h in pairs]
    p = _softmax_rows(jnp.concatenate(scores, axis=0) * (X_HD ** -0.5)).astype(BF16)
    for n, (b, h) in enumerate(pairs):
        o_ref[b, :, h * X_HD:(h + 1) * X_HD] = _dot(p[n * tq:(n + 1) * tq], gather(v_ref, b, h)).astype(o_ref.dtype)


def _xattn_cache(q, ck, cv, *, bb):
    nb, tq, d = q.shape
    nchunk = X_HD // LANES

    def stored_order(c):
        c = c.reshape(nb, MEM_LEN, X_HEADS, nchunk, LANES).transpose(0, 1, 3, 2, 4)
        return c.reshape(nb, MEM_LEN * nchunk * X_HEADS, LANES)

    rows = MEM_LEN * nchunk * X_HEADS
    return pl.pallas_call(
        _xattn_cache_kernel,
        grid=(nb // bb,),
        in_specs=[pl.BlockSpec((bb, tq, d), lambda b: (b, 0, 0)),
                  pl.BlockSpec((bb, rows, LANES), lambda b: (b, 0, 0)),
                  pl.BlockSpec((bb, rows, LANES), lambda b: (b, 0, 0))],
        out_specs=pl.BlockSpec((bb, tq, d), lambda b: (b, 0, 0)),
        out_shape=jax.ShapeDtypeStruct((nb, tq, d), BF16),
        compiler_params=_params("arbitrary"),
        name="xattn_cache",
    )(q, stored_order(ck), stored_order(cv))


def _ffn_up_kernel(hn_ref, wg_ref, wu_ref, cw_ref, cb_ref, prev_ref, o_ref, new_ref, buf_ref, up_ref,
                   wgb_ref, wub_ref, *, shift):
    rows = hn_ref.shape[0]
    halo = _halo_rows(shift)
    ch = _row_chunk(rows)

    @pl.when(pl.program_id(1) == 0)
    def _():
        _cast_into(wg_ref, wgb_ref)
        _cast_into(wu_ref, wub_ref)

    _conv_stage(buf_ref, halo - 2 * shift, prev_ref[0])
    _conv_stage(buf_ref, halo, _dot(hn_ref[...], wgb_ref[...]))
    up_ref[...] = _dot(hn_ref[...], wub_ref[...])

    def act_body(c, carry):
        r0 = pl.multiple_of(c * ch, ch)
        r = pl.ds(r0, ch)
        for s in range(buf_ref.shape[0]):
            g2, g1, g0 = _conv_taps(buf_ref, s, r0, ch, shift)
            gc = (cw_ref[0:1, _slab(s)] * g2 + cw_ref[1:2, _slab(s)] * g1 + cw_ref[2:3, _slab(s)] * g0
                  + cb_ref[:, _slab(s)])
            o_ref[r, _slab(s)] = ((gc * jax.nn.sigmoid(gc)) * up_ref[r, _slab(s)]).astype(o_ref.dtype)
        return carry

    lax.fori_loop(0, rows // ch, act_body, 0)
    _conv_last(buf_ref, new_ref, rows, shift)


def _ffn_up(hn, wg, wu, cw, cb, prev, nseq, rows, shift, tn):
    d = hn.shape[1]
    kern = functools.partial(_ffn_up_kernel, shift=shift)
    return pl.pallas_call(
        kern,
        grid=(D_FF // tn, nseq),
        in_specs=[pl.BlockSpec((rows, d), lambda j, b: (b, 0)),
                  pl.BlockSpec((d, tn), lambda j, b: (0, j)),
                  pl.BlockSpec((d, tn), lambda j, b: (0, j)),
                  pl.BlockSpec((CONV_WIDTH, tn), lambda j, b: (0, j)),
                  pl.BlockSpec((1, tn), lambda j, b: (0, j)),
                  pl.BlockSpec((1, 2 * shift, tn), lambda j, b: (b, 0, j))],
        out_specs=[pl.BlockSpec((rows, tn), lambda j, b: (b, j)),
                   pl.BlockSpec((1, 2 * shift, tn), lambda j, b: (b, 0, j))],
        out_shape=[jax.ShapeDtypeStruct((nseq * rows, D_FF), BF16),
                   jax.ShapeDtypeStruct((nseq, 2 * shift, D_FF), F32)],
        scratch_shapes=[_conv_buf(rows, tn, shift), pltpu.VMEM((rows, tn), F32),
                        pltpu.VMEM((d, tn), BF16), pltpu.VMEM((d, tn), BF16)],
        compiler_params=_params("arbitrary", "arbitrary"),
        name="ffn_up",
    )(hn, wg, wu, cw, cb, prev)


def _layer(x, w, *, nseq, rows, shift, prev_conv, s0, prev_ffn, mk, mv, time_major):
    m = x.shape[0]
    tm = min(m, 1024)
    p, logf = _in_proj(x, w["norm_mix"], w["w_in"], w["w_g1"], w["w_g2"], w["b_gate"], tm, 1024)
    conv_out, conv_new = _conv_gate(p, prev_conv, w["conv_w"], nseq, rows, shift, 256)

    if time_major:
        nt = m // shift
        pad = ((0, 0), (0, SUBLANES - nt), (0, 0))
        pg = jnp.pad(p.reshape(nt, shift, N_MAIN)[:, :, 3 * D_CONV:].transpose(1, 0, 2), pad)
        lg = jnp.pad(logf.reshape(nt, shift, -1).transpose(1, 0, 2), pad)
        o, s_new = _gla(pg, lg, s0, w["gla_norm"], q_blk=0, k_blk=1, v_blk=1, r_blk=2,
                        bb=8, rows=SUBLANES, chunk=SUBLANES, group=4)
        gla_out = o[:, :nt].transpose(1, 0, 2).reshape(m, D_GLA)
    else:
        q0 = 3 * D_CONV // (GLA_HEADS * GLA_DK)
        v0 = (3 * D_CONV + 2 * GLA_HEADS * GLA_DK) // (GLA_HEADS * GLA_DV)
        o, s_new = _gla(p.reshape(nseq, rows, N_MAIN), logf.reshape(nseq, rows, -1), s0, w["gla_norm"],
                        q_blk=q0, k_blk=q0 + 1, v_blk=v0, r_blk=v0 + 1,
                        bb=2, rows=512, chunk=GLA_CHUNK, group=2)
        gla_out = o.reshape(m, D_GLA)

    tm2 = min(m, 512)
    h, hn = _proj_res_norm([conv_out, gla_out], w["w_out"], x, w["norm_x"], tm=tm2, tk=D_MODEL, final=False)
    qx = _matmul(hn, w["w_xq"], BF16, tm)

    if time_major:
        nt = m // shift
        qb = jnp.pad(qx.astype(F32).reshape(nt, shift, D_MODEL).transpose(1, 0, 2),
                     ((0, 0), (0, SUBLANES - nt), (0, 0)))
        ob = _xattn_cache(qb, mk, mv, bb=4)
        attn = ob[:, :nt].transpose(1, 0, 2).reshape(m, D_MODEL)
    else:
        attn = _xattn(qx.reshape(nseq, rows, D_MODEL), mk, mv, bb=1, tq=rows).reshape(m, D_MODEL)

    h2, hn2 = _proj_res_norm([attn], w["w_xo"], h, w["norm_ffn"], tm=tm2, tk=D_MODEL, final=False)
    act, ffn_new = _ffn_up(hn2, w["w_fg"], w["w_fu"], w["ffn_conv_w"], w["ffn_conv_b"], prev_ffn,
                           nseq, rows, shift, 512)
    return act, h2, conv_new, s_new, ffn_new


def kernel(x_prompt, x_sample, mem_prompt, cache_conv, state_gla, cache_ffn, cache_mem_k, cache_mem_v,
           norm_mix, w_in, conv_w, w_gate2, b_gate, gla_norm, w_out, norm_x, norm_mem, w_xq, w_xk, w_xv,
           w_xo, norm_ffn, w_ffn_gate, w_ffn_up, ffn_conv_w, ffn_conv_b, w_ffn_down, norm_final):
    depth = w_in.shape[0]
    nb, seq, d = x_prompt.shape
    db, dseq, _ = x_sample.shape
    hp = x_prompt.reshape(nb * seq, d)
    hs = x_sample.transpose(1, 0, 2).reshape(dseq * db, d)
    outs = {k: [] for k in ("conv_p", "gla_p", "ffn_p", "mk", "mv", "conv_s", "gla_s", "ffn_s")}
    nfinal = norm_final.reshape(1, d)
    yp = ys = None
    for l in range(depth):
        w = {
            "norm_mix": norm_mix[l].reshape(1, d),
            "w_in": w_in[l].astype(BF16),
            "w_g1": jnp.pad(w_in[l][:, N_MAIN:], ((0, 0), (0, LANES - GLA_RANK))).astype(BF16),
            "w_g2": jnp.pad(w_gate2[l], ((0, LANES - GLA_RANK), (0, 0))).astype(BF16),
            "b_gate": b_gate[l].reshape(1, -1),
            "conv_w": conv_w[l],
            "gla_norm": gla_norm[l].reshape(1, -1),
            "w_out": w_out[l].astype(BF16),
            "norm_x": norm_x[l].reshape(1, d),
            "w_xq": w_xq[l],
            "w_xo": w_xo[l].astype(BF16),
            "norm_ffn": norm_ffn[l].reshape(1, d),
            "w_fg": w_ffn_gate[l],
            "w_fu": w_ffn_up[l],
            "ffn_conv_w": ffn_conv_w[l],
            "ffn_conv_b": ffn_conv_b[l].reshape(1, -1),
        }
        w_fd = w_ffn_down[l].astype(BF16)
        last = l == depth - 1
        gain_next = nfinal if last else None

        mem = mem_prompt.reshape(nb * MEM_LEN, d)
        nmem = norm_mem[l].reshape(1, d)
        mk, mk_cache = _mem_proj(mem, nmem, w_xk[l], 1024)
        mv, mv_cache = _mem_proj(mem, nmem, w_xv[l], 1024)
        act, h2, c1, s1, f1 = _layer(
            hp, w, nseq=nb, rows=seq, shift=1,
            prev_conv=jnp.zeros((nb, CONV_WIDTH - 1, D_CONV), F32),
            s0=jnp.zeros((nb, GLA_HEADS, GLA_DK, GLA_DV), F32),
            prev_ffn=jnp.zeros((nb, CONV_WIDTH - 1, D_FF), F32),
            mk=mk.reshape(nb, MEM_LEN, d), mv=mv.reshape(nb, MEM_LEN, d), time_major=False)
        assert last, "only the final layer's epilogue (final rmsnorm) is implemented"
        yp = _proj_res_norm([act], w_fd, h2, gain_next, tm=256, tk=D_FF, final=True)
        outs["conv_p"].append(c1)
        outs["gla_p"].append(s1)
        outs["ffn_p"].append(f1)
        outs["mk"].append(mk_cache)
        outs["mv"].append(mv_cache)

        def tmajor(c):
            return c.transpose(1, 0, 2).reshape(1, (CONV_WIDTH - 1) * db, c.shape[-1])

        act, h2, c2, s2, f2 = _layer(
            hs, w, nseq=1, rows=dseq * db, shift=db,
            prev_conv=tmajor(cache_conv[l]), s0=state_gla[l], prev_ffn=tmajor(cache_ffn[l]),
            mk=cache_mem_k[l], mv=cache_mem_v[l], time_major=True)
        ys = _proj_res_norm([act], w_fd, h2, gain_next, tm=256, tk=D_FF, final=True)
        outs["conv_s"].append(c2.reshape(CONV_WIDTH - 1, db, D_CONV).transpose(1, 0, 2))
        outs["gla_s"].append(s2)
        outs["ffn_s"].append(f2.reshape(CONV_WIDTH - 1, db, D_FF).transpose(1, 0, 2))

    y_prompt = yp.reshape(nb, seq, d)
    y_sample = ys.reshape(dseq, db, d).transpose(1, 0, 2)
    st = lambda k: jnp.stack(outs[k])
    return (y_prompt, y_sample, st("conv_p"), st("gla_p"), st("ffn_p"), st("mk"), st("mv"),
            st("conv_s"), st("gla_s"), st("ffn_s"))
```

```python
import functools

import jax
import jax.numpy as jnp
from jax import lax
from jax.experimental import pallas as pl
from jax.experimental.pallas import tpu as pltpu

F32 = jnp.float32
BF16 = jnp.bfloat16

D_MODEL = 2048
EPS = 1e-6
CONV_WIDTH = 3
D_CONV = 1024
D_GLA = 1024
GLA_HEADS = 4
GLA_DV = 256
GLA_DK = 128
GLA_RANK = 16
GLA_TAU = 16.0
GLA_CHUNK = 64
X_HEADS = 4
X_HD = 512
MEM_LEN = 256
D_FF = 5632
N_MAIN = 3 * D_CONV + 2 * GLA_HEADS * GLA_DK + 2 * GLA_HEADS * GLA_DV

LANES = 128
SUBLANES = 8
VMEM_LIMIT_BYTES = 56 * 1024 * 1024


def _params(*sem):
    return pltpu.CompilerParams(dimension_semantics=sem, vmem_limit_bytes=VMEM_LIMIT_BYTES)


def _dot(a, b):
    return jnp.dot(a, b, preferred_element_type=F32)


def _dot_nt(a, b):
    return lax.dot_general(a, b, (((1,), (1,)), ((), ())), preferred_element_type=F32)


def _dot_tn(a, b):
    return lax.dot_general(a, b, (((0,), (0,)), ((), ())), preferred_element_type=F32)


def _rms_rows(x, g):
    ms = jnp.mean(x * x, axis=-1, keepdims=True)
    return (x * lax.rsqrt(ms + EPS)) * g


def _row_chunk(rows, limit=256):
    for c in (256, 128, 64, 32, 16, 8):
        if c <= limit and rows % c == 0:
            return c
    return rows


NORM_ROWS = 128


def _norm_into(x_ref, g_ref, xn_ref):
    rows = x_ref.shape[0]
    ch = _row_chunk(rows)
    g = g_ref[...]

    def body(c, carry):
        r = pl.ds(pl.multiple_of(c * ch, ch), ch)
        xn_ref[r, :] = _rms_rows(x_ref[r, :], g).astype(xn_ref.dtype)
        return carry

    lax.fori_loop(0, rows // ch, body, 0)


def _norm_matmul_kernel(x_ref, g_ref, w_ref, o_ref, oc_ref, xn_ref):
    j = pl.program_id(1)

    @pl.when(j == 0)
    def _():
        _norm_into(x_ref, g_ref, xn_ref)

    o_ref[...] = _dot(xn_ref[...], w_ref[...].astype(BF16)).astype(o_ref.dtype)
    nchunk = X_HD // LANES
    pitch = nchunk * X_HEADS
    heads_per_tile = o_ref.shape[1] // X_HD
    for b in range(oc_ref.shape[0]):
        for hh in range(heads_per_tile):
            for c in range(nchunk):
                col = hh * X_HD + c * LANES
                row = c * X_HEADS + j * heads_per_tile + hh
                oc_ref[b, pl.ds(row, MEM_LEN, stride=pitch), :] = (
                    o_ref[b * MEM_LEN:(b + 1) * MEM_LEN, col:col + LANES])


def _norm_gate_kernel(x_ref, g_ref, wg1_ref, wg2_ref, bg_ref, xn_ref, lf_ref):
    _norm_into(x_ref, g_ref, xn_ref)
    g1 = _dot(xn_ref[...], wg1_ref[...])
    z = _dot(g1.astype(BF16), wg2_ref[...]) + bg_ref[...]
    lf_ref[...] = (jnp.minimum(z, 0.0) - jnp.log1p(jnp.exp(-jnp.abs(z)))) * (1.0 / GLA_TAU)


def _mem_proj(mem, gain, w, tn):
    m, d = mem.shape
    n = w.shape[1]
    nb = m // MEM_LEN
    nchunk = X_HD // LANES
    rows = MEM_LEN * nchunk * X_HEADS
    flat, stored = pl.pallas_call(
        _norm_matmul_kernel,
        grid=(1, n // tn),
        in_specs=[pl.BlockSpec((m, d), lambda i, j: (0, 0)),
                  pl.BlockSpec((1, d), lambda i, j: (0, 0)),
                  pl.BlockSpec((d, tn), lambda i, j: (0, j))],
        out_specs=[pl.BlockSpec((m, tn), lambda i, j: (0, j)),
                   pl.BlockSpec((nb, rows, LANES), lambda i, j: (0, 0, 0))],
        out_shape=[jax.ShapeDtypeStruct((m, n), F32),
                   jax.ShapeDtypeStruct((nb, rows, LANES), F32)],
        scratch_shapes=[pltpu.VMEM((m, d), BF16)],
        compiler_params=_params("arbitrary", "arbitrary"),
        name="norm_matmul",
    )(mem, gain, w)
    cache = stored.reshape(nb, MEM_LEN, nchunk, X_HEADS, LANES).transpose(0, 1, 3, 2, 4)
    return flat, cache.reshape(nb, MEM_LEN, X_HEADS, X_HD)


def _norm_gate(x, gain, wg1, wg2, b_gate, tm):
    m, d = x.shape
    ng = wg2.shape[1]
    return pl.pallas_call(
        _norm_gate_kernel,
        grid=(m // tm,),
        in_specs=[pl.BlockSpec((tm, d), lambda i: (i, 0)),
                  pl.BlockSpec((1, d), lambda i: (0, 0)),
                  pl.BlockSpec((d, LANES), lambda i: (0, 0)),
                  pl.BlockSpec((LANES, ng), lambda i: (0, 0)),
                  pl.BlockSpec((1, ng), lambda i: (0, 0))],
        out_specs=[pl.BlockSpec((tm, d), lambda i: (i, 0)),
                   pl.BlockSpec((tm, ng), lambda i: (i, 0))],
        out_shape=[jax.ShapeDtypeStruct((m, d), BF16),
                   jax.ShapeDtypeStruct((m, ng), F32)],
        compiler_params=_params("arbitrary"),
        name="norm_gate",
    )(x, gain, wg1, wg2, b_gate)


def _cast_into(src_ref, dst_ref):
    rows = src_ref.shape[0]
    ch = _row_chunk(rows)

    def body(c, carry):
        r = pl.ds(pl.multiple_of(c * ch, ch), ch)
        dst_ref[r, :] = src_ref[r, :].astype(dst_ref.dtype)
        return carry

    lax.fori_loop(0, rows // ch, body, 0)


def _matmul_kernel(a_ref, w_ref, o_ref, wb_ref):
    @pl.when(pl.program_id(0) == 0)
    def _():
        _cast_into(w_ref, wb_ref)

    o_ref[...] = _dot(a_ref[...], wb_ref[...]).astype(o_ref.dtype)


def _matmul(a, w, out_dtype, tm):
    m, k = a.shape
    n = w.shape[1]
    return pl.pallas_call(
        _matmul_kernel,
        grid=(m // tm,),
        in_specs=[pl.BlockSpec((tm, k), lambda i: (i, 0)),
                  pl.BlockSpec((k, n), lambda i: (0, 0), pipeline_mode=pl.Buffered(1))],
        out_specs=pl.BlockSpec((tm, n), lambda i: (i, 0)),
        out_shape=jax.ShapeDtypeStruct((m, n), out_dtype),
        scratch_shapes=[pltpu.VMEM((k, n), BF16)],
        compiler_params=_params("arbitrary"),
        name="matmul",
    )(a, w)


def _col_matmul_kernel(a_ref, w_ref, o_ref, wb_ref):
    @pl.when(pl.program_id(1) == 0)
    def _():
        _cast_into(w_ref, wb_ref)

    o_ref[...] = _dot(a_ref[...], wb_ref[...]).astype(o_ref.dtype)


def _col_matmul(a, w, col0, ncols, out_dtype, tm, tn):
    m, k = a.shape
    j0 = col0 // tn
    return pl.pallas_call(
        _col_matmul_kernel,
        grid=(ncols // tn, m // tm),
        in_specs=[pl.BlockSpec((tm, k), lambda j, i: (i, 0)),
                  pl.BlockSpec((k, tn), lambda j, i: (0, j0 + j))],
        out_specs=pl.BlockSpec((tm, tn), lambda j, i: (i, j)),
        out_shape=jax.ShapeDtypeStruct((m, ncols), out_dtype),
        scratch_shapes=[pltpu.VMEM((k, tn), BF16)],
        compiler_params=_params("arbitrary", "arbitrary"),
        name="col_matmul",
    )(a, w)


def _halo_rows(shift):
    return max(2 * shift, SUBLANES)


def _conv_buf(rows, tn, shift):
    return pltpu.VMEM((tn // LANES, _halo_rows(shift) + rows, LANES), F32)


def _slab(s):
    return slice(s * LANES, (s + 1) * LANES)


def _conv_stage(buf_ref, row0, x):
    for s in range(buf_ref.shape[0]):
        buf_ref[s, row0:row0 + x.shape[0], :] = x[:, _slab(s)]


def _conv_taps(buf_ref, s, r0, rows, shift):
    halo = _halo_rows(shift)

    def back(steps):
        start = r0 + (halo - steps * shift)
        if (steps * shift) % SUBLANES == 0:
            return pl.ds(start if isinstance(start, int) else pl.multiple_of(start, SUBLANES), rows)
        return pl.ds(start, rows, stride=1)

    return buf_ref[s, back(2), :], buf_ref[s, back(1), :], buf_ref[s, back(0), :]


def _conv_last(buf_ref, new_ref, rows, shift):
    halo = _halo_rows(shift)
    for s in range(buf_ref.shape[0]):
        new_ref[0, :, _slab(s)] = buf_ref[s, halo + rows - 2 * shift:halo + rows, :]


def _conv_proj_kernel(xn_ref, wbg_ref, wcg_ref, wvc_ref, prev_ref, cw_ref, o_ref, new_ref,
                      bgb_ref, cgb_ref, vcb_ref, buf_ref, *, shift, tiles_per_seq):
    rows = xn_ref.shape[0]
    halo = _halo_rows(shift)
    i = pl.program_id(1)

    @pl.when(i == 0)
    def _():
        _cast_into(wbg_ref, bgb_ref)
        _cast_into(wcg_ref, cgb_ref)
        _cast_into(wvc_ref, vcb_ref)

    def from_cache():
        _conv_stage(buf_ref, halo - 2 * shift, prev_ref[0])

    if tiles_per_seq == 1:
        from_cache()
    else:
        pl.when(i % tiles_per_seq == 0)(from_cache)

        @pl.when(i % tiles_per_seq != 0)
        def _():
            for s in range(buf_ref.shape[0]):
                buf_ref[s, halo - 2 * shift:halo, :] = buf_ref[s, halo + rows - 2 * shift:halo + rows, :]

    xn = xn_ref[...]
    _conv_stage(buf_ref, halo, _dot(xn, cgb_ref[...]) * _dot(xn, vcb_ref[...]))
    bg = _dot(xn, bgb_ref[...])
    for s in range(buf_ref.shape[0]):
        u2, u1, u0 = _conv_taps(buf_ref, s, 0, rows, shift)
        y = cw_ref[0:1, _slab(s)] * u2 + cw_ref[1:2, _slab(s)] * u1 + cw_ref[2:3, _slab(s)] * u0
        o_ref[:, _slab(s)] = (bg[:, _slab(s)] * y).astype(o_ref.dtype)
    _conv_last(buf_ref, new_ref, rows, shift)


def _conv_proj(xn, w_in, prev, conv_w, *, nseq, shift, tm, tn):
    m, d = xn.shape
    nj = D_CONV // tn
    tiles_per_seq = m // (nseq * tm)
    kern = functools.partial(_conv_proj_kernel, shift=shift, tiles_per_seq=tiles_per_seq)
    state = pl.BlockSpec((1, 2 * shift, tn), lambda j, i: (i // tiles_per_seq, 0, j))
    wb = pltpu.VMEM((d, tn), BF16)
    return pl.pallas_call(
        kern,
        grid=(nj, m // tm),
        in_specs=[pl.BlockSpec((tm, d), lambda j, i: (i, 0)),
                  pl.BlockSpec((d, tn), lambda j, i: (0, j)),
                  pl.BlockSpec((d, tn), lambda j, i: (0, j + nj)),
                  pl.BlockSpec((d, tn), lambda j, i: (0, j + 2 * nj)),
                  state,
                  pl.BlockSpec((CONV_WIDTH, tn), lambda j, i: (0, j))],
        out_specs=[pl.BlockSpec((tm, tn), lambda j, i: (i, j)),
                   state],
        out_shape=[jax.ShapeDtypeStruct((m, D_CONV), BF16),
                   jax.ShapeDtypeStruct((nseq, 2 * shift, D_CONV), F32)],
        scratch_shapes=[wb, wb, wb, _conv_buf(tm, tn, shift)],
        compiler_params=_params("arbitrary", "arbitrary"),
        name="conv_proj",
    )(xn, w_in, w_in, w_in, prev, conv_w)


def _cumsum_rows(g):
    c = g.shape[0]
    row = lax.broadcasted_iota(jnp.int32, g.shape, 0)
    x = g
    s = 1
    while s < c:
        x = x + jnp.where(row >= s, pltpu.roll(x, s, 0), 0.0)
        s *= 2
    return x


def _bcast_block_row(x, s, k):
    c, lanes = x.shape
    if s == c:
        return jnp.broadcast_to(x[k:k + 1, :], x.shape)
    if s >= SUBLANES:
        y = x.reshape(c // s, s, lanes)
        return jnp.broadcast_to(y[:, k:k + 1, :], y.shape).reshape(c, lanes)
    y = x.reshape(c // SUBLANES, SUBLANES, lanes)
    sub = lax.broadcasted_iota(jnp.int32, y.shape, 1)
    out = None
    for blk in range(SUBLANES // s):
        src = jnp.broadcast_to(y[:, blk * s + k:blk * s + k + 1, :], y.shape)
        out = src if out is None else jnp.where(sub >= blk * s, src, out)
    return out.reshape(c, lanes)


def _gla_pair_masks(c):
    ri = lax.broadcasted_iota(jnp.int32, (c, c), 0)
    ci = lax.broadcasted_iota(jnp.int32, (c, c), 1)
    diff_bits = ri ^ ci
    masks = [diff_bits == 0]
    level = 0
    while (1 << level) < c:
        masks.append(((diff_bits >> level) == 1) & (((ri >> level) & 1) == 1))
        level += 1
    return masks


def _gla_chunk(q, k, v, g, s_prev, masks):
    c = q.shape[0]
    cum = _cumsum_rows(g)
    a = jnp.where(masks[0], _dot_nt(q.astype(BF16), k.astype(BF16)), 0.0)
    for level in range(len(masks) - 1):
        half = 1 << level
        ref = _bcast_block_row(cum, 2 * half, half - 1)
        qe = q * jnp.exp(jnp.minimum(cum - ref, 0.0))
        ke = k * jnp.exp(jnp.minimum(ref - cum, 0.0))
        a = a + jnp.where(masks[1 + level], _dot_nt(qe.astype(BF16), ke.astype(BF16)), 0.0)
    o = _dot(a.astype(BF16), v.astype(BF16)) + _dot((q * jnp.exp(cum)).astype(BF16), s_prev.astype(BF16))
    last = cum[c - 1:c, :]
    kd = k * jnp.exp(last - cum)
    dk = last.shape[1]
    decay_t = jnp.transpose(jnp.broadcast_to(jnp.exp(last), (dk, dk)))
    decayed = jnp.concatenate([decay_t * s_prev[:, i:i + dk] for i in range(0, s_prev.shape[1], dk)], axis=1)
    s_new = decayed + _dot_tn(kd.astype(BF16), v.astype(BF16))
    return o, s_new


def _gla_kernel(q_ref, k_ref, v_ref, r_ref, g_ref, s0_ref, gn_ref, o_ref, sn_ref, *, chunk, single_chunk, group):
    bb, rows = q_ref.shape[0], q_ref.shape[1]
    nchunk = rows // chunk
    state_in = s0_ref if single_chunk else sn_ref

    if not single_chunk:
        @pl.when(pl.program_id(1) == 0)
        def _():
            sn_ref[...] = s0_ref[...]

    masks = _gla_pair_masks(chunk)

    def one(b, r):
        for h in range(GLA_HEADS):
            kc = slice(h * GLA_DK, (h + 1) * GLA_DK)
            vc = slice(h * GLA_DV, (h + 1) * GLA_DV)
            q = q_ref[b, r, kc] * (GLA_DK ** -0.5)
            o, s_new = _gla_chunk(q, k_ref[b, r, kc], v_ref[b, r, vc], g_ref[b, r, kc], state_in[b, h], masks)
            sn_ref[b, h] = s_new
            rr = r_ref[b, r, vc]
            o_ref[b, r, vc] = (_rms_rows(o, gn_ref[:, vc]) * (rr * jax.nn.sigmoid(rr))).astype(o_ref.dtype)

    def body(n, carry):
        r = pl.ds(pl.multiple_of((n % nchunk) * chunk, chunk), chunk)
        for u in range(group):
            one((n // nchunk) * group + u, r)
        return carry

    lax.fori_loop(0, (bb // group) * nchunk, body, 0)


def _gla(p, logf, s0, gla_norm, *, q_blk, k_blk, v_blk, r_blk, bb, rows, chunk, group):
    nb, t, _ = p.shape
    nk, nv = GLA_HEADS * GLA_DK, GLA_HEADS * GLA_DV
    kern = functools.partial(_gla_kernel, chunk=chunk, single_chunk=(t == chunk), group=group)
    state_spec = pl.BlockSpec((bb, GLA_HEADS, GLA_DK, GLA_DV), lambda b, c: (b, 0, 0, 0))
    return pl.pallas_call(
        kern,
        grid=(nb // bb, t // rows),
        in_specs=[pl.BlockSpec((bb, rows, nk), lambda b, c: (b, c, q_blk)),
                  pl.BlockSpec((bb, rows, nk), lambda b, c: (b, c, k_blk)),
                  pl.BlockSpec((bb, rows, nv), lambda b, c: (b, c, v_blk)),
                  pl.BlockSpec((bb, rows, nv), lambda b, c: (b, c, r_blk)),
                  pl.BlockSpec((bb, rows, nk), lambda b, c: (b, c, 0)),
                  state_spec,
                  pl.BlockSpec((1, nv), lambda b, c: (0, 0))],
        out_specs=[pl.BlockSpec((bb, rows, nv), lambda b, c: (b, c, 0)),
                   state_spec],
        out_shape=[jax.ShapeDtypeStruct((nb, t, D_GLA), BF16),
                   jax.ShapeDtypeStruct((nb, GLA_HEADS, GLA_DK, GLA_DV), F32)],
        compiler_params=_params("arbitrary", "arbitrary"),
        name="gla",
    )(p, p, p, p, logf, s0, gla_norm)


def _proj_res_norm_kernel(*refs, n_a, nk, final):
    a_refs, (w_ref, res_ref, g_ref) = refs[:n_a], refs[n_a:n_a + 3]
    out_refs, acc_ref = refs[n_a + 3:-1], refs[-1]
    k = pl.program_id(1)
    part, r0 = None, 0
    for a_ref in a_refs:
        kw = a_ref.shape[1]
        term = _dot(a_ref[...], w_ref[r0:r0 + kw, :])
        part = term if part is None else part + term
        r0 += kw

    @pl.when(k == 0)
    def _():
        acc_ref[...] = part

    if nk > 1:
        @pl.when(k > 0)
        def _():
            acc_ref[...] += part

    @pl.when(k == nk - 1)
    def _():
        rows = acc_ref.shape[0]
        ch = _row_chunk(rows, NORM_ROWS)
        g = g_ref[...]

        def body(c, carry):
            r = pl.ds(pl.multiple_of(c * ch, ch), ch)
            h = res_ref[r, :] + acc_ref[r, :]
            hn = _rms_rows(h, g)
            if final:
                out_refs[0][r, :] = hn
            else:
                out_refs[0][r, :] = h
                out_refs[1][r, :] = hn.astype(out_refs[1].dtype)
            return carry

        lax.fori_loop(0, rows // ch, body, 0)


def _proj_res_norm(a_list, w, res, gain, *, tm, tk, final):
    m = a_list[0].shape[0]
    kdim, d = w.shape
    nk = kdim // tk
    assert len(a_list) == 1 or nk == 1
    kern = functools.partial(_proj_res_norm_kernel, n_a=len(a_list), nk=nk, final=final)
    a_specs = ([pl.BlockSpec((tm, tk), lambda i, k: (i, k))] if len(a_list) == 1 else
               [pl.BlockSpec((tm, a.shape[1]), lambda i, k: (i, 0)) for a in a_list])
    row_spec = pl.BlockSpec((tm, d), lambda i, k: (i, 0))
    if final:
        out_specs = row_spec
        out_shape = jax.ShapeDtypeStruct((m, d), F32)
    else:
        out_specs = [row_spec, row_spec]
        out_shape = [jax.ShapeDtypeStruct((m, d), F32), jax.ShapeDtypeStruct((m, d), BF16)]
    return pl.pallas_call(
        kern,
        grid=(m // tm, nk),
        in_specs=a_specs + [pl.BlockSpec((tk, d), lambda i, k: (k, 0),
                                         pipeline_mode=pl.Buffered(1 if nk == 1 else 2)),
                            row_spec,
                            pl.BlockSpec((1, d), lambda i, k: (0, 0))],
        out_specs=out_specs,
        out_shape=out_shape,
        scratch_shapes=[pltpu.VMEM((tm, d), F32)],
        compiler_params=_params("arbitrary", "arbitrary"),
        name="proj_res_norm",
    )(*a_list, w, res, gain)


def _xattn_kernel(q_ref, k_ref, v_ref, o_ref):
    bb = q_ref.shape[0]

    def body(b, carry):
        q = q_ref[b].astype(BF16)
        p = _softmax_rows(_dot_nt(q, k_ref[b].astype(BF16)) * (X_HD ** -0.5))
        o_ref[b] = _dot(p.astype(BF16), v_ref[b].astype(BF16)).astype(o_ref.dtype)
        return carry

    lax.fori_loop(0, bb, body, 0)


def _xattn(q, mk, mv, *, bb, tq):
    nb, t, d = q.shape
    return pl.pallas_call(
        _xattn_kernel,
        grid=(nb // bb, X_HEADS, t // tq),
        in_specs=[pl.BlockSpec((bb, tq, X_HD), lambda b, h, i: (b, i, h)),
                  pl.BlockSpec((bb, MEM_LEN, X_HD), lambda b, h, i: (b, 0, h)),
                  pl.BlockSpec((bb, MEM_LEN, X_HD), lambda b, h, i: (b, 0, h))],
        out_specs=pl.BlockSpec((bb, tq, X_HD), lambda b, h, i: (b, i, h)),
        out_shape=jax.ShapeDtypeStruct((nb, t, d), BF16),
        compiler_params=_params("arbitrary", "arbitrary", "arbitrary"),
        name="xattn",
    )(q, mk, mv)


def _softmax_rows(s):
    s = s - jnp.max(s, axis=-1, keepdims=True)
    e = jnp.exp(s)
    return e / jnp.sum(e, axis=-1, keepdims=True)


def _xattn_cache_kernel(q_ref, k_ref, v_ref, o_ref):
    bb, tq = q_ref.shape[0], q_ref.shape[1]
    nchunk = X_HD // LANES
    pitch = nchunk * X_HEADS

    def gather(ref, b, h):
        parts = [ref[b, pl.ds(c * X_HEADS + h, MEM_LEN, stride=pitch), :] for c in range(nchunk)]
        return jnp.concatenate(parts, axis=1).astype(BF16)

    pairs = [(b, h) for b in range(bb) for h in range(X_HEADS)]
    scores = [_dot_nt(q_ref[b, :, h * X_HD:(h + 1) * X_HD].astype(BF16), gather(k_ref, b, h)) for b, h in pairs]
    p = _softmax_rows(jnp.concatenate(scores, axis=0) * (X_HD ** -0.5)).astype(BF16)
    for n, (b, h) in enumerate(pairs):
        o_ref[b, :, h * X_HD:(h + 1) * X_HD] = _dot(p[n * tq:(n + 1) * tq], gather(v_ref, b, h)).astype(o_ref.dtype)


def _xattn_cache(q, ck, cv, *, bb):
    nb, tq, d = q.shape
    nchunk = X_HD // LANES

    def stored_order(c):
        c = c.reshape(nb, MEM_LEN, X_HEADS, nchunk, LANES).transpose(0, 1, 3, 2, 4)
        return c.reshape(nb, MEM_LEN * nchunk * X_HEADS, LANES)

    rows = MEM_LEN * nchunk * X_HEADS
    return pl.pallas_call(
        _xattn_cache_kernel,
        grid=(nb // bb,),
        in_specs=[pl.BlockSpec((bb, tq, d), lambda b: (b, 0, 0)),
                  pl.BlockSpec((bb, rows, LANES), lambda b: (b, 0, 0)),
                  pl.BlockSpec((bb, rows, LANES), lambda b: (b, 0, 0))],
        out_specs=pl.BlockSpec((bb, tq, d), lambda b: (b, 0, 0)),
        out_shape=jax.ShapeDtypeStruct((nb, tq, d), BF16),
        compiler_params=_params("arbitrary"),
        name="xattn_cache",
    )(q, stored_order(ck), stored_order(cv))


def _ffn_up_kernel(hn_ref, wg_ref, wu_ref, cw_ref, cb_ref, prev_ref, o_ref, new_ref, buf_ref, up_ref,
                   wgb_ref, wub_ref, *, shift):
    rows = hn_ref.shape[0]
    halo = _halo_rows(shift)
    ch = _row_chunk(rows)

    @pl.when(pl.program_id(1) == 0)
    def _():
        _cast_into(wg_ref, wgb_ref)
        _cast_into(wu_ref, wub_ref)

    _conv_stage(buf_ref, halo - 2 * shift, prev_ref[0])
    _conv_stage(buf_ref, halo, _dot(hn_ref[...], wgb_ref[...]))
    up_ref[...] = _dot(hn_ref[...], wub_ref[...])

    def act_body(c, carry):
        r0 = pl.multiple_of(c * ch, ch)
        r = pl.ds(r0, ch)
        for s in range(buf_ref.shape[0]):
            g2, g1, g0 = _conv_taps(buf_ref, s, r0, ch, shift)
            gc = (cw_ref[0:1, _slab(s)] * g2 + cw_ref[1:2, _slab(s)] * g1 + cw_ref[2:3, _slab(s)] * g0
                  + cb_ref[:, _slab(s)])
            o_ref[r, _slab(s)] = ((gc * jax.nn.sigmoid(gc)) * up_ref[r, _slab(s)]).astype(o_ref.dtype)
        return carry

    lax.fori_loop(0, rows // ch, act_body, 0)
    _conv_last(buf_ref, new_ref, rows, shift)


def _ffn_up(hn, wg, wu, cw, cb, prev, nseq, rows, shift, tn):
    d = hn.shape[1]
    kern = functools.partial(_ffn_up_kernel, shift=shift)
    return pl.pallas_call(
        kern,
        grid=(D_FF // tn, nseq),
        in_specs=[pl.BlockSpec((rows, d), lambda j, b: (b, 0)),
                  pl.BlockSpec((d, tn), lambda j, b: (0, j)),
                  pl.BlockSpec((d, tn), lambda j, b: (0, j)),
                  pl.BlockSpec((CONV_WIDTH, tn), lambda j, b: (0, j)),
                  pl.BlockSpec((1, tn), lambda j, b: (0, j)),
                  pl.BlockSpec((1, 2 * shift, tn), lambda j, b: (b, 0, j))],
        out_specs=[pl.BlockSpec((rows, tn), lambda j, b: (b, j)),
                   pl.BlockSpec((1, 2 * shift, tn), lambda j, b: (b, 0, j))],
        out_shape=[jax.ShapeDtypeStruct((nseq * rows, D_FF), BF16),
                   jax.ShapeDtypeStruct((nseq, 2 * shift, D_FF), F32)],
        scratch_shapes=[_conv_buf(rows, tn, shift), pltpu.VMEM((rows, tn), F32),
                        pltpu.VMEM((d, tn), BF16), pltpu.VMEM((d, tn), BF16)],
        compiler_params=_params("arbitrary", "arbitrary"),
        name="ffn_up",
    )(hn, wg, wu, cw, cb, prev)


def _layer(x, w, *, nseq, rows, shift, prev_conv, s0, prev_ffn, mk, mv, time_major):
    m = x.shape[0]
    tm = min(m, 1024)
    xn, logf = _norm_gate(x, w["norm_mix"], w["w_g1"], w["w_g2"], w["b_gate"], tm)
    conv_out, conv_new = _conv_proj(xn, w["w_in"], prev_conv, w["conv_w"], nseq=nseq, shift=shift, tm=tm, tn=512)
    n_gla = N_MAIN - 3 * D_CONV
    p = _col_matmul(xn, w["w_in"], 3 * D_CONV, n_gla, F32, tm, 1024)

    if time_major:
        nt = m // shift
        pad = ((0, 0), (0, SUBLANES - nt), (0, 0))
        pg = jnp.pad(p.reshape(nt, shift, n_gla).transpose(1, 0, 2), pad)
        lg = jnp.pad(logf.reshape(nt, shift, -1).transpose(1, 0, 2), pad)
        o, s_new = _gla(pg, lg, s0, w["gla_norm"], q_blk=0, k_blk=1, v_blk=1, r_blk=2,
                        bb=8, rows=SUBLANES, chunk=SUBLANES, group=4)
        gla_out = o[:, :nt].transpose(1, 0, 2).reshape(m, D_GLA)
    else:
        o, s_new = _gla(p.reshape(nseq, rows, n_gla), logf.reshape(nseq, rows, -1), s0, w["gla_norm"],
                        q_blk=0, k_blk=1, v_blk=1, r_blk=2, bb=2, rows=512, chunk=GLA_CHUNK, group=2)
        gla_out = o.reshape(m, D_GLA)

    tm2 = min(m, 512)
    h, hn = _proj_res_norm([conv_out, gla_out], w["w_out"], x, w["norm_x"], tm=tm2, tk=D_MODEL, final=False)
    qx = _matmul(hn, w["w_xq"], BF16, tm)

    if time_major:
        nt = m // shift
        qb = jnp.pad(qx.astype(F32).reshape(nt, shift, D_MODEL).transpose(1, 0, 2),
                     ((0, 0), (0, SUBLANES - nt), (0, 0)))
        ob = _xattn_cache(qb, mk, mv, bb=4)
        attn = ob[:, :nt].transpose(1, 0, 2).reshape(m, D_MODEL)
    else:
        attn = _xattn(qx.reshape(nseq, rows, D_MODEL), mk, mv, bb=1, tq=rows).reshape(m, D_MODEL)

    h2, hn2 = _proj_res_norm([attn], w["w_xo"], h, w["norm_ffn"], tm=tm2, tk=D_MODEL, final=False)
    act, ffn_new = _ffn_up(hn2, w["w_fg"], w["w_fu"], w["ffn_conv_w"], w["ffn_conv_b"], prev_ffn,
                           nseq, rows, shift, 512)
    return act, h2, conv_new, s_new, ffn_new


def kernel(x_prompt, x_sample, mem_prompt, cache_conv, state_gla, cache_ffn, cache_mem_k, cache_mem_v,
           norm_mix, w_in, conv_w, w_gate2, b_gate, gla_norm, w_out, norm_x, norm_mem, w_xq, w_xk, w_xv,
           w_xo, norm_ffn, w_ffn_gate, w_ffn_up, ffn_conv_w, ffn_conv_b, w_ffn_down, norm_final):
    depth = w_in.shape[0]
    nb, seq, d = x_prompt.shape
    db, dseq, _ = x_sample.shape
    hp = x_prompt.reshape(nb * seq, d)
    hs = x_sample.transpose(1, 0, 2).reshape(dseq * db, d)
    outs = {k: [] for k in ("conv_p", "gla_p", "ffn_p", "mk", "mv", "conv_s", "gla_s", "ffn_s")}
    nfinal = norm_final.reshape(1, d)
    yp = ys = None
    for l in range(depth):
        w = {
            "norm_mix": norm_mix[l].reshape(1, d),
            "w_in": w_in[l],
            "w_g1": jnp.pad(w_in[l][:, N_MAIN:], ((0, 0), (0, LANES - GLA_RANK))).astype(BF16),
            "w_g2": jnp.pad(w_gate2[l], ((0, LANES - GLA_RANK), (0, 0))).astype(BF16),
            "b_gate": b_gate[l].reshape(1, -1),
            "conv_w": conv_w[l],
            "gla_norm": gla_norm[l].reshape(1, -1),
            "w_out": w_out[l].astype(BF16),
            "norm_x": norm_x[l].reshape(1, d),
            "w_xq": w_xq[l],
            "w_xo": w_xo[l].astype(BF16),
            "norm_ffn": norm_ffn[l].reshape(1, d),
            "w_fg": w_ffn_gate[l],
            "w_fu": w_ffn_up[l],
            "ffn_conv_w": ffn_conv_w[l],
            "ffn_conv_b": ffn_conv_b[l].reshape(1, -1),
        }
        w_fd = w_ffn_down[l].astype(BF16)
        last = l == depth - 1
        gain_next = nfinal if last else None

        mem = mem_prompt.reshape(nb * MEM_LEN, d)
        nmem = norm_mem[l].reshape(1, d)
        mk, mk_cache = _mem_proj(mem, nmem, w_xk[l], 1024)
        mv, mv_cache = _mem_proj(mem, nmem, w_xv[l], 1024)
        act, h2, c1, s1, f1 = _layer(
            hp, w, nseq=nb, rows=seq, shift=1,
            prev_conv=jnp.zeros((nb, CONV_WIDTH - 1, D_CONV), F32),
            s0=jnp.zeros((nb, GLA_HEADS, GLA_DK, GLA_DV), F32),
            prev_ffn=jnp.zeros((nb, CONV_WIDTH - 1, D_FF), F32),
            mk=mk.reshape(nb, MEM_LEN, d), mv=mv.reshape(nb, MEM_LEN, d), time_major=False)
        assert last, "only the final layer's epilogue (final rmsnorm) is implemented"
        yp = _proj_res_norm([act], w_fd, h2, gain_next, tm=256, tk=D_FF, final=True)
        outs["conv_p"].append(c1)
        outs["gla_p"].append(s1)
        outs["ffn_p"].append(f1)
        outs["mk"].append(mk_cache)
        outs["mv"].append(mv_cache)

        def tmajor(c):
            return c.transpose(1, 0, 2).reshape(1, (CONV_WIDTH - 1) * db, c.shape[-1])

        act, h2, c2, s2, f2 = _layer(
            hs, w, nseq=1, rows=dseq * db, shift=db,
            prev_conv=tmajor(cache_conv[l]), s0=state_gla[l], prev_ffn=tmajor(cache_ffn[l]),
            mk=cache_mem_k[l], mv=cache_mem_v[l], time_major=True)
        ys = _proj_res_norm([act], w_fd, h2, gain_next, tm=256, tk=D_FF, final=True)
        outs["conv_s"].append(c2.reshape(CONV_WIDTH - 1, db, D_CONV).transpose(1, 0, 2))
        outs["gla_s"].append(s2)
        outs["ffn_s"].append(f2.reshape(CONV_WIDTH - 1, db, D_FF).transpose(1, 0, 2))

    y_prompt = yp.reshape(nb, seq, d)
    y_sample = ys.reshape(dseq, db, d).transpose(1, 0, 2)
    st = lambda k: jnp.stack(outs[k])
    return (y_prompt, y_sample, st("conv_p"), st("gla_p"), st("ffn_p"), st("mk"), st("mv"),
            st("conv_s"), st("gla_s"), st("ffn_s"))
```

```python
import functools

import jax
import jax.numpy as jnp
from jax import lax
from jax.experimental import pallas as pl
from jax.experimental.pallas import tpu as pltpu

F32 = jnp.float32
BF16 = jnp.bfloat16

D_MODEL = 2048
EPS = 1e-6
CONV_WIDTH = 3
D_CONV = 1024
D_GLA = 1024
GLA_HEADS = 4
GLA_DV = 256
GLA_DK = 128
GLA_RANK = 16
GLA_TAU = 16.0
GLA_CHUNK = 64
X_HEADS = 4
X_HD = 512
MEM_LEN = 256
D_FF = 5632
N_MAIN = 3 * D_CONV + 2 * GLA_HEADS * GLA_DK + 2 * GLA_HEADS * GLA_DV

LANES = 128
SUBLANES = 8
VMEM_LIMIT_BYTES = 56 * 1024 * 1024


def _params(*sem):
    return pltpu.CompilerParams(dimension_semantics=sem, vmem_limit_bytes=VMEM_LIMIT_BYTES)


def _dot(a, b):
    return jnp.dot(a, b, preferred_element_type=F32)


def _dot_nt(a, b):
    return lax.dot_general(a, b, (((1,), (1,)), ((), ())), preferred_element_type=F32)


def _dot_tn(a, b):
    return lax.dot_general(a, b, (((0,), (0,)), ((), ())), preferred_element_type=F32)


def _rms_rows(x, g):
    ms = jnp.mean(x * x, axis=-1, keepdims=True)
    return (x * lax.rsqrt(ms + EPS)) * g


def _row_chunk(rows, limit=256):
    for c in (256, 128, 64, 32, 16, 8):
        if c <= limit and rows % c == 0:
            return c
    return rows


NORM_ROWS = 128


def _norm_into(x_ref, g_ref, xn_ref):
    rows = x_ref.shape[0]
    ch = _row_chunk(rows)
    g = g_ref[...]

    def body(c, carry):
        r = pl.ds(pl.multiple_of(c * ch, ch), ch)
        xn_ref[r, :] = _rms_rows(x_ref[r, :], g).astype(xn_ref.dtype)
        return carry

    lax.fori_loop(0, rows // ch, body, 0)


def _norm_matmul_kernel(x_ref, g_ref, w_ref, o_ref, oc_ref, xn_ref):
    j = pl.program_id(1)

    @pl.when(j == 0)
    def _():
        _norm_into(x_ref, g_ref, xn_ref)

    o_ref[...] = _dot(xn_ref[...], w_ref[...].astype(BF16)).astype(o_ref.dtype)
    nchunk = X_HD // LANES
    pitch = nchunk * X_HEADS
    heads_per_tile = o_ref.shape[1] // X_HD
    for b in range(oc_ref.shape[0]):
        for hh in range(heads_per_tile):
            for c in range(nchunk):
                col = hh * X_HD + c * LANES
                row = c * X_HEADS + j * heads_per_tile + hh
                oc_ref[b, pl.ds(row, MEM_LEN, stride=pitch), :] = (
                    o_ref[b * MEM_LEN:(b + 1) * MEM_LEN, col:col + LANES])


def _norm_gate_kernel(x_ref, g_ref, wg1_ref, wg2_ref, bg_ref, xn_ref, lf_ref):
    _norm_into(x_ref, g_ref, xn_ref)
    g1 = _dot_nt(xn_ref[...], wg1_ref[...])
    z = _dot(g1.astype(BF16), wg2_ref[...]) + bg_ref[...]
    lf_ref[...] = (jnp.minimum(z, 0.0) - jnp.log1p(jnp.exp(-jnp.abs(z)))) * (1.0 / GLA_TAU)


def _mem_proj(mem, gain, w, tn):
    m, d = mem.shape
    n = w.shape[1]
    nb = m // MEM_LEN
    nchunk = X_HD // LANES
    rows = MEM_LEN * nchunk * X_HEADS
    flat, stored = pl.pallas_call(
        _norm_matmul_kernel,
        grid=(1, n // tn),
        in_specs=[pl.BlockSpec((m, d), lambda i, j: (0, 0)),
                  pl.BlockSpec((1, d), lambda i, j: (0, 0)),
                  pl.BlockSpec((d, tn), lambda i, j: (0, j))],
        out_specs=[pl.BlockSpec((m, tn), lambda i, j: (0, j)),
                   pl.BlockSpec((nb, rows, LANES), lambda i, j: (0, 0, 0))],
        out_shape=[jax.ShapeDtypeStruct((m, n), F32),
                   jax.ShapeDtypeStruct((nb, rows, LANES), F32)],
        scratch_shapes=[pltpu.VMEM((m, d), BF16)],
        compiler_params=_params("arbitrary", "arbitrary"),
        name="norm_matmul",
    )(mem, gain, w)
    cache = stored.reshape(nb, MEM_LEN, nchunk, X_HEADS, LANES).transpose(0, 1, 3, 2, 4)
    return flat, cache.reshape(nb, MEM_LEN, X_HEADS, X_HD)


def _norm_gate(x, gain, wg1, wg2, b_gate, tm):
    m, d = x.shape
    ng = wg2.shape[1]
    return pl.pallas_call(
        _norm_gate_kernel,
        grid=(m // tm,),
        in_specs=[pl.BlockSpec((tm, d), lambda i: (i, 0)),
                  pl.BlockSpec((1, d), lambda i: (0, 0)),
                  pl.BlockSpec((LANES, d), lambda i: (0, 0)),
                  pl.BlockSpec((LANES, ng), lambda i: (0, 0)),
                  pl.BlockSpec((1, ng), lambda i: (0, 0))],
        out_specs=[pl.BlockSpec((tm, d), lambda i: (i, 0)),
                   pl.BlockSpec((tm, ng), lambda i: (i, 0))],
        out_shape=[jax.ShapeDtypeStruct((m, d), BF16),
                   jax.ShapeDtypeStruct((m, ng), F32)],
        compiler_params=_params("arbitrary"),
        name="norm_gate",
    )(x, gain, wg1, wg2, b_gate)


def _cast_into(src_ref, dst_ref):
    rows = src_ref.shape[0]
    ch = _row_chunk(rows)

    def body(c, carry):
        r = pl.ds(pl.multiple_of(c * ch, ch), ch)
        dst_ref[r, :] = src_ref[r, :].astype(dst_ref.dtype)
        return carry

    lax.fori_loop(0, rows // ch, body, 0)


def _matmul_kernel(a_ref, w_ref, o_ref, wb_ref):
    @pl.when(pl.program_id(0) == 0)
    def _():
        _cast_into(w_ref, wb_ref)

    o_ref[...] = _dot(a_ref[...], wb_ref[...]).astype(o_ref.dtype)


def _matmul(a, w, out_dtype, tm):
    m, k = a.shape
    n = w.shape[1]
    return pl.pallas_call(
        _matmul_kernel,
        grid=(m // tm,),
        in_specs=[pl.BlockSpec((tm, k), lambda i: (i, 0)),
                  pl.BlockSpec((k, n), lambda i: (0, 0), pipeline_mode=pl.Buffered(1))],
        out_specs=pl.BlockSpec((tm, n), lambda i: (i, 0)),
        out_shape=jax.ShapeDtypeStruct((m, n), out_dtype),
        scratch_shapes=[pltpu.VMEM((k, n), BF16)],
        compiler_params=_params("arbitrary"),
        name="matmul",
    )(a, w)


def _col_matmul_kernel(a_ref, wt_ref, o_ref, wb_ref):
    @pl.when(pl.program_id(1) == 0)
    def _():
        _cast_into(wt_ref, wb_ref)

    o_ref[...] = _dot_nt(a_ref[...], wb_ref[...]).astype(o_ref.dtype)


def _col_matmul(a, wt, col0, ncols, out_dtype, tm, tn):
    m, k = a.shape
    j0 = col0 // tn
    return pl.pallas_call(
        _col_matmul_kernel,
        grid=(ncols // tn, m // tm),
        in_specs=[pl.BlockSpec((tm, k), lambda j, i: (i, 0)),
                  pl.BlockSpec((tn, k), lambda j, i: (j0 + j, 0))],
        out_specs=pl.BlockSpec((tm, tn), lambda j, i: (i, j)),
        out_shape=jax.ShapeDtypeStruct((m, ncols), out_dtype),
        scratch_shapes=[pltpu.VMEM((tn, k), BF16)],
        compiler_params=_params("arbitrary", "arbitrary"),
        name="col_matmul",
    )(a, wt)


def _halo_rows(shift):
    return max(2 * shift, SUBLANES)


def _conv_buf(rows, tn, shift):
    return pltpu.VMEM((tn // LANES, _halo_rows(shift) + rows, LANES), F32)


def _slab(s):
    return slice(s * LANES, (s + 1) * LANES)


def _conv_stage(buf_ref, row0, x):
    for s in range(buf_ref.shape[0]):
        buf_ref[s, row0:row0 + x.shape[0], :] = x[:, _slab(s)]


def _conv_taps(buf_ref, s, r0, rows, shift):
    halo = _halo_rows(shift)

    def back(steps):
        start = r0 + (halo - steps * shift)
        if (steps * shift) % SUBLANES == 0:
            return pl.ds(start if isinstance(start, int) else pl.multiple_of(start, SUBLANES), rows)
        return pl.ds(start, rows, stride=1)

    return buf_ref[s, back(2), :], buf_ref[s, back(1), :], buf_ref[s, back(0), :]


def _conv_last(buf_ref, new_ref, rows, shift):
    halo = _halo_rows(shift)
    for s in range(buf_ref.shape[0]):
        new_ref[0, :, _slab(s)] = buf_ref[s, halo + rows - 2 * shift:halo + rows, :]


def _conv_proj_kernel(xn_ref, wbg_ref, wcg_ref, wvc_ref, prev_ref, cw_ref, o_ref, new_ref,
                      bgb_ref, cgb_ref, vcb_ref, buf_ref, *, shift, tiles_per_seq):
    rows = xn_ref.shape[0]
    halo = _halo_rows(shift)
    i = pl.program_id(1)

    @pl.when(i == 0)
    def _():
        _cast_into(wbg_ref, bgb_ref)
        _cast_into(wcg_ref, cgb_ref)
        _cast_into(wvc_ref, vcb_ref)

    def from_cache():
        _conv_stage(buf_ref, halo - 2 * shift, prev_ref[0])

    if tiles_per_seq == 1:
        from_cache()
    else:
        pl.when(i % tiles_per_seq == 0)(from_cache)

        @pl.when(i % tiles_per_seq != 0)
        def _():
            for s in range(buf_ref.shape[0]):
                buf_ref[s, halo - 2 * shift:halo, :] = buf_ref[s, halo + rows - 2 * shift:halo + rows, :]

    xn = xn_ref[...]
    _conv_stage(buf_ref, halo, _dot_nt(xn, cgb_ref[...]) * _dot_nt(xn, vcb_ref[...]))
    bg = _dot_nt(xn, bgb_ref[...])
    for s in range(buf_ref.shape[0]):
        u2, u1, u0 = _conv_taps(buf_ref, s, 0, rows, shift)
        y = cw_ref[0:1, _slab(s)] * u2 + cw_ref[1:2, _slab(s)] * u1 + cw_ref[2:3, _slab(s)] * u0
        o_ref[:, _slab(s)] = (bg[:, _slab(s)] * y).astype(o_ref.dtype)
    _conv_last(buf_ref, new_ref, rows, shift)


def _conv_proj(xn, w_in_t, prev, conv_w, *, nseq, shift, tm, tn):
    m, d = xn.shape
    nj = D_CONV // tn
    tiles_per_seq = m // (nseq * tm)
    kern = functools.partial(_conv_proj_kernel, shift=shift, tiles_per_seq=tiles_per_seq)
    state = pl.BlockSpec((1, 2 * shift, tn), lambda j, i: (i // tiles_per_seq, 0, j))
    wb = pltpu.VMEM((tn, d), BF16)
    return pl.pallas_call(
        kern,
        grid=(nj, m // tm),
        in_specs=[pl.BlockSpec((tm, d), lambda j, i: (i, 0)),
                  pl.BlockSpec((tn, d), lambda j, i: (j, 0)),
                  pl.BlockSpec((tn, d), lambda j, i: (j + nj, 0)),
                  pl.BlockSpec((tn, d), lambda j, i: (j + 2 * nj, 0)),
                  state,
                  pl.BlockSpec((CONV_WIDTH, tn), lambda j, i: (0, j))],
        out_specs=[pl.BlockSpec((tm, tn), lambda j, i: (i, j)),
                   state],
        out_shape=[jax.ShapeDtypeStruct((m, D_CONV), BF16),
                   jax.ShapeDtypeStruct((nseq, 2 * shift, D_CONV), F32)],
        scratch_shapes=[wb, wb, wb, _conv_buf(tm, tn, shift)],
        compiler_params=_params("arbitrary", "arbitrary"),
        name="conv_proj",
    )(xn, w_in_t, w_in_t, w_in_t, prev, conv_w)


def _cumsum_rows(g):
    c = g.shape[0]
    row = lax.broadcasted_iota(jnp.int32, g.shape, 0)
    x = g
    s = 1
    while s < c:
        x = x + jnp.where(row >= s, pltpu.roll(x, s, 0), 0.0)
        s *= 2
    return x


def _bcast_block_row(x, s, k):
    c, lanes = x.shape
    if s == c:
        return jnp.broadcast_to(x[k:k + 1, :], x.shape)
    if s >= SUBLANES:
        y = x.reshape(c // s, s, lanes)
        return jnp.broadcast_to(y[:, k:k + 1, :], y.shape).reshape(c, lanes)
    y = x.reshape(c // SUBLANES, SUBLANES, lanes)
    sub = lax.broadcasted_iota(jnp.int32, y.shape, 1)
    out = None
    for blk in range(SUBLANES // s):
        src = jnp.broadcast_to(y[:, blk * s + k:blk * s + k + 1, :], y.shape)
        out = src if out is None else jnp.where(sub >= blk * s, src, out)
    return out.reshape(c, lanes)


def _gla_pair_masks(c):
    ri = lax.broadcasted_iota(jnp.int32, (c, c), 0)
    ci = lax.broadcasted_iota(jnp.int32, (c, c), 1)
    diff_bits = ri ^ ci
    masks = [diff_bits == 0]
    level = 0
    while (1 << level) < c:
        masks.append(((diff_bits >> level) == 1) & (((ri >> level) & 1) == 1))
        level += 1
    return masks


def _gla_chunk(q, k, v, g, s_prev, masks):
    c = q.shape[0]
    cum = _cumsum_rows(g)
    a = jnp.where(masks[0], _dot_nt(q.astype(BF16), k.astype(BF16)), 0.0)
    for level in range(len(masks) - 1):
        half = 1 << level
        ref = _bcast_block_row(cum, 2 * half, half - 1)
        qe = q * jnp.exp(jnp.minimum(cum - ref, 0.0))
        ke = k * jnp.exp(jnp.minimum(ref - cum, 0.0))
        a = a + jnp.where(masks[1 + level], _dot_nt(qe.astype(BF16), ke.astype(BF16)), 0.0)
    o = _dot(a.astype(BF16), v.astype(BF16)) + _dot((q * jnp.exp(cum)).astype(BF16), s_prev.astype(BF16))
    last = cum[c - 1:c, :]
    kd = k * jnp.exp(last - cum)
    dk = last.shape[1]
    decay_t = jnp.transpose(jnp.broadcast_to(jnp.exp(last), (dk, dk)))
    decayed = jnp.concatenate([decay_t * s_prev[:, i:i + dk] for i in range(0, s_prev.shape[1], dk)], axis=1)
    s_new = decayed + _dot_tn(kd.astype(BF16), v.astype(BF16))
    return o, s_new


def _gla_kernel(q_ref, k_ref, v_ref, r_ref, g_ref, s0_ref, gn_ref, o_ref, sn_ref, *, chunk, single_chunk, group):
    bb, rows = q_ref.shape[0], q_ref.shape[1]
    nchunk = rows // chunk
    state_in = s0_ref if single_chunk else sn_ref

    if not single_chunk:
        @pl.when(pl.program_id(1) == 0)
        def _():
            sn_ref[...] = s0_ref[...]

    masks = _gla_pair_masks(chunk)

    def one(b, r):
        for h in range(GLA_HEADS):
            kc = slice(h * GLA_DK, (h + 1) * GLA_DK)
            vc = slice(h * GLA_DV, (h + 1) * GLA_DV)
            q = q_ref[b, r, kc] * (GLA_DK ** -0.5)
            o, s_new = _gla_chunk(q, k_ref[b, r, kc], v_ref[b, r, vc], g_ref[b, r, kc], state_in[b, h], masks)
            sn_ref[b, h] = s_new
            rr = r_ref[b, r, vc]
            o_ref[b, r, vc] = (_rms_rows(o, gn_ref[:, vc]) * (rr * jax.nn.sigmoid(rr))).astype(o_ref.dtype)

    def body(n, carry):
        r = pl.ds(pl.multiple_of((n % nchunk) * chunk, chunk), chunk)
        for u in range(group):
            one((n // nchunk) * group + u, r)
        return carry

    lax.fori_loop(0, (bb // group) * nchunk, body, 0)


def _gla(p, logf, s0, gla_norm, *, q_blk, k_blk, v_blk, r_blk, bb, rows, chunk, group):
    nb, t, _ = p.shape
    nk, nv = GLA_HEADS * GLA_DK, GLA_HEADS * GLA_DV
    kern = functools.partial(_gla_kernel, chunk=chunk, single_chunk=(t == chunk), group=group)
    state_spec = pl.BlockSpec((bb, GLA_HEADS, GLA_DK, GLA_DV), lambda b, c: (b, 0, 0, 0))
    return pl.pallas_call(
        kern,
        grid=(nb // bb, t // rows),
        in_specs=[pl.BlockSpec((bb, rows, nk), lambda b, c: (b, c, q_blk)),
                  pl.BlockSpec((bb, rows, nk), lambda b, c: (b, c, k_blk)),
                  pl.BlockSpec((bb, rows, nv), lambda b, c: (b, c, v_blk)),
                  pl.BlockSpec((bb, rows, nv), lambda b, c: (b, c, r_blk)),
                  pl.BlockSpec((bb, rows, nk), lambda b, c: (b, c, 0)),
                  state_spec,
                  pl.BlockSpec((1, nv), lambda b, c: (0, 0))],
        out_specs=[pl.BlockSpec((bb, rows, nv), lambda b, c: (b, c, 0)),
                   state_spec],
        out_shape=[jax.ShapeDtypeStruct((nb, t, D_GLA), BF16),
                   jax.ShapeDtypeStruct((nb, GLA_HEADS, GLA_DK, GLA_DV), F32)],
        compiler_params=_params("arbitrary", "arbitrary"),
        name="gla",
    )(p, p, p, p, logf, s0, gla_norm)


def _proj_res_norm_kernel(*refs, n_a, nk, final):
    a_refs, (w_ref, res_ref, g_ref) = refs[:n_a], refs[n_a:n_a + 3]
    out_refs, acc_ref = refs[n_a + 3:-1], refs[-1]
    k = pl.program_id(1)
    part, r0 = None, 0
    for a_ref in a_refs:
        kw = a_ref.shape[1]
        term = _dot(a_ref[...], w_ref[r0:r0 + kw, :])
        part = term if part is None else part + term
        r0 += kw

    @pl.when(k == 0)
    def _():
        acc_ref[...] = part

    if nk > 1:
        @pl.when(k > 0)
        def _():
            acc_ref[...] += part

    @pl.when(k == nk - 1)
    def _():
        rows = acc_ref.shape[0]
        ch = _row_chunk(rows, NORM_ROWS)
        g = g_ref[...]

        def body(c, carry):
            r = pl.ds(pl.multiple_of(c * ch, ch), ch)
            h = res_ref[r, :] + acc_ref[r, :]
            hn = _rms_rows(h, g)
            if final:
                out_refs[0][r, :] = hn
            else:
                out_refs[0][r, :] = h
                out_refs[1][r, :] = hn.astype(out_refs[1].dtype)
            return carry

        lax.fori_loop(0, rows // ch, body, 0)


def _proj_res_norm(a_list, w, res, gain, *, tm, tk, final):
    m = a_list[0].shape[0]
    kdim, d = w.shape
    nk = kdim // tk
    assert len(a_list) == 1 or nk == 1
    kern = functools.partial(_proj_res_norm_kernel, n_a=len(a_list), nk=nk, final=final)
    a_specs = ([pl.BlockSpec((tm, tk), lambda i, k: (i, k))] if len(a_list) == 1 else
               [pl.BlockSpec((tm, a.shape[1]), lambda i, k: (i, 0)) for a in a_list])
    row_spec = pl.BlockSpec((tm, d), lambda i, k: (i, 0))
    if final:
        out_specs = row_spec
        out_shape = jax.ShapeDtypeStruct((m, d), F32)
    else:
        out_specs = [row_spec, row_spec]
        out_shape = [jax.ShapeDtypeStruct((m, d), F32), jax.ShapeDtypeStruct((m, d), BF16)]
    return pl.pallas_call(
        kern,
        grid=(m // tm, nk),
        in_specs=a_specs + [pl.BlockSpec((tk, d), lambda i, k: (k, 0),
                                         pipeline_mode=pl.Buffered(1 if nk == 1 else 2)),
                            row_spec,
                            pl.BlockSpec((1, d), lambda i, k: (0, 0))],
        out_specs=out_specs,
        out_shape=out_shape,
        scratch_shapes=[pltpu.VMEM((tm, d), F32)],
        compiler_params=_params("arbitrary", "arbitrary"),
        name="proj_res_norm",
    )(*a_list, w, res, gain)


def _xattn_kernel(q_ref, k_ref, v_ref, o_ref):
    bb = q_ref.shape[0]

    def body(b, carry):
        q = q_ref[b].astype(BF16)
        p = _softmax_rows(_dot_nt(q, k_ref[b].astype(BF16)) * (X_HD ** -0.5))
        o_ref[b] = _dot(p.astype(BF16), v_ref[b].astype(BF16)).astype(o_ref.dtype)
        return carry

    lax.fori_loop(0, bb, body, 0)


def _xattn(q, mk, mv, *, bb, tq):
    nb, t, d = q.shape
    return pl.pallas_call(
        _xattn_kernel,
        grid=(nb // bb, X_HEADS, t // tq),
        in_specs=[pl.BlockSpec((bb, tq, X_HD), lambda b, h, i: (b, i, h)),
                  pl.BlockSpec((bb, MEM_LEN, X_HD), lambda b, h, i: (b, 0, h)),
                  pl.BlockSpec((bb, MEM_LEN, X_HD), lambda b, h, i: (b, 0, h))],
        out_specs=pl.BlockSpec((bb, tq, X_HD), lambda b, h, i: (b, i, h)),
        out_shape=jax.ShapeDtypeStruct((nb, t, d), BF16),
        compiler_params=_params("arbitrary", "arbitrary", "arbitrary"),
        name="xattn",
    )(q, mk, mv)


def _softmax_rows(s):
    s = s - jnp.max(s, axis=-1, keepdims=True)
    e = jnp.exp(s)
    return e / jnp.sum(e, axis=-1, keepdims=True)


def _xattn_cache_kernel(q_ref, k_ref, v_ref, o_ref):
    bb, tq = q_ref.shape[0], q_ref.shape[1]
    nchunk = X_HD // LANES
    pitch = nchunk * X_HEADS

    def gather(ref, b, h):
        parts = [ref[b, pl.ds(c * X_HEADS + h, MEM_LEN, stride=pitch), :] for c in range(nchunk)]
        return jnp.concatenate(parts, axis=1).astype(BF16)

    pairs = [(b, h) for b in range(bb) for h in range(X_HEADS)]
    scores = [_dot_nt(q_ref[b, :, h * X_HD:(h + 1) * X_HD].astype(BF16), gather(k_ref, b, h)) for b, h in pairs]
    p = _softmax_rows(jnp.concatenate(scores, axis=0) * (X_HD ** -0.5)).astype(BF16)
    for n, (b, h) in enumerate(pairs):
        o_ref[b, :, h * X_HD:(h + 1) * X_HD] = _dot(p[n * tq:(n + 1) * tq], gather(v_ref, b, h)).astype(o_ref.dtype)


def _xattn_cache(q, ck, cv, *, bb):
    nb, tq, d = q.shape
    nchunk = X_HD // LANES

    def stored_order(c):
        c = c.reshape(nb, MEM_LEN, X_HEADS, nchunk, LANES).transpose(0, 1, 3, 2, 4)
        return c.reshape(nb, MEM_LEN * nchunk * X_HEADS, LANES)

    rows = MEM_LEN * nchunk * X_HEADS
    return pl.pallas_call(
        _xattn_cache_kernel,
        grid=(nb // bb,),
        in_specs=[pl.BlockSpec((bb, tq, d), lambda b: (b, 0, 0)),
                  pl.BlockSpec((bb, rows, LANES), lambda b: (b, 0, 0)),
                  pl.BlockSpec((bb, rows, LANES), lambda b: (b, 0, 0))],
        out_specs=pl.BlockSpec((bb, tq, d), lambda b: (b, 0, 0)),
        out_shape=jax.ShapeDtypeStruct((nb, tq, d), BF16),
        compiler_params=_params("arbitrary"),
        name="xattn_cache",
    )(q, stored_order(ck), stored_order(cv))


def _ffn_up_kernel(hn_ref, wg_ref, wu_ref, cw_ref, cb_ref, prev_ref, o_ref, new_ref, buf_ref, up_ref,
                   wgb_ref, wub_ref, *, shift):
    rows = hn_ref.shape[0]
    halo = _halo_rows(shift)
    ch = _row_chunk(rows)

    @pl.when(pl.program_id(1) == 0)
    def _():
        _cast_into(wg_ref, wgb_ref)
        _cast_into(wu_ref, wub_ref)

    _conv_stage(buf_ref, halo - 2 * shift, prev_ref[0])
    _conv_stage(buf_ref, halo, _dot(hn_ref[...], wgb_ref[...]))
    up_ref[...] = _dot(hn_ref[...], wub_ref[...])

    def act_body(c, carry):
        r0 = pl.multiple_of(c * ch, ch)
        r = pl.ds(r0, ch)
        for s in range(buf_ref.shape[0]):
            g2, g1, g0 = _conv_taps(buf_ref, s, r0, ch, shift)
            gc = (cw_ref[0:1, _slab(s)] * g2 + cw_ref[1:2, _slab(s)] * g1 + cw_ref[2:3, _slab(s)] * g0
                  + cb_ref[:, _slab(s)])
            o_ref[r, _slab(s)] = ((gc * jax.nn.sigmoid(gc)) * up_ref[r, _slab(s)]).astype(o_ref.dtype)
        return carry

    lax.fori_loop(0, rows // ch, act_body, 0)
    _conv_last(buf_ref, new_ref, rows, shift)


def _ffn_up(hn, wg, wu, cw, cb, prev, nseq, rows, shift, tn):
    d = hn.shape[1]
    kern = functools.partial(_ffn_up_kernel, shift=shift)
    return pl.pallas_call(
        kern,
        grid=(D_FF // tn, nseq),
        in_specs=[pl.BlockSpec((rows, d), lambda j, b: (b, 0)),
                  pl.BlockSpec((d, tn), lambda j, b: (0, j)),
                  pl.BlockSpec((d, tn), lambda j, b: (0, j)),
                  pl.BlockSpec((CONV_WIDTH, tn), lambda j, b: (0, j)),
                  pl.BlockSpec((1, tn), lambda j, b: (0, j)),
                  pl.BlockSpec((1, 2 * shift, tn), lambda j, b: (b, 0, j))],
        out_specs=[pl.BlockSpec((rows, tn), lambda j, b: (b, j)),
                   pl.BlockSpec((1, 2 * shift, tn), lambda j, b: (b, 0, j))],
        out_shape=[jax.ShapeDtypeStruct((nseq * rows, D_FF), BF16),
                   jax.ShapeDtypeStruct((nseq, 2 * shift, D_FF), F32)],
        scratch_shapes=[_conv_buf(rows, tn, shift), pltpu.VMEM((rows, tn), F32),
                        pltpu.VMEM((d, tn), BF16), pltpu.VMEM((d, tn), BF16)],
        compiler_params=_params("arbitrary", "arbitrary"),
        name="ffn_up",
    )(hn, wg, wu, cw, cb, prev)


def _layer(x, w, *, nseq, rows, shift, prev_conv, s0, prev_ffn, mk, mv, time_major):
    m = x.shape[0]
    tm = min(m, 1024)
    xn, logf = _norm_gate(x, w["norm_mix"], w["w_g1_t"], w["w_g2"], w["b_gate"], tm)
    conv_out, conv_new = _conv_proj(xn, w["w_in_t"], prev_conv, w["conv_w"], nseq=nseq, shift=shift, tm=tm, tn=512)
    n_gla = N_MAIN - 3 * D_CONV
    p = _col_matmul(xn, w["w_in_t"], 3 * D_CONV, n_gla, F32, tm, 1024)

    if time_major:
        nt = m // shift
        pad = ((0, 0), (0, SUBLANES - nt), (0, 0))
        pg = jnp.pad(p.reshape(nt, shift, n_gla).transpose(1, 0, 2), pad)
        lg = jnp.pad(logf.reshape(nt, shift, -1).transpose(1, 0, 2), pad)
        o, s_new = _gla(pg, lg, s0, w["gla_norm"], q_blk=0, k_blk=1, v_blk=1, r_blk=2,
                        bb=8, rows=SUBLANES, chunk=SUBLANES, group=4)
        gla_out = o[:, :nt].transpose(1, 0, 2).reshape(m, D_GLA)
    else:
        o, s_new = _gla(p.reshape(nseq, rows, n_gla), logf.reshape(nseq, rows, -1), s0, w["gla_norm"],
                        q_blk=0, k_blk=1, v_blk=1, r_blk=2, bb=2, rows=512, chunk=GLA_CHUNK, group=2)
        gla_out = o.reshape(m, D_GLA)

    tm2 = min(m, 512)
    h, hn = _proj_res_norm([conv_out, gla_out], w["w_out"], x, w["norm_x"], tm=tm2, tk=D_MODEL, final=False)
    qx = _matmul(hn, w["w_xq"], BF16, tm)

    if time_major:
        nt = m // shift
        qb = jnp.pad(qx.astype(F32).reshape(nt, shift, D_MODEL).transpose(1, 0, 2),
                     ((0, 0), (0, SUBLANES - nt), (0, 0)))
        ob = _xattn_cache(qb, mk, mv, bb=4)
        attn = ob[:, :nt].transpose(1, 0, 2).reshape(m, D_MODEL)
    else:
        attn = _xattn(qx.reshape(nseq, rows, D_MODEL), mk, mv, bb=1, tq=rows).reshape(m, D_MODEL)

    h2, hn2 = _proj_res_norm([attn], w["w_xo"], h, w["norm_ffn"], tm=tm2, tk=D_MODEL, final=False)
    act, ffn_new = _ffn_up(hn2, w["w_fg"], w["w_fu"], w["ffn_conv_w"], w["ffn_conv_b"], prev_ffn,
                           nseq, rows, shift, 512)
    return act, h2, conv_new, s_new, ffn_new


def kernel(x_prompt, x_sample, mem_prompt, cache_conv, state_gla, cache_ffn, cache_mem_k, cache_mem_v,
           norm_mix, w_in, conv_w, w_gate2, b_gate, gla_norm, w_out, norm_x, norm_mem, w_xq, w_xk, w_xv,
           w_xo, norm_ffn, w_ffn_gate, w_ffn_up, ffn_conv_w, ffn_conv_b, w_ffn_down, norm_final):
    depth = w_in.shape[0]
    nb, seq, d = x_prompt.shape
    db, dseq, _ = x_sample.shape
    hp = x_prompt.reshape(nb * seq, d)
    hs = x_sample.transpose(1, 0, 2).reshape(dseq * db, d)
    outs = {k: [] for k in ("conv_p", "gla_p", "ffn_p", "mk", "mv", "conv_s", "gla_s", "ffn_s")}
    nfinal = norm_final.reshape(1, d)
    yp = ys = None
    for l in range(depth):
        w = {
            "norm_mix": norm_mix[l].reshape(1, d),
            "w_in_t": w_in[l].T,
            "w_g1_t": jnp.pad(w_in[l].T[N_MAIN:], ((0, LANES - GLA_RANK), (0, 0))).astype(BF16),
            "w_g2": jnp.pad(w_gate2[l], ((0, LANES - GLA_RANK), (0, 0))).astype(BF16),
            "b_gate": b_gate[l].reshape(1, -1),
            "conv_w": conv_w[l],
            "gla_norm": gla_norm[l].reshape(1, -1),
            "w_out": w_out[l].astype(BF16),
            "norm_x": norm_x[l].reshape(1, d),
            "w_xq": w_xq[l],
            "w_xo": w_xo[l].astype(BF16),
            "norm_ffn": norm_ffn[l].reshape(1, d),
            "w_fg": w_ffn_gate[l],
            "w_fu": w_ffn_up[l],
            "ffn_conv_w": ffn_conv_w[l],
            "ffn_conv_b": ffn_conv_b[l].reshape(1, -1),
        }
        w_fd = w_ffn_down[l].astype(BF16)
        last = l == depth - 1
        gain_next = nfinal if last else None

        mem = mem_prompt.reshape(nb * MEM_LEN, d)
        nmem = norm_mem[l].reshape(1, d)
        mk, mk_cache = _mem_proj(mem, nmem, w_xk[l], 1024)
        mv, mv_cache = _mem_proj(mem, nmem, w_xv[l], 1024)
        act, h2, c1, s1, f1 = _layer(
            hp, w, nseq=nb, rows=seq, shift=1,
            prev_conv=jnp.zeros((nb, CONV_WIDTH - 1, D_CONV), F32),
            s0=jnp.zeros((nb, GLA_HEADS, GLA_DK, GLA_DV), F32),
            prev_ffn=jnp.zeros((nb, CONV_WIDTH - 1, D_FF), F32),
            mk=mk.reshape(nb, MEM_LEN, d), mv=mv.reshape(nb, MEM_LEN, d), time_major=False)
        assert last, "only the final layer's epilogue (final rmsnorm) is implemented"
        yp = _proj_res_norm([act], w_fd, h2, gain_next, tm=256, tk=D_FF, final=True)
        outs["conv_p"].append(c1)
        outs["gla_p"].append(s1)
        outs["ffn_p"].append(f1)
        outs["mk"].append(mk_cache)
        outs["mv"].append(mv_cache)

        def tmajor(c):
            return c.transpose(1, 0, 2).reshape(1, (CONV_WIDTH - 1) * db, c.shape[-1])

        act, h2, c2, s2, f2 = _layer(
            hs, w, nseq=1, rows=dseq * db, shift=db,
            prev_conv=tmajor(cache_conv[l]), s0=state_gla[l], prev_ffn=tmajor(cache_ffn[l]),
            mk=cache_mem_k[l], mv=cache_mem_v[l], time_major=True)
        ys = _proj_res_norm([act], w_fd, h2, gain_next, tm=256, tk=D_FF, final=True)
        outs["conv_s"].append(c2.reshape(CONV_WIDTH - 1, db, D_CONV).transpose(1, 0, 2))
        outs["gla_s"].append(s2)
        outs["ffn_s"].append(f2.reshape(CONV_WIDTH - 1, db, D_FF).transpose(1, 0, 2))

    y_prompt = yp.reshape(nb, seq, d)
    y_sample = ys.reshape(dseq, db, d).transpose(1, 0, 2)
    st = lambda k: jnp.stack(outs[k])
    return (y_prompt, y_sample, st("conv_p"), st("gla_p"), st("ffn_p"), st("mk"), st("mv"),
            st("conv_s"), st("gla_s"), st("ffn_s"))
```

```python
import functools

import jax
import jax.numpy as jnp
from jax import lax
from jax.experimental import pallas as pl
from jax.experimental.pallas import tpu as pltpu

F32 = jnp.float32
BF16 = jnp.bfloat16

D_MODEL = 2048
EPS = 1e-6
CONV_WIDTH = 3
D_CONV = 1024
D_GLA = 1024
GLA_HEADS = 4
GLA_DV = 256
GLA_DK = 128
GLA_RANK = 16
GLA_TAU = 16.0
LOG2_E = 1.4426950408889634
GLA_CHUNK = 64
X_HEADS = 4
X_HD = 512
MEM_LEN = 256
D_FF = 5632
N_MAIN = 3 * D_CONV + 2 * GLA_HEADS * GLA_DK + 2 * GLA_HEADS * GLA_DV

LANES = 128
SUBLANES = 8
VMEM_LIMIT_BYTES = 56 * 1024 * 1024


def _params(*sem):
    return pltpu.CompilerParams(dimension_semantics=sem, vmem_limit_bytes=VMEM_LIMIT_BYTES)


def _dot(a, b):
    return jnp.dot(a, b, preferred_element_type=F32)


def _dot_nt(a, b):
    return lax.dot_general(a, b, (((1,), (1,)), ((), ())), preferred_element_type=F32)


def _dot_tn(a, b):
    return lax.dot_general(a, b, (((0,), (0,)), ((), ())), preferred_element_type=F32)


def _rms_rows(x, g):
    ms = jnp.mean(x * x, axis=-1, keepdims=True)
    return (x * lax.rsqrt(ms + EPS)) * g


def _row_chunk(rows, limit=256):
    for c in (256, 128, 64, 32, 16, 8):
        if c <= limit and rows % c == 0:
            return c
    return rows


NORM_ROWS = 128


def _norm_into(x_ref, g_ref, xn_ref):
    rows = x_ref.shape[0]
    ch = _row_chunk(rows)
    g = g_ref[...]

    def body(c, carry):
        r = pl.ds(pl.multiple_of(c * ch, ch), ch)
        xn_ref[r, :] = _rms_rows(x_ref[r, :], g).astype(xn_ref.dtype)
        return carry

    lax.fori_loop(0, rows // ch, body, 0)


def _norm_matmul_kernel(x_ref, g_ref, w_ref, o_ref, oc_ref, xn_ref):
    j = pl.program_id(1)

    @pl.when(j == 0)
    def _():
        _norm_into(x_ref, g_ref, xn_ref)

    o_ref[...] = _dot(xn_ref[...], w_ref[...].astype(BF16)).astype(o_ref.dtype)
    nchunk = X_HD // LANES
    pitch = nchunk * X_HEADS
    heads_per_tile = o_ref.shape[1] // X_HD
    for b in range(oc_ref.shape[0]):
        for hh in range(heads_per_tile):
            for c in range(nchunk):
                col = hh * X_HD + c * LANES
                row = c * X_HEADS + j * heads_per_tile + hh
                oc_ref[b, pl.ds(row, MEM_LEN, stride=pitch), :] = (
                    o_ref[b * MEM_LEN:(b + 1) * MEM_LEN, col:col + LANES])


def _norm_gate_kernel(x_ref, g_ref, wg1_ref, wg2_ref, bg_ref, xn_ref, lf_ref):
    _norm_into(x_ref, g_ref, xn_ref)
    g1 = _dot_nt(xn_ref[...], wg1_ref[...])
    z = _dot(g1.astype(BF16), wg2_ref[...]) + bg_ref[...]
    lf_ref[...] = (jnp.minimum(z, 0.0) - jnp.log1p(jnp.exp(-jnp.abs(z)))) * (LOG2_E / GLA_TAU)


def _mem_proj(mem, gain, w, tn):
    m, d = mem.shape
    n = w.shape[1]
    nb = m // MEM_LEN
    nchunk = X_HD // LANES
    rows = MEM_LEN * nchunk * X_HEADS
    flat, stored = pl.pallas_call(
        _norm_matmul_kernel,
        grid=(1, n // tn),
        in_specs=[pl.BlockSpec((m, d), lambda i, j: (0, 0)),
                  pl.BlockSpec((1, d), lambda i, j: (0, 0)),
                  pl.BlockSpec((d, tn), lambda i, j: (0, j))],
        out_specs=[pl.BlockSpec((m, tn), lambda i, j: (0, j)),
                   pl.BlockSpec((nb, rows, LANES), lambda i, j: (0, 0, 0))],
        out_shape=[jax.ShapeDtypeStruct((m, n), F32),
                   jax.ShapeDtypeStruct((nb, rows, LANES), F32)],
        scratch_shapes=[pltpu.VMEM((m, d), BF16)],
        compiler_params=_params("arbitrary", "arbitrary"),
        name="norm_matmul",
    )(mem, gain, w)
    cache = stored.reshape(nb, MEM_LEN, nchunk, X_HEADS, LANES).transpose(0, 1, 3, 2, 4)
    return flat, cache.reshape(nb, MEM_LEN, X_HEADS, X_HD)


def _norm_gate(x, gain, wg1, wg2, b_gate, tm):
    m, d = x.shape
    ng = wg2.shape[1]
    return pl.pallas_call(
        _norm_gate_kernel,
        grid=(m // tm,),
        in_specs=[pl.BlockSpec((tm, d), lambda i: (i, 0)),
                  pl.BlockSpec((1, d), lambda i: (0, 0)),
                  pl.BlockSpec((LANES, d), lambda i: (0, 0)),
                  pl.BlockSpec((LANES, ng), lambda i: (0, 0)),
                  pl.BlockSpec((1, ng), lambda i: (0, 0))],
        out_specs=[pl.BlockSpec((tm, d), lambda i: (i, 0)),
                   pl.BlockSpec((tm, ng), lambda i: (i, 0))],
        out_shape=[jax.ShapeDtypeStruct((m, d), BF16),
                   jax.ShapeDtypeStruct((m, ng), F32)],
        compiler_params=_params("arbitrary"),
        name="norm_gate",
    )(x, gain, wg1, wg2, b_gate)


def _cast_into(src_ref, dst_ref):
    rows = src_ref.shape[0]
    ch = _row_chunk(rows)

    def body(c, carry):
        r = pl.ds(pl.multiple_of(c * ch, ch), ch)
        dst_ref[r, :] = src_ref[r, :].astype(dst_ref.dtype)
        return carry

    lax.fori_loop(0, rows // ch, body, 0)


def _matmul_kernel(a_ref, w_ref, o_ref, wb_ref):
    @pl.when(pl.program_id(0) == 0)
    def _():
        _cast_into(w_ref, wb_ref)

    o_ref[...] = _dot(a_ref[...], wb_ref[...]).astype(o_ref.dtype)


def _matmul(a, w, out_dtype, tm):
    m, k = a.shape
    n = w.shape[1]
    return pl.pallas_call(
        _matmul_kernel,
        grid=(m // tm,),
        in_specs=[pl.BlockSpec((tm, k), lambda i: (i, 0)),
                  pl.BlockSpec((k, n), lambda i: (0, 0), pipeline_mode=pl.Buffered(1))],
        out_specs=pl.BlockSpec((tm, n), lambda i: (i, 0)),
        out_shape=jax.ShapeDtypeStruct((m, n), out_dtype),
        scratch_shapes=[pltpu.VMEM((k, n), BF16)],
        compiler_params=_params("arbitrary"),
        name="matmul",
    )(a, w)


def _col_matmul_kernel(a_ref, wt_ref, o_ref, wb_ref):
    @pl.when(pl.program_id(1) == 0)
    def _():
        _cast_into(wt_ref, wb_ref)

    o_ref[...] = _dot_nt(a_ref[...], wb_ref[...]).astype(o_ref.dtype)


def _col_matmul(a, wt, col0, ncols, out_dtype, tm, tn):
    m, k = a.shape
    j0 = col0 // tn
    return pl.pallas_call(
        _col_matmul_kernel,
        grid=(ncols // tn, m // tm),
        in_specs=[pl.BlockSpec((tm, k), lambda j, i: (i, 0)),
                  pl.BlockSpec((tn, k), lambda j, i: (j0 + j, 0))],
        out_specs=pl.BlockSpec((tm, tn), lambda j, i: (i, j)),
        out_shape=jax.ShapeDtypeStruct((m, ncols), out_dtype),
        scratch_shapes=[pltpu.VMEM((tn, k), BF16)],
        compiler_params=_params("arbitrary", "arbitrary"),
        name="col_matmul",
    )(a, wt)


def _halo_rows(shift):
    return max(2 * shift, SUBLANES)


def _conv_buf(rows, tn, shift):
    return pltpu.VMEM((tn // LANES, _halo_rows(shift) + rows, LANES), F32)


def _slab(s):
    return slice(s * LANES, (s + 1) * LANES)


def _conv_stage(buf_ref, row0, x):
    for s in range(buf_ref.shape[0]):
        buf_ref[s, row0:row0 + x.shape[0], :] = x[:, _slab(s)]


def _conv_taps(buf_ref, s, r0, rows, shift):
    halo = _halo_rows(shift)

    def back(steps):
        start = r0 + (halo - steps * shift)
        if (steps * shift) % SUBLANES == 0:
            return pl.ds(start if isinstance(start, int) else pl.multiple_of(start, SUBLANES), rows)
        return pl.ds(start, rows, stride=1)

    return buf_ref[s, back(2), :], buf_ref[s, back(1), :], buf_ref[s, back(0), :]


def _conv_last(buf_ref, new_ref, rows, shift):
    halo = _halo_rows(shift)
    for s in range(buf_ref.shape[0]):
        new_ref[0, :, _slab(s)] = buf_ref[s, halo + rows - 2 * shift:halo + rows, :]


def _conv_proj_kernel(xn_ref, wbg_ref, wcg_ref, wvc_ref, prev_ref, cw_ref, o_ref, new_ref,
                      bgb_ref, cgb_ref, vcb_ref, buf_ref, *, shift, tiles_per_seq):
    rows = xn_ref.shape[0]
    halo = _halo_rows(shift)
    i = pl.program_id(1)

    @pl.when(i == 0)
    def _():
        _cast_into(wbg_ref, bgb_ref)
        _cast_into(wcg_ref, cgb_ref)
        _cast_into(wvc_ref, vcb_ref)

    def from_cache():
        _conv_stage(buf_ref, halo - 2 * shift, prev_ref[0])

    if tiles_per_seq == 1:
        from_cache()
    else:
        pl.when(i % tiles_per_seq == 0)(from_cache)

        @pl.when(i % tiles_per_seq != 0)
        def _():
            for s in range(buf_ref.shape[0]):
                buf_ref[s, halo - 2 * shift:halo, :] = buf_ref[s, halo + rows - 2 * shift:halo + rows, :]

    xn = xn_ref[...]
    _conv_stage(buf_ref, halo, _dot_nt(xn, cgb_ref[...]) * _dot_nt(xn, vcb_ref[...]))
    bg = _dot_nt(xn, bgb_ref[...])
    for s in range(buf_ref.shape[0]):
        u2, u1, u0 = _conv_taps(buf_ref, s, 0, rows, shift)
        y = cw_ref[0:1, _slab(s)] * u2 + cw_ref[1:2, _slab(s)] * u1 + cw_ref[2:3, _slab(s)] * u0
        o_ref[:, _slab(s)] = (bg[:, _slab(s)] * y).astype(o_ref.dtype)
    _conv_last(buf_ref, new_ref, rows, shift)


def _conv_proj(xn, w_in_t, prev, conv_w, *, nseq, shift, tm, tn):
    m, d = xn.shape
    nj = D_CONV // tn
    tiles_per_seq = m // (nseq * tm)
    kern = functools.partial(_conv_proj_kernel, shift=shift, tiles_per_seq=tiles_per_seq)
    state = pl.BlockSpec((1, 2 * shift, tn), lambda j, i: (i // tiles_per_seq, 0, j))
    wb = pltpu.VMEM((tn, d), BF16)
    return pl.pallas_call(
        kern,
        grid=(nj, m // tm),
        in_specs=[pl.BlockSpec((tm, d), lambda j, i: (i, 0)),
                  pl.BlockSpec((tn, d), lambda j, i: (j, 0)),
                  pl.BlockSpec((tn, d), lambda j, i: (j + nj, 0)),
                  pl.BlockSpec((tn, d), lambda j, i: (j + 2 * nj, 0)),
                  state,
                  pl.BlockSpec((CONV_WIDTH, tn), lambda j, i: (0, j))],
        out_specs=[pl.BlockSpec((tm, tn), lambda j, i: (i, j)),
                   state],
        out_shape=[jax.ShapeDtypeStruct((m, D_CONV), BF16),
                   jax.ShapeDtypeStruct((nseq, 2 * shift, D_CONV), F32)],
        scratch_shapes=[wb, wb, wb, _conv_buf(tm, tn, shift)],
        compiler_params=_params("arbitrary", "arbitrary"),
        name="conv_proj",
    )(xn, w_in_t, w_in_t, w_in_t, prev, conv_w)


def _cumsum_rows(g):
    c = g.shape[0]
    row = lax.broadcasted_iota(jnp.int32, g.shape, 0)
    x = g
    s = 1
    while s < c:
        x = x + jnp.where(row >= s, pltpu.roll(x, s, 0), 0.0)
        s *= 2
    return x


def _bcast_block_row(x, s, k):
    c, lanes = x.shape
    if s == c:
        return jnp.broadcast_to(x[k:k + 1, :], x.shape)
    if s >= SUBLANES:
        y = x.reshape(c // s, s, lanes)
        return jnp.broadcast_to(y[:, k:k + 1, :], y.shape).reshape(c, lanes)
    y = x.reshape(c // SUBLANES, SUBLANES, lanes)
    sub = lax.broadcasted_iota(jnp.int32, y.shape, 1)
    out = None
    for blk in range(SUBLANES // s):
        src = jnp.broadcast_to(y[:, blk * s + k:blk * s + k + 1, :], y.shape)
        out = src if out is None else jnp.where(sub >= blk * s, src, out)
    return out.reshape(c, lanes)


def _gla_pair_masks(c):
    ri = lax.broadcasted_iota(jnp.int32, (c, c), 0)
    ci = lax.broadcasted_iota(jnp.int32, (c, c), 1)
    diff_bits = ri ^ ci
    masks = [diff_bits == 0]
    level = 0
    while (1 << level) < c:
        masks.append(((diff_bits >> level) == 1) & (((ri >> level) & 1) == 1))
        level += 1
    return masks


def _gla_chunk(q, k, v, g, s_prev, masks):
    c = q.shape[0]
    cum = _cumsum_rows(g)
    a = jnp.where(masks[0], _dot_nt(q.astype(BF16), k.astype(BF16)), 0.0)
    for level in range(len(masks) - 1):
        half = 1 << level
        ref = _bcast_block_row(cum, 2 * half, half - 1)
        d = cum - ref
        up = jnp.minimum(d, 0.0)
        qe = q * jnp.exp2(up)
        ke = k * jnp.exp2(up - d)
        a = a + jnp.where(masks[1 + level], _dot_nt(qe.astype(BF16), ke.astype(BF16)), 0.0)
    o = _dot(a.astype(BF16), v.astype(BF16)) + _dot((q * jnp.exp2(cum)).astype(BF16), s_prev.astype(BF16))
    last = cum[c - 1:c, :]
    kd = k * jnp.exp2(last - cum)
    dk = last.shape[1]
    decay_t = jnp.transpose(jnp.broadcast_to(jnp.exp2(last), (dk, dk)))
    decayed = jnp.concatenate([decay_t * s_prev[:, i:i + dk] for i in range(0, s_prev.shape[1], dk)], axis=1)
    s_new = decayed + _dot_tn(kd.astype(BF16), v.astype(BF16))
    return o, s_new


def _gla_kernel(q_ref, k_ref, v_ref, r_ref, g_ref, s0_ref, gn_ref, *refs, chunk, single_chunk, group, n_cast):
    cast_src, (o_ref, sn_ref), cast_dst = refs[:n_cast], refs[n_cast:n_cast + 2], refs[n_cast + 2:]
    bb, rows = q_ref.shape[0], q_ref.shape[1]
    nchunk = rows // chunk
    state_in = s0_ref if single_chunk else sn_ref

    if not single_chunk:
        @pl.when(pl.program_id(1) == 0)
        def _():
            sn_ref[...] = s0_ref[...]

    masks = _gla_pair_masks(chunk)

    def one(b, r):
        for h in range(GLA_HEADS):
            kc = slice(h * GLA_DK, (h + 1) * GLA_DK)
            vc = slice(h * GLA_DV, (h + 1) * GLA_DV)
            q = q_ref[b, r, kc] * (GLA_DK ** -0.5)
            o, s_new = _gla_chunk(q, k_ref[b, r, kc], v_ref[b, r, vc], g_ref[b, r, kc], state_in[b, h], masks)
            sn_ref[b, h] = s_new
            rr = r_ref[b, r, vc]
            o_ref[b, r, vc] = (_rms_rows(o, gn_ref[:, vc]) * (rr * jax.nn.sigmoid(rr))).astype(o_ref.dtype)

    def body(n, carry):
        r = pl.ds(pl.multiple_of((n % nchunk) * chunk, chunk), chunk)
        for u in range(group):
            one((n // nchunk) * group + u, r)
        return carry

    lax.fori_loop(0, (bb // group) * nchunk, body, 0)
    for src, dst in zip(cast_src, cast_dst):
        _cast_into(src, dst)


def _gla(p, logf, s0, gla_norm, *, q_blk, k_blk, v_blk, r_blk, bb, rows, chunk, group, cast=()):
    nb, t, _ = p.shape
    nk, nv = GLA_HEADS * GLA_DK, GLA_HEADS * GLA_DV
    nt = t // rows
    nsteps = (nb // bb) * nt
    kern = functools.partial(_gla_kernel, chunk=chunk, single_chunk=(t == chunk), group=group, n_cast=len(cast))
    state_spec = pl.BlockSpec((bb, GLA_HEADS, GLA_DK, GLA_DV), lambda b, c: (b, 0, 0, 0))
    cast_specs = [pl.BlockSpec((w.shape[0] // nsteps, w.shape[1]), lambda b, c: (b * nt + c, 0)) for w in cast]
    return pl.pallas_call(
        kern,
        grid=(nb // bb, nt),
        in_specs=[pl.BlockSpec((bb, rows, nk), lambda b, c: (b, c, q_blk)),
                  pl.BlockSpec((bb, rows, nk), lambda b, c: (b, c, k_blk)),
                  pl.BlockSpec((bb, rows, nv), lambda b, c: (b, c, v_blk)),
                  pl.BlockSpec((bb, rows, nv), lambda b, c: (b, c, r_blk)),
                  pl.BlockSpec((bb, rows, nk), lambda b, c: (b, c, 0)),
                  state_spec,
                  pl.BlockSpec((1, nv), lambda b, c: (0, 0))] + cast_specs,
        out_specs=[pl.BlockSpec((bb, rows, nv), lambda b, c: (b, c, 0)),
                   state_spec] + cast_specs,
        out_shape=[jax.ShapeDtypeStruct((nb, t, D_GLA), BF16),
                   jax.ShapeDtypeStruct((nb, GLA_HEADS, GLA_DK, GLA_DV), F32)]
                  + [jax.ShapeDtypeStruct(w.shape, BF16) for w in cast],
        compiler_params=_params("arbitrary", "arbitrary"),
        name="gla",
    )(p, p, p, p, logf, s0, gla_norm, *cast)


def _proj_res_norm_kernel(*refs, n_a, nk, final):
    a_refs, (w_ref, res_ref, g_ref) = refs[:n_a], refs[n_a:n_a + 3]
    out_refs, acc_ref = refs[n_a + 3:-1], refs[-1]
    k = pl.program_id(1)
    part, r0 = None, 0
    for a_ref in a_refs:
        kw = a_ref.shape[1]
        term = _dot(a_ref[...], w_ref[r0:r0 + kw, :])
        part = term if part is None else part + term
        r0 += kw

    @pl.when(k == 0)
    def _():
        acc_ref[...] = part

    if nk > 1:
        @pl.when(k > 0)
        def _():
            acc_ref[...] += part

    @pl.when(k == nk - 1)
    def _():
        rows = acc_ref.shape[0]
        ch = _row_chunk(rows, NORM_ROWS)
        g = g_ref[...]

        def body(c, carry):
            r = pl.ds(pl.multiple_of(c * ch, ch), ch)
            h = res_ref[r, :] + acc_ref[r, :]
            hn = _rms_rows(h, g)
            if final:
                out_refs[0][r, :] = hn
            else:
                out_refs[0][r, :] = h
                out_refs[1][r, :] = hn.astype(out_refs[1].dtype)
            return carry

        lax.fori_loop(0, rows // ch, body, 0)


def _proj_res_norm(a_list, w, res, gain, *, tm, tk, final):
    m = a_list[0].shape[0]
    kdim, d = w.shape
    nk = kdim // tk
    assert len(a_list) == 1 or nk == 1
    kern = functools.partial(_proj_res_norm_kernel, n_a=len(a_list), nk=nk, final=final)
    a_specs = ([pl.BlockSpec((tm, tk), lambda i, k: (i, k))] if len(a_list) == 1 else
               [pl.BlockSpec((tm, a.shape[1]), lambda i, k: (i, 0)) for a in a_list])
    row_spec = pl.BlockSpec((tm, d), lambda i, k: (i, 0))
    if final:
        out_specs = row_spec
        out_shape = jax.ShapeDtypeStruct((m, d), F32)
    else:
        out_specs = [row_spec, row_spec]
        out_shape = [jax.ShapeDtypeStruct((m, d), F32), jax.ShapeDtypeStruct((m, d), BF16)]
    return pl.pallas_call(
        kern,
        grid=(m // tm, nk),
        in_specs=a_specs + [pl.BlockSpec((tk, d), lambda i, k: (k, 0),
                                         pipeline_mode=pl.Buffered(1 if nk == 1 else 2)),
                            row_spec,
                            pl.BlockSpec((1, d), lambda i, k: (0, 0))],
        out_specs=out_specs,
        out_shape=out_shape,
        scratch_shapes=[pltpu.VMEM((tm, d), F32)],
        compiler_params=_params("arbitrary", "arbitrary"),
        name="proj_res_norm",
    )(*a_list, w, res, gain)


def _xattn_kernel(q_ref, k_ref, v_ref, o_ref):
    bb = q_ref.shape[0]

    def body(b, carry):
        q = q_ref[b].astype(BF16)
        p = _softmax_rows(_dot_nt(q, k_ref[b].astype(BF16)) * (X_HD ** -0.5))
        o_ref[b] = _dot(p.astype(BF16), v_ref[b].astype(BF16)).astype(o_ref.dtype)
        return carry

    lax.fori_loop(0, bb, body, 0)


def _xattn(q, mk, mv, *, bb, tq):
    nb, t, d = q.shape
    return pl.pallas_call(
        _xattn_kernel,
        grid=(nb // bb, X_HEADS, t // tq),
        in_specs=[pl.BlockSpec((bb, tq, X_HD), lambda b, h, i: (b, i, h)),
                  pl.BlockSpec((bb, MEM_LEN, X_HD), lambda b, h, i: (b, 0, h)),
                  pl.BlockSpec((bb, MEM_LEN, X_HD), lambda b, h, i: (b, 0, h))],
        out_specs=pl.BlockSpec((bb, tq, X_HD), lambda b, h, i: (b, i, h)),
        out_shape=jax.ShapeDtypeStruct((nb, t, d), BF16),
        compiler_params=_params("arbitrary", "arbitrary", "arbitrary"),
        name="xattn",
    )(q, mk, mv)


def _softmax_rows(s):
    s = s - jnp.max(s, axis=-1, keepdims=True)
    e = jnp.exp(s)
    return e / jnp.sum(e, axis=-1, keepdims=True)


def _xattn_cache_kernel(q_ref, k_ref, v_ref, o_ref):
    bb, tq = q_ref.shape[0], q_ref.shape[1]
    nchunk = X_HD // LANES
    pitch = nchunk * X_HEADS

    def gather(ref, b, h):
        parts = [ref[b, pl.ds(c * X_HEADS + h, MEM_LEN, stride=pitch), :] for c in range(nchunk)]
        return jnp.concatenate(parts, axis=1).astype(BF16)

    pairs = [(b, h) for b in range(bb) for h in range(X_HEADS)]
    scores = [_dot_nt(q_ref[b, :, h * X_HD:(h + 1) * X_HD].astype(BF16), gather(k_ref, b, h)) for b, h in pairs]
    p = _softmax_rows(jnp.concatenate(scores, axis=0) * (X_HD ** -0.5)).astype(BF16)
    for n, (b, h) in enumerate(pairs):
        o_ref[b, :, h * X_HD:(h + 1) * X_HD] = _dot(p[n * tq:(n + 1) * tq], gather(v_ref, b, h)).astype(o_ref.dtype)


def _xattn_cache(q, ck, cv, *, bb):
    nb, tq, d = q.shape
    nchunk = X_HD // LANES

    def stored_order(c):
        c = c.reshape(nb, MEM_LEN, X_HEADS, nchunk, LANES).transpose(0, 1, 3, 2, 4)
        return c.reshape(nb, MEM_LEN * nchunk * X_HEADS, LANES)

    rows = MEM_LEN * nchunk * X_HEADS
    return pl.pallas_call(
        _xattn_cache_kernel,
        grid=(nb // bb,),
        in_specs=[pl.BlockSpec((bb, tq, d), lambda b: (b, 0, 0)),
                  pl.BlockSpec((bb, rows, LANES), lambda b: (b, 0, 0)),
                  pl.BlockSpec((bb, rows, LANES), lambda b: (b, 0, 0))],
        out_specs=pl.BlockSpec((bb, tq, d), lambda b: (b, 0, 0)),
        out_shape=jax.ShapeDtypeStruct((nb, tq, d), BF16),
        compiler_params=_params("arbitrary"),
        name="xattn_cache",
    )(q, stored_order(ck), stored_order(cv))


def _ffn_up_kernel(hn_ref, wg_ref, wu_ref, cw_ref, cb_ref, prev_ref, o_ref, new_ref, buf_ref, up_ref,
                   wgb_ref, wub_ref, *, shift):
    rows = hn_ref.shape[0]
    halo = _halo_rows(shift)
    ch = _row_chunk(rows)

    @pl.when(pl.program_id(1) == 0)
    def _():
        _cast_into(wg_ref, wgb_ref)
        _cast_into(wu_ref, wub_ref)

    _conv_stage(buf_ref, halo - 2 * shift, prev_ref[0])
    _conv_stage(buf_ref, halo, _dot(hn_ref[...], wgb_ref[...]))
    up_ref[...] = _dot(hn_ref[...], wub_ref[...])

    def act_body(c, carry):
        r0 = pl.multiple_of(c * ch, ch)
        r = pl.ds(r0, ch)
        for s in range(buf_ref.shape[0]):
            g2, g1, g0 = _conv_taps(buf_ref, s, r0, ch, shift)
            gc = (cw_ref[0:1, _slab(s)] * g2 + cw_ref[1:2, _slab(s)] * g1 + cw_ref[2:3, _slab(s)] * g0
                  + cb_ref[:, _slab(s)])
            o_ref[r, _slab(s)] = ((gc * jax.nn.sigmoid(gc)) * up_ref[r, _slab(s)]).astype(o_ref.dtype)
        return carry

    lax.fori_loop(0, rows // ch, act_body, 0)
    _conv_last(buf_ref, new_ref, rows, shift)


def _ffn_up(hn, wg, wu, cw, cb, prev, nseq, rows, shift, tn):
    d = hn.shape[1]
    kern = functools.partial(_ffn_up_kernel, shift=shift)
    return pl.pallas_call(
        kern,
        grid=(D_FF // tn, nseq),
        in_specs=[pl.BlockSpec((rows, d), lambda j, b: (b, 0)),
                  pl.BlockSpec((d, tn), lambda j, b: (0, j)),
                  pl.BlockSpec((d, tn), lambda j, b: (0, j)),
                  pl.BlockSpec((CONV_WIDTH, tn), lambda j, b: (0, j)),
                  pl.BlockSpec((1, tn), lambda j, b: (0, j)),
                  pl.BlockSpec((1, 2 * shift, tn), lambda j, b: (b, 0, j))],
        out_specs=[pl.BlockSpec((rows, tn), lambda j, b: (b, j)),
                   pl.BlockSpec((1, 2 * shift, tn), lambda j, b: (b, 0, j))],
        out_shape=[jax.ShapeDtypeStruct((nseq * rows, D_FF), BF16),
                   jax.ShapeDtypeStruct((nseq, 2 * shift, D_FF), F32)],
        scratch_shapes=[_conv_buf(rows, tn, shift), pltpu.VMEM((rows, tn), F32),
                        pltpu.VMEM((d, tn), BF16), pltpu.VMEM((d, tn), BF16)],
        compiler_params=_params("arbitrary", "arbitrary"),
        name="ffn_up",
    )(hn, wg, wu, cw, cb, prev)


def _layer(x, w, wb, *, nseq, rows, shift, prev_conv, s0, prev_ffn, mk, mv, time_major):
    m = x.shape[0]
    tm = min(m, 1024)
    xn, logf = _norm_gate(x, w["norm_mix"], w["w_g1_t"], w["w_g2"], w["b_gate"], tm)
    conv_out, conv_new = _conv_proj(xn, w["w_in_t"], prev_conv, w["conv_w"], nseq=nseq, shift=shift, tm=tm, tn=512)
    n_gla = N_MAIN - 3 * D_CONV
    p = _col_matmul(xn, w["w_in_t"], 3 * D_CONV, n_gla, F32, tm, 1024)

    if time_major:
        nt = m // shift
        pad = ((0, 0), (0, SUBLANES - nt), (0, 0))
        pg = jnp.pad(p.reshape(nt, shift, n_gla).transpose(1, 0, 2), pad)
        lg = jnp.pad(logf.reshape(nt, shift, -1).transpose(1, 0, 2), pad)
        gla_args = dict(bb=8, rows=SUBLANES, chunk=SUBLANES, group=4)
    else:
        pg, lg = p.reshape(nseq, rows, n_gla), logf.reshape(nseq, rows, -1)
        gla_args = dict(bb=2, rows=256, chunk=GLA_CHUNK, group=2)
    cast = () if wb is not None else (w["w_out"], w["w_xo"], w["w_fd"])
    o, s_new, *made = _gla(pg, lg, s0, w["gla_norm"], q_blk=0, k_blk=1, v_blk=1, r_blk=2, cast=cast, **gla_args)
    if wb is None:
        wb = dict(zip(("w_out", "w_xo", "w_fd"), made))
    if time_major:
        gla_out = o[:, :nt].transpose(1, 0, 2).reshape(m, D_GLA)
    else:
        gla_out = o.reshape(m, D_GLA)

    tm2 = min(m, 512)
    h, hn = _proj_res_norm([conv_out, gla_out], wb["w_out"], x, w["norm_x"], tm=tm2, tk=D_MODEL, final=False)
    qx = _matmul(hn, w["w_xq"], BF16, tm)

    if time_major:
        nt = m // shift
        qb = jnp.pad(qx.astype(F32).reshape(nt, shift, D_MODEL).transpose(1, 0, 2),
                     ((0, 0), (0, SUBLANES - nt), (0, 0)))
        ob = _xattn_cache(qb, mk, mv, bb=4)
        attn = ob[:, :nt].transpose(1, 0, 2).reshape(m, D_MODEL)
    else:
        attn = _xattn(qx.reshape(nseq, rows, D_MODEL), mk, mv, bb=1, tq=rows).reshape(m, D_MODEL)

    h2, hn2 = _proj_res_norm([attn], wb["w_xo"], h, w["norm_ffn"], tm=tm2, tk=D_MODEL, final=False)
    act, ffn_new = _ffn_up(hn2, w["w_fg"], w["w_fu"], w["ffn_conv_w"], w["ffn_conv_b"], prev_ffn,
                           nseq, rows, shift, 512)
    return act, h2, conv_new, s_new, ffn_new, wb


def kernel(x_prompt, x_sample, mem_prompt, cache_conv, state_gla, cache_ffn, cache_mem_k, cache_mem_v,
           norm_mix, w_in, conv_w, w_gate2, b_gate, gla_norm, w_out, norm_x, norm_mem, w_xq, w_xk, w_xv,
           w_xo, norm_ffn, w_ffn_gate, w_ffn_up, ffn_conv_w, ffn_conv_b, w_ffn_down, norm_final):
    depth = w_in.shape[0]
    nb, seq, d = x_prompt.shape
    db, dseq, _ = x_sample.shape
    hp = x_prompt.reshape(nb * seq, d)
    hs = x_sample.transpose(1, 0, 2).reshape(dseq * db, d)
    outs = {k: [] for k in ("conv_p", "gla_p", "ffn_p", "mk", "mv", "conv_s", "gla_s", "ffn_s")}
    nfinal = norm_final.reshape(1, d)
    yp = ys = None
    for l in range(depth):
        w = {
            "norm_mix": norm_mix[l].reshape(1, d),
            "w_in_t": w_in[l].T,
            "w_g1_t": jnp.pad(w_in[l].T[N_MAIN:], ((0, LANES - GLA_RANK), (0, 0))).astype(BF16),
            "w_g2": jnp.pad(w_gate2[l], ((0, LANES - GLA_RANK), (0, 0))).astype(BF16),
            "b_gate": b_gate[l].reshape(1, -1),
            "conv_w": conv_w[l],
            "gla_norm": gla_norm[l].reshape(1, -1),
            "w_out": w_out[l],
            "norm_x": norm_x[l].reshape(1, d),
            "w_xq": w_xq[l],
            "w_xo": w_xo[l],
            "w_fd": w_ffn_down[l],
            "norm_ffn": norm_ffn[l].reshape(1, d),
            "w_fg": w_ffn_gate[l],
            "w_fu": w_ffn_up[l],
            "ffn_conv_w": ffn_conv_w[l],
            "ffn_conv_b": ffn_conv_b[l].reshape(1, -1),
        }
        last = l == depth - 1
        gain_next = nfinal if last else None

        mem = mem_prompt.reshape(nb * MEM_LEN, d)
        nmem = norm_mem[l].reshape(1, d)
        mk, mk_cache = _mem_proj(mem, nmem, w_xk[l], 1024)
        mv, mv_cache = _mem_proj(mem, nmem, w_xv[l], 1024)
        act, h2, c1, s1, f1, wb = _layer(
            hp, w, None, nseq=nb, rows=seq, shift=1,
            prev_conv=jnp.zeros((nb, CONV_WIDTH - 1, D_CONV), F32),
            s0=jnp.zeros((nb, GLA_HEADS, GLA_DK, GLA_DV), F32),
            prev_ffn=jnp.zeros((nb, CONV_WIDTH - 1, D_FF), F32),
            mk=mk.reshape(nb, MEM_LEN, d), mv=mv.reshape(nb, MEM_LEN, d), time_major=False)
        assert last, "only the final layer's epilogue (final rmsnorm) is implemented"
        yp = _proj_res_norm([act], wb["w_fd"], h2, gain_next, tm=256, tk=D_FF, final=True)
        outs["conv_p"].append(c1)
        outs["gla_p"].append(s1)
        outs["ffn_p"].append(f1)
        outs["mk"].append(mk_cache)
        outs["mv"].append(mv_cache)

        def tmajor(c):
            return c.transpose(1, 0, 2).reshape(1, (CONV_WIDTH - 1) * db, c.shape[-1])

        act, h2, c2, s2, f2, _ = _layer(
            hs, w, wb, nseq=1, rows=dseq * db, shift=db,
            prev_conv=tmajor(cache_conv[l]), s0=state_gla[l], prev_ffn=tmajor(cache_ffn[l]),
            mk=cache_mem_k[l], mv=cache_mem_v[l], time_major=True)
        ys = _proj_res_norm([act], wb["w_fd"], h2, gain_next, tm=256, tk=D_FF, final=True)
        outs["conv_s"].append(c2.reshape(CONV_WIDTH - 1, db, D_CONV).transpose(1, 0, 2))
        outs["gla_s"].append(s2)
        outs["ffn_s"].append(f2.reshape(CONV_WIDTH - 1, db, D_FF).transpose(1, 0, 2))

    y_prompt = yp.reshape(nb, seq, d)
    y_sample = ys.reshape(dseq, db, d).transpose(1, 0, 2)
    st = lambda k: jnp.stack(outs[k])
    return (y_prompt, y_sample, st("conv_p"), st("gla_p"), st("ffn_p"), st("mk"), st("mv"),
            st("conv_s"), st("gla_s"), st("ffn_s"))
```

```python
import functools

import jax
import jax.numpy as jnp
from jax import lax
from jax.experimental import pallas as pl
from jax.experimental.pallas import tpu as pltpu

F32 = jnp.float32
BF16 = jnp.bfloat16

D_MODEL = 2048
EPS = 1e-6
CONV_WIDTH = 3
D_CONV = 1024
D_GLA = 1024
GLA_HEADS = 4
GLA_DV = 256
GLA_DK = 128
GLA_RANK = 16
GLA_TAU = 16.0
LOG2_E = 1.4426950408889634
GLA_CHUNK = 64
X_HEADS = 4
X_HD = 512
MEM_LEN = 256
D_FF = 5632
N_MAIN = 3 * D_CONV + 2 * GLA_HEADS * GLA_DK + 2 * GLA_HEADS * GLA_DV

MXU_COLS = 256
LANES = 128
SUBLANES = 8
VMEM_LIMIT_BYTES = 56 * 1024 * 1024


def _params(*sem):
    return pltpu.CompilerParams(dimension_semantics=sem, vmem_limit_bytes=VMEM_LIMIT_BYTES)


def _dot(a, b):
    return jnp.dot(a, b, preferred_element_type=F32)


def _dot_nt(a, b):
    return lax.dot_general(a, b, (((1,), (1,)), ((), ())), preferred_element_type=F32)


def _dot_tn(a, b):
    return lax.dot_general(a, b, (((0,), (0,)), ((), ())), preferred_element_type=F32)


def _rms_rows(x, g):
    ms = jnp.mean(x * x, axis=-1, keepdims=True)
    return (x * lax.rsqrt(ms + EPS)) * g


def _row_chunk(rows, limit=256):
    for c in (256, 128, 64, 32, 16, 8):
        if c <= limit and rows % c == 0:
            return c
    return rows


NORM_ROWS = 128


def _norm_into(x_ref, g_ref, xn_ref):
    rows = x_ref.shape[0]
    ch = _row_chunk(rows)
    g = g_ref[...]

    def body(c, carry):
        r = pl.ds(pl.multiple_of(c * ch, ch), ch)
        xn_ref[r, :] = _rms_rows(x_ref[r, :], g).astype(xn_ref.dtype)
        return carry

    lax.fori_loop(0, rows // ch, body, 0)


def _norm_matmul_kernel(x_ref, g_ref, w_ref, o_ref, oc_ref, xn_ref):
    j = pl.program_id(1)

    @pl.when(j == 0)
    def _():
        _norm_into(x_ref, g_ref, xn_ref)

    o_ref[...] = _dot(xn_ref[...], w_ref[...].astype(BF16)).astype(o_ref.dtype)
    nchunk = X_HD // LANES
    pitch = nchunk * X_HEADS
    heads_per_tile = o_ref.shape[1] // X_HD
    for b in range(oc_ref.shape[0]):
        for hh in range(heads_per_tile):
            for c in range(nchunk):
                col = hh * X_HD + c * LANES
                row = c * X_HEADS + j * heads_per_tile + hh
                oc_ref[b, pl.ds(row, MEM_LEN, stride=pitch), :] = (
                    o_ref[b * MEM_LEN:(b + 1) * MEM_LEN, col:col + LANES])


def _norm_gate_kernel(x_ref, g_ref, wg1_ref, wg2_ref, bg_ref, xn_ref, lf_ref):
    _norm_into(x_ref, g_ref, xn_ref)
    g1 = _dot_nt(xn_ref[...], wg1_ref[...])
    z = _dot(g1.astype(BF16), wg2_ref[...]) + bg_ref[...]
    lf_ref[...] = (jnp.minimum(z, 0.0) - jnp.log1p(jnp.exp(-jnp.abs(z)))) * (LOG2_E / GLA_TAU)


def _mem_proj(mem, gain, w, tn):
    m, d = mem.shape
    n = w.shape[1]
    nb = m // MEM_LEN
    nchunk = X_HD // LANES
    rows = MEM_LEN * nchunk * X_HEADS
    flat, stored = pl.pallas_call(
        _norm_matmul_kernel,
        grid=(1, n // tn),
        in_specs=[pl.BlockSpec((m, d), lambda i, j: (0, 0)),
                  pl.BlockSpec((1, d), lambda i, j: (0, 0)),
                  pl.BlockSpec((d, tn), lambda i, j: (0, j))],
        out_specs=[pl.BlockSpec((m, tn), lambda i, j: (0, j)),
                   pl.BlockSpec((nb, rows, LANES), lambda i, j: (0, 0, 0))],
        out_shape=[jax.ShapeDtypeStruct((m, n), F32),
                   jax.ShapeDtypeStruct((nb, rows, LANES), F32)],
        scratch_shapes=[pltpu.VMEM((m, d), BF16)],
        compiler_params=_params("arbitrary", "arbitrary"),
        name="norm_matmul",
    )(mem, gain, w)
    cache = stored.reshape(nb, MEM_LEN, nchunk, X_HEADS, LANES).transpose(0, 1, 3, 2, 4)
    return flat, cache.reshape(nb, MEM_LEN, X_HEADS, X_HD)


def _norm_gate(x, gain, wg1, wg2, b_gate, tm):
    m, d = x.shape
    ng = wg2.shape[1]
    return pl.pallas_call(
        _norm_gate_kernel,
        grid=(m // tm,),
        in_specs=[pl.BlockSpec((tm, d), lambda i: (i, 0)),
                  pl.BlockSpec((1, d), lambda i: (0, 0)),
                  pl.BlockSpec((LANES, d), lambda i: (0, 0)),
                  pl.BlockSpec((LANES, ng), lambda i: (0, 0)),
                  pl.BlockSpec((1, ng), lambda i: (0, 0))],
        out_specs=[pl.BlockSpec((tm, d), lambda i: (i, 0)),
                   pl.BlockSpec((tm, ng), lambda i: (i, 0))],
        out_shape=[jax.ShapeDtypeStruct((m, d), BF16),
                   jax.ShapeDtypeStruct((m, ng), F32)],
        compiler_params=_params("arbitrary"),
        name="norm_gate",
    )(x, gain, wg1, wg2, b_gate)


def _cast_into(src_ref, dst_ref):
    rows = src_ref.shape[0]
    ch = _row_chunk(rows)

    def body(c, carry):
        r = pl.ds(pl.multiple_of(c * ch, ch), ch)
        dst_ref[r, :] = src_ref[r, :].astype(dst_ref.dtype)
        return carry

    lax.fori_loop(0, rows // ch, body, 0)


def _matmul_kernel(a_ref, w_ref, o_ref, wb_ref):
    @pl.when(pl.program_id(0) == 0)
    def _():
        _cast_into(w_ref, wb_ref)

    o_ref[...] = _dot(a_ref[...], wb_ref[...]).astype(o_ref.dtype)


def _matmul(a, w, out_dtype, tm):
    m, k = a.shape
    n = w.shape[1]
    return pl.pallas_call(
        _matmul_kernel,
        grid=(m // tm,),
        in_specs=[pl.BlockSpec((tm, k), lambda i: (i, 0)),
                  pl.BlockSpec((k, n), lambda i: (0, 0), pipeline_mode=pl.Buffered(1))],
        out_specs=pl.BlockSpec((tm, n), lambda i: (i, 0)),
        out_shape=jax.ShapeDtypeStruct((m, n), out_dtype),
        scratch_shapes=[pltpu.VMEM((k, n), BF16)],
        compiler_params=_params("arbitrary"),
        name="matmul",
    )(a, w)


def _col_matmul_kernel(a_ref, wt_ref, o_ref, wb_ref):
    @pl.when(pl.program_id(1) == 0)
    def _():
        _cast_into(wt_ref, wb_ref)

    o_ref[...] = _dot_nt(a_ref[...], wb_ref[...]).astype(o_ref.dtype)


def _col_matmul(a, wt, col0, ncols, out_dtype, tm, tn):
    m, k = a.shape
    j0 = col0 // tn
    return pl.pallas_call(
        _col_matmul_kernel,
        grid=(ncols // tn, m // tm),
        in_specs=[pl.BlockSpec((tm, k), lambda j, i: (i, 0)),
                  pl.BlockSpec((tn, k), lambda j, i: (j0 + j, 0))],
        out_specs=pl.BlockSpec((tm, tn), lambda j, i: (i, j)),
        out_shape=jax.ShapeDtypeStruct((m, ncols), out_dtype),
        scratch_shapes=[pltpu.VMEM((tn, k), BF16)],
        compiler_params=_params("arbitrary", "arbitrary"),
        name="col_matmul",
    )(a, wt)


def _halo_rows(shift):
    return max(2 * shift, SUBLANES)


def _conv_buf(rows, tn, shift):
    return pltpu.VMEM((tn // LANES, _halo_rows(shift) + rows, LANES), F32)


def _slab(s):
    return slice(s * LANES, (s + 1) * LANES)


def _conv_stage(buf_ref, row0, x):
    for s in range(buf_ref.shape[0]):
        buf_ref[s, row0:row0 + x.shape[0], :] = x[:, _slab(s)]


def _conv_taps(buf_ref, s, r0, rows, shift):
    halo = _halo_rows(shift)

    def back(steps):
        start = r0 + (halo - steps * shift)
        if (steps * shift) % SUBLANES == 0:
            return pl.ds(start if isinstance(start, int) else pl.multiple_of(start, SUBLANES), rows)
        return pl.ds(start, rows, stride=1)

    return buf_ref[s, back(2), :], buf_ref[s, back(1), :], buf_ref[s, back(0), :]


def _conv_last(buf_ref, new_ref, rows, shift):
    halo = _halo_rows(shift)
    for s in range(buf_ref.shape[0]):
        new_ref[0, :, _slab(s)] = buf_ref[s, halo + rows - 2 * shift:halo + rows, :]


def _conv_proj_kernel(xn_ref, wbg_ref, wcg_ref, wvc_ref, prev_ref, cw_ref, o_ref, new_ref,
                      bgb_ref, cgb_ref, vcb_ref, buf_ref, *, shift, tiles_per_seq):
    rows = xn_ref.shape[0]
    halo = _halo_rows(shift)
    i = pl.program_id(1)

    @pl.when(i == 0)
    def _():
        _cast_into(wbg_ref, bgb_ref)
        _cast_into(wcg_ref, cgb_ref)
        _cast_into(wvc_ref, vcb_ref)

    def from_cache():
        _conv_stage(buf_ref, halo - 2 * shift, prev_ref[0])

    if tiles_per_seq == 1:
        from_cache()
    else:
        pl.when(i % tiles_per_seq == 0)(from_cache)

        @pl.when(i % tiles_per_seq != 0)
        def _():
            for s in range(buf_ref.shape[0]):
                buf_ref[s, halo - 2 * shift:halo, :] = buf_ref[s, halo + rows - 2 * shift:halo + rows, :]

    xn = xn_ref[...]
    _conv_stage(buf_ref, halo, _dot_nt(xn, cgb_ref[...]) * _dot_nt(xn, vcb_ref[...]))
    bg = _dot_nt(xn, bgb_ref[...])
    for s in range(buf_ref.shape[0]):
        u2, u1, u0 = _conv_taps(buf_ref, s, 0, rows, shift)
        y = cw_ref[0:1, _slab(s)] * u2 + cw_ref[1:2, _slab(s)] * u1 + cw_ref[2:3, _slab(s)] * u0
        o_ref[:, _slab(s)] = (bg[:, _slab(s)] * y).astype(o_ref.dtype)
    _conv_last(buf_ref, new_ref, rows, shift)


def _conv_proj(xn, w_in_t, prev, conv_w, *, nseq, shift, tm, tn):
    m, d = xn.shape
    nj = D_CONV // tn
    tiles_per_seq = m // (nseq * tm)
    kern = functools.partial(_conv_proj_kernel, shift=shift, tiles_per_seq=tiles_per_seq)
    state = pl.BlockSpec((1, 2 * shift, tn), lambda j, i: (i // tiles_per_seq, 0, j))
    wb = pltpu.VMEM((tn, d), BF16)
    return pl.pallas_call(
        kern,
        grid=(nj, m // tm),
        in_specs=[pl.BlockSpec((tm, d), lambda j, i: (i, 0)),
                  pl.BlockSpec((tn, d), lambda j, i: (j, 0)),
                  pl.BlockSpec((tn, d), lambda j, i: (j + nj, 0)),
                  pl.BlockSpec((tn, d), lambda j, i: (j + 2 * nj, 0)),
                  state,
                  pl.BlockSpec((CONV_WIDTH, tn), lambda j, i: (0, j))],
        out_specs=[pl.BlockSpec((tm, tn), lambda j, i: (i, j)),
                   state],
        out_shape=[jax.ShapeDtypeStruct((m, D_CONV), BF16),
                   jax.ShapeDtypeStruct((nseq, 2 * shift, D_CONV), F32)],
        scratch_shapes=[wb, wb, wb, _conv_buf(tm, tn, shift)],
        compiler_params=_params("arbitrary", "arbitrary"),
        name="conv_proj",
    )(xn, w_in_t, w_in_t, w_in_t, prev, conv_w)


def _cumsum_rows(g):
    c = g.shape[0]
    row = lax.broadcasted_iota(jnp.int32, g.shape, 0)
    x = g
    s = 1
    while s < c:
        x = x + jnp.where(row >= s, pltpu.roll(x, s, 0), 0.0)
        s *= 2
    return x


def _bcast_block_row(x, s, k):
    c, lanes = x.shape
    if s == c:
        return jnp.broadcast_to(x[k:k + 1, :], x.shape)
    if s >= SUBLANES:
        y = x.reshape(c // s, s, lanes)
        return jnp.broadcast_to(y[:, k:k + 1, :], y.shape).reshape(c, lanes)
    y = x.reshape(c // SUBLANES, SUBLANES, lanes)
    sub = lax.broadcasted_iota(jnp.int32, y.shape, 1)
    out = None
    for blk in range(SUBLANES // s):
        src = jnp.broadcast_to(y[:, blk * s + k:blk * s + k + 1, :], y.shape)
        out = src if out is None else jnp.where(sub >= blk * s, src, out)
    return out.reshape(c, lanes)


def _gla_pair_masks(c):
    ri = lax.broadcasted_iota(jnp.int32, (c, c), 0)
    ci = lax.broadcasted_iota(jnp.int32, (c, c), 1)
    diff_bits = ri ^ ci
    masks = [diff_bits == 0]
    level = 0
    while (1 << level) < c:
        masks.append(((diff_bits >> level) == 1) & (((ri >> level) & 1) == 1))
        level += 1
    return masks


def _gla_chunk(q, k, v, g, s_prev, masks):
    c = q.shape[0]
    cum = _cumsum_rows(g)
    a = jnp.where(masks[0], _dot_nt(q.astype(BF16), k.astype(BF16)), 0.0)
    for level in range(len(masks) - 1):
        half = 1 << level
        ref = _bcast_block_row(cum, 2 * half, half - 1)
        d = cum - ref
        up = jnp.minimum(d, 0.0)
        qe = q * jnp.exp2(up)
        ke = k * jnp.exp2(up - d)
        a = a + jnp.where(masks[1 + level], _dot_nt(qe.astype(BF16), ke.astype(BF16)), 0.0)
    o = _dot(a.astype(BF16), v.astype(BF16)) + _dot((q * jnp.exp2(cum)).astype(BF16), s_prev.astype(BF16))
    last = cum[c - 1:c, :]
    kd = k * jnp.exp2(last - cum)
    dk = last.shape[1]
    decay_t = jnp.transpose(jnp.broadcast_to(jnp.exp2(last), (dk, dk)))
    decayed = jnp.concatenate([decay_t * s_prev[:, i:i + dk] for i in range(0, s_prev.shape[1], dk)], axis=1)
    s_new = decayed + _dot_tn(kd.astype(BF16), v.astype(BF16))
    return o, s_new


def _gla_kernel(q_ref, k_ref, v_ref, r_ref, g_ref, s0_ref, gn_ref, *refs, chunk, single_chunk, group, n_cast):
    cast_src, (o_ref, sn_ref), cast_dst = refs[:n_cast], refs[n_cast:n_cast + 2], refs[n_cast + 2:]
    bb, rows = q_ref.shape[0], q_ref.shape[1]
    nchunk = rows // chunk
    state_in = s0_ref if single_chunk else sn_ref

    if not single_chunk:
        @pl.when(pl.program_id(1) == 0)
        def _():
            sn_ref[...] = s0_ref[...]

    masks = _gla_pair_masks(chunk)

    def one(b, r):
        for h in range(GLA_HEADS):
            kc = slice(h * GLA_DK, (h + 1) * GLA_DK)
            vc = slice(h * GLA_DV, (h + 1) * GLA_DV)
            q = q_ref[b, r, kc] * (GLA_DK ** -0.5)
            o, s_new = _gla_chunk(q, k_ref[b, r, kc], v_ref[b, r, vc], g_ref[b, r, kc], state_in[b, h], masks)
            sn_ref[b, h] = s_new
            rr = r_ref[b, r, vc]
            o_ref[b, r, vc] = (_rms_rows(o, gn_ref[:, vc]) * (rr * jax.nn.sigmoid(rr))).astype(o_ref.dtype)

    def body(n, carry):
        r = pl.ds(pl.multiple_of((n % nchunk) * chunk, chunk), chunk)
        for u in range(group):
            one((n // nchunk) * group + u, r)
        return carry

    lax.fori_loop(0, (bb // group) * nchunk, body, 0)
    for src, dst in zip(cast_src, cast_dst):
        _cast_into(src, dst)


def _gla(p, logf, s0, gla_norm, *, q_blk, k_blk, v_blk, r_blk, bb, rows, chunk, group, cast=()):
    nb, t, _ = p.shape
    nk, nv = GLA_HEADS * GLA_DK, GLA_HEADS * GLA_DV
    nt = t // rows
    nsteps = (nb // bb) * nt
    kern = functools.partial(_gla_kernel, chunk=chunk, single_chunk=(t == chunk), group=group, n_cast=len(cast))
    state_spec = pl.BlockSpec((bb, GLA_HEADS, GLA_DK, GLA_DV), lambda b, c: (b, 0, 0, 0))
    cast_specs = [pl.BlockSpec((w.shape[0] // nsteps, w.shape[1]), lambda b, c: (b * nt + c, 0)) for w in cast]
    return pl.pallas_call(
        kern,
        grid=(nb // bb, nt),
        in_specs=[pl.BlockSpec((bb, rows, nk), lambda b, c: (b, c, q_blk)),
                  pl.BlockSpec((bb, rows, nk), lambda b, c: (b, c, k_blk)),
                  pl.BlockSpec((bb, rows, nv), lambda b, c: (b, c, v_blk)),
                  pl.BlockSpec((bb, rows, nv), lambda b, c: (b, c, r_blk)),
                  pl.BlockSpec((bb, rows, nk), lambda b, c: (b, c, 0)),
                  state_spec,
                  pl.BlockSpec((1, nv), lambda b, c: (0, 0))] + cast_specs,
        out_specs=[pl.BlockSpec((bb, rows, nv), lambda b, c: (b, c, 0)),
                   state_spec] + cast_specs,
        out_shape=[jax.ShapeDtypeStruct((nb, t, D_GLA), BF16),
                   jax.ShapeDtypeStruct((nb, GLA_HEADS, GLA_DK, GLA_DV), F32)]
                  + [jax.ShapeDtypeStruct(w.shape, BF16) for w in cast],
        compiler_params=_params("arbitrary", "arbitrary"),
        name="gla",
    )(p, p, p, p, logf, s0, gla_norm, *cast)


def _proj_res_norm_kernel(*refs, n_a, nk, final):
    a_refs, (w_ref, res_ref, g_ref) = refs[:n_a], refs[n_a:n_a + 3]
    out_refs, acc_ref = refs[n_a + 3:-1], refs[-1]
    k = pl.program_id(1)
    part, r0 = None, 0
    for a_ref in a_refs:
        kw = a_ref.shape[1]
        term = _dot(a_ref[...], w_ref[r0:r0 + kw, :])
        part = term if part is None else part + term
        r0 += kw

    @pl.when(k == 0)
    def _():
        acc_ref[...] = part

    if nk > 1:
        @pl.when(k > 0)
        def _():
            acc_ref[...] += part

    @pl.when(k == nk - 1)
    def _():
        rows = acc_ref.shape[0]
        ch = _row_chunk(rows, NORM_ROWS)
        g = g_ref[...]

        def body(c, carry):
            r = pl.ds(pl.multiple_of(c * ch, ch), ch)
            h = res_ref[r, :] + acc_ref[r, :]
            hn = _rms_rows(h, g)
            if final:
                out_refs[0][r, :] = hn
            else:
                out_refs[0][r, :] = h
                out_refs[1][r, :] = hn.astype(out_refs[1].dtype)
            return carry

        lax.fori_loop(0, rows // ch, body, 0)


def _proj_res_norm(a_list, w, res, gain, *, tm, tk, final):
    m = a_list[0].shape[0]
    kdim, d = w.shape
    nk = kdim // tk
    assert len(a_list) == 1 or nk == 1
    kern = functools.partial(_proj_res_norm_kernel, n_a=len(a_list), nk=nk, final=final)
    a_specs = ([pl.BlockSpec((tm, tk), lambda i, k: (i, k))] if len(a_list) == 1 else
               [pl.BlockSpec((tm, a.shape[1]), lambda i, k: (i, 0)) for a in a_list])
    row_spec = pl.BlockSpec((tm, d), lambda i, k: (i, 0))
    if final:
        out_specs = row_spec
        out_shape = jax.ShapeDtypeStruct((m, d), F32)
    else:
        out_specs = [row_spec, row_spec]
        out_shape = [jax.ShapeDtypeStruct((m, d), F32), jax.ShapeDtypeStruct((m, d), BF16)]
    return pl.pallas_call(
        kern,
        grid=(m // tm, nk),
        in_specs=a_specs + [pl.BlockSpec((tk, d), lambda i, k: (k, 0),
                                         pipeline_mode=pl.Buffered(1 if nk == 1 else 2)),
                            row_spec,
                            pl.BlockSpec((1, d), lambda i, k: (0, 0))],
        out_specs=out_specs,
        out_shape=out_shape,
        scratch_shapes=[pltpu.VMEM((tm, d), F32)],
        compiler_params=_params("arbitrary", "arbitrary"),
        name="proj_res_norm",
    )(*a_list, w, res, gain)


def _xattn_kernel(q_ref, k_ref, v_ref, o_ref):
    bb = q_ref.shape[0]

    def body(b, carry):
        q = q_ref[b].astype(BF16)
        p = _softmax_rows(_dot_nt(q, k_ref[b].astype(BF16)) * (X_HD ** -0.5))
        o_ref[b] = _dot(p.astype(BF16), v_ref[b].astype(BF16)).astype(o_ref.dtype)
        return carry

    lax.fori_loop(0, bb, body, 0)


def _xattn(q, mk, mv, *, bb, tq):
    nb, t, d = q.shape
    return pl.pallas_call(
        _xattn_kernel,
        grid=(nb // bb, X_HEADS, t // tq),
        in_specs=[pl.BlockSpec((bb, tq, X_HD), lambda b, h, i: (b, i, h)),
                  pl.BlockSpec((bb, MEM_LEN, X_HD), lambda b, h, i: (b, 0, h)),
                  pl.BlockSpec((bb, MEM_LEN, X_HD), lambda b, h, i: (b, 0, h))],
        out_specs=pl.BlockSpec((bb, tq, X_HD), lambda b, h, i: (b, i, h)),
        out_shape=jax.ShapeDtypeStruct((nb, t, d), BF16),
        compiler_params=_params("arbitrary", "arbitrary", "arbitrary"),
        name="xattn",
    )(q, mk, mv)


def _softmax_rows(s):
    s = s - jnp.max(s, axis=-1, keepdims=True)
    e = jnp.exp(s)
    return e / jnp.sum(e, axis=-1, keepdims=True)


def _xattn_cache_kernel(q_ref, k_ref, v_ref, o_ref):
    bb, tq = q_ref.shape[0], q_ref.shape[1]
    nchunk = X_HD // LANES
    pitch = nchunk * X_HEADS

    def gather(ref, b, h):
        parts = [ref[b, pl.ds(c * X_HEADS + h, MEM_LEN, stride=pitch), :] for c in range(nchunk)]
        return jnp.concatenate(parts, axis=1).astype(BF16)

    pairs = [(b, h) for b in range(bb) for h in range(X_HEADS)]
    scores = [_dot_nt(q_ref[b, :, h * X_HD:(h + 1) * X_HD].astype(BF16), gather(k_ref, b, h)) for b, h in pairs]
    p = _softmax_rows(jnp.concatenate(scores, axis=0) * (X_HD ** -0.5)).astype(BF16)
    for n, (b, h) in enumerate(pairs):
        o_ref[b, :, h * X_HD:(h + 1) * X_HD] = _dot(p[n * tq:(n + 1) * tq], gather(v_ref, b, h)).astype(o_ref.dtype)


def _xattn_cache(q, ck, cv, *, bb):
    nb, tq, d = q.shape
    nchunk = X_HD // LANES

    def stored_order(c):
        c = c.reshape(nb, MEM_LEN, X_HEADS, nchunk, LANES).transpose(0, 1, 3, 2, 4)
        return c.reshape(nb, MEM_LEN * nchunk * X_HEADS, LANES)

    rows = MEM_LEN * nchunk * X_HEADS
    return pl.pallas_call(
        _xattn_cache_kernel,
        grid=(nb // bb,),
        in_specs=[pl.BlockSpec((bb, tq, d), lambda b: (b, 0, 0)),
                  pl.BlockSpec((bb, rows, LANES), lambda b: (b, 0, 0)),
                  pl.BlockSpec((bb, rows, LANES), lambda b: (b, 0, 0))],
        out_specs=pl.BlockSpec((bb, tq, d), lambda b: (b, 0, 0)),
        out_shape=jax.ShapeDtypeStruct((nb, tq, d), BF16),
        compiler_params=_params("arbitrary"),
        name="xattn_cache",
    )(q, stored_order(ck), stored_order(cv))


def _ffn_up_kernel(hn_ref, wg_ref, wu_ref, cw_ref, cb_ref, prev_ref, o_ref, new_ref, buf_ref,
                   wgb_ref, wub_ref, *, shift):
    rows = hn_ref.shape[0]
    halo = _halo_rows(shift)

    @pl.when(pl.program_id(1) == 0)
    def _():
        _cast_into(wg_ref, wgb_ref)
        _cast_into(wu_ref, wub_ref)

    _conv_stage(buf_ref, halo - 2 * shift, prev_ref[0])
    slabs_per_pass = MXU_COLS // LANES
    for p0 in range(0, o_ref.shape[1], MXU_COLS):
        cols = slice(p0, p0 + MXU_COLS)
        gate = _dot(hn_ref[...], wgb_ref[:, cols])
        s0 = p0 // LANES
        for s in range(slabs_per_pass):
            buf_ref[s0 + s, halo:halo + rows, :] = gate[:, _slab(s)]
        up = _dot(hn_ref[...], wub_ref[:, cols])
        for s in range(slabs_per_pass):
            g2, g1, g0 = _conv_taps(buf_ref, s0 + s, 0, rows, shift)
            lanes = _slab(s0 + s)
            gc = cw_ref[0:1, lanes] * g2 + cw_ref[1:2, lanes] * g1 + cw_ref[2:3, lanes] * g0 + cb_ref[:, lanes]
            o_ref[:, lanes] = ((gc * jax.nn.sigmoid(gc)) * up[:, _slab(s)]).astype(o_ref.dtype)
    _conv_last(buf_ref, new_ref, rows, shift)


def _ffn_up(hn, wg, wu, cw, cb, prev, nseq, rows, shift, tn):
    d = hn.shape[1]
    kern = functools.partial(_ffn_up_kernel, shift=shift)
    return pl.pallas_call(
        kern,
        grid=(D_FF // tn, nseq),
        in_specs=[pl.BlockSpec((rows, d), lambda j, b: (b, 0)),
                  pl.BlockSpec((d, tn), lambda j, b: (0, j)),
                  pl.BlockSpec((d, tn), lambda j, b: (0, j)),
                  pl.BlockSpec((CONV_WIDTH, tn), lambda j, b: (0, j)),
                  pl.BlockSpec((1, tn), lambda j, b: (0, j)),
                  pl.BlockSpec((1, 2 * shift, tn), lambda j, b: (b, 0, j))],
        out_specs=[pl.BlockSpec((rows, tn), lambda j, b: (b, j)),
                   pl.BlockSpec((1, 2 * shift, tn), lambda j, b: (b, 0, j))],
        out_shape=[jax.ShapeDtypeStruct((nseq * rows, D_FF), BF16),
                   jax.ShapeDtypeStruct((nseq, 2 * shift, D_FF), F32)],
        scratch_shapes=[_conv_buf(rows, tn, shift), pltpu.VMEM((d, tn), BF16), pltpu.VMEM((d, tn), BF16)],
        compiler_params=_params("arbitrary", "arbitrary"),
        name="ffn_up",
    )(hn, wg, wu, cw, cb, prev)


def _layer(x, w, wb, *, nseq, rows, shift, prev_conv, s0, prev_ffn, mk, mv, time_major):
    m = x.shape[0]
    tm = min(m, 1024)
    xn, logf = _norm_gate(x, w["norm_mix"], w["w_g1_t"], w["w_g2"], w["b_gate"], tm)
    conv_out, conv_new = _conv_proj(xn, w["w_in_t"], prev_conv, w["conv_w"], nseq=nseq, shift=shift, tm=tm, tn=512)
    n_gla = N_MAIN - 3 * D_CONV
    p = _col_matmul(xn, w["w_in_t"], 3 * D_CONV, n_gla, F32, tm, 1024)

    if time_major:
        nt = m // shift
        pad = ((0, 0), (0, SUBLANES - nt), (0, 0))
        pg = jnp.pad(p.reshape(nt, shift, n_gla).transpose(1, 0, 2), pad)
        lg = jnp.pad(logf.reshape(nt, shift, -1).transpose(1, 0, 2), pad)
        gla_args = dict(bb=8, rows=SUBLANES, chunk=SUBLANES, group=4)
    else:
        pg, lg = p.reshape(nseq, rows, n_gla), logf.reshape(nseq, rows, -1)
        gla_args = dict(bb=2, rows=256, chunk=GLA_CHUNK, group=2)
    cast = () if wb is not None else (w["w_out"], w["w_xo"], w["w_fd"])
    o, s_new, *made = _gla(pg, lg, s0, w["gla_norm"], q_blk=0, k_blk=1, v_blk=1, r_blk=2, cast=cast, **gla_args)
    if wb is None:
        wb = dict(zip(("w_out", "w_xo", "w_fd"), made))
    if time_major:
        gla_out = o[:, :nt].transpose(1, 0, 2).reshape(m, D_GLA)
    else:
        gla_out = o.reshape(m, D_GLA)

    tm2 = min(m, 512)
    h, hn = _proj_res_norm([conv_out, gla_out], wb["w_out"], x, w["norm_x"], tm=tm2, tk=D_MODEL, final=False)
    qx = _matmul(hn, w["w_xq"], BF16, tm)

    if time_major:
        nt = m // shift
        qb = jnp.pad(qx.astype(F32).reshape(nt, shift, D_MODEL).transpose(1, 0, 2),
                     ((0, 0), (0, SUBLANES - nt), (0, 0)))
        ob = _xattn_cache(qb, mk, mv, bb=4)
        attn = ob[:, :nt].transpose(1, 0, 2).reshape(m, D_MODEL)
    else:
        attn = _xattn(qx.reshape(nseq, rows, D_MODEL), mk, mv, bb=1, tq=rows).reshape(m, D_MODEL)

    h2, hn2 = _proj_res_norm([attn], wb["w_xo"], h, w["norm_ffn"], tm=tm2, tk=D_MODEL, final=False)
    act, ffn_new = _ffn_up(hn2, w["w_fg"], w["w_fu"], w["ffn_conv_w"], w["ffn_conv_b"], prev_ffn,
                           nseq, rows, shift, 512)
    return act, h2, conv_new, s_new, ffn_new, wb


def kernel(x_prompt, x_sample, mem_prompt, cache_conv, state_gla, cache_ffn, cache_mem_k, cache_mem_v,
           norm_mix, w_in, conv_w, w_gate2, b_gate, gla_norm, w_out, norm_x, norm_mem, w_xq, w_xk, w_xv,
           w_xo, norm_ffn, w_ffn_gate, w_ffn_up, ffn_conv_w, ffn_conv_b, w_ffn_down, norm_final):
    depth = w_in.shape[0]
    nb, seq, d = x_prompt.shape
    db, dseq, _ = x_sample.shape
    hp = x_prompt.reshape(nb * seq, d)
    hs = x_sample.transpose(1, 0, 2).reshape(dseq * db, d)
    outs = {k: [] for k in ("conv_p", "gla_p", "ffn_p", "mk", "mv", "conv_s", "gla_s", "ffn_s")}
    nfinal = norm_final.reshape(1, d)
    yp = ys = None
    for l in range(depth):
        w = {
            "norm_mix": norm_mix[l].reshape(1, d),
            "w_in_t": w_in[l].T,
            "w_g1_t": jnp.pad(w_in[l].T[N_MAIN:], ((0, LANES - GLA_RANK), (0, 0))).astype(BF16),
            "w_g2": jnp.pad(w_gate2[l], ((0, LANES - GLA_RANK), (0, 0))).astype(BF16),
            "b_gate": b_gate[l].reshape(1, -1),
            "conv_w": conv_w[l],
            "gla_norm": gla_norm[l].reshape(1, -1),
            "w_out": w_out[l],
            "norm_x": norm_x[l].reshape(1, d),
            "w_xq": w_xq[l],
            "w_xo": w_xo[l],
            "w_fd": w_ffn_down[l],
            "norm_ffn": norm_ffn[l].reshape(1, d),
            "w_fg": w_ffn_gate[l],
            "w_fu": w_ffn_up[l],
            "ffn_conv_w": ffn_conv_w[l],
            "ffn_conv_b": ffn_conv_b[l].reshape(1, -1),
        }
        last = l == depth - 1
        gain_next = nfinal if last else None

        mem = mem_prompt.reshape(nb * MEM_LEN, d)
        nmem = norm_mem[l].reshape(1, d)
        mk, mk_cache = _mem_proj(mem, nmem, w_xk[l], 1024)
        mv, mv_cache = _mem_proj(mem, nmem, w_xv[l], 1024)
        act, h2, c1, s1, f1, wb = _layer(
            hp, w, None, nseq=nb, rows=seq, shift=1,
            prev_conv=jnp.zeros((nb, CONV_WIDTH - 1, D_CONV), F32),
            s0=jnp.zeros((nb, GLA_HEADS, GLA_DK, GLA_DV), F32),
            prev_ffn=jnp.zeros((nb, CONV_WIDTH - 1, D_FF), F32),
            mk=mk.reshape(nb, MEM_LEN, d), mv=mv.reshape(nb, MEM_LEN, d), time_major=False)
        assert last, "only the final layer's epilogue (final rmsnorm) is implemented"
        yp = _proj_res_norm([act], wb["w_fd"], h2, gain_next, tm=256, tk=D_FF, final=True)
        outs["conv_p"].append(c1)
        outs["gla_p"].append(s1)
        outs["ffn_p"].append(f1)
        outs["mk"].append(mk_cache)
        outs["mv"].append(mv_cache)

        def tmajor(c):
            return c.transpose(1, 0, 2).reshape(1, (CONV_WIDTH - 1) * db, c.shape[-1])

        act, h2, c2, s2, f2, _ = _layer(
            hs, w, wb, nseq=1, rows=dseq * db, shift=db,
            prev_conv=tmajor(cache_conv[l]), s0=state_gla[l], prev_ffn=tmajor(cache_ffn[l]),
            mk=cache_mem_k[l], mv=cache_mem_v[l], time_major=True)
        ys = _proj_res_norm([act], wb["w_fd"], h2, gain_next, tm=256, tk=D_FF, final=True)
        outs["conv_s"].append(c2.reshape(CONV_WIDTH - 1, db, D_CONV).transpose(1, 0, 2))
        outs["gla_s"].append(s2)
        outs["ffn_s"].append(f2.reshape(CONV_WIDTH - 1, db, D_FF).transpose(1, 0, 2))

    y_prompt = yp.reshape(nb, seq, d)
    y_sample = ys.reshape(dseq, db, d).transpose(1, 0, 2)
    st = lambda k: jnp.stack(outs[k])
    return (y_prompt, y_sample, st("conv_p"), st("gla_p"), st("ffn_p"), st("mk"), st("mv"),
            st("conv_s"), st("gla_s"), st("ffn_s"))
```

```python
import functools

import jax
import jax.numpy as jnp
from jax import lax
from jax.experimental import pallas as pl
from jax.experimental.pallas import tpu as pltpu

F32 = jnp.float32
BF16 = jnp.bfloat16

D_MODEL = 2048
EPS = 1e-6
CONV_WIDTH = 3
D_CONV = 1024
D_GLA = 1024
GLA_HEADS = 4
GLA_DV = 256
GLA_DK = 128
GLA_RANK = 16
GLA_TAU = 16.0
LOG2_E = 1.4426950408889634
GLA_CHUNK = 64
X_HEADS = 4
X_HD = 512
MEM_LEN = 256
D_FF = 5632
N_MAIN = 3 * D_CONV + 2 * GLA_HEADS * GLA_DK + 2 * GLA_HEADS * GLA_DV

LANES = 128
SUBLANES = 8
VMEM_LIMIT_BYTES = 56 * 1024 * 1024


def _params(*sem):
    return pltpu.CompilerParams(dimension_semantics=sem, vmem_limit_bytes=VMEM_LIMIT_BYTES)


def _dot(a, b):
    return jnp.dot(a, b, preferred_element_type=F32)


def _dot_nt(a, b):
    return lax.dot_general(a, b, (((1,), (1,)), ((), ())), preferred_element_type=F32)


def _dot_tn(a, b):
    return lax.dot_general(a, b, (((0,), (0,)), ((), ())), preferred_element_type=F32)


def _rms_rows(x, g):
    ms = jnp.mean(x * x, axis=-1, keepdims=True)
    return (x * lax.rsqrt(ms + EPS)) * g


def _row_chunk(rows, limit=256):
    for c in (256, 128, 64, 32, 16, 8):
        if c <= limit and rows % c == 0:
            return c
    return rows


NORM_ROWS = 128


def _norm_into(x_ref, g_ref, xn_ref):
    rows = x_ref.shape[0]
    ch = _row_chunk(rows)
    g = g_ref[...]

    def body(c, carry):
        r = pl.ds(pl.multiple_of(c * ch, ch), ch)
        xn_ref[r, :] = _rms_rows(x_ref[r, :], g).astype(xn_ref.dtype)
        return carry

    lax.fori_loop(0, rows // ch, body, 0)


def _norm_matmul_kernel(x_ref, g_ref, w_ref, o_ref, oc_ref, xn_ref):
    j = pl.program_id(1)

    @pl.when(j == 0)
    def _():
        _norm_into(x_ref, g_ref, xn_ref)

    o_ref[...] = _dot(xn_ref[...], w_ref[...].astype(BF16)).astype(o_ref.dtype)
    nchunk = X_HD // LANES
    pitch = nchunk * X_HEADS
    heads_per_tile = o_ref.shape[1] // X_HD
    for b in range(oc_ref.shape[0]):
        for hh in range(heads_per_tile):
            for c in range(nchunk):
                col = hh * X_HD + c * LANES
                row = c * X_HEADS + j * heads_per_tile + hh
                oc_ref[b, pl.ds(row, MEM_LEN, stride=pitch), :] = (
                    o_ref[b * MEM_LEN:(b + 1) * MEM_LEN, col:col + LANES])


def _norm_gate_kernel(x_ref, g_ref, wg1_ref, wg2_ref, bg_ref, xn_ref, lf_ref):
    _norm_into(x_ref, g_ref, xn_ref)
    g1 = _dot_nt(xn_ref[...], wg1_ref[...])
    z = _dot(g1.astype(BF16), wg2_ref[...]) + bg_ref[...]
    lf_ref[...] = (jnp.minimum(z, 0.0) - jnp.log1p(jnp.exp(-jnp.abs(z)))) * (LOG2_E / GLA_TAU)


def _mem_proj(mem, gain, w, tn):
    m, d = mem.shape
    n = w.shape[1]
    nb = m // MEM_LEN
    nchunk = X_HD // LANES
    rows = MEM_LEN * nchunk * X_HEADS
    flat, stored = pl.pallas_call(
        _norm_matmul_kernel,
        grid=(1, n // tn),
        in_specs=[pl.BlockSpec((m, d), lambda i, j: (0, 0)),
                  pl.BlockSpec((1, d), lambda i, j: (0, 0)),
                  pl.BlockSpec((d, tn), lambda i, j: (0, j))],
        out_specs=[pl.BlockSpec((m, tn), lambda i, j: (0, j)),
                   pl.BlockSpec((nb, rows, LANES), lambda i, j: (0, 0, 0))],
        out_shape=[jax.ShapeDtypeStruct((m, n), F32),
                   jax.ShapeDtypeStruct((nb, rows, LANES), F32)],
        scratch_shapes=[pltpu.VMEM((m, d), BF16)],
        compiler_params=_params("arbitrary", "arbitrary"),
        name="norm_matmul",
    )(mem, gain, w)
    cache = stored.reshape(nb, MEM_LEN, nchunk, X_HEADS, LANES).transpose(0, 1, 3, 2, 4)
    return flat, cache.reshape(nb, MEM_LEN, X_HEADS, X_HD)


def _norm_gate(x, gain, wg1, wg2, b_gate, tm):
    m, d = x.shape
    ng = wg2.shape[1]
    return pl.pallas_call(
        _norm_gate_kernel,
        grid=(m // tm,),
        in_specs=[pl.BlockSpec((tm, d), lambda i: (i, 0)),
                  pl.BlockSpec((1, d), lambda i: (0, 0)),
                  pl.BlockSpec((LANES, d), lambda i: (0, 0)),
                  pl.BlockSpec((LANES, ng), lambda i: (0, 0)),
                  pl.BlockSpec((1, ng), lambda i: (0, 0))],
        out_specs=[pl.BlockSpec((tm, d), lambda i: (i, 0)),
                   pl.BlockSpec((tm, ng), lambda i: (i, 0))],
        out_shape=[jax.ShapeDtypeStruct((m, d), BF16),
                   jax.ShapeDtypeStruct((m, ng), F32)],
        compiler_params=_params("arbitrary"),
        name="norm_gate",
    )(x, gain, wg1, wg2, b_gate)


def _cast_into(src_ref, dst_ref):
    rows = src_ref.shape[0]
    ch = _row_chunk(rows)

    def body(c, carry):
        r = pl.ds(pl.multiple_of(c * ch, ch), ch)
        dst_ref[r, :] = src_ref[r, :].astype(dst_ref.dtype)
        return carry

    lax.fori_loop(0, rows // ch, body, 0)


def _matmul_kernel(a_ref, w_ref, o_ref, wb_ref):
    @pl.when(pl.program_id(0) == 0)
    def _():
        _cast_into(w_ref, wb_ref)

    o_ref[...] = _dot(a_ref[...], wb_ref[...]).astype(o_ref.dtype)


def _matmul(a, w, out_dtype, tm):
    m, k = a.shape
    n = w.shape[1]
    return pl.pallas_call(
        _matmul_kernel,
        grid=(m // tm,),
        in_specs=[pl.BlockSpec((tm, k), lambda i: (i, 0)),
                  pl.BlockSpec((k, n), lambda i: (0, 0), pipeline_mode=pl.Buffered(1))],
        out_specs=pl.BlockSpec((tm, n), lambda i: (i, 0)),
        out_shape=jax.ShapeDtypeStruct((m, n), out_dtype),
        scratch_shapes=[pltpu.VMEM((k, n), BF16)],
        compiler_params=_params("arbitrary"),
        name="matmul",
    )(a, w)


def _col_matmul_kernel(a_ref, wt_ref, o_ref, wb_ref):
    @pl.when(pl.program_id(1) == 0)
    def _():
        _cast_into(wt_ref, wb_ref)

    o_ref[...] = _dot_nt(a_ref[...], wb_ref[...]).astype(o_ref.dtype)


def _col_matmul(a, wt, col0, ncols, out_dtype, tm, tn):
    m, k = a.shape
    j0 = col0 // tn
    return pl.pallas_call(
        _col_matmul_kernel,
        grid=(ncols // tn, m // tm),
        in_specs=[pl.BlockSpec((tm, k), lambda j, i: (i, 0)),
                  pl.BlockSpec((tn, k), lambda j, i: (j0 + j, 0))],
        out_specs=pl.BlockSpec((tm, tn), lambda j, i: (i, j)),
        out_shape=jax.ShapeDtypeStruct((m, ncols), out_dtype),
        scratch_shapes=[pltpu.VMEM((tn, k), BF16)],
        compiler_params=_params("arbitrary", "arbitrary"),
        name="col_matmul",
    )(a, wt)


def _halo_rows(shift):
    return max(2 * shift, SUBLANES)


def _conv_buf(rows, tn, shift):
    return pltpu.VMEM((tn // LANES, _halo_rows(shift) + rows, LANES), F32)


def _slab(s):
    return slice(s * LANES, (s + 1) * LANES)


def _conv_stage(buf_ref, row0, x):
    for s in range(buf_ref.shape[0]):
        buf_ref[s, row0:row0 + x.shape[0], :] = x[:, _slab(s)]


def _conv_taps(buf_ref, s, r0, rows, shift):
    halo = _halo_rows(shift)

    def back(steps):
        start = r0 + (halo - steps * shift)
        if (steps * shift) % SUBLANES == 0:
            return pl.ds(start if isinstance(start, int) else pl.multiple_of(start, SUBLANES), rows)
        return pl.ds(start, rows, stride=1)

    return buf_ref[s, back(2), :], buf_ref[s, back(1), :], buf_ref[s, back(0), :]


def _conv_last(buf_ref, new_ref, rows, shift):
    halo = _halo_rows(shift)
    for s in range(buf_ref.shape[0]):
        new_ref[0, :, _slab(s)] = buf_ref[s, halo + rows - 2 * shift:halo + rows, :]


def _conv_proj_kernel(xn_ref, wbg_ref, wcg_ref, wvc_ref, prev_ref, cw_ref, o_ref, new_ref,
                      bgb_ref, cgb_ref, vcb_ref, buf_ref, *, shift, tiles_per_seq):
    rows = xn_ref.shape[0]
    halo = _halo_rows(shift)
    i = pl.program_id(1)

    @pl.when(i == 0)
    def _():
        _cast_into(wbg_ref, bgb_ref)
        _cast_into(wcg_ref, cgb_ref)
        _cast_into(wvc_ref, vcb_ref)

    def from_cache():
        _conv_stage(buf_ref, halo - 2 * shift, prev_ref[0])

    if tiles_per_seq == 1:
        from_cache()
    else:
        pl.when(i % tiles_per_seq == 0)(from_cache)

        @pl.when(i % tiles_per_seq != 0)
        def _():
            for s in range(buf_ref.shape[0]):
                buf_ref[s, halo - 2 * shift:halo, :] = buf_ref[s, halo + rows - 2 * shift:halo + rows, :]

    xn = xn_ref[...]
    _conv_stage(buf_ref, halo, _dot_nt(xn, cgb_ref[...]) * _dot_nt(xn, vcb_ref[...]))
    bg = _dot_nt(xn, bgb_ref[...])
    for s in range(buf_ref.shape[0]):
        u2, u1, u0 = _conv_taps(buf_ref, s, 0, rows, shift)
        y = cw_ref[0:1, _slab(s)] * u2 + cw_ref[1:2, _slab(s)] * u1 + cw_ref[2:3, _slab(s)] * u0
        o_ref[:, _slab(s)] = (bg[:, _slab(s)] * y).astype(o_ref.dtype)
    _conv_last(buf_ref, new_ref, rows, shift)


def _conv_proj(xn, w_in_t, prev, conv_w, *, nseq, shift, tm, tn):
    m, d = xn.shape
    nj = D_CONV // tn
    tiles_per_seq = m // (nseq * tm)
    kern = functools.partial(_conv_proj_kernel, shift=shift, tiles_per_seq=tiles_per_seq)
    state = pl.BlockSpec((1, 2 * shift, tn), lambda j, i: (i // tiles_per_seq, 0, j))
    wb = pltpu.VMEM((tn, d), BF16)
    return pl.pallas_call(
        kern,
        grid=(nj, m // tm),
        in_specs=[pl.BlockSpec((tm, d), lambda j, i: (i, 0)),
                  pl.BlockSpec((tn, d), lambda j, i: (j, 0)),
                  pl.BlockSpec((tn, d), lambda j, i: (j + nj, 0)),
                  pl.BlockSpec((tn, d), lambda j, i: (j + 2 * nj, 0)),
                  state,
                  pl.BlockSpec((CONV_WIDTH, tn), lambda j, i: (0, j))],
        out_specs=[pl.BlockSpec((tm, tn), lambda j, i: (i, j)),
                   state],
        out_shape=[jax.ShapeDtypeStruct((m, D_CONV), BF16),
                   jax.ShapeDtypeStruct((nseq, 2 * shift, D_CONV), F32)],
        scratch_shapes=[wb, wb, wb, _conv_buf(tm, tn, shift)],
        compiler_params=_params("arbitrary", "arbitrary"),
        name="conv_proj",
    )(xn, w_in_t, w_in_t, w_in_t, prev, conv_w)


def _cumsum_rows(g):
    c = g.shape[0]
    row = lax.broadcasted_iota(jnp.int32, g.shape, 0)
    x = g
    s = 1
    while s < c:
        x = x + jnp.where(row >= s, pltpu.roll(x, s, 0), 0.0)
        s *= 2
    return x


def _bcast_block_row(x, s, k):
    c, lanes = x.shape
    if s == c:
        return jnp.broadcast_to(x[k:k + 1, :], x.shape)
    if s >= SUBLANES:
        y = x.reshape(c // s, s, lanes)
        return jnp.broadcast_to(y[:, k:k + 1, :], y.shape).reshape(c, lanes)
    y = x.reshape(c // SUBLANES, SUBLANES, lanes)
    sub = lax.broadcasted_iota(jnp.int32, y.shape, 1)
    out = None
    for blk in range(SUBLANES // s):
        src = jnp.broadcast_to(y[:, blk * s + k:blk * s + k + 1, :], y.shape)
        out = src if out is None else jnp.where(sub >= blk * s, src, out)
    return out.reshape(c, lanes)


def _gla_pair_masks(c):
    ri = lax.broadcasted_iota(jnp.int32, (c, c), 0)
    ci = lax.broadcasted_iota(jnp.int32, (c, c), 1)
    diff_bits = ri ^ ci
    masks = [diff_bits == 0]
    level = 0
    while (1 << level) < c:
        masks.append(((diff_bits >> level) == 1) & (((ri >> level) & 1) == 1))
        level += 1
    return masks


def _gla_chunk(q, k, v, g, s_prev, masks):
    c = q.shape[0]
    cum = _cumsum_rows(g)
    a = jnp.where(masks[0], _dot_nt(q.astype(BF16), k.astype(BF16)), 0.0)
    for level in range(len(masks) - 1):
        half = 1 << level
        ref = _bcast_block_row(cum, 2 * half, half - 1)
        d = cum - ref
        up = jnp.minimum(d, 0.0)
        qe = q * jnp.exp2(up)
        ke = k * jnp.exp2(up - d)
        a = a + jnp.where(masks[1 + level], _dot_nt(qe.astype(BF16), ke.astype(BF16)), 0.0)
    o = _dot(a.astype(BF16), v.astype(BF16)) + _dot((q * jnp.exp2(cum)).astype(BF16), s_prev.astype(BF16))
    last = cum[c - 1:c, :]
    kd = k * jnp.exp2(last - cum)
    dk = last.shape[1]
    decay_t = jnp.transpose(jnp.broadcast_to(jnp.exp2(last), (dk, dk)))
    decayed = jnp.concatenate([decay_t * s_prev[:, i:i + dk] for i in range(0, s_prev.shape[1], dk)], axis=1)
    s_new = decayed + _dot_tn(kd.astype(BF16), v.astype(BF16))
    return o, s_new


def _gla_kernel(q_ref, k_ref, v_ref, r_ref, g_ref, s0_ref, gn_ref, *refs, chunk, single_chunk, group, n_cast):
    cast_src, (o_ref, sn_ref), cast_dst = refs[:n_cast], refs[n_cast:n_cast + 2], refs[n_cast + 2:]
    bb, rows = q_ref.shape[0], q_ref.shape[1]
    nchunk = rows // chunk
    state_in = s0_ref if single_chunk else sn_ref

    if not single_chunk:
        @pl.when(pl.program_id(1) == 0)
        def _():
            sn_ref[...] = s0_ref[...]

    masks = _gla_pair_masks(chunk)

    def one(b, r):
        for h in range(GLA_HEADS):
            kc = slice(h * GLA_DK, (h + 1) * GLA_DK)
            vc = slice(h * GLA_DV, (h + 1) * GLA_DV)
            q = q_ref[b, r, kc] * (GLA_DK ** -0.5)
            o, s_new = _gla_chunk(q, k_ref[b, r, kc], v_ref[b, r, vc], g_ref[b, r, kc], state_in[b, h], masks)
            sn_ref[b, h] = s_new
            rr = r_ref[b, r, vc]
            o_ref[b, r, vc] = (_rms_rows(o, gn_ref[:, vc]) * (rr * jax.nn.sigmoid(rr))).astype(o_ref.dtype)

    def body(n, carry):
        r = pl.ds(pl.multiple_of((n % nchunk) * chunk, chunk), chunk)
        for u in range(group):
            one((n // nchunk) * group + u, r)
        return carry

    lax.fori_loop(0, (bb // group) * nchunk, body, 0)
    for src, dst in zip(cast_src, cast_dst):
        _cast_into(src, dst)


def _gla(p, logf, s0, gla_norm, *, q_blk, k_blk, v_blk, r_blk, bb, rows, chunk, group, cast=()):
    nb, t, _ = p.shape
    nk, nv = GLA_HEADS * GLA_DK, GLA_HEADS * GLA_DV
    nt = t // rows
    nsteps = (nb // bb) * nt
    kern = functools.partial(_gla_kernel, chunk=chunk, single_chunk=(t == chunk), group=group, n_cast=len(cast))
    state_spec = pl.BlockSpec((bb, GLA_HEADS, GLA_DK, GLA_DV), lambda b, c: (b, 0, 0, 0))
    cast_specs = [pl.BlockSpec((w.shape[0] // nsteps, w.shape[1]), lambda b, c: (b * nt + c, 0)) for w in cast]
    return pl.pallas_call(
        kern,
        grid=(nb // bb, nt),
        in_specs=[pl.BlockSpec((bb, rows, nk), lambda b, c: (b, c, q_blk)),
                  pl.BlockSpec((bb, rows, nk), lambda b, c: (b, c, k_blk)),
                  pl.BlockSpec((bb, rows, nv), lambda b, c: (b, c, v_blk)),
                  pl.BlockSpec((bb, rows, nv), lambda b, c: (b, c, r_blk)),
                  pl.BlockSpec((bb, rows, nk), lambda b, c: (b, c, 0)),
                  state_spec,
                  pl.BlockSpec((1, nv), lambda b, c: (0, 0))] + cast_specs,
        out_specs=[pl.BlockSpec((bb, rows, nv), lambda b, c: (b, c, 0)),
                   state_spec] + cast_specs,
        out_shape=[jax.ShapeDtypeStruct((nb, t, D_GLA), BF16),
                   jax.ShapeDtypeStruct((nb, GLA_HEADS, GLA_DK, GLA_DV), F32)]
                  + [jax.ShapeDtypeStruct(w.shape, BF16) for w in cast],
        compiler_params=_params("arbitrary", "arbitrary"),
        name="gla",
    )(p, p, p, p, logf, s0, gla_norm, *cast)


def _proj_res_norm_kernel(*refs, n_a, nk, final):
    a_refs, (w_ref, res_ref, g_ref) = refs[:n_a], refs[n_a:n_a + 3]
    out_refs, acc_ref = refs[n_a + 3:-1], refs[-1]
    k = pl.program_id(1)
    part, r0 = None, 0
    for a_ref in a_refs:
        kw = a_ref.shape[1]
        term = _dot(a_ref[...], w_ref[r0:r0 + kw, :])
        part = term if part is None else part + term
        r0 += kw

    @pl.when(k == 0)
    def _():
        acc_ref[...] = part

    if nk > 1:
        @pl.when(k > 0)
        def _():
            acc_ref[...] += part

    @pl.when(k == nk - 1)
    def _():
        rows = acc_ref.shape[0]
        ch = _row_chunk(rows, NORM_ROWS)
        g = g_ref[...]

        def body(c, carry):
            r = pl.ds(pl.multiple_of(c * ch, ch), ch)
            h = res_ref[r, :] + acc_ref[r, :]
            hn = _rms_rows(h, g)
            if final:
                out_refs[0][r, :] = hn
            else:
                out_refs[0][r, :] = h
                out_refs[1][r, :] = hn.astype(out_refs[1].dtype)
            return carry

        lax.fori_loop(0, rows // ch, body, 0)


def _proj_res_norm(a_list, w, res, gain, *, tm, tk, final):
    m = a_list[0].shape[0]
    kdim, d = w.shape
    nk = kdim // tk
    assert len(a_list) == 1 or nk == 1
    kern = functools.partial(_proj_res_norm_kernel, n_a=len(a_list), nk=nk, final=final)
    a_specs = ([pl.BlockSpec((tm, tk), lambda i, k: (i, k))] if len(a_list) == 1 else
               [pl.BlockSpec((tm, a.shape[1]), lambda i, k: (i, 0)) for a in a_list])
    row_spec = pl.BlockSpec((tm, d), lambda i, k: (i, 0))
    if final:
        out_specs = row_spec
        out_shape = jax.ShapeDtypeStruct((m, d), F32)
    else:
        out_specs = [row_spec, row_spec]
        out_shape = [jax.ShapeDtypeStruct((m, d), F32), jax.ShapeDtypeStruct((m, d), BF16)]
    return pl.pallas_call(
        kern,
        grid=(m // tm, nk),
        in_specs=a_specs + [pl.BlockSpec((tk, d), lambda i, k: (k, 0),
                                         pipeline_mode=pl.Buffered(1 if nk == 1 else 2)),
                            row_spec,
                            pl.BlockSpec((1, d), lambda i, k: (0, 0))],
        out_specs=out_specs,
        out_shape=out_shape,
        scratch_shapes=[pltpu.VMEM((tm, d), F32)],
        compiler_params=_params("arbitrary", "arbitrary"),
        name="proj_res_norm",
    )(*a_list, w, res, gain)


def _xattn_kernel(hn_ref, wq_ref, k_ref, v_ref, o_ref, wb_ref):
    @pl.when(pl.program_id(1) == 0)
    def _():
        _cast_into(wq_ref, wb_ref)

    q = _dot(hn_ref[0], wb_ref[...]).astype(BF16)
    p = _softmax_rows(_dot_nt(q, k_ref[0].astype(BF16)) * (X_HD ** -0.5))
    o_ref[0] = _dot(p.astype(BF16), v_ref[0].astype(BF16)).astype(o_ref.dtype)


def _xattn(hn, w_xq, mk, mv):
    nb, t, d = hn.shape
    return pl.pallas_call(
        _xattn_kernel,
        grid=(X_HEADS, nb),
        in_specs=[pl.BlockSpec((1, t, d), lambda h, b: (b, 0, 0)),
                  pl.BlockSpec((d, X_HD), lambda h, b: (0, h)),
                  pl.BlockSpec((1, MEM_LEN, X_HD), lambda h, b: (b, 0, h)),
                  pl.BlockSpec((1, MEM_LEN, X_HD), lambda h, b: (b, 0, h))],
        out_specs=pl.BlockSpec((1, t, X_HD), lambda h, b: (b, 0, h)),
        out_shape=jax.ShapeDtypeStruct((nb, t, d), BF16),
        scratch_shapes=[pltpu.VMEM((d, X_HD), BF16)],
        compiler_params=_params("arbitrary", "arbitrary"),
        name="xattn",
    )(hn, w_xq, mk, mv)


def _softmax_rows(s):
    s = s - jnp.max(s, axis=-1, keepdims=True)
    e = jnp.exp(s)
    return e / jnp.sum(e, axis=-1, keepdims=True)


def _xattn_cache_kernel(q_ref, k_ref, v_ref, o_ref):
    bb, tq = q_ref.shape[0], q_ref.shape[1]
    nchunk = X_HD // LANES
    pitch = nchunk * X_HEADS

    def gather(ref, b, h):
        parts = [ref[b, pl.ds(c * X_HEADS + h, MEM_LEN, stride=pitch), :] for c in range(nchunk)]
        return jnp.concatenate(parts, axis=1).astype(BF16)

    pairs = [(b, h) for b in range(bb) for h in range(X_HEADS)]
    scores = [_dot_nt(q_ref[b, :, h * X_HD:(h + 1) * X_HD].astype(BF16), gather(k_ref, b, h)) for b, h in pairs]
    p = _softmax_rows(jnp.concatenate(scores, axis=0) * (X_HD ** -0.5)).astype(BF16)
    for n, (b, h) in enumerate(pairs):
        o_ref[b, :, h * X_HD:(h + 1) * X_HD] = _dot(p[n * tq:(n + 1) * tq], gather(v_ref, b, h)).astype(o_ref.dtype)


def _xattn_cache(q, ck, cv, *, bb):
    nb, tq, d = q.shape
    nchunk = X_HD // LANES

    def stored_order(c):
        c = c.reshape(nb, MEM_LEN, X_HEADS, nchunk, LANES).transpose(0, 1, 3, 2, 4)
        return c.reshape(nb, MEM_LEN * nchunk * X_HEADS, LANES)

    rows = MEM_LEN * nchunk * X_HEADS
    return pl.pallas_call(
        _xattn_cache_kernel,
        grid=(nb // bb,),
        in_specs=[pl.BlockSpec((bb, tq, d), lambda b: (b, 0, 0)),
                  pl.BlockSpec((bb, rows, LANES), lambda b: (b, 0, 0)),
                  pl.BlockSpec((bb, rows, LANES), lambda b: (b, 0, 0))],
        out_specs=pl.BlockSpec((bb, tq, d), lambda b: (b, 0, 0)),
        out_shape=jax.ShapeDtypeStruct((nb, tq, d), BF16),
        compiler_params=_params("arbitrary"),
        name="xattn_cache",
    )(q, stored_order(ck), stored_order(cv))


def _ffn_up_kernel(hn_ref, wg_ref, wu_ref, cw_ref, cb_ref, prev_ref, o_ref, new_ref, buf_ref, up_ref,
                   wgb_ref, wub_ref, *, shift):
    rows = hn_ref.shape[0]
    halo = _halo_rows(shift)
    ch = _row_chunk(rows)

    @pl.when(pl.program_id(1) == 0)
    def _():
        _cast_into(wg_ref, wgb_ref)
        _cast_into(wu_ref, wub_ref)

    _conv_stage(buf_ref, halo - 2 * shift, prev_ref[0])
    _conv_stage(buf_ref, halo, _dot(hn_ref[...], wgb_ref[...]))
    up_ref[...] = _dot(hn_ref[...], wub_ref[...])

    def act_body(c, carry):
        r0 = pl.multiple_of(c * ch, ch)
        r = pl.ds(r0, ch)
        for s in range(buf_ref.shape[0]):
            g2, g1, g0 = _conv_taps(buf_ref, s, r0, ch, shift)
            gc = (cw_ref[0:1, _slab(s)] * g2 + cw_ref[1:2, _slab(s)] * g1 + cw_ref[2:3, _slab(s)] * g0
                  + cb_ref[:, _slab(s)])
            o_ref[r, _slab(s)] = ((gc * jax.nn.sigmoid(gc)) * up_ref[r, _slab(s)]).astype(o_ref.dtype)
        return carry

    lax.fori_loop(0, rows // ch, act_body, 0)
    _conv_last(buf_ref, new_ref, rows, shift)


def _ffn_up(hn, wg, wu, cw, cb, prev, nseq, rows, shift, tn):
    d = hn.shape[1]
    kern = functools.partial(_ffn_up_kernel, shift=shift)
    return pl.pallas_call(
        kern,
        grid=(D_FF // tn, nseq),
        in_specs=[pl.BlockSpec((rows, d), lambda j, b: (b, 0)),
                  pl.BlockSpec((d, tn), lambda j, b: (0, j)),
                  pl.BlockSpec((d, tn), lambda j, b: (0, j)),
                  pl.BlockSpec((CONV_WIDTH, tn), lambda j, b: (0, j)),
                  pl.BlockSpec((1, tn), lambda j, b: (0, j)),
                  pl.BlockSpec((1, 2 * shift, tn), lambda j, b: (b, 0, j))],
        out_specs=[pl.BlockSpec((rows, tn), lambda j, b: (b, j)),
                   pl.BlockSpec((1, 2 * shift, tn), lambda j, b: (b, 0, j))],
        out_shape=[jax.ShapeDtypeStruct((nseq * rows, D_FF), BF16),
                   jax.ShapeDtypeStruct((nseq, 2 * shift, D_FF), F32)],
        scratch_shapes=[_conv_buf(rows, tn, shift), pltpu.VMEM((rows, tn), F32),
                        pltpu.VMEM((d, tn), BF16), pltpu.VMEM((d, tn), BF16)],
        compiler_params=_params("arbitrary", "arbitrary"),
        name="ffn_up",
    )(hn, wg, wu, cw, cb, prev)


def _layer(x, w, wb, *, nseq, rows, shift, prev_conv, s0, prev_ffn, mk, mv, time_major):
    m = x.shape[0]
    tm = min(m, 1024)
    xn, logf = _norm_gate(x, w["norm_mix"], w["w_g1_t"], w["w_g2"], w["b_gate"], tm)
    conv_out, conv_new = _conv_proj(xn, w["w_in_t"], prev_conv, w["conv_w"], nseq=nseq, shift=shift, tm=tm, tn=512)
    n_gla = N_MAIN - 3 * D_CONV
    p = _col_matmul(xn, w["w_in_t"], 3 * D_CONV, n_gla, F32, tm, 1024)

    if time_major:
        nt = m // shift
        pad = ((0, 0), (0, SUBLANES - nt), (0, 0))
        pg = jnp.pad(p.reshape(nt, shift, n_gla).transpose(1, 0, 2), pad)
        lg = jnp.pad(logf.reshape(nt, shift, -1).transpose(1, 0, 2), pad)
        gla_args = dict(bb=8, rows=SUBLANES, chunk=SUBLANES, group=4)
    else:
        pg, lg = p.reshape(nseq, rows, n_gla), logf.reshape(nseq, rows, -1)
        gla_args = dict(bb=4, rows=128, chunk=GLA_CHUNK, group=4)
    cast = () if wb is not None else (w["w_out"], w["w_xo"], w["w_fd"])
    o, s_new, *made = _gla(pg, lg, s0, w["gla_norm"], q_blk=0, k_blk=1, v_blk=1, r_blk=2, cast=cast, **gla_args)
    if wb is None:
        wb = dict(zip(("w_out", "w_xo", "w_fd"), made))
    if time_major:
        gla_out = o[:, :nt].transpose(1, 0, 2).reshape(m, D_GLA)
    else:
        gla_out = o.reshape(m, D_GLA)

    tm2 = min(m, 512)
    h, hn = _proj_res_norm([conv_out, gla_out], wb["w_out"], x, w["norm_x"], tm=tm2, tk=D_MODEL, final=False)
    if time_major:
        nt = m // shift
        qx = _matmul(hn, w["w_xq"], BF16, tm)
        qb = jnp.pad(qx.astype(F32).reshape(nt, shift, D_MODEL).transpose(1, 0, 2),
                     ((0, 0), (0, SUBLANES - nt), (0, 0)))
        ob = _xattn_cache(qb, mk, mv, bb=4)
        attn = ob[:, :nt].transpose(1, 0, 2).reshape(m, D_MODEL)
    else:
        attn = _xattn(hn.reshape(nseq, rows, D_MODEL), w["w_xq"], mk, mv).reshape(m, D_MODEL)

    h2, hn2 = _proj_res_norm([attn], wb["w_xo"], h, w["norm_ffn"], tm=tm2, tk=D_MODEL, final=False)
    act, ffn_new = _ffn_up(hn2, w["w_fg"], w["w_fu"], w["ffn_conv_w"], w["ffn_conv_b"], prev_ffn,
                           nseq, rows, shift, 512)
    return act, h2, conv_new, s_new, ffn_new, wb


def kernel(x_prompt, x_sample, mem_prompt, cache_conv, state_gla, cache_ffn, cache_mem_k, cache_mem_v,
           norm_mix, w_in, conv_w, w_gate2, b_gate, gla_norm, w_out, norm_x, norm_mem, w_xq, w_xk, w_xv,
           w_xo, norm_ffn, w_ffn_gate, w_ffn_up, ffn_conv_w, ffn_conv_b, w_ffn_down, norm_final):
    depth = w_in.shape[0]
    nb, seq, d = x_prompt.shape
    db, dseq, _ = x_sample.shape
    hp = x_prompt.reshape(nb * seq, d)
    hs = x_sample.transpose(1, 0, 2).reshape(dseq * db, d)
    outs = {k: [] for k in ("conv_p", "gla_p", "ffn_p", "mk", "mv", "conv_s", "gla_s", "ffn_s")}
    nfinal = norm_final.reshape(1, d)
    yp = ys = None
    for l in range(depth):
        w = {
            "norm_mix": norm_mix[l].reshape(1, d),
            "w_in_t": w_in[l].T,
            "w_g1_t": jnp.pad(w_in[l].T[N_MAIN:], ((0, LANES - GLA_RANK), (0, 0))).astype(BF16),
            "w_g2": jnp.pad(w_gate2[l], ((0, LANES - GLA_RANK), (0, 0))).astype(BF16),
            "b_gate": b_gate[l].reshape(1, -1),
            "conv_w": conv_w[l],
            "gla_norm": gla_norm[l].reshape(1, -1),
            "w_out": w_out[l],
            "norm_x": norm_x[l].reshape(1, d),
            "w_xq": w_xq[l],
            "w_xo": w_xo[l],
            "w_fd": w_ffn_down[l],
            "norm_ffn": norm_ffn[l].reshape(1, d),
            "w_fg": w_ffn_gate[l],
            "w_fu": w_ffn_up[l],
            "ffn_conv_w": ffn_conv_w[l],
            "ffn_conv_b": ffn_conv_b[l].reshape(1, -1),
        }
        last = l == depth - 1
        gain_next = nfinal if last else None

        mem = mem_prompt.reshape(nb * MEM_LEN, d)
        nmem = norm_mem[l].reshape(1, d)
        mk, mk_cache = _mem_proj(mem, nmem, w_xk[l], 1024)
        mv, mv_cache = _mem_proj(mem, nmem, w_xv[l], 1024)
        act, h2, c1, s1, f1, wb = _layer(
            hp, w, None, nseq=nb, rows=seq, shift=1,
            prev_conv=jnp.zeros((nb, CONV_WIDTH - 1, D_CONV), F32),
            s0=jnp.zeros((nb, GLA_HEADS, GLA_DK, GLA_DV), F32),
            prev_ffn=jnp.zeros((nb, CONV_WIDTH - 1, D_FF), F32),
            mk=mk.reshape(nb, MEM_LEN, d), mv=mv.reshape(nb, MEM_LEN, d), time_major=False)
        assert last, "only the final layer's epilogue (final rmsnorm) is implemented"
        yp = _proj_res_norm([act], wb["w_fd"], h2, gain_next, tm=256, tk=D_FF, final=True)
        outs["conv_p"].append(c1)
        outs["gla_p"].append(s1)
        outs["ffn_p"].append(f1)
        outs["mk"].append(mk_cache)
        outs["mv"].append(mv_cache)

        def tmajor(c):
            return c.transpose(1, 0, 2).reshape(1, (CONV_WIDTH - 1) * db, c.shape[-1])

        act, h2, c2, s2, f2, _ = _layer(
            hs, w, wb, nseq=1, rows=dseq * db, shift=db,
            prev_conv=tmajor(cache_conv[l]), s0=state_gla[l], prev_ffn=tmajor(cache_ffn[l]),
            mk=cache_mem_k[l], mv=cache_mem_v[l], time_major=True)
        ys = _proj_res_norm([act], wb["w_fd"], h2, gain_next, tm=256, tk=D_FF, final=True)
        outs["conv_s"].append(c2.reshape(CONV_WIDTH - 1, db, D_CONV).transpose(1, 0, 2))
        outs["gla_s"].append(s2)
        outs["ffn_s"].append(f2.reshape(CONV_WIDTH - 1, db, D_FF).transpose(1, 0, 2))

    y_prompt = yp.reshape(nb, seq, d)
    y_sample = ys.reshape(dseq, db, d).transpose(1, 0, 2)
    st = lambda k: jnp.stack(outs[k])
    return (y_prompt, y_sample, st("conv_p"), st("gla_p"), st("ffn_p"), st("mk"), st("mv"),
            st("conv_s"), st("gla_s"), st("ffn_s"))
```

```python
import functools

import jax
import jax.numpy as jnp
from jax import lax
from jax.experimental import pallas as pl
from jax.experimental.pallas import tpu as pltpu

F32 = jnp.float32
BF16 = jnp.bfloat16

D_MODEL = 2048
EPS = 1e-6
CONV_WIDTH = 3
D_CONV = 1024
D_GLA = 1024
GLA_HEADS = 4
GLA_DV = 256
GLA_DK = 128
GLA_RANK = 16
GLA_TAU = 16.0
LOG2_E = 1.4426950408889634
GLA_CHUNK = 64
X_HEADS = 4
X_HD = 512
MEM_LEN = 256
D_FF = 5632
N_MAIN = 3 * D_CONV + 2 * GLA_HEADS * GLA_DK + 2 * GLA_HEADS * GLA_DV

LANES = 128
SUBLANES = 8
VMEM_LIMIT_BYTES = 56 * 1024 * 1024


def _params(*sem):
    return pltpu.CompilerParams(dimension_semantics=sem, vmem_limit_bytes=VMEM_LIMIT_BYTES)


def _dot(a, b):
    return jnp.dot(a, b, preferred_element_type=F32)


def _dot_nt(a, b):
    return lax.dot_general(a, b, (((1,), (1,)), ((), ())), preferred_element_type=F32)


def _dot_tn(a, b):
    return lax.dot_general(a, b, (((0,), (0,)), ((), ())), preferred_element_type=F32)


def _rms_rows(x, g):
    ms = jnp.mean(x * x, axis=-1, keepdims=True)
    return (x * lax.rsqrt(ms + EPS)) * g


def _row_chunk(rows, limit=256):
    for c in (256, 128, 64, 32, 16, 8):
        if c <= limit and rows % c == 0:
            return c
    return rows


NORM_ROWS = 128


def _norm_into(x_ref, g_ref, xn_ref):
    rows = x_ref.shape[0]
    ch = _row_chunk(rows)
    g = g_ref[...]

    def body(c, carry):
        r = pl.ds(pl.multiple_of(c * ch, ch), ch)
        xn_ref[r, :] = _rms_rows(x_ref[r, :], g).astype(xn_ref.dtype)
        return carry

    lax.fori_loop(0, rows // ch, body, 0)


def _norm_matmul_kernel(x_ref, g_ref, w_ref, o_ref, oc_ref, xn_ref):
    j = pl.program_id(1)

    @pl.when(j == 0)
    def _():
        _norm_into(x_ref, g_ref, xn_ref)

    o_ref[...] = _dot(xn_ref[...], w_ref[...].astype(BF16)).astype(o_ref.dtype)
    nchunk = X_HD // LANES
    pitch = nchunk * X_HEADS
    heads_per_tile = o_ref.shape[1] // X_HD
    for b in range(oc_ref.shape[0]):
        for hh in range(heads_per_tile):
            for c in range(nchunk):
                col = hh * X_HD + c * LANES
                row = c * X_HEADS + j * heads_per_tile + hh
                oc_ref[b, pl.ds(row, MEM_LEN, stride=pitch), :] = (
                    o_ref[b * MEM_LEN:(b + 1) * MEM_LEN, col:col + LANES])


def _norm_gate_kernel(x_ref, g_ref, wg1_ref, wg2_ref, bg_ref, xn_ref, lf_ref):
    _norm_into(x_ref, g_ref, xn_ref)
    g1 = _dot_nt(xn_ref[...], wg1_ref[...])
    z = _dot(g1.astype(BF16), wg2_ref[...]) + bg_ref[...]
    lf_ref[...] = (jnp.minimum(z, 0.0) - jnp.log1p(jnp.exp(-jnp.abs(z)))) * (LOG2_E / GLA_TAU)


def _mem_proj(mem, gain, w, tn):
    m, d = mem.shape
    n = w.shape[1]
    nb = m // MEM_LEN
    nchunk = X_HD // LANES
    rows = MEM_LEN * nchunk * X_HEADS
    flat, stored = pl.pallas_call(
        _norm_matmul_kernel,
        grid=(1, n // tn),
        in_specs=[pl.BlockSpec((m, d), lambda i, j: (0, 0)),
                  pl.BlockSpec((1, d), lambda i, j: (0, 0)),
                  pl.BlockSpec((d, tn), lambda i, j: (0, j))],
        out_specs=[pl.BlockSpec((m, tn), lambda i, j: (0, j)),
                   pl.BlockSpec((nb, rows, LANES), lambda i, j: (0, 0, 0))],
        out_shape=[jax.ShapeDtypeStruct((m, n), F32),
                   jax.ShapeDtypeStruct((nb, rows, LANES), F32)],
        scratch_shapes=[pltpu.VMEM((m, d), BF16)],
        compiler_params=_params("arbitrary", "arbitrary"),
        name="norm_matmul",
    )(mem, gain, w)
    cache = stored.reshape(nb, MEM_LEN, nchunk, X_HEADS, LANES).transpose(0, 1, 3, 2, 4)
    return flat, cache.reshape(nb, MEM_LEN, X_HEADS, X_HD)


def _norm_gate(x, gain, wg1, wg2, b_gate, tm):
    m, d = x.shape
    ng = wg2.shape[1]
    return pl.pallas_call(
        _norm_gate_kernel,
        grid=(m // tm,),
        in_specs=[pl.BlockSpec((tm, d), lambda i: (i, 0)),
                  pl.BlockSpec((1, d), lambda i: (0, 0)),
                  pl.BlockSpec((LANES, d), lambda i: (0, 0)),
                  pl.BlockSpec((LANES, ng), lambda i: (0, 0)),
                  pl.BlockSpec((1, ng), lambda i: (0, 0))],
        out_specs=[pl.BlockSpec((tm, d), lambda i: (i, 0)),
                   pl.BlockSpec((tm, ng), lambda i: (i, 0))],
        out_shape=[jax.ShapeDtypeStruct((m, d), BF16),
                   jax.ShapeDtypeStruct((m, ng), F32)],
        compiler_params=_params("arbitrary"),
        name="norm_gate",
    )(x, gain, wg1, wg2, b_gate)


def _cast_into(src_ref, dst_ref):
    rows = src_ref.shape[0]
    ch = _row_chunk(rows)

    def body(c, carry):
        r = pl.ds(pl.multiple_of(c * ch, ch), ch)
        dst_ref[r, :] = src_ref[r, :].astype(dst_ref.dtype)
        return carry

    lax.fori_loop(0, rows // ch, body, 0)


def _matmul_kernel(a_ref, w_ref, o_ref):
    o_ref[...] = _dot(a_ref[...], w_ref[...]).astype(o_ref.dtype)


def _matmul(a, w, out_dtype, tm):
    m, k = a.shape
    n = w.shape[1]
    return pl.pallas_call(
        _matmul_kernel,
        grid=(m // tm,),
        in_specs=[pl.BlockSpec((tm, k), lambda i: (i, 0)),
                  pl.BlockSpec((k, n), lambda i: (0, 0), pipeline_mode=pl.Buffered(1))],
        out_specs=pl.BlockSpec((tm, n), lambda i: (i, 0)),
        out_shape=jax.ShapeDtypeStruct((m, n), out_dtype),
        compiler_params=_params("arbitrary"),
        name="matmul",
    )(a, w)


def _col_matmul_kernel(a_ref, wt_ref, o_ref, *wb):
    w_ref = wt_ref
    if wb:
        w_ref = wb[0]

        @pl.when(pl.program_id(1) == 0)
        def _():
            _cast_into(wt_ref, w_ref)

    o_ref[...] = _dot_nt(a_ref[...], w_ref[...]).astype(o_ref.dtype)


def _col_matmul(a, wt, col0, ncols, out_dtype, tm, tn):
    m, k = a.shape
    j0 = col0 // tn
    out_specs = [pl.BlockSpec((tm, tn), lambda j, i: (i, j))]
    out_shape = [jax.ShapeDtypeStruct((m, ncols), out_dtype)]
    if wt.dtype != BF16:
        out_specs.append(pl.BlockSpec((tn, k), lambda j, i: (j, 0)))
        out_shape.append(jax.ShapeDtypeStruct((ncols, k), BF16))
    res = pl.pallas_call(
        _col_matmul_kernel,
        grid=(ncols // tn, m // tm),
        in_specs=[pl.BlockSpec((tm, k), lambda j, i: (i, 0)),
                  pl.BlockSpec((tn, k), lambda j, i: (j0 + j, 0))],
        out_specs=out_specs,
        out_shape=out_shape,
        compiler_params=_params("arbitrary", "arbitrary"),
        name="col_matmul",
    )(a, wt)
    return res if len(res) == 2 else (res[0], wt)


def _halo_rows(shift):
    return max(2 * shift, SUBLANES)


def _conv_buf(rows, tn, shift):
    return pltpu.VMEM((tn // LANES, _halo_rows(shift) + rows, LANES), F32)


def _slab(s):
    return slice(s * LANES, (s + 1) * LANES)


def _conv_stage(buf_ref, row0, x):
    for s in range(buf_ref.shape[0]):
        buf_ref[s, row0:row0 + x.shape[0], :] = x[:, _slab(s)]


def _conv_taps(buf_ref, s, r0, rows, shift):
    halo = _halo_rows(shift)

    def back(steps):
        start = r0 + (halo - steps * shift)
        if (steps * shift) % SUBLANES == 0:
            return pl.ds(start if isinstance(start, int) else pl.multiple_of(start, SUBLANES), rows)
        return pl.ds(start, rows, stride=1)

    return buf_ref[s, back(2), :], buf_ref[s, back(1), :], buf_ref[s, back(0), :]


def _conv_last(buf_ref, new_ref, rows, shift):
    halo = _halo_rows(shift)
    for s in range(buf_ref.shape[0]):
        new_ref[0, :, _slab(s)] = buf_ref[s, halo + rows - 2 * shift:halo + rows, :]


def _conv_proj_kernel(xn_ref, wbg_ref, wcg_ref, wvc_ref, prev_ref, cw_ref, o_ref, new_ref, *scratch,
                      shift, tiles_per_seq):
    rows = xn_ref.shape[0]
    halo = _halo_rows(shift)
    i = pl.program_id(1)
    buf_ref = scratch[-1]
    bgb_ref, cgb_ref, vcb_ref = scratch[:-1] if len(scratch) > 1 else (wbg_ref, wcg_ref, wvc_ref)

    if len(scratch) > 1:
        @pl.when(i == 0)
        def _():
            _cast_into(wbg_ref, bgb_ref)
            _cast_into(wcg_ref, cgb_ref)
            _cast_into(wvc_ref, vcb_ref)

    def from_cache():
        _conv_stage(buf_ref, halo - 2 * shift, prev_ref[0])

    if tiles_per_seq == 1:
        from_cache()
    else:
        pl.when(i % tiles_per_seq == 0)(from_cache)

        @pl.when(i % tiles_per_seq != 0)
        def _():
            for s in range(buf_ref.shape[0]):
                buf_ref[s, halo - 2 * shift:halo, :] = buf_ref[s, halo + rows - 2 * shift:halo + rows, :]

    xn = xn_ref[...]
    _conv_stage(buf_ref, halo, _dot_nt(xn, cgb_ref[...]) * _dot_nt(xn, vcb_ref[...]))
    bg = _dot_nt(xn, bgb_ref[...])
    for s in range(buf_ref.shape[0]):
        u2, u1, u0 = _conv_taps(buf_ref, s, 0, rows, shift)
        y = cw_ref[0:1, _slab(s)] * u2 + cw_ref[1:2, _slab(s)] * u1 + cw_ref[2:3, _slab(s)] * u0
        o_ref[:, _slab(s)] = (bg[:, _slab(s)] * y).astype(o_ref.dtype)
    _conv_last(buf_ref, new_ref, rows, shift)


def _conv_proj(xn, w_in_t, prev, conv_w, *, nseq, shift, tm, tn):
    m, d = xn.shape
    nj = D_CONV // tn
    tiles_per_seq = m // (nseq * tm)
    kern = functools.partial(_conv_proj_kernel, shift=shift, tiles_per_seq=tiles_per_seq)
    state = pl.BlockSpec((1, 2 * shift, tn), lambda j, i: (i // tiles_per_seq, 0, j))
    wb = [pltpu.VMEM((tn, d), BF16)] * (0 if w_in_t.dtype == BF16 else 3)
    return pl.pallas_call(
        kern,
        grid=(nj, m // tm),
        in_specs=[pl.BlockSpec((tm, d), lambda j, i: (i, 0)),
                  pl.BlockSpec((tn, d), lambda j, i: (j, 0)),
                  pl.BlockSpec((tn, d), lambda j, i: (j + nj, 0)),
                  pl.BlockSpec((tn, d), lambda j, i: (j + 2 * nj, 0)),
                  state,
                  pl.BlockSpec((CONV_WIDTH, tn), lambda j, i: (0, j))],
        out_specs=[pl.BlockSpec((tm, tn), lambda j, i: (i, j)),
                   state],
        out_shape=[jax.ShapeDtypeStruct((m, D_CONV), BF16),
                   jax.ShapeDtypeStruct((nseq, 2 * shift, D_CONV), F32)],
        scratch_shapes=wb + [_conv_buf(tm, tn, shift)],
        compiler_params=_params("arbitrary", "arbitrary"),
        name="conv_proj",
    )(xn, w_in_t, w_in_t, w_in_t, prev, conv_w)


def _cumsum_rows(g):
    c = g.shape[0]
    row = lax.broadcasted_iota(jnp.int32, g.shape, 0)
    x = g
    s = 1
    while s < c:
        x = x + jnp.where(row >= s, pltpu.roll(x, s, 0), 0.0)
        s *= 2
    return x


def _bcast_block_row(x, s, k):
    c, lanes = x.shape
    if s == c:
        return jnp.broadcast_to(x[k:k + 1, :], x.shape)
    if s >= SUBLANES:
        y = x.reshape(c // s, s, lanes)
        return jnp.broadcast_to(y[:, k:k + 1, :], y.shape).reshape(c, lanes)
    y = x.reshape(c // SUBLANES, SUBLANES, lanes)
    sub = lax.broadcasted_iota(jnp.int32, y.shape, 1)
    out = None
    for blk in range(SUBLANES // s):
        src = jnp.broadcast_to(y[:, blk * s + k:blk * s + k + 1, :], y.shape)
        out = src if out is None else jnp.where(sub >= blk * s, src, out)
    return out.reshape(c, lanes)


def _gla_pair_masks(c):
    ri = lax.broadcasted_iota(jnp.int32, (c, c), 0)
    ci = lax.broadcasted_iota(jnp.int32, (c, c), 1)
    diff_bits = ri ^ ci
    masks = [diff_bits == 0]
    level = 0
    while (1 << level) < c:
        masks.append(((diff_bits >> level) == 1) & (((ri >> level) & 1) == 1))
        level += 1
    return masks


def _gla_chunk(q, k, v, g, s_prev, masks):
    c = q.shape[0]
    cum = _cumsum_rows(g)
    a = jnp.where(masks[0], _dot_nt(q.astype(BF16), k.astype(BF16)), 0.0)
    for level in range(len(masks) - 1):
        half = 1 << level
        ref = _bcast_block_row(cum, 2 * half, half - 1)
        d = cum - ref
        up = jnp.minimum(d, 0.0)
        qe = q * jnp.exp2(up)
        ke = k * jnp.exp2(up - d)
        a = a + jnp.where(masks[1 + level], _dot_nt(qe.astype(BF16), ke.astype(BF16)), 0.0)
    o = _dot(a.astype(BF16), v.astype(BF16)) + _dot((q * jnp.exp2(cum)).astype(BF16), s_prev.astype(BF16))
    last = cum[c - 1:c, :]
    kd = k * jnp.exp2(last - cum)
    dk = last.shape[1]
    decay_t = jnp.transpose(jnp.broadcast_to(jnp.exp2(last), (dk, dk)))
    decayed = jnp.concatenate([decay_t * s_prev[:, i:i + dk] for i in range(0, s_prev.shape[1], dk)], axis=1)
    s_new = decayed + _dot_tn(kd.astype(BF16), v.astype(BF16))
    return o, s_new


def _gla_kernel(q_ref, k_ref, v_ref, r_ref, g_ref, s0_ref, gn_ref, *refs, chunk, single_chunk, group, n_cast):
    cast_src, (o_ref, sn_ref), cast_dst = refs[:n_cast], refs[n_cast:n_cast + 2], refs[n_cast + 2:]
    bb, rows = q_ref.shape[0], q_ref.shape[1]
    nchunk = rows // chunk
    state_in = s0_ref if single_chunk else sn_ref

    if not single_chunk:
        @pl.when(pl.program_id(1) == 0)
        def _():
            sn_ref[...] = s0_ref[...]

    masks = _gla_pair_masks(chunk)

    def one(b, r):
        for h in range(GLA_HEADS):
            kc = slice(h * GLA_DK, (h + 1) * GLA_DK)
            vc = slice(h * GLA_DV, (h + 1) * GLA_DV)
            q = q_ref[b, r, kc] * (GLA_DK ** -0.5)
            o, s_new = _gla_chunk(q, k_ref[b, r, kc], v_ref[b, r, vc], g_ref[b, r, kc], state_in[b, h], masks)
            sn_ref[b, h] = s_new
            rr = r_ref[b, r, vc]
            o_ref[b, r, vc] = (_rms_rows(o, gn_ref[:, vc]) * (rr * jax.nn.sigmoid(rr))).astype(o_ref.dtype)

    def body(n, carry):
        r = pl.ds(pl.multiple_of((n % nchunk) * chunk, chunk), chunk)
        for u in range(group):
            one((n // nchunk) * group + u, r)
        return carry

    lax.fori_loop(0, (bb // group) * nchunk, body, 0)
    for src, dst in zip(cast_src, cast_dst):
        _cast_into(src, dst)


def _gla(p, logf, s0, gla_norm, *, q_blk, k_blk, v_blk, r_blk, bb, rows, chunk, group, cast=()):
    nb, t, _ = p.shape
    nk, nv = GLA_HEADS * GLA_DK, GLA_HEADS * GLA_DV
    nt = t // rows
    nsteps = (nb // bb) * nt
    kern = functools.partial(_gla_kernel, chunk=chunk, single_chunk=(t == chunk), group=group, n_cast=len(cast))
    state_spec = pl.BlockSpec((bb, GLA_HEADS, GLA_DK, GLA_DV), lambda b, c: (b, 0, 0, 0))
    cast_specs = [pl.BlockSpec((nrows // nsteps, w.shape[1]), lambda b, c: (b * nt + c, 0)) for w, nrows in cast]
    return pl.pallas_call(
        kern,
        grid=(nb // bb, nt),
        in_specs=[pl.BlockSpec((bb, rows, nk), lambda b, c: (b, c, q_blk)),
                  pl.BlockSpec((bb, rows, nk), lambda b, c: (b, c, k_blk)),
                  pl.BlockSpec((bb, rows, nv), lambda b, c: (b, c, v_blk)),
                  pl.BlockSpec((bb, rows, nv), lambda b, c: (b, c, r_blk)),
                  pl.BlockSpec((bb, rows, nk), lambda b, c: (b, c, 0)),
                  state_spec,
                  pl.BlockSpec((1, nv), lambda b, c: (0, 0))] + cast_specs,
        out_specs=[pl.BlockSpec((bb, rows, nv), lambda b, c: (b, c, 0)),
                   state_spec] + cast_specs,
        out_shape=[jax.ShapeDtypeStruct((nb, t, D_GLA), BF16),
                   jax.ShapeDtypeStruct((nb, GLA_HEADS, GLA_DK, GLA_DV), F32)]
                  + [jax.ShapeDtypeStruct((nrows, w.shape[1]), BF16) for w, nrows in cast],
        compiler_params=_params("arbitrary", "arbitrary"),
        name="gla",
    )(p, p, p, p, logf, s0, gla_norm, *[w for w, _ in cast])


def _proj_res_norm_kernel(*refs, n_a, nk, final):
    a_refs, (w_ref, res_ref, g_ref) = refs[:n_a], refs[n_a:n_a + 3]
    out_refs, acc_ref = refs[n_a + 3:-1], refs[-1]
    k = pl.program_id(1)
    part, r0 = None, 0
    for a_ref in a_refs:
        kw = a_ref.shape[1]
        term = _dot(a_ref[...], w_ref[r0:r0 + kw, :])
        part = term if part is None else part + term
        r0 += kw

    @pl.when(k == 0)
    def _():
        acc_ref[...] = part

    if nk > 1:
        @pl.when(k > 0)
        def _():
            acc_ref[...] += part

    @pl.when(k == nk - 1)
    def _():
        rows = acc_ref.shape[0]
        ch = _row_chunk(rows, NORM_ROWS)
        g = g_ref[...]

        def body(c, carry):
            r = pl.ds(pl.multiple_of(c * ch, ch), ch)
            h = res_ref[r, :] + acc_ref[r, :]
            hn = _rms_rows(h, g)
            if final:
                out_refs[0][r, :] = hn
            else:
                out_refs[0][r, :] = h
                out_refs[1][r, :] = hn.astype(out_refs[1].dtype)
            return carry

        lax.fori_loop(0, rows // ch, body, 0)


def _proj_res_norm(a_list, w, res, gain, *, tm, tk, final):
    m = a_list[0].shape[0]
    kdim, d = w.shape
    nk = kdim // tk
    assert len(a_list) == 1 or nk == 1
    kern = functools.partial(_proj_res_norm_kernel, n_a=len(a_list), nk=nk, final=final)
    a_specs = ([pl.BlockSpec((tm, tk), lambda i, k: (i, k))] if len(a_list) == 1 else
               [pl.BlockSpec((tm, a.shape[1]), lambda i, k: (i, 0)) for a in a_list])
    row_spec = pl.BlockSpec((tm, d), lambda i, k: (i, 0))
    if final:
        out_specs = row_spec
        out_shape = jax.ShapeDtypeStruct((m, d), F32)
    else:
        out_specs = [row_spec, row_spec]
        out_shape = [jax.ShapeDtypeStruct((m, d), F32), jax.ShapeDtypeStruct((m, d), BF16)]
    return pl.pallas_call(
        kern,
        grid=(m // tm, nk),
        in_specs=a_specs + [pl.BlockSpec((tk, d), lambda i, k: (k, 0),
                                         pipeline_mode=pl.Buffered(1 if nk == 1 else 2)),
                            row_spec,
                            pl.BlockSpec((1, d), lambda i, k: (0, 0))],
        out_specs=out_specs,
        out_shape=out_shape,
        scratch_shapes=[pltpu.VMEM((tm, d), F32)],
        compiler_params=_params("arbitrary", "arbitrary"),
        name="proj_res_norm",
    )(*a_list, w, res, gain)


def _xattn_kernel(hn_ref, wq_ref, k_ref, v_ref, o_ref, wb_ref):
    @pl.when(pl.program_id(1) == 0)
    def _():
        _cast_into(wq_ref, wb_ref)

    q = _dot(hn_ref[0], wb_ref[...]).astype(BF16)
    p = _softmax_rows(_dot_nt(q, k_ref[0].astype(BF16)) * (X_HD ** -0.5))
    o_ref[0] = _dot(p.astype(BF16), v_ref[0].astype(BF16)).astype(o_ref.dtype)


def _xattn(hn, w_xq, mk, mv):
    nb, t, d = hn.shape
    return pl.pallas_call(
        _xattn_kernel,
        grid=(X_HEADS, nb),
        in_specs=[pl.BlockSpec((1, t, d), lambda h, b: (b, 0, 0)),
                  pl.BlockSpec((d, X_HD), lambda h, b: (0, h)),
                  pl.BlockSpec((1, MEM_LEN, X_HD), lambda h, b: (b, 0, h)),
                  pl.BlockSpec((1, MEM_LEN, X_HD), lambda h, b: (b, 0, h))],
        out_specs=pl.BlockSpec((1, t, X_HD), lambda h, b: (b, 0, h)),
        out_shape=jax.ShapeDtypeStruct((nb, t, d), BF16),
        scratch_shapes=[pltpu.VMEM((d, X_HD), BF16)],
        compiler_params=_params("arbitrary", "arbitrary"),
        name="xattn",
    )(hn, w_xq, mk, mv)


def _softmax_rows(s):
    s = s - jnp.max(s, axis=-1, keepdims=True)
    e = jnp.exp(s)
    return e / jnp.sum(e, axis=-1, keepdims=True)


def _xattn_cache_kernel(q_ref, k_ref, v_ref, o_ref):
    bb, tq = q_ref.shape[0], q_ref.shape[1]
    nchunk = X_HD // LANES
    pitch = nchunk * X_HEADS

    def gather(ref, b, h):
        parts = [ref[b, pl.ds(c * X_HEADS + h, MEM_LEN, stride=pitch), :] for c in range(nchunk)]
        return jnp.concatenate(parts, axis=1).astype(BF16)

    pairs = [(b, h) for b in range(bb) for h in range(X_HEADS)]
    scores = [_dot_nt(q_ref[b, :, h * X_HD:(h + 1) * X_HD].astype(BF16), gather(k_ref, b, h)) for b, h in pairs]
    p = _softmax_rows(jnp.concatenate(scores, axis=0) * (X_HD ** -0.5)).astype(BF16)
    for n, (b, h) in enumerate(pairs):
        o_ref[b, :, h * X_HD:(h + 1) * X_HD] = _dot(p[n * tq:(n + 1) * tq], gather(v_ref, b, h)).astype(o_ref.dtype)


def _xattn_cache(q, ck, cv, *, bb):
    nb, tq, d = q.shape
    nchunk = X_HD // LANES

    def stored_order(c):
        c = c.reshape(nb, MEM_LEN, X_HEADS, nchunk, LANES).transpose(0, 1, 3, 2, 4)
        return c.reshape(nb, MEM_LEN * nchunk * X_HEADS, LANES)

    rows = MEM_LEN * nchunk * X_HEADS
    return pl.pallas_call(
        _xattn_cache_kernel,
        grid=(nb // bb,),
        in_specs=[pl.BlockSpec((bb, tq, d), lambda b: (b, 0, 0)),
                  pl.BlockSpec((bb, rows, LANES), lambda b: (b, 0, 0)),
                  pl.BlockSpec((bb, rows, LANES), lambda b: (b, 0, 0))],
        out_specs=pl.BlockSpec((bb, tq, d), lambda b: (b, 0, 0)),
        out_shape=jax.ShapeDtypeStruct((nb, tq, d), BF16),
        compiler_params=_params("arbitrary"),
        name="xattn_cache",
    )(q, stored_order(ck), stored_order(cv))


def _ffn_up_kernel(hn_ref, wg_ref, wu_ref, cw_ref, cb_ref, prev_ref, o_ref, new_ref, buf_ref, up_ref,
                   wgb_ref, wub_ref, *, shift):
    rows = hn_ref.shape[0]
    halo = _halo_rows(shift)
    ch = _row_chunk(rows)

    @pl.when(pl.program_id(1) == 0)
    def _():
        _cast_into(wg_ref, wgb_ref)
        _cast_into(wu_ref, wub_ref)

    _conv_stage(buf_ref, halo - 2 * shift, prev_ref[0])
    _conv_stage(buf_ref, halo, _dot(hn_ref[...], wgb_ref[...]))

    def gate_body(c, carry):
        r0 = pl.multiple_of(c * ch, ch)
        r = pl.ds(r0, ch)
        for s in range(buf_ref.shape[0]):
            g2, g1, g0 = _conv_taps(buf_ref, s, r0, ch, shift)
            gc = (cw_ref[0:1, _slab(s)] * g2 + cw_ref[1:2, _slab(s)] * g1 + cw_ref[2:3, _slab(s)] * g0
                  + cb_ref[:, _slab(s)])
            up_ref[r, _slab(s)] = gc * jax.nn.sigmoid(gc)
        return carry

    lax.fori_loop(0, rows // ch, gate_body, 0)
    o_ref[...] = (up_ref[...] * _dot(hn_ref[...], wub_ref[...])).astype(o_ref.dtype)
    _conv_last(buf_ref, new_ref, rows, shift)


def _ffn_up(hn, wg, wu, cw, cb, prev, nseq, rows, shift, tn):
    d = hn.shape[1]
    kern = functools.partial(_ffn_up_kernel, shift=shift)
    return pl.pallas_call(
        kern,
        grid=(D_FF // tn, nseq),
        in_specs=[pl.BlockSpec((rows, d), lambda j, b: (b, 0)),
                  pl.BlockSpec((d, tn), lambda j, b: (0, j)),
                  pl.BlockSpec((d, tn), lambda j, b: (0, j)),
                  pl.BlockSpec((CONV_WIDTH, tn), lambda j, b: (0, j)),
                  pl.BlockSpec((1, tn), lambda j, b: (0, j)),
                  pl.BlockSpec((1, 2 * shift, tn), lambda j, b: (b, 0, j))],
        out_specs=[pl.BlockSpec((rows, tn), lambda j, b: (b, j)),
                   pl.BlockSpec((1, 2 * shift, tn), lambda j, b: (b, 0, j))],
        out_shape=[jax.ShapeDtypeStruct((nseq * rows, D_FF), BF16),
                   jax.ShapeDtypeStruct((nseq, 2 * shift, D_FF), F32)],
        scratch_shapes=[_conv_buf(rows, tn, shift), pltpu.VMEM((rows, tn), F32),
                        pltpu.VMEM((d, tn), BF16), pltpu.VMEM((d, tn), BF16)],
        compiler_params=_params("arbitrary", "arbitrary"),
        name="ffn_up",
    )(hn, wg, wu, cw, cb, prev)


def _layer(x, w, wb, *, nseq, rows, shift, prev_conv, s0, prev_ffn, mk, mv, time_major):
    m = x.shape[0]
    tm = min(m, 1024)
    first = wb is None
    n_gla = N_MAIN - 3 * D_CONV
    xn, logf = _norm_gate(x, w["norm_mix"], w["w_g1_t"], w["w_g2"], w["b_gate"], tm)
    conv_out, conv_new = _conv_proj(xn, w["w_in_t"] if first else wb["w_conv_t"], prev_conv, w["conv_w"],
                                    nseq=nseq, shift=shift, tm=tm, tn=512)
    if first:
        p, w_gla_t = _col_matmul(xn, w["w_in_t"], 3 * D_CONV, n_gla, F32, tm, 1024)
    else:
        p, _ = _col_matmul(xn, wb["w_gla_t"], 0, n_gla, F32, tm, 1024)

    if time_major:
        nt = m // shift
        pad = ((0, 0), (0, SUBLANES - nt), (0, 0))
        pg = jnp.pad(p.reshape(nt, shift, n_gla).transpose(1, 0, 2), pad)
        lg = jnp.pad(logf.reshape(nt, shift, -1).transpose(1, 0, 2), pad)
        gla_args = dict(bb=8, rows=SUBLANES, chunk=SUBLANES, group=4)
    else:
        pg, lg = p.reshape(nseq, rows, n_gla), logf.reshape(nseq, rows, -1)
        gla_args = dict(bb=4, rows=128, chunk=GLA_CHUNK, group=4)
    names = ("w_out", "w_xo", "w_fd", "w_xq", "w_conv_t")
    cast = [(w[n], w[n].shape[0]) for n in names[:4]] + [(w["w_in_t"], 3 * D_CONV)] if first else []
    o, s_new, *made = _gla(pg, lg, s0, w["gla_norm"], q_blk=0, k_blk=1, v_blk=1, r_blk=2, cast=cast, **gla_args)
    if first:
        wb = dict(zip(names, made), w_gla_t=w_gla_t)
    if time_major:
        gla_out = o[:, :nt].transpose(1, 0, 2).reshape(m, D_GLA)
    else:
        gla_out = o.reshape(m, D_GLA)

    tm2 = min(m, 512)
    h, hn = _proj_res_norm([conv_out, gla_out], wb["w_out"], x, w["norm_x"], tm=tm2, tk=D_MODEL, final=False)
    if time_major:
        nt = m // shift
        qx = _matmul(hn, wb["w_xq"], BF16, tm)
        qb = jnp.pad(qx.astype(F32).reshape(nt, shift, D_MODEL).transpose(1, 0, 2),
                     ((0, 0), (0, SUBLANES - nt), (0, 0)))
        ob = _xattn_cache(qb, mk, mv, bb=4)
        attn = ob[:, :nt].transpose(1, 0, 2).reshape(m, D_MODEL)
    else:
        attn = _xattn(hn.reshape(nseq, rows, D_MODEL), w["w_xq"], mk, mv).reshape(m, D_MODEL)

    h2, hn2 = _proj_res_norm([attn], wb["w_xo"], h, w["norm_ffn"], tm=tm2, tk=D_MODEL, final=False)
    act, ffn_new = _ffn_up(hn2, w["w_fg"], w["w_fu"], w["ffn_conv_w"], w["ffn_conv_b"], prev_ffn,
                           nseq, rows, shift, 512)
    return act, h2, conv_new, s_new, ffn_new, wb


def kernel(x_prompt, x_sample, mem_prompt, cache_conv, state_gla, cache_ffn, cache_mem_k, cache_mem_v,
           norm_mix, w_in, conv_w, w_gate2, b_gate, gla_norm, w_out, norm_x, norm_mem, w_xq, w_xk, w_xv,
           w_xo, norm_ffn, w_ffn_gate, w_ffn_up, ffn_conv_w, ffn_conv_b, w_ffn_down, norm_final):
    depth = w_in.shape[0]
    nb, seq, d = x_prompt.shape
    db, dseq, _ = x_sample.shape
    hp = x_prompt.reshape(nb * seq, d)
    hs = x_sample.transpose(1, 0, 2).reshape(dseq * db, d)
    outs = {k: [] for k in ("conv_p", "gla_p", "ffn_p", "mk", "mv", "conv_s", "gla_s", "ffn_s")}
    nfinal = norm_final.reshape(1, d)
    yp = ys = None
    for l in range(depth):
        w = {
            "norm_mix": norm_mix[l].reshape(1, d),
            "w_in_t": w_in[l].T,
            "w_g1_t": jnp.pad(w_in[l].T[N_MAIN:], ((0, LANES - GLA_RANK), (0, 0))).astype(BF16),
            "w_g2": jnp.pad(w_gate2[l], ((0, LANES - GLA_RANK), (0, 0))).astype(BF16),
            "b_gate": b_gate[l].reshape(1, -1),
            "conv_w": conv_w[l],
            "gla_norm": gla_norm[l].reshape(1, -1),
            "w_out": w_out[l],
            "norm_x": norm_x[l].reshape(1, d),
            "w_xq": w_xq[l],
            "w_xo": w_xo[l],
            "w_fd": w_ffn_down[l],
            "norm_ffn": norm_ffn[l].reshape(1, d),
            "w_fg": w_ffn_gate[l],
            "w_fu": w_ffn_up[l],
            "ffn_conv_w": ffn_conv_w[l],
            "ffn_conv_b": ffn_conv_b[l].reshape(1, -1),
        }
        last = l == depth - 1
        gain_next = nfinal if last else None

        mem = mem_prompt.reshape(nb * MEM_LEN, d)
        nmem = norm_mem[l].reshape(1, d)
        mk, mk_cache = _mem_proj(mem, nmem, w_xk[l], 1024)
        mv, mv_cache = _mem_proj(mem, nmem, w_xv[l], 1024)
        act, h2, c1, s1, f1, wb = _layer(
            hp, w, None, nseq=nb, rows=seq, shift=1,
            prev_conv=jnp.zeros((nb, CONV_WIDTH - 1, D_CONV), F32),
            s0=jnp.zeros((nb, GLA_HEADS, GLA_DK, GLA_DV), F32),
            prev_ffn=jnp.zeros((nb, CONV_WIDTH - 1, D_FF), F32),
            mk=mk.reshape(nb, MEM_LEN, d), mv=mv.reshape(nb, MEM_LEN, d), time_major=False)
        assert last, "only the final layer's epilogue (final rmsnorm) is implemented"
        yp = _proj_res_norm([act], wb["w_fd"], h2, gain_next, tm=256, tk=D_FF, final=True)
        outs["conv_p"].append(c1)
        outs["gla_p"].append(s1)
        outs["ffn_p"].append(f1)
        outs["mk"].append(mk_cache)
        outs["mv"].append(mv_cache)

        def tmajor(c):
            return c.transpose(1, 0, 2).reshape(1, (CONV_WIDTH - 1) * db, c.shape[-1])

        act, h2, c2, s2, f2, _ = _layer(
            hs, w, wb, nseq=1, rows=dseq * db, shift=db,
            prev_conv=tmajor(cache_conv[l]), s0=state_gla[l], prev_ffn=tmajor(cache_ffn[l]),
            mk=cache_mem_k[l], mv=cache_mem_v[l], time_major=True)
        ys = _proj_res_norm([act], wb["w_fd"], h2, gain_next, tm=256, tk=D_FF, final=True)
        outs["conv_s"].append(c2.reshape(CONV_WIDTH - 1, db, D_CONV).transpose(1, 0, 2))
        outs["gla_s"].append(s2)
        outs["ffn_s"].append(f2.reshape(CONV_WIDTH - 1, db, D_FF).transpose(1, 0, 2))

    y_prompt = yp.reshape(nb, seq, d)
    y_sample = ys.reshape(dseq, db, d).transpose(1, 0, 2)
    st = lambda k: jnp.stack(outs[k])
    return (y_prompt, y_sample, st("conv_p"), st("gla_p"), st("ffn_p"), st("mk"), st("mv"),
            st("conv_s"), st("gla_s"), st("ffn_s"))
```

```python
import functools

import jax
import jax.numpy as jnp
from jax import lax
from jax.experimental import pallas as pl
from jax.experimental.pallas import tpu as pltpu

F32 = jnp.float32
BF16 = jnp.bfloat16

D_MODEL = 2048
EPS = 1e-6
CONV_WIDTH = 3
D_CONV = 1024
D_GLA = 1024
GLA_HEADS = 4
GLA_DV = 256
GLA_DK = 128
GLA_RANK = 16
GLA_TAU = 16.0
LOG2_E = 1.4426950408889634
GLA_CHUNK = 64
X_HEADS = 4
X_HD = 512
MEM_LEN = 256
D_FF = 5632
N_MAIN = 3 * D_CONV + 2 * GLA_HEADS * GLA_DK + 2 * GLA_HEADS * GLA_DV

LANES = 128
SUBLANES = 8
VMEM_LIMIT_BYTES = 56 * 1024 * 1024


def _params(*sem):
    return pltpu.CompilerParams(dimension_semantics=sem, vmem_limit_bytes=VMEM_LIMIT_BYTES)


def _dot(a, b):
    return jnp.dot(a, b, preferred_element_type=F32)


def _dot_nt(a, b):
    return lax.dot_general(a, b, (((1,), (1,)), ((), ())), preferred_element_type=F32)


def _dot_tn(a, b):
    return lax.dot_general(a, b, (((0,), (0,)), ((), ())), preferred_element_type=F32)


def _rms_rows(x, g):
    ms = jnp.mean(x * x, axis=-1, keepdims=True)
    return (x * lax.rsqrt(ms + EPS)) * g


def _row_chunk(rows, limit=256):
    for c in (256, 128, 64, 32, 16, 8):
        if c <= limit and rows % c == 0:
            return c
    return rows


NORM_ROWS = 128


def _norm_into(x_ref, g_ref, xn_ref):
    rows = x_ref.shape[0]
    ch = _row_chunk(rows)
    g = g_ref[...]

    def body(c, carry):
        r = pl.ds(pl.multiple_of(c * ch, ch), ch)
        xn_ref[r, :] = _rms_rows(x_ref[r, :], g).astype(xn_ref.dtype)
        return carry

    lax.fori_loop(0, rows // ch, body, 0)


def _norm_matmul_kernel(x_ref, g_ref, w_ref, o_ref, oc_ref, xn_ref):
    j = pl.program_id(1)

    @pl.when(j == 0)
    def _():
        _norm_into(x_ref, g_ref, xn_ref)

    o_ref[...] = _dot(xn_ref[...], w_ref[...].astype(BF16)).astype(o_ref.dtype)
    nchunk = X_HD // LANES
    pitch = nchunk * X_HEADS
    heads_per_tile = o_ref.shape[1] // X_HD
    for b in range(oc_ref.shape[0]):
        for hh in range(heads_per_tile):
            for c in range(nchunk):
                col = hh * X_HD + c * LANES
                row = c * X_HEADS + j * heads_per_tile + hh
                oc_ref[b, pl.ds(row, MEM_LEN, stride=pitch), :] = (
                    o_ref[b * MEM_LEN:(b + 1) * MEM_LEN, col:col + LANES])


def _norm_gate_kernel(x_ref, g_ref, wg1_ref, wg2_ref, bg_ref, xn_ref, lf_ref):
    _norm_into(x_ref, g_ref, xn_ref)
    g1 = _dot_nt(xn_ref[...], wg1_ref[...])
    z = _dot(g1.astype(BF16), wg2_ref[...]) + bg_ref[...]
    lf_ref[...] = (jnp.minimum(z, 0.0) - jnp.log1p(jnp.exp(-jnp.abs(z)))) * (LOG2_E / GLA_TAU)


def _mem_proj(mem, gain, w, tn):
    m, d = mem.shape
    n = w.shape[1]
    nb = m // MEM_LEN
    nchunk = X_HD // LANES
    rows = MEM_LEN * nchunk * X_HEADS
    flat, stored = pl.pallas_call(
        _norm_matmul_kernel,
        grid=(1, n // tn),
        in_specs=[pl.BlockSpec((m, d), lambda i, j: (0, 0)),
                  pl.BlockSpec((1, d), lambda i, j: (0, 0)),
                  pl.BlockSpec((d, tn), lambda i, j: (0, j))],
        out_specs=[pl.BlockSpec((m, tn), lambda i, j: (0, j)),
                   pl.BlockSpec((nb, rows, LANES), lambda i, j: (0, 0, 0))],
        out_shape=[jax.ShapeDtypeStruct((m, n), F32),
                   jax.ShapeDtypeStruct((nb, rows, LANES), F32)],
        scratch_shapes=[pltpu.VMEM((m, d), BF16)],
        compiler_params=_params("arbitrary", "arbitrary"),
        name="norm_matmul",
    )(mem, gain, w)
    cache = stored.reshape(nb, MEM_LEN, nchunk, X_HEADS, LANES).transpose(0, 1, 3, 2, 4)
    return flat, cache.reshape(nb, MEM_LEN, X_HEADS, X_HD)


def _norm_gate(x, gain, wg1, wg2, b_gate, tm):
    m, d = x.shape
    ng = wg2.shape[1]
    return pl.pallas_call(
        _norm_gate_kernel,
        grid=(m // tm,),
        in_specs=[pl.BlockSpec((tm, d), lambda i: (i, 0)),
                  pl.BlockSpec((1, d), lambda i: (0, 0)),
                  pl.BlockSpec((LANES, d), lambda i: (0, 0)),
                  pl.BlockSpec((LANES, ng), lambda i: (0, 0)),
                  pl.BlockSpec((1, ng), lambda i: (0, 0))],
        out_specs=[pl.BlockSpec((tm, d), lambda i: (i, 0)),
                   pl.BlockSpec((tm, ng), lambda i: (i, 0))],
        out_shape=[jax.ShapeDtypeStruct((m, d), BF16),
                   jax.ShapeDtypeStruct((m, ng), F32)],
        compiler_params=_params("arbitrary"),
        name="norm_gate",
    )(x, gain, wg1, wg2, b_gate)


def _cast_into(src_ref, dst_ref):
    rows = src_ref.shape[0]
    ch = _row_chunk(rows)

    def body(c, carry):
        r = pl.ds(pl.multiple_of(c * ch, ch), ch)
        dst_ref[r, :] = src_ref[r, :].astype(dst_ref.dtype)
        return carry

    lax.fori_loop(0, rows // ch, body, 0)


def _matmul_kernel(a_ref, w_ref, o_ref, wb_ref):
    @pl.when(pl.program_id(0) == 0)
    def _():
        _cast_into(w_ref, wb_ref)

    o_ref[...] = _dot(a_ref[...], wb_ref[...]).astype(o_ref.dtype)


def _matmul(a, w, out_dtype, tm):
    m, k = a.shape
    n = w.shape[1]
    return pl.pallas_call(
        _matmul_kernel,
        grid=(m // tm,),
        in_specs=[pl.BlockSpec((tm, k), lambda i: (i, 0)),
                  pl.BlockSpec((k, n), lambda i: (0, 0), pipeline_mode=pl.Buffered(1))],
        out_specs=pl.BlockSpec((tm, n), lambda i: (i, 0)),
        out_shape=jax.ShapeDtypeStruct((m, n), out_dtype),
        scratch_shapes=[pltpu.VMEM((k, n), BF16)],
        compiler_params=_params("arbitrary"),
        name="matmul",
    )(a, w)


def _col_matmul_kernel(a_ref, wt_ref, o_ref, wb_ref):
    @pl.when(pl.program_id(1) == 0)
    def _():
        _cast_into(wt_ref, wb_ref)

    o_ref[...] = _dot_nt(a_ref[...], wb_ref[...]).astype(o_ref.dtype)


def _col_matmul(a, wt, col0, ncols, out_dtype, tm, tn):
    m, k = a.shape
    j0 = col0 // tn
    return pl.pallas_call(
        _col_matmul_kernel,
        grid=(ncols // tn, m // tm),
        in_specs=[pl.BlockSpec((tm, k), lambda j, i: (i, 0)),
                  pl.BlockSpec((tn, k), lambda j, i: (j0 + j, 0))],
        out_specs=pl.BlockSpec((tm, tn), lambda j, i: (i, j)),
        out_shape=jax.ShapeDtypeStruct((m, ncols), out_dtype),
        scratch_shapes=[pltpu.VMEM((tn, k), BF16)],
        compiler_params=_params("arbitrary", "arbitrary"),
        name="col_matmul",
    )(a, wt)


def _halo_rows(shift):
    return max(2 * shift, SUBLANES)


def _conv_buf(rows, tn, shift):
    return pltpu.VMEM((tn // LANES, _halo_rows(shift) + rows, LANES), F32)


def _slab(s):
    return slice(s * LANES, (s + 1) * LANES)


def _conv_stage(buf_ref, row0, x):
    for s in range(buf_ref.shape[0]):
        buf_ref[s, row0:row0 + x.shape[0], :] = x[:, _slab(s)]


def _conv_taps(buf_ref, s, r0, rows, shift):
    halo = _halo_rows(shift)

    def back(steps):
        start = r0 + (halo - steps * shift)
        if (steps * shift) % SUBLANES == 0:
            return pl.ds(start if isinstance(start, int) else pl.multiple_of(start, SUBLANES), rows)
        return pl.ds(start, rows, stride=1)

    return buf_ref[s, back(2), :], buf_ref[s, back(1), :], buf_ref[s, back(0), :]


def _conv_last(buf_ref, new_ref, rows, shift):
    halo = _halo_rows(shift)
    for s in range(buf_ref.shape[0]):
        new_ref[0, :, _slab(s)] = buf_ref[s, halo + rows - 2 * shift:halo + rows, :]


def _conv_proj_kernel(xn_ref, wbg_ref, wcg_ref, wvc_ref, prev_ref, cw_ref, o_ref, new_ref,
                      bgb_ref, cgb_ref, vcb_ref, buf_ref, *, shift, tiles_per_seq):
    rows = xn_ref.shape[0]
    halo = _halo_rows(shift)
    i = pl.program_id(1)

    @pl.when(i == 0)
    def _():
        _cast_into(wbg_ref, bgb_ref)
        _cast_into(wcg_ref, cgb_ref)
        _cast_into(wvc_ref, vcb_ref)

    def from_cache():
        _conv_stage(buf_ref, halo - 2 * shift, prev_ref[0])

    if tiles_per_seq == 1:
        from_cache()
    else:
        pl.when(i % tiles_per_seq == 0)(from_cache)

        @pl.when(i % tiles_per_seq != 0)
        def _():
            for s in range(buf_ref.shape[0]):
                buf_ref[s, halo - 2 * shift:halo, :] = buf_ref[s, halo + rows - 2 * shift:halo + rows, :]

    xn = xn_ref[...]
    _conv_stage(buf_ref, halo, _dot_nt(xn, cgb_ref[...]) * _dot_nt(xn, vcb_ref[...]))
    bg = _dot_nt(xn, bgb_ref[...])
    for s in range(buf_ref.shape[0]):
        u2, u1, u0 = _conv_taps(buf_ref, s, 0, rows, shift)
        y = cw_ref[0:1, _slab(s)] * u2 + cw_ref[1:2, _slab(s)] * u1 + cw_ref[2:3, _slab(s)] * u0
        o_ref[:, _slab(s)] = (bg[:, _slab(s)] * y).astype(o_ref.dtype)
    _conv_last(buf_ref, new_ref, rows, shift)


def _conv_proj(xn, w_in_t, prev, conv_w, *, nseq, shift, tm, tn):
    m, d = xn.shape
    nj = D_CONV // tn
    tiles_per_seq = m // (nseq * tm)
    kern = functools.partial(_conv_proj_kernel, shift=shift, tiles_per_seq=tiles_per_seq)
    state = pl.BlockSpec((1, 2 * shift, tn), lambda j, i: (i // tiles_per_seq, 0, j))
    wb = pltpu.VMEM((tn, d), BF16)
    return pl.pallas_call(
        kern,
        grid=(nj, m // tm),
        in_specs=[pl.BlockSpec((tm, d), lambda j, i: (i, 0)),
                  pl.BlockSpec((tn, d), lambda j, i: (j, 0)),
                  pl.BlockSpec((tn, d), lambda j, i: (j + nj, 0)),
                  pl.BlockSpec((tn, d), lambda j, i: (j + 2 * nj, 0)),
                  state,
                  pl.BlockSpec((CONV_WIDTH, tn), lambda j, i: (0, j))],
        out_specs=[pl.BlockSpec((tm, tn), lambda j, i: (i, j)),
                   state],
        out_shape=[jax.ShapeDtypeStruct((m, D_CONV), BF16),
                   jax.ShapeDtypeStruct((nseq, 2 * shift, D_CONV), F32)],
        scratch_shapes=[wb, wb, wb, _conv_buf(tm, tn, shift)],
        compiler_params=_params("arbitrary", "arbitrary"),
        name="conv_proj",
    )(xn, w_in_t, w_in_t, w_in_t, prev, conv_w)


def _cumsum_rows(g):
    c = g.shape[0]
    row = lax.broadcasted_iota(jnp.int32, g.shape, 0)
    x = g
    s = 1
    while s < c:
        x = x + jnp.where(row >= s, pltpu.roll(x, s, 0), 0.0)
        s *= 2
    return x


def _bcast_block_row(x, s, k):
    c, lanes = x.shape
    if s == c:
        return jnp.broadcast_to(x[k:k + 1, :], x.shape)
    if s >= SUBLANES:
        y = x.reshape(c // s, s, lanes)
        return jnp.broadcast_to(y[:, k:k + 1, :], y.shape).reshape(c, lanes)
    y = x.reshape(c // SUBLANES, SUBLANES, lanes)
    sub = lax.broadcasted_iota(jnp.int32, y.shape, 1)
    out = None
    for blk in range(SUBLANES // s):
        src = jnp.broadcast_to(y[:, blk * s + k:blk * s + k + 1, :], y.shape)
        out = src if out is None else jnp.where(sub >= blk * s, src, out)
    return out.reshape(c, lanes)


def _gla_pair_masks(c):
    ri = lax.broadcasted_iota(jnp.int32, (c, c), 0)
    ci = lax.broadcasted_iota(jnp.int32, (c, c), 1)
    diff_bits = ri ^ ci
    masks = [diff_bits == 0]
    level = 0
    while (1 << level) < c:
        masks.append(((diff_bits >> level) == 1) & (((ri >> level) & 1) == 1))
        level += 1
    return masks


def _gla_chunk(q, k, v, g, s_prev, masks):
    c = q.shape[0]
    cum = _cumsum_rows(g)
    a = jnp.where(masks[0], _dot_nt(q.astype(BF16), k.astype(BF16)), 0.0)
    for level in range(len(masks) - 1):
        half = 1 << level
        ref = _bcast_block_row(cum, 2 * half, half - 1)
        d = cum - ref
        up = jnp.minimum(d, 0.0)
        qe = q * jnp.exp2(up)
        ke = k * jnp.exp2(up - d)
        a = a + jnp.where(masks[1 + level], _dot_nt(qe.astype(BF16), ke.astype(BF16)), 0.0)
    o = _dot(a.astype(BF16), v.astype(BF16)) + _dot((q * jnp.exp2(cum)).astype(BF16), s_prev.astype(BF16))
    last = cum[c - 1:c, :]
    kd = k * jnp.exp2(last - cum)
    dk = last.shape[1]
    decay_t = jnp.transpose(jnp.broadcast_to(jnp.exp2(last), (dk, dk)))
    decayed = jnp.concatenate([decay_t * s_prev[:, i:i + dk] for i in range(0, s_prev.shape[1], dk)], axis=1)
    s_new = decayed + _dot_tn(kd.astype(BF16), v.astype(BF16))
    return o, s_new


def _gla_kernel(q_ref, k_ref, v_ref, r_ref, g_ref, s0_ref, gn_ref, *refs, chunk, single_chunk, group, n_cast):
    cast_src, (o_ref, sn_ref), cast_dst = refs[:n_cast], refs[n_cast:n_cast + 2], refs[n_cast + 2:]
    bb, rows = q_ref.shape[0], q_ref.shape[1]
    nchunk = rows // chunk
    state_in = s0_ref if single_chunk else sn_ref

    if not single_chunk:
        @pl.when(pl.program_id(1) == 0)
        def _():
            sn_ref[...] = s0_ref[...]

    masks = _gla_pair_masks(chunk)

    def one(b, r):
        for h in range(GLA_HEADS):
            kc = slice(h * GLA_DK, (h + 1) * GLA_DK)
            vc = slice(h * GLA_DV, (h + 1) * GLA_DV)
            q = q_ref[b, r, kc] * (GLA_DK ** -0.5)
            o, s_new = _gla_chunk(q, k_ref[b, r, kc], v_ref[b, r, vc], g_ref[b, r, kc], state_in[b, h], masks)
            sn_ref[b, h] = s_new
            rr = r_ref[b, r, vc]
            o_ref[b, r, vc] = (_rms_rows(o, gn_ref[:, vc]) * (rr * jax.nn.sigmoid(rr))).astype(o_ref.dtype)

    def body(n, carry):
        r = pl.ds(pl.multiple_of((n % nchunk) * chunk, chunk), chunk)
        for u in range(group):
            one((n // nchunk) * group + u, r)
        return carry

    lax.fori_loop(0, (bb // group) * nchunk, body, 0)
    for src, dst in zip(cast_src, cast_dst):
        _cast_into(src, dst)


def _gla(p, logf, s0, gla_norm, *, q_blk, k_blk, v_blk, r_blk, bb, rows, chunk, group, cast=()):
    nb, t, _ = p.shape
    nk, nv = GLA_HEADS * GLA_DK, GLA_HEADS * GLA_DV
    nt = t // rows
    nsteps = (nb // bb) * nt
    kern = functools.partial(_gla_kernel, chunk=chunk, single_chunk=(t == chunk), group=group, n_cast=len(cast))
    state_spec = pl.BlockSpec((bb, GLA_HEADS, GLA_DK, GLA_DV), lambda b, c: (b, 0, 0, 0))
    cast_specs = [pl.BlockSpec((w.shape[0] // nsteps, w.shape[1]), lambda b, c: (b * nt + c, 0)) for w in cast]
    return pl.pallas_call(
        kern,
        grid=(nb // bb, nt),
        in_specs=[pl.BlockSpec((bb, rows, nk), lambda b, c: (b, c, q_blk)),
                  pl.BlockSpec((bb, rows, nk), lambda b, c: (b, c, k_blk)),
                  pl.BlockSpec((bb, rows, nv), lambda b, c: (b, c, v_blk)),
                  pl.BlockSpec((bb, rows, nv), lambda b, c: (b, c, r_blk)),
                  pl.BlockSpec((bb, rows, nk), lambda b, c: (b, c, 0)),
                  state_spec,
                  pl.BlockSpec((1, nv), lambda b, c: (0, 0))] + cast_specs,
        out_specs=[pl.BlockSpec((bb, rows, nv), lambda b, c: (b, c, 0)),
                   state_spec] + cast_specs,
        out_shape=[jax.ShapeDtypeStruct((nb, t, D_GLA), BF16),
                   jax.ShapeDtypeStruct((nb, GLA_HEADS, GLA_DK, GLA_DV), F32)]
                  + [jax.ShapeDtypeStruct(w.shape, BF16) for w in cast],
        compiler_params=_params("arbitrary", "arbitrary"),
        name="gla",
    )(p, p, p, p, logf, s0, gla_norm, *cast)


def _proj_res_norm_kernel(*refs, n_a, nk, final):
    a_refs, (w_ref, res_ref, g_ref) = refs[:n_a], refs[n_a:n_a + 3]
    out_refs, acc_ref = refs[n_a + 3:-1], refs[-1]
    k = pl.program_id(1)
    part, r0 = None, 0
    for a_ref in a_refs:
        kw = a_ref.shape[1]
        term = _dot(a_ref[...], w_ref[r0:r0 + kw, :])
        part = term if part is None else part + term
        r0 += kw

    if nk == 1:
        ch = _row_chunk(part.shape[0], NORM_ROWS)
        for c0 in range(0, part.shape[0], ch):
            h = res_ref[c0:c0 + ch, :] + part[c0:c0 + ch, :]
            hn = _rms_rows(h, g_ref[...])
            if final:
                out_refs[0][c0:c0 + ch, :] = hn
            else:
                out_refs[0][c0:c0 + ch, :] = h
                out_refs[1][c0:c0 + ch, :] = hn.astype(out_refs[1].dtype)
        return

    @pl.when(k == 0)
    def _():
        acc_ref[...] = part

    if nk > 1:
        @pl.when(k > 0)
        def _():
            acc_ref[...] += part

    @pl.when(k == nk - 1)
    def _():
        rows = acc_ref.shape[0]
        ch = _row_chunk(rows, NORM_ROWS)
        g = g_ref[...]

        def body(c, carry):
            r = pl.ds(pl.multiple_of(c * ch, ch), ch)
            h = res_ref[r, :] + acc_ref[r, :]
            hn = _rms_rows(h, g)
            if final:
                out_refs[0][r, :] = hn
            else:
                out_refs[0][r, :] = h
                out_refs[1][r, :] = hn.astype(out_refs[1].dtype)
            return carry

        lax.fori_loop(0, rows // ch, body, 0)


def _proj_res_norm(a_list, w, res, gain, *, tm, tk, final):
    m = a_list[0].shape[0]
    kdim, d = w.shape
    nk = kdim // tk
    assert len(a_list) == 1 or nk == 1
    kern = functools.partial(_proj_res_norm_kernel, n_a=len(a_list), nk=nk, final=final)
    a_specs = ([pl.BlockSpec((tm, tk), lambda i, k: (i, k))] if len(a_list) == 1 else
               [pl.BlockSpec((tm, a.shape[1]), lambda i, k: (i, 0)) for a in a_list])
    row_spec = pl.BlockSpec((tm, d), lambda i, k: (i, 0))
    if final:
        out_specs = row_spec
        out_shape = jax.ShapeDtypeStruct((m, d), F32)
    else:
        out_specs = [row_spec, row_spec]
        out_shape = [jax.ShapeDtypeStruct((m, d), F32), jax.ShapeDtypeStruct((m, d), BF16)]
    return pl.pallas_call(
        kern,
        grid=(m // tm, nk),
        in_specs=a_specs + [pl.BlockSpec((tk, d), lambda i, k: (k, 0),
                                         pipeline_mode=pl.Buffered(1 if nk == 1 else 2)),
                            row_spec,
                            pl.BlockSpec((1, d), lambda i, k: (0, 0))],
        out_specs=out_specs,
        out_shape=out_shape,
        scratch_shapes=[pltpu.VMEM((tm, d), F32)],
        compiler_params=_params("arbitrary", "arbitrary"),
        name="proj_res_norm",
    )(*a_list, w, res, gain)


def _xattn_kernel(hn_ref, wq_ref, k_ref, v_ref, o_ref, wb_ref):
    @pl.when(pl.program_id(1) == 0)
    def _():
        _cast_into(wq_ref, wb_ref)

    q = _dot(hn_ref[0], wb_ref[...]).astype(BF16)
    p = _softmax_rows(_dot_nt(q, k_ref[0].astype(BF16)) * (X_HD ** -0.5))
    o_ref[0] = _dot(p.astype(BF16), v_ref[0].astype(BF16)).astype(o_ref.dtype)


def _xattn(hn, w_xq, mk, mv):
    nb, t, d = hn.shape
    return pl.pallas_call(
        _xattn_kernel,
        grid=(X_HEADS, nb),
        in_specs=[pl.BlockSpec((1, t, d), lambda h, b: (b, 0, 0)),
                  pl.BlockSpec((d, X_HD), lambda h, b: (0, h)),
                  pl.BlockSpec((1, MEM_LEN, X_HD), lambda h, b: (b, 0, h)),
                  pl.BlockSpec((1, MEM_LEN, X_HD), lambda h, b: (b, 0, h))],
        out_specs=pl.BlockSpec((1, t, X_HD), lambda h, b: (b, 0, h)),
        out_shape=jax.ShapeDtypeStruct((nb, t, d), BF16),
        scratch_shapes=[pltpu.VMEM((d, X_HD), BF16)],
        compiler_params=_params("arbitrary", "arbitrary"),
        name="xattn",
    )(hn, w_xq, mk, mv)


def _softmax_rows(s):
    s = s - jnp.max(s, axis=-1, keepdims=True)
    e = jnp.exp(s)
    return e / jnp.sum(e, axis=-1, keepdims=True)


def _xattn_cache_kernel(q_ref, k_ref, v_ref, o_ref):
    bb, tq = q_ref.shape[0], q_ref.shape[1]
    nchunk = X_HD // LANES
    pitch = nchunk * X_HEADS

    def gather(ref, b, h):
        parts = [ref[b, pl.ds(c * X_HEADS + h, MEM_LEN, stride=pitch), :] for c in range(nchunk)]
        return jnp.concatenate(parts, axis=1).astype(BF16)

    pairs = [(b, h) for b in range(bb) for h in range(X_HEADS)]
    scores = [_dot_nt(q_ref[b, :, h * X_HD:(h + 1) * X_HD].astype(BF16), gather(k_ref, b, h)) for b, h in pairs]
    p = _softmax_rows(jnp.concatenate(scores, axis=0) * (X_HD ** -0.5)).astype(BF16)
    for n, (b, h) in enumerate(pairs):
        o_ref[b, :, h * X_HD:(h + 1) * X_HD] = _dot(p[n * tq:(n + 1) * tq], gather(v_ref, b, h)).astype(o_ref.dtype)


def _xattn_cache(q, ck, cv, *, bb):
    nb, tq, d = q.shape
    nchunk = X_HD // LANES

    def stored_order(c):
        c = c.reshape(nb, MEM_LEN, X_HEADS, nchunk, LANES).transpose(0, 1, 3, 2, 4)
        return c.reshape(nb, MEM_LEN * nchunk * X_HEADS, LANES)

    rows = MEM_LEN * nchunk * X_HEADS
    return pl.pallas_call(
        _xattn_cache_kernel,
        grid=(nb // bb,),
        in_specs=[pl.BlockSpec((bb, tq, d), lambda b: (b, 0, 0)),
                  pl.BlockSpec((bb, rows, LANES), lambda b: (b, 0, 0)),
                  pl.BlockSpec((bb, rows, LANES), lambda b: (b, 0, 0))],
        out_specs=pl.BlockSpec((bb, tq, d), lambda b: (b, 0, 0)),
        out_shape=jax.ShapeDtypeStruct((nb, tq, d), BF16),
        compiler_params=_params("arbitrary"),
        name="xattn_cache",
    )(q, stored_order(ck), stored_order(cv))


def _ffn_up_kernel(hn_ref, wg_ref, wu_ref, cw_ref, cb_ref, prev_ref, o_ref, new_ref, buf_ref, up_ref,
                   wgb_ref, wub_ref, *, shift):
    rows = hn_ref.shape[0]
    halo = _halo_rows(shift)
    ch = _row_chunk(rows)

    @pl.when(pl.program_id(1) == 0)
    def _():
        _cast_into(wg_ref, wgb_ref)
        _cast_into(wu_ref, wub_ref)

    _conv_stage(buf_ref, halo - 2 * shift, prev_ref[0])
    _conv_stage(buf_ref, halo, _dot(hn_ref[...], wgb_ref[...]))
    up_ref[...] = _dot(hn_ref[...], wub_ref[...])

    def act_body(c, carry):
        r0 = pl.multiple_of(c * ch, ch)
        r = pl.ds(r0, ch)
        for s in range(buf_ref.shape[0]):
            g2, g1, g0 = _conv_taps(buf_ref, s, r0, ch, shift)
            gc = (cw_ref[0:1, _slab(s)] * g2 + cw_ref[1:2, _slab(s)] * g1 + cw_ref[2:3, _slab(s)] * g0
                  + cb_ref[:, _slab(s)])
            o_ref[r, _slab(s)] = ((gc * jax.nn.sigmoid(gc)) * up_ref[r, _slab(s)]).astype(o_ref.dtype)
        return carry

    lax.fori_loop(0, rows // ch, act_body, 0)
    _conv_last(buf_ref, new_ref, rows, shift)


def _ffn_up(hn, wg, wu, cw, cb, prev, nseq, rows, shift, tn):
    d = hn.shape[1]
    kern = functools.partial(_ffn_up_kernel, shift=shift)
    return pl.pallas_call(
        kern,
        grid=(D_FF // tn, nseq),
        in_specs=[pl.BlockSpec((rows, d), lambda j, b: (b, 0)),
                  pl.BlockSpec((d, tn), lambda j, b: (0, j)),
                  pl.BlockSpec((d, tn), lambda j, b: (0, j)),
                  pl.BlockSpec((CONV_WIDTH, tn), lambda j, b: (0, j)),
                  pl.BlockSpec((1, tn), lambda j, b: (0, j)),
                  pl.BlockSpec((1, 2 * shift, tn), lambda j, b: (b, 0, j))],
        out_specs=[pl.BlockSpec((rows, tn), lambda j, b: (b, j)),
                   pl.BlockSpec((1, 2 * shift, tn), lambda j, b: (b, 0, j))],
        out_shape=[jax.ShapeDtypeStruct((nseq * rows, D_FF), BF16),
                   jax.ShapeDtypeStruct((nseq, 2 * shift, D_FF), F32)],
        scratch_shapes=[_conv_buf(rows, tn, shift), pltpu.VMEM((rows, tn), F32),
                        pltpu.VMEM((d, tn), BF16), pltpu.VMEM((d, tn), BF16)],
        compiler_params=_params("arbitrary", "arbitrary"),
        name="ffn_up",
    )(hn, wg, wu, cw, cb, prev)


def _layer(x, w, wb, *, nseq, rows, shift, prev_conv, s0, prev_ffn, mk, mv, time_major):
    m = x.shape[0]
    tm = min(m, 1024)
    xn, logf = _norm_gate(x, w["norm_mix"], w["w_g1_t"], w["w_g2"], w["b_gate"], tm)
    conv_out, conv_new = _conv_proj(xn, w["w_in_t"], prev_conv, w["conv_w"], nseq=nseq, shift=shift, tm=tm, tn=512)
    n_gla = N_MAIN - 3 * D_CONV
    p = _col_matmul(xn, w["w_in_t"], 3 * D_CONV, n_gla, F32, tm, 1024)

    if time_major:
        nt = m // shift
        pad = ((0, 0), (0, SUBLANES - nt), (0, 0))
        pg = jnp.pad(p.reshape(nt, shift, n_gla).transpose(1, 0, 2), pad)
        lg = jnp.pad(logf.reshape(nt, shift, -1).transpose(1, 0, 2), pad)
        gla_args = dict(bb=8, rows=SUBLANES, chunk=SUBLANES, group=4)
    else:
        pg, lg = p.reshape(nseq, rows, n_gla), logf.reshape(nseq, rows, -1)
        gla_args = dict(bb=4, rows=128, chunk=GLA_CHUNK, group=4)
    cast = () if wb is not None else (w["w_out"], w["w_xo"], w["w_fd"])
    o, s_new, *made = _gla(pg, lg, s0, w["gla_norm"], q_blk=0, k_blk=1, v_blk=1, r_blk=2, cast=cast, **gla_args)
    if wb is None:
        wb = dict(zip(("w_out", "w_xo", "w_fd"), made))
    if time_major:
        gla_out = o[:, :nt].transpose(1, 0, 2).reshape(m, D_GLA)
    else:
        gla_out = o.reshape(m, D_GLA)

    tm2 = min(m, 512)
    h, hn = _proj_res_norm([conv_out, gla_out], wb["w_out"], x, w["norm_x"], tm=tm2, tk=D_MODEL, final=False)
    if time_major:
        nt = m // shift
        qx = _matmul(hn, w["w_xq"], BF16, tm)
        qb = jnp.pad(qx.astype(F32).reshape(nt, shift, D_MODEL).transpose(1, 0, 2),
                     ((0, 0), (0, SUBLANES - nt), (0, 0)))
        ob = _xattn_cache(qb, mk, mv, bb=4)
        attn = ob[:, :nt].transpose(1, 0, 2).reshape(m, D_MODEL)
    else:
        attn = _xattn(hn.reshape(nseq, rows, D_MODEL), w["w_xq"], mk, mv).reshape(m, D_MODEL)

    h2, hn2 = _proj_res_norm([attn], wb["w_xo"], h, w["norm_ffn"], tm=tm2, tk=D_MODEL, final=False)
    act, ffn_new = _ffn_up(hn2, w["w_fg"], w["w_fu"], w["ffn_conv_w"], w["ffn_conv_b"], prev_ffn,
                           nseq, rows, shift, 512)
    return act, h2, conv_new, s_new, ffn_new, wb


def kernel(x_prompt, x_sample, mem_prompt, cache_conv, state_gla, cache_ffn, cache_mem_k, cache_mem_v,
           norm_mix, w_in, conv_w, w_gate2, b_gate, gla_norm, w_out, norm_x, norm_mem, w_xq, w_xk, w_xv,
           w_xo, norm_ffn, w_ffn_gate, w_ffn_up, ffn_conv_w, ffn_conv_b, w_ffn_down, norm_final):
    depth = w_in.shape[0]
    nb, seq, d = x_prompt.shape
    db, dseq, _ = x_sample.shape
    hp = x_prompt.reshape(nb * seq, d)
    hs = x_sample.transpose(1, 0, 2).reshape(dseq * db, d)
    outs = {k: [] for k in ("conv_p", "gla_p", "ffn_p", "mk", "mv", "conv_s", "gla_s", "ffn_s")}
    nfinal = norm_final.reshape(1, d)
    yp = ys = None
    for l in range(depth):
        w = {
            "norm_mix": norm_mix[l].reshape(1, d),
            "w_in_t": w_in[l].T,
            "w_g1_t": jnp.pad(w_in[l].T[N_MAIN:], ((0, LANES - GLA_RANK), (0, 0))).astype(BF16),
            "w_g2": jnp.pad(w_gate2[l], ((0, LANES - GLA_RANK), (0, 0))).astype(BF16),
            "b_gate": b_gate[l].reshape(1, -1),
            "conv_w": conv_w[l],
            "gla_norm": gla_norm[l].reshape(1, -1),
            "w_out": w_out[l],
            "norm_x": norm_x[l].reshape(1, d),
            "w_xq": w_xq[l],
            "w_xo": w_xo[l],
            "w_fd": w_ffn_down[l],
            "norm_ffn": norm_ffn[l].reshape(1, d),
            "w_fg": w_ffn_gate[l],
            "w_fu": w_ffn_up[l],
            "ffn_conv_w": ffn_conv_w[l],
            "ffn_conv_b": ffn_conv_b[l].reshape(1, -1),
        }
        last = l == depth - 1
        gain_next = nfinal if last else None

        mem = mem_prompt.reshape(nb * MEM_LEN, d)
        nmem = norm_mem[l].reshape(1, d)
        mk, mk_cache = _mem_proj(mem, nmem, w_xk[l], 1024)
        mv, mv_cache = _mem_proj(mem, nmem, w_xv[l], 1024)
        act, h2, c1, s1, f1, wb = _layer(
            hp, w, None, nseq=nb, rows=seq, shift=1,
            prev_conv=jnp.zeros((nb, CONV_WIDTH - 1, D_CONV), F32),
            s0=jnp.zeros((nb, GLA_HEADS, GLA_DK, GLA_DV), F32),
            prev_ffn=jnp.zeros((nb, CONV_WIDTH - 1, D_FF), F32),
            mk=mk.reshape(nb, MEM_LEN, d), mv=mv.reshape(nb, MEM_LEN, d), time_major=False)
        assert last, "only the final layer's epilogue (final rmsnorm) is implemented"
        yp = _proj_res_norm([act], wb["w_fd"], h2, gain_next, tm=256, tk=D_FF, final=True)
        outs["conv_p"].append(c1)
        outs["gla_p"].append(s1)
        outs["ffn_p"].append(f1)
        outs["mk"].append(mk_cache)
        outs["mv"].append(mv_cache)

        def tmajor(c):
            return c.transpose(1, 0, 2).reshape(1, (CONV_WIDTH - 1) * db, c.shape[-1])

        act, h2, c2, s2, f2, _ = _layer(
            hs, w, wb, nseq=1, rows=dseq * db, shift=db,
            prev_conv=tmajor(cache_conv[l]), s0=state_gla[l], prev_ffn=tmajor(cache_ffn[l]),
            mk=cache_mem_k[l], mv=cache_mem_v[l], time_major=True)
        ys = _proj_res_norm([act], wb["w_fd"], h2, gain_next, tm=256, tk=D_FF, final=True)
        outs["conv_s"].append(c2.reshape(CONV_WIDTH - 1, db, D_CONV).transpose(1, 0, 2))
        outs["gla_s"].append(s2)
        outs["ffn_s"].append(f2.reshape(CONV_WIDTH - 1, db, D_FF).transpose(1, 0, 2))

    y_prompt = yp.reshape(nb, seq, d)
    y_sample = ys.reshape(dseq, db, d).transpose(1, 0, 2)
    st = lambda k: jnp.stack(outs[k])
    return (y_prompt, y_sample, st("conv_p"), st("gla_p"), st("ffn_p"), st("mk"), st("mv"),
            st("conv_s"), st("gla_s"), st("ffn_s"))
```

```python
import functools

import jax
import jax.numpy as jnp
from jax import lax
from jax.experimental import pallas as pl
from jax.experimental.pallas import tpu as pltpu

F32 = jnp.float32
BF16 = jnp.bfloat16

D_MODEL = 2048
EPS = 1e-6
CONV_WIDTH = 3
D_CONV = 1024
D_GLA = 1024
GLA_HEADS = 4
GLA_DV = 256
GLA_DK = 128
GLA_RANK = 16
GLA_TAU = 16.0
LOG2_E = 1.4426950408889634
GLA_CHUNK = 64
X_HEADS = 4
X_HD = 512
MEM_LEN = 256
D_FF = 5632
N_MAIN = 3 * D_CONV + 2 * GLA_HEADS * GLA_DK + 2 * GLA_HEADS * GLA_DV

LANES = 128
SUBLANES = 8
VMEM_LIMIT_BYTES = 56 * 1024 * 1024


def _params(*sem):
    return pltpu.CompilerParams(dimension_semantics=sem, vmem_limit_bytes=VMEM_LIMIT_BYTES)


def _dot(a, b):
    return jnp.dot(a, b, preferred_element_type=F32)


def _dot_nt(a, b):
    return lax.dot_general(a, b, (((1,), (1,)), ((), ())), preferred_element_type=F32)


def _dot_tn(a, b):
    return lax.dot_general(a, b, (((0,), (0,)), ((), ())), preferred_element_type=F32)


def _rms_rows(x, g):
    ms = jnp.mean(x * x, axis=-1, keepdims=True)
    return (x * lax.rsqrt(ms + EPS)) * g


def _row_chunk(rows, limit=256):
    for c in (256, 128, 64, 32, 16, 8):
        if c <= limit and rows % c == 0:
            return c
    return rows


NORM_ROWS = 128
FFN_ROW_BLOCK = 512


def _norm_into(x_ref, g_ref, xn_ref):
    rows = x_ref.shape[0]
    ch = _row_chunk(rows)
    g = g_ref[...]

    def body(c, carry):
        r = pl.ds(pl.multiple_of(c * ch, ch), ch)
        xn_ref[r, :] = _rms_rows(x_ref[r, :], g).astype(xn_ref.dtype)
        return carry

    lax.fori_loop(0, rows // ch, body, 0)


def _norm_matmul_kernel(x_ref, g_ref, w_ref, o_ref, oc_ref, xn_ref):
    j = pl.program_id(1)

    @pl.when(j == 0)
    def _():
        _norm_into(x_ref, g_ref, xn_ref)

    o_ref[...] = _dot(xn_ref[...], w_ref[...].astype(BF16)).astype(o_ref.dtype)
    nchunk = X_HD // LANES
    pitch = nchunk * X_HEADS
    heads_per_tile = o_ref.shape[1] // X_HD
    for b in range(oc_ref.shape[0]):
        for hh in range(heads_per_tile):
            for c in range(nchunk):
                col = hh * X_HD + c * LANES
                row = c * X_HEADS + j * heads_per_tile + hh
                oc_ref[b, pl.ds(row, MEM_LEN, stride=pitch), :] = (
                    o_ref[b * MEM_LEN:(b + 1) * MEM_LEN, col:col + LANES])


def _norm_gate_kernel(x_ref, g_ref, wg1_ref, wg2_ref, bg_ref, xn_ref, lf_ref):
    _norm_into(x_ref, g_ref, xn_ref)
    g1 = _dot_nt(xn_ref[...], wg1_ref[...])
    z = _dot(g1.astype(BF16), wg2_ref[...]) + bg_ref[...]
    lf_ref[...] = (jnp.minimum(z, 0.0) - jnp.log1p(jnp.exp(-jnp.abs(z)))) * (LOG2_E / GLA_TAU)


def _mem_proj(mem, gain, w, tn):
    m, d = mem.shape
    n = w.shape[1]
    nb = m // MEM_LEN
    nchunk = X_HD // LANES
    rows = MEM_LEN * nchunk * X_HEADS
    flat, stored = pl.pallas_call(
        _norm_matmul_kernel,
        grid=(1, n // tn),
        in_specs=[pl.BlockSpec((m, d), lambda i, j: (0, 0)),
                  pl.BlockSpec((1, d), lambda i, j: (0, 0)),
                  pl.BlockSpec((d, tn), lambda i, j: (0, j))],
        out_specs=[pl.BlockSpec((m, tn), lambda i, j: (0, j)),
                   pl.BlockSpec((nb, rows, LANES), lambda i, j: (0, 0, 0))],
        out_shape=[jax.ShapeDtypeStruct((m, n), F32),
                   jax.ShapeDtypeStruct((nb, rows, LANES), F32)],
        scratch_shapes=[pltpu.VMEM((m, d), BF16)],
        compiler_params=_params("arbitrary", "arbitrary"),
        name="norm_matmul",
    )(mem, gain, w)
    cache = stored.reshape(nb, MEM_LEN, nchunk, X_HEADS, LANES).transpose(0, 1, 3, 2, 4)
    return flat, cache.reshape(nb, MEM_LEN, X_HEADS, X_HD)


def _norm_gate(x, gain, wg1, wg2, b_gate, tm):
    m, d = x.shape
    ng = wg2.shape[1]
    return pl.pallas_call(
        _norm_gate_kernel,
        grid=(m // tm,),
        in_specs=[pl.BlockSpec((tm, d), lambda i: (i, 0)),
                  pl.BlockSpec((1, d), lambda i: (0, 0)),
                  pl.BlockSpec((LANES, d), lambda i: (0, 0)),
                  pl.BlockSpec((LANES, ng), lambda i: (0, 0)),
                  pl.BlockSpec((1, ng), lambda i: (0, 0))],
        out_specs=[pl.BlockSpec((tm, d), lambda i: (i, 0)),
                   pl.BlockSpec((tm, ng), lambda i: (i, 0))],
        out_shape=[jax.ShapeDtypeStruct((m, d), BF16),
                   jax.ShapeDtypeStruct((m, ng), F32)],
        compiler_params=_params("arbitrary"),
        name="norm_gate",
    )(x, gain, wg1, wg2, b_gate)


def _cast_into(src_ref, dst_ref):
    rows = src_ref.shape[0]
    ch = _row_chunk(rows)

    def body(c, carry):
        r = pl.ds(pl.multiple_of(c * ch, ch), ch)
        dst_ref[r, :] = src_ref[r, :].astype(dst_ref.dtype)
        return carry

    lax.fori_loop(0, rows // ch, body, 0)


def _matmul_kernel(a_ref, w_ref, o_ref, wb_ref):
    @pl.when(pl.program_id(0) == 0)
    def _():
        _cast_into(w_ref, wb_ref)

    o_ref[...] = _dot(a_ref[...], wb_ref[...]).astype(o_ref.dtype)


def _matmul(a, w, out_dtype, tm):
    m, k = a.shape
    n = w.shape[1]
    return pl.pallas_call(
        _matmul_kernel,
        grid=(m // tm,),
        in_specs=[pl.BlockSpec((tm, k), lambda i: (i, 0)),
                  pl.BlockSpec((k, n), lambda i: (0, 0), pipeline_mode=pl.Buffered(1))],
        out_specs=pl.BlockSpec((tm, n), lambda i: (i, 0)),
        out_shape=jax.ShapeDtypeStruct((m, n), out_dtype),
        scratch_shapes=[pltpu.VMEM((k, n), BF16)],
        compiler_params=_params("arbitrary"),
        name="matmul",
    )(a, w)


def _col_matmul_kernel(a_ref, wt_ref, o_ref, wb_ref):
    @pl.when(pl.program_id(1) == 0)
    def _():
        _cast_into(wt_ref, wb_ref)

    o_ref[...] = _dot_nt(a_ref[...], wb_ref[...]).astype(o_ref.dtype)


def _col_matmul(a, wt, col0, ncols, out_dtype, tm, tn):
    m, k = a.shape
    j0 = col0 // tn
    return pl.pallas_call(
        _col_matmul_kernel,
        grid=(ncols // tn, m // tm),
        in_specs=[pl.BlockSpec((tm, k), lambda j, i: (i, 0)),
                  pl.BlockSpec((tn, k), lambda j, i: (j0 + j, 0))],
        out_specs=pl.BlockSpec((tm, tn), lambda j, i: (i, j)),
        out_shape=jax.ShapeDtypeStruct((m, ncols), out_dtype),
        scratch_shapes=[pltpu.VMEM((tn, k), BF16)],
        compiler_params=_params("arbitrary", "arbitrary"),
        name="col_matmul",
    )(a, wt)


def _halo_rows(shift):
    return max(2 * shift, SUBLANES)


def _conv_buf(rows, tn, shift):
    return pltpu.VMEM((tn // LANES, _halo_rows(shift) + rows, LANES), F32)


def _slab(s):
    return slice(s * LANES, (s + 1) * LANES)


def _conv_stage(buf_ref, row0, x):
    for s in range(buf_ref.shape[0]):
        buf_ref[s, row0:row0 + x.shape[0], :] = x[:, _slab(s)]


def _conv_taps(buf_ref, s, r0, rows, shift):
    halo = _halo_rows(shift)

    def back(steps):
        start = r0 + (halo - steps * shift)
        if (steps * shift) % SUBLANES == 0:
            return pl.ds(start if isinstance(start, int) else pl.multiple_of(start, SUBLANES), rows)
        return pl.ds(start, rows, stride=1)

    return buf_ref[s, back(2), :], buf_ref[s, back(1), :], buf_ref[s, back(0), :]


def _conv_last(buf_ref, new_ref, rows, shift):
    halo = _halo_rows(shift)
    for s in range(buf_ref.shape[0]):
        new_ref[0, :, _slab(s)] = buf_ref[s, halo + rows - 2 * shift:halo + rows, :]


def _conv_proj_kernel(xn_ref, wbg_ref, wcg_ref, wvc_ref, prev_ref, cw_ref, o_ref, new_ref,
                      bgb_ref, cgb_ref, vcb_ref, buf_ref, *, shift, tiles_per_seq):
    rows = xn_ref.shape[0]
    halo = _halo_rows(shift)
    i = pl.program_id(1)

    @pl.when(i == 0)
    def _():
        _cast_into(wbg_ref, bgb_ref)
        _cast_into(wcg_ref, cgb_ref)
        _cast_into(wvc_ref, vcb_ref)

    def from_cache():
        _conv_stage(buf_ref, halo - 2 * shift, prev_ref[0])

    if tiles_per_seq == 1:
        from_cache()
    else:
        pl.when(i % tiles_per_seq == 0)(from_cache)

        @pl.when(i % tiles_per_seq != 0)
        def _():
            for s in range(buf_ref.shape[0]):
                buf_ref[s, halo - 2 * shift:halo, :] = buf_ref[s, halo + rows - 2 * shift:halo + rows, :]

    xn = xn_ref[...]
    _conv_stage(buf_ref, halo, _dot_nt(xn, cgb_ref[...]) * _dot_nt(xn, vcb_ref[...]))
    bg = _dot_nt(xn, bgb_ref[...])
    for s in range(buf_ref.shape[0]):
        u2, u1, u0 = _conv_taps(buf_ref, s, 0, rows, shift)
        y = cw_ref[0:1, _slab(s)] * u2 + cw_ref[1:2, _slab(s)] * u1 + cw_ref[2:3, _slab(s)] * u0
        o_ref[:, _slab(s)] = (bg[:, _slab(s)] * y).astype(o_ref.dtype)
    _conv_last(buf_ref, new_ref, rows, shift)


def _conv_proj(xn, w_in_t, prev, conv_w, *, nseq, shift, tm, tn):
    m, d = xn.shape
    nj = D_CONV // tn
    tiles_per_seq = m // (nseq * tm)
    kern = functools.partial(_conv_proj_kernel, shift=shift, tiles_per_seq=tiles_per_seq)
    state = pl.BlockSpec((1, 2 * shift, tn), lambda j, i: (i // tiles_per_seq, 0, j))
    wb = pltpu.VMEM((tn, d), BF16)
    return pl.pallas_call(
        kern,
        grid=(nj, m // tm),
        in_specs=[pl.BlockSpec((tm, d), lambda j, i: (i, 0)),
                  pl.BlockSpec((tn, d), lambda j, i: (j, 0)),
                  pl.BlockSpec((tn, d), lambda j, i: (j + nj, 0)),
                  pl.BlockSpec((tn, d), lambda j, i: (j + 2 * nj, 0)),
                  state,
                  pl.BlockSpec((CONV_WIDTH, tn), lambda j, i: (0, j))],
        out_specs=[pl.BlockSpec((tm, tn), lambda j, i: (i, j)),
                   state],
        out_shape=[jax.ShapeDtypeStruct((m, D_CONV), BF16),
                   jax.ShapeDtypeStruct((nseq, 2 * shift, D_CONV), F32)],
        scratch_shapes=[wb, wb, wb, _conv_buf(tm, tn, shift)],
        compiler_params=_params("arbitrary", "arbitrary"),
        name="conv_proj",
    )(xn, w_in_t, w_in_t, w_in_t, prev, conv_w)


def _cumsum_rows(g):
    c = g.shape[0]
    row = lax.broadcasted_iota(jnp.int32, g.shape, 0)
    x = g
    s = 1
    while s < c:
        x = x + jnp.where(row >= s, pltpu.roll(x, s, 0), 0.0)
        s *= 2
    return x


def _bcast_block_row(x, s, k):
    c, lanes = x.shape
    if s == c:
        return jnp.broadcast_to(x[k:k + 1, :], x.shape)
    if s >= SUBLANES:
        y = x.reshape(c // s, s, lanes)
        return jnp.broadcast_to(y[:, k:k + 1, :], y.shape).reshape(c, lanes)
    y = x.reshape(c // SUBLANES, SUBLANES, lanes)
    sub = lax.broadcasted_iota(jnp.int32, y.shape, 1)
    out = None
    for blk in range(SUBLANES // s):
        src = jnp.broadcast_to(y[:, blk * s + k:blk * s + k + 1, :], y.shape)
        out = src if out is None else jnp.where(sub >= blk * s, src, out)
    return out.reshape(c, lanes)


def _gla_pair_masks(c):
    ri = lax.broadcasted_iota(jnp.int32, (c, c), 0)
    ci = lax.broadcasted_iota(jnp.int32, (c, c), 1)
    diff_bits = ri ^ ci
    masks = [diff_bits == 0]
    level = 0
    while (1 << level) < c:
        masks.append(((diff_bits >> level) == 1) & (((ri >> level) & 1) == 1))
        level += 1
    return masks


def _gla_chunk(q, k, v, g, s_prev, masks):
    c = q.shape[0]
    cum = _cumsum_rows(g)
    a = jnp.where(masks[0], _dot_nt(q.astype(BF16), k.astype(BF16)), 0.0)
    for level in range(len(masks) - 1):
        half = 1 << level
        ref = _bcast_block_row(cum, 2 * half, half - 1)
        d = cum - ref
        up = jnp.minimum(d, 0.0)
        qe = q * jnp.exp2(up)
        ke = k * jnp.exp2(up - d)
        a = a + jnp.where(masks[1 + level], _dot_nt(qe.astype(BF16), ke.astype(BF16)), 0.0)
    o = _dot(a.astype(BF16), v.astype(BF16)) + _dot((q * jnp.exp2(cum)).astype(BF16), s_prev.astype(BF16))
    last = cum[c - 1:c, :]
    kd = k * jnp.exp2(last - cum)
    dk = last.shape[1]
    decay_t = jnp.transpose(jnp.broadcast_to(jnp.exp2(last), (dk, dk)))
    decayed = jnp.concatenate([decay_t * s_prev[:, i:i + dk] for i in range(0, s_prev.shape[1], dk)], axis=1)
    s_new = decayed + _dot_tn(kd.astype(BF16), v.astype(BF16))
    return o, s_new


def _gla_kernel(q_ref, k_ref, v_ref, r_ref, g_ref, s0_ref, gn_ref, *refs, chunk, single_chunk, group, n_cast):
    cast_src, (o_ref, sn_ref), cast_dst = refs[:n_cast], refs[n_cast:n_cast + 2], refs[n_cast + 2:]
    bb, rows = q_ref.shape[0], q_ref.shape[1]
    nchunk = rows // chunk
    state_in = s0_ref if single_chunk else sn_ref

    if not single_chunk:
        @pl.when(pl.program_id(1) == 0)
        def _():
            sn_ref[...] = s0_ref[...]

    masks = _gla_pair_masks(chunk)

    def one(b, r):
        for h in range(GLA_HEADS):
            kc = slice(h * GLA_DK, (h + 1) * GLA_DK)
            vc = slice(h * GLA_DV, (h + 1) * GLA_DV)
            q = q_ref[b, r, kc] * (GLA_DK ** -0.5)
            o, s_new = _gla_chunk(q, k_ref[b, r, kc], v_ref[b, r, vc], g_ref[b, r, kc], state_in[b, h], masks)
            sn_ref[b, h] = s_new
            rr = r_ref[b, r, vc]
            o_ref[b, r, vc] = (_rms_rows(o, gn_ref[:, vc]) * (rr * jax.nn.sigmoid(rr))).astype(o_ref.dtype)

    def body(n, carry):
        r = pl.ds(pl.multiple_of((n % nchunk) * chunk, chunk), chunk)
        for u in range(group):
            one((n // nchunk) * group + u, r)
        return carry

    lax.fori_loop(0, (bb // group) * nchunk, body, 0)
    for src, dst in zip(cast_src, cast_dst):
        _cast_into(src, dst)


def _gla(p, logf, s0, gla_norm, *, q_blk, k_blk, v_blk, r_blk, bb, rows, chunk, group, cast=()):
    nb, t, _ = p.shape
    nk, nv = GLA_HEADS * GLA_DK, GLA_HEADS * GLA_DV
    nt = t // rows
    nsteps = (nb // bb) * nt
    kern = functools.partial(_gla_kernel, chunk=chunk, single_chunk=(t == chunk), group=group, n_cast=len(cast))
    state_spec = pl.BlockSpec((bb, GLA_HEADS, GLA_DK, GLA_DV), lambda b, c: (b, 0, 0, 0))
    cast_specs = [pl.BlockSpec((w.shape[0] // nsteps, w.shape[1]), lambda b, c: (b * nt + c, 0)) for w in cast]
    return pl.pallas_call(
        kern,
        grid=(nb // bb, nt),
        in_specs=[pl.BlockSpec((bb, rows, nk), lambda b, c: (b, c, q_blk)),
                  pl.BlockSpec((bb, rows, nk), lambda b, c: (b, c, k_blk)),
                  pl.BlockSpec((bb, rows, nv), lambda b, c: (b, c, v_blk)),
                  pl.BlockSpec((bb, rows, nv), lambda b, c: (b, c, r_blk)),
                  pl.BlockSpec((bb, rows, nk), lambda b, c: (b, c, 0)),
                  state_spec,
                  pl.BlockSpec((1, nv), lambda b, c: (0, 0))] + cast_specs,
        out_specs=[pl.BlockSpec((bb, rows, nv), lambda b, c: (b, c, 0)),
                   state_spec] + cast_specs,
        out_shape=[jax.ShapeDtypeStruct((nb, t, D_GLA), BF16),
                   jax.ShapeDtypeStruct((nb, GLA_HEADS, GLA_DK, GLA_DV), F32)]
                  + [jax.ShapeDtypeStruct(w.shape, BF16) for w in cast],
        compiler_params=_params("arbitrary", "arbitrary"),
        name="gla",
    )(p, p, p, p, logf, s0, gla_norm, *cast)


def _proj_res_norm_kernel(*refs, n_a, nk, final):
    a_refs, (w_ref, res_ref, g_ref) = refs[:n_a], refs[n_a:n_a + 3]
    out_refs, acc_ref = refs[n_a + 3:-1], refs[-1]
    k = pl.program_id(1)
    part, r0 = None, 0
    for a_ref in a_refs:
        kw = a_ref.shape[1]
        term = _dot(a_ref[...], w_ref[r0:r0 + kw, :])
        part = term if part is None else part + term
        r0 += kw

    if nk == 1:
        ch = _row_chunk(part.shape[0], NORM_ROWS)
        for c0 in range(0, part.shape[0], ch):
            h = res_ref[c0:c0 + ch, :] + part[c0:c0 + ch, :]
            hn = _rms_rows(h, g_ref[...])
            if final:
                out_refs[0][c0:c0 + ch, :] = hn
            else:
                out_refs[0][c0:c0 + ch, :] = h
                out_refs[1][c0:c0 + ch, :] = hn.astype(out_refs[1].dtype)
        return

    @pl.when(k == 0)
    def _():
        acc_ref[...] = part

    if nk > 1:
        @pl.when(k > 0)
        def _():
            acc_ref[...] += part

    @pl.when(k == nk - 1)
    def _():
        rows = acc_ref.shape[0]
        ch = _row_chunk(rows, NORM_ROWS)
        g = g_ref[...]

        def body(c, carry):
            r = pl.ds(pl.multiple_of(c * ch, ch), ch)
            h = res_ref[r, :] + acc_ref[r, :]
            hn = _rms_rows(h, g)
            if final:
                out_refs[0][r, :] = hn
            else:
                out_refs[0][r, :] = h
                out_refs[1][r, :] = hn.astype(out_refs[1].dtype)
            return carry

        lax.fori_loop(0, rows // ch, body, 0)


def _proj_res_norm(a_list, w, res, gain, *, tm, tk, final):
    m = a_list[0].shape[0]
    kdim, d = w.shape
    nk = kdim // tk
    assert len(a_list) == 1 or nk == 1
    kern = functools.partial(_proj_res_norm_kernel, n_a=len(a_list), nk=nk, final=final)
    a_specs = ([pl.BlockSpec((tm, tk), lambda i, k: (i, k))] if len(a_list) == 1 else
               [pl.BlockSpec((tm, a.shape[1]), lambda i, k: (i, 0)) for a in a_list])
    row_spec = pl.BlockSpec((tm, d), lambda i, k: (i, 0))
    if final:
        out_specs = row_spec
        out_shape = jax.ShapeDtypeStruct((m, d), F32)
    else:
        out_specs = [row_spec, row_spec]
        out_shape = [jax.ShapeDtypeStruct((m, d), F32), jax.ShapeDtypeStruct((m, d), BF16)]
    return pl.pallas_call(
        kern,
        grid=(m // tm, nk),
        in_specs=a_specs + [pl.BlockSpec((tk, d), lambda i, k: (k, 0),
                                         pipeline_mode=pl.Buffered(1 if nk == 1 else 2)),
                            row_spec,
                            pl.BlockSpec((1, d), lambda i, k: (0, 0))],
        out_specs=out_specs,
        out_shape=out_shape,
        scratch_shapes=[pltpu.VMEM((tm, d), F32)],
        compiler_params=_params("arbitrary", "arbitrary"),
        name="proj_res_norm",
    )(*a_list, w, res, gain)


def _xattn_kernel(hn_ref, wq_ref, k_ref, v_ref, o_ref, wb_ref):
    @pl.when(pl.program_id(1) == 0)
    def _():
        _cast_into(wq_ref, wb_ref)

    q = _dot(hn_ref[0], wb_ref[...]).astype(BF16)
    p = _softmax_rows(_dot_nt(q, k_ref[0].astype(BF16)) * (X_HD ** -0.5))
    o_ref[0] = _dot(p.astype(BF16), v_ref[0].astype(BF16)).astype(o_ref.dtype)


def _xattn(hn, w_xq, mk, mv):
    nb, t, d = hn.shape
    return pl.pallas_call(
        _xattn_kernel,
        grid=(X_HEADS, nb),
        in_specs=[pl.BlockSpec((1, t, d), lambda h, b: (b, 0, 0)),
                  pl.BlockSpec((d, X_HD), lambda h, b: (0, h)),
                  pl.BlockSpec((1, MEM_LEN, X_HD), lambda h, b: (b, 0, h)),
                  pl.BlockSpec((1, MEM_LEN, X_HD), lambda h, b: (b, 0, h))],
        out_specs=pl.BlockSpec((1, t, X_HD), lambda h, b: (b, 0, h)),
        out_shape=jax.ShapeDtypeStruct((nb, t, d), BF16),
        scratch_shapes=[pltpu.VMEM((d, X_HD), BF16)],
        compiler_params=_params("arbitrary", "arbitrary"),
        name="xattn",
    )(hn, w_xq, mk, mv)


def _softmax_rows(s):
    s = s - jnp.max(s, axis=-1, keepdims=True)
    e = jnp.exp(s)
    return e / jnp.sum(e, axis=-1, keepdims=True)


def _xattn_cache_kernel(q_ref, k_ref, v_ref, o_ref):
    bb, tq = q_ref.shape[0], q_ref.shape[1]
    nchunk = X_HD // LANES
    pitch = nchunk * X_HEADS

    def gather(ref, b, h):
        parts = [ref[b, pl.ds(c * X_HEADS + h, MEM_LEN, stride=pitch), :] for c in range(nchunk)]
        return jnp.concatenate(parts, axis=1).astype(BF16)

    pairs = [(b, h) for b in range(bb) for h in range(X_HEADS)]
    scores = [_dot_nt(q_ref[b, :, h * X_HD:(h + 1) * X_HD].astype(BF16), gather(k_ref, b, h)) for b, h in pairs]
    p = _softmax_rows(jnp.concatenate(scores, axis=0) * (X_HD ** -0.5)).astype(BF16)
    for n, (b, h) in enumerate(pairs):
        o_ref[b, :, h * X_HD:(h + 1) * X_HD] = _dot(p[n * tq:(n + 1) * tq], gather(v_ref, b, h)).astype(o_ref.dtype)


def _xattn_cache(q, ck, cv, *, bb):
    nb, tq, d = q.shape
    nchunk = X_HD // LANES

    def stored_order(c):
        c = c.reshape(nb, MEM_LEN, X_HEADS, nchunk, LANES).transpose(0, 1, 3, 2, 4)
        return c.reshape(nb, MEM_LEN * nchunk * X_HEADS, LANES)

    rows = MEM_LEN * nchunk * X_HEADS
    return pl.pallas_call(
        _xattn_cache_kernel,
        grid=(nb // bb,),
        in_specs=[pl.BlockSpec((bb, tq, d), lambda b: (b, 0, 0)),
                  pl.BlockSpec((bb, rows, LANES), lambda b: (b, 0, 0)),
                  pl.BlockSpec((bb, rows, LANES), lambda b: (b, 0, 0))],
        out_specs=pl.BlockSpec((bb, tq, d), lambda b: (b, 0, 0)),
        out_shape=jax.ShapeDtypeStruct((nb, tq, d), BF16),
        compiler_params=_params("arbitrary"),
        name="xattn_cache",
    )(q, stored_order(ck), stored_order(cv))


def _shift_rows(x, prev, steps, shift):
    n = steps * shift
    head = prev[2 * shift - n:, :]
    if n % SUBLANES == 0:
        return jnp.concatenate([head, x[:-n, :]], axis=0)
    rolled = pltpu.roll(x, n, 0)
    row = lax.broadcasted_iota(jnp.int32, (SUBLANES, x.shape[1]), 0)
    first = rolled[:SUBLANES, :]
    for i in range(n):
        first = jnp.where(row == i, head[i:i + 1, :], first)
    return jnp.concatenate([first, rolled[SUBLANES:, :]], axis=0)


def _ffn_up_kernel(hn_ref, wg_ref, wu_ref, cw_ref, cb_ref, prev_ref, o_ref, new_ref, wgb_ref, wub_ref, *,
                   shift):
    rows = hn_ref.shape[0]

    @pl.when(pl.program_id(1) == 0)
    def _():
        _cast_into(wg_ref, wgb_ref)
        _cast_into(wu_ref, wub_ref)

    rb = max(FFN_ROW_BLOCK, 2 * shift)
    prev = prev_ref[0]
    for r0 in range(0, rows, rb):
        hn = hn_ref[r0:r0 + rb, :]
        gate = _dot(hn, wgb_ref[...])
        gc = (cw_ref[0:1, :] * _shift_rows(gate, prev, 2, shift)
              + cw_ref[1:2, :] * _shift_rows(gate, prev, 1, shift) + cw_ref[2:3, :] * gate + cb_ref[...])
        o_ref[r0:r0 + rb, :] = ((gc * jax.nn.sigmoid(gc)) * _dot(hn, wub_ref[...])).astype(o_ref.dtype)
        prev = gate[rb - 2 * shift:, :]
    new_ref[0] = prev


def _ffn_up(hn, wg, wu, cw, cb, prev, nseq, rows, shift, tn):
    d = hn.shape[1]
    kern = functools.partial(_ffn_up_kernel, shift=shift)
    return pl.pallas_call(
        kern,
        grid=(D_FF // tn, nseq),
        in_specs=[pl.BlockSpec((rows, d), lambda j, b: (b, 0)),
                  pl.BlockSpec((d, tn), lambda j, b: (0, j)),
                  pl.BlockSpec((d, tn), lambda j, b: (0, j)),
                  pl.BlockSpec((CONV_WIDTH, tn), lambda j, b: (0, j)),
                  pl.BlockSpec((1, tn), lambda j, b: (0, j)),
                  pl.BlockSpec((1, 2 * shift, tn), lambda j, b: (b, 0, j))],
        out_specs=[pl.BlockSpec((rows, tn), lambda j, b: (b, j)),
                   pl.BlockSpec((1, 2 * shift, tn), lambda j, b: (b, 0, j))],
        out_shape=[jax.ShapeDtypeStruct((nseq * rows, D_FF), BF16),
                   jax.ShapeDtypeStruct((nseq, 2 * shift, D_FF), F32)],
        scratch_shapes=[pltpu.VMEM((d, tn), BF16), pltpu.VMEM((d, tn), BF16)],
        compiler_params=_params("arbitrary", "arbitrary"),
        name="ffn_up",
    )(hn, wg, wu, cw, cb, prev)


def _layer(x, w, wb, *, nseq, rows, shift, prev_conv, s0, prev_ffn, mk, mv, time_major):
    m = x.shape[0]
    tm = min(m, 1024)
    xn, logf = _norm_gate(x, w["norm_mix"], w["w_g1_t"], w["w_g2"], w["b_gate"], tm)
    conv_out, conv_new = _conv_proj(xn, w["w_in_t"], prev_conv, w["conv_w"], nseq=nseq, shift=shift, tm=tm, tn=512)
    n_gla = N_MAIN - 3 * D_CONV
    p = _col_matmul(xn, w["w_in_t"], 3 * D_CONV, n_gla, F32, tm, 1024)

    if time_major:
        nt = m // shift
        pad = ((0, 0), (0, SUBLANES - nt), (0, 0))
        pg = jnp.pad(p.reshape(nt, shift, n_gla).transpose(1, 0, 2), pad)
        lg = jnp.pad(logf.reshape(nt, shift, -1).transpose(1, 0, 2), pad)
        gla_args = dict(bb=8, rows=SUBLANES, chunk=SUBLANES, group=4)
    else:
        pg, lg = p.reshape(nseq, rows, n_gla), logf.reshape(nseq, rows, -1)
        gla_args = dict(bb=4, rows=128, chunk=GLA_CHUNK, group=4)
    cast = () if wb is not None else (w["w_out"], w["w_xo"], w["w_fd"])
    o, s_new, *made = _gla(pg, lg, s0, w["gla_norm"], q_blk=0, k_blk=1, v_blk=1, r_blk=2, cast=cast, **gla_args)
    if wb is None:
        wb = dict(zip(("w_out", "w_xo", "w_fd"), made))
    if time_major:
        gla_out = o[:, :nt].transpose(1, 0, 2).reshape(m, D_GLA)
    else:
        gla_out = o.reshape(m, D_GLA)

    tm2 = min(m, 512)
    h, hn = _proj_res_norm([conv_out, gla_out], wb["w_out"], x, w["norm_x"], tm=tm2, tk=D_MODEL, final=False)
    if time_major:
        nt = m // shift
        qx = _matmul(hn, w["w_xq"], BF16, tm)
        qb = jnp.pad(qx.astype(F32).reshape(nt, shift, D_MODEL).transpose(1, 0, 2),
                     ((0, 0), (0, SUBLANES - nt), (0, 0)))
        ob = _xattn_cache(qb, mk, mv, bb=4)
        attn = ob[:, :nt].transpose(1, 0, 2).reshape(m, D_MODEL)
    else:
        attn = _xattn(hn.reshape(nseq, rows, D_MODEL), w["w_xq"], mk, mv).reshape(m, D_MODEL)

    h2, hn2 = _proj_res_norm([attn], wb["w_xo"], h, w["norm_ffn"], tm=tm2, tk=D_MODEL, final=False)
    act, ffn_new = _ffn_up(hn2, w["w_fg"], w["w_fu"], w["ffn_conv_w"], w["ffn_conv_b"], prev_ffn,
                           nseq, rows, shift, 512)
    return act, h2, conv_new, s_new, ffn_new, wb


def kernel(x_prompt, x_sample, mem_prompt, cache_conv, state_gla, cache_ffn, cache_mem_k, cache_mem_v,
           norm_mix, w_in, conv_w, w_gate2, b_gate, gla_norm, w_out, norm_x, norm_mem, w_xq, w_xk, w_xv,
           w_xo, norm_ffn, w_ffn_gate, w_ffn_up, ffn_conv_w, ffn_conv_b, w_ffn_down, norm_final):
    depth = w_in.shape[0]
    nb, seq, d = x_prompt.shape
    db, dseq, _ = x_sample.shape
    hp = x_prompt.reshape(nb * seq, d)
    hs = x_sample.transpose(1, 0, 2).reshape(dseq * db, d)
    outs = {k: [] for k in ("conv_p", "gla_p", "ffn_p", "mk", "mv", "conv_s", "gla_s", "ffn_s")}
    nfinal = norm_final.reshape(1, d)
    yp = ys = None
    for l in range(depth):
        w = {
            "norm_mix": norm_mix[l].reshape(1, d),
            "w_in_t": w_in[l].T,
            "w_g1_t": jnp.pad(w_in[l].T[N_MAIN:], ((0, LANES - GLA_RANK), (0, 0))).astype(BF16),
            "w_g2": jnp.pad(w_gate2[l], ((0, LANES - GLA_RANK), (0, 0))).astype(BF16),
            "b_gate": b_gate[l].reshape(1, -1),
            "conv_w": conv_w[l],
            "gla_norm": gla_norm[l].reshape(1, -1),
            "w_out": w_out[l],
            "norm_x": norm_x[l].reshape(1, d),
            "w_xq": w_xq[l],
            "w_xo": w_xo[l],
            "w_fd": w_ffn_down[l],
            "norm_ffn": norm_ffn[l].reshape(1, d),
            "w_fg": w_ffn_gate[l],
            "w_fu": w_ffn_up[l],
            "ffn_conv_w": ffn_conv_w[l],
            "ffn_conv_b": ffn_conv_b[l].reshape(1, -1),
        }
        last = l == depth - 1
        gain_next = nfinal if last else None

        mem = mem_prompt.reshape(nb * MEM_LEN, d)
        nmem = norm_mem[l].reshape(1, d)
        mk, mk_cache = _mem_proj(mem, nmem, w_xk[l], 1024)
        mv, mv_cache = _mem_proj(mem, nmem, w_xv[l], 1024)
        act, h2, c1, s1, f1, wb = _layer(
            hp, w, None, nseq=nb, rows=seq, shift=1,
            prev_conv=jnp.zeros((nb, CONV_WIDTH - 1, D_CONV), F32),
            s0=jnp.zeros((nb, GLA_HEADS, GLA_DK, GLA_DV), F32),
            prev_ffn=jnp.zeros((nb, CONV_WIDTH - 1, D_FF), F32),
            mk=mk.reshape(nb, MEM_LEN, d), mv=mv.reshape(nb, MEM_LEN, d), time_major=False)
        assert last, "only the final layer's epilogue (final rmsnorm) is implemented"
        yp = _proj_res_norm([act], wb["w_fd"], h2, gain_next, tm=256, tk=D_FF, final=True)
        outs["conv_p"].append(c1)
        outs["gla_p"].append(s1)
        outs["ffn_p"].append(f1)
        outs["mk"].append(mk_cache)
        outs["mv"].append(mv_cache)

        def tmajor(c):
            return c.transpose(1, 0, 2).reshape(1, (CONV_WIDTH - 1) * db, c.shape[-1])

        act, h2, c2, s2, f2, _ = _layer(
            hs, w, wb, nseq=1, rows=dseq * db, shift=db,
            prev_conv=tmajor(cache_conv[l]), s0=state_gla[l], prev_ffn=tmajor(cache_ffn[l]),
            mk=cache_mem_k[l], mv=cache_mem_v[l], time_major=True)
        ys = _proj_res_norm([act], wb["w_fd"], h2, gain_next, tm=256, tk=D_FF, final=True)
        outs["conv_s"].append(c2.reshape(CONV_WIDTH - 1, db, D_CONV).transpose(1, 0, 2))
        outs["gla_s"].append(s2)
        outs["ffn_s"].append(f2.reshape(CONV_WIDTH - 1, db, D_FF).transpose(1, 0, 2))

    y_prompt = yp.reshape(nb, seq, d)
    y_sample = ys.reshape(dseq, db, d).transpose(1, 0, 2)
    st = lambda k: jnp.stack(outs[k])
    return (y_prompt, y_sample, st("conv_p"), st("gla_p"), st("ffn_p"), st("mk"), st("mv"),
            st("conv_s"), st("gla_s"), st("ffn_s"))
```

```python
import functools

import jax
import jax.numpy as jnp
from jax import lax
from jax.experimental import pallas as pl
from jax.experimental.pallas import tpu as pltpu

F32 = jnp.float32
BF16 = jnp.bfloat16

D_MODEL = 2048
EPS = 1e-6
CONV_WIDTH = 3
D_CONV = 1024
D_GLA = 1024
GLA_HEADS = 4
GLA_DV = 256
GLA_DK = 128
GLA_RANK = 16
GLA_TAU = 16.0
LOG2_E = 1.4426950408889634
GLA_CHUNK = 64
X_HEADS = 4
X_HD = 512
MEM_LEN = 256
D_FF = 5632
N_MAIN = 3 * D_CONV + 2 * GLA_HEADS * GLA_DK + 2 * GLA_HEADS * GLA_DV

LANES = 128
SUBLANES = 8
VMEM_LIMIT_BYTES = 56 * 1024 * 1024


def _params(*sem):
    return pltpu.CompilerParams(dimension_semantics=sem, vmem_limit_bytes=VMEM_LIMIT_BYTES)


def _dot(a, b):
    return jnp.dot(a, b, preferred_element_type=F32)


def _dot_nt(a, b):
    return lax.dot_general(a, b, (((1,), (1,)), ((), ())), preferred_element_type=F32)


def _dot_tn(a, b):
    return lax.dot_general(a, b, (((0,), (0,)), ((), ())), preferred_element_type=F32)


def _rms_rows(x, g):
    ms = jnp.mean(x * x, axis=-1, keepdims=True)
    return (x * lax.rsqrt(ms + EPS)) * g


def _row_chunk(rows, limit=256):
    for c in (256, 128, 64, 32, 16, 8):
        if c <= limit and rows % c == 0:
            return c
    return rows


NORM_ROWS = 128
FFN_ROW_BLOCK = 512
PROJ_ROW_BLOCK = 256


def _norm_into(x_ref, g_ref, xn_ref):
    rows = x_ref.shape[0]
    ch = _row_chunk(rows)
    g = g_ref[...]

    def body(c, carry):
        r = pl.ds(pl.multiple_of(c * ch, ch), ch)
        xn_ref[r, :] = _rms_rows(x_ref[r, :], g).astype(xn_ref.dtype)
        return carry

    lax.fori_loop(0, rows // ch, body, 0)


def _norm_matmul_kernel(x_ref, g_ref, w_ref, o_ref, oc_ref, xn_ref):
    j = pl.program_id(1)

    @pl.when(j == 0)
    def _():
        _norm_into(x_ref, g_ref, xn_ref)

    o_ref[...] = _dot(xn_ref[...], w_ref[...].astype(BF16)).astype(o_ref.dtype)
    nchunk = X_HD // LANES
    pitch = nchunk * X_HEADS
    heads_per_tile = o_ref.shape[1] // X_HD
    for b in range(oc_ref.shape[0]):
        for hh in range(heads_per_tile):
            for c in range(nchunk):
                col = hh * X_HD + c * LANES
                row = c * X_HEADS + j * heads_per_tile + hh
                oc_ref[b, pl.ds(row, MEM_LEN, stride=pitch), :] = (
                    o_ref[b * MEM_LEN:(b + 1) * MEM_LEN, col:col + LANES])


def _norm_gate_kernel(x_ref, g_ref, wg1_ref, wg2_ref, bg_ref, xn_ref, lf_ref):
    _norm_into(x_ref, g_ref, xn_ref)
    g1 = _dot_nt(xn_ref[...], wg1_ref[...])
    z = _dot(g1.astype(BF16), wg2_ref[...]) + bg_ref[...]
    lf_ref[...] = (jnp.minimum(z, 0.0) - jnp.log1p(jnp.exp(-jnp.abs(z)))) * (LOG2_E / GLA_TAU)


def _mem_proj(mem, gain, w, tn):
    m, d = mem.shape
    n = w.shape[1]
    nb = m // MEM_LEN
    nchunk = X_HD // LANES
    rows = MEM_LEN * nchunk * X_HEADS
    flat, stored = pl.pallas_call(
        _norm_matmul_kernel,
        grid=(1, n // tn),
        in_specs=[pl.BlockSpec((m, d), lambda i, j: (0, 0)),
                  pl.BlockSpec((1, d), lambda i, j: (0, 0)),
                  pl.BlockSpec((d, tn), lambda i, j: (0, j))],
        out_specs=[pl.BlockSpec((m, tn), lambda i, j: (0, j)),
                   pl.BlockSpec((nb, rows, LANES), lambda i, j: (0, 0, 0))],
        out_shape=[jax.ShapeDtypeStruct((m, n), F32),
                   jax.ShapeDtypeStruct((nb, rows, LANES), F32)],
        scratch_shapes=[pltpu.VMEM((m, d), BF16)],
        compiler_params=_params("arbitrary", "arbitrary"),
        name="norm_matmul",
    )(mem, gain, w)
    cache = stored.reshape(nb, MEM_LEN, nchunk, X_HEADS, LANES).transpose(0, 1, 3, 2, 4)
    return flat, cache.reshape(nb, MEM_LEN, X_HEADS, X_HD)


def _norm_gate(x, gain, wg1, wg2, b_gate, tm):
    m, d = x.shape
    ng = wg2.shape[1]
    return pl.pallas_call(
        _norm_gate_kernel,
        grid=(m // tm,),
        in_specs=[pl.BlockSpec((tm, d), lambda i: (i, 0)),
                  pl.BlockSpec((1, d), lambda i: (0, 0)),
                  pl.BlockSpec((LANES, d), lambda i: (0, 0)),
                  pl.BlockSpec((LANES, ng), lambda i: (0, 0)),
                  pl.BlockSpec((1, ng), lambda i: (0, 0))],
        out_specs=[pl.BlockSpec((tm, d), lambda i: (i, 0)),
                   pl.BlockSpec((tm, ng), lambda i: (i, 0))],
        out_shape=[jax.ShapeDtypeStruct((m, d), BF16),
                   jax.ShapeDtypeStruct((m, ng), F32)],
        compiler_params=_params("arbitrary"),
        name="norm_gate",
    )(x, gain, wg1, wg2, b_gate)


def _cast_into(src_ref, dst_ref):
    rows = src_ref.shape[0]
    ch = _row_chunk(rows)

    def body(c, carry):
        r = pl.ds(pl.multiple_of(c * ch, ch), ch)
        dst_ref[r, :] = src_ref[r, :].astype(dst_ref.dtype)
        return carry

    lax.fori_loop(0, rows // ch, body, 0)


def _matmul_kernel(a_ref, w_ref, o_ref, wb_ref):
    @pl.when(pl.program_id(0) == 0)
    def _():
        _cast_into(w_ref, wb_ref)

    o_ref[...] = _dot(a_ref[...], wb_ref[...]).astype(o_ref.dtype)


def _matmul(a, w, out_dtype, tm):
    m, k = a.shape
    n = w.shape[1]
    return pl.pallas_call(
        _matmul_kernel,
        grid=(m // tm,),
        in_specs=[pl.BlockSpec((tm, k), lambda i: (i, 0)),
                  pl.BlockSpec((k, n), lambda i: (0, 0), pipeline_mode=pl.Buffered(1))],
        out_specs=pl.BlockSpec((tm, n), lambda i: (i, 0)),
        out_shape=jax.ShapeDtypeStruct((m, n), out_dtype),
        scratch_shapes=[pltpu.VMEM((k, n), BF16)],
        compiler_params=_params("arbitrary"),
        name="matmul",
    )(a, w)


def _col_matmul_kernel(a_ref, wt_ref, o_ref, wb_ref):
    @pl.when(pl.program_id(1) == 0)
    def _():
        _cast_into(wt_ref, wb_ref)

    o_ref[...] = _dot_nt(a_ref[...], wb_ref[...]).astype(o_ref.dtype)


def _col_matmul(a, wt, col0, ncols, out_dtype, tm, tn):
    m, k = a.shape
    j0 = col0 // tn
    return pl.pallas_call(
        _col_matmul_kernel,
        grid=(ncols // tn, m // tm),
        in_specs=[pl.BlockSpec((tm, k), lambda j, i: (i, 0)),
                  pl.BlockSpec((tn, k), lambda j, i: (j0 + j, 0))],
        out_specs=pl.BlockSpec((tm, tn), lambda j, i: (i, j)),
        out_shape=jax.ShapeDtypeStruct((m, ncols), out_dtype),
        scratch_shapes=[pltpu.VMEM((tn, k), BF16)],
        compiler_params=_params("arbitrary", "arbitrary"),
        name="col_matmul",
    )(a, wt)


def _halo_rows(shift):
    return max(2 * shift, SUBLANES)


def _conv_buf(rows, tn, shift):
    return pltpu.VMEM((tn // LANES, _halo_rows(shift) + rows, LANES), F32)


def _slab(s):
    return slice(s * LANES, (s + 1) * LANES)


def _conv_stage(buf_ref, row0, x):
    for s in range(buf_ref.shape[0]):
        buf_ref[s, row0:row0 + x.shape[0], :] = x[:, _slab(s)]


def _conv_taps(buf_ref, s, r0, rows, shift):
    halo = _halo_rows(shift)

    def back(steps):
        start = r0 + (halo - steps * shift)
        if (steps * shift) % SUBLANES == 0:
            return pl.ds(start if isinstance(start, int) else pl.multiple_of(start, SUBLANES), rows)
        return pl.ds(start, rows, stride=1)

    return buf_ref[s, back(2), :], buf_ref[s, back(1), :], buf_ref[s, back(0), :]


def _conv_last(buf_ref, new_ref, rows, shift):
    halo = _halo_rows(shift)
    for s in range(buf_ref.shape[0]):
        new_ref[0, :, _slab(s)] = buf_ref[s, halo + rows - 2 * shift:halo + rows, :]


def _conv_proj_kernel(xn_ref, wbg_ref, wcg_ref, wvc_ref, prev_ref, cw_ref, o_ref, new_ref,
                      bgb_ref, cgb_ref, vcb_ref, buf_ref, *, shift, tiles_per_seq):
    rows = xn_ref.shape[0]
    halo = _halo_rows(shift)
    i = pl.program_id(1)

    @pl.when(i == 0)
    def _():
        _cast_into(wbg_ref, bgb_ref)
        _cast_into(wcg_ref, cgb_ref)
        _cast_into(wvc_ref, vcb_ref)

    def from_cache():
        _conv_stage(buf_ref, halo - 2 * shift, prev_ref[0])

    if tiles_per_seq == 1:
        from_cache()
    else:
        pl.when(i % tiles_per_seq == 0)(from_cache)

        @pl.when(i % tiles_per_seq != 0)
        def _():
            for s in range(buf_ref.shape[0]):
                buf_ref[s, halo - 2 * shift:halo, :] = buf_ref[s, halo + rows - 2 * shift:halo + rows, :]

    xn = xn_ref[...]
    _conv_stage(buf_ref, halo, _dot_nt(xn, cgb_ref[...]) * _dot_nt(xn, vcb_ref[...]))
    bg = _dot_nt(xn, bgb_ref[...])
    for s in range(buf_ref.shape[0]):
        u2, u1, u0 = _conv_taps(buf_ref, s, 0, rows, shift)
        y = cw_ref[0:1, _slab(s)] * u2 + cw_ref[1:2, _slab(s)] * u1 + cw_ref[2:3, _slab(s)] * u0
        o_ref[:, _slab(s)] = (bg[:, _slab(s)] * y).astype(o_ref.dtype)
    _conv_last(buf_ref, new_ref, rows, shift)


def _conv_proj(xn, w_in_t, prev, conv_w, *, nseq, shift, tm, tn):
    m, d = xn.shape
    nj = D_CONV // tn
    tiles_per_seq = m // (nseq * tm)
    kern = functools.partial(_conv_proj_kernel, shift=shift, tiles_per_seq=tiles_per_seq)
    state = pl.BlockSpec((1, 2 * shift, tn), lambda j, i: (i // tiles_per_seq, 0, j))
    wb = pltpu.VMEM((tn, d), BF16)
    return pl.pallas_call(
        kern,
        grid=(nj, m // tm),
        in_specs=[pl.BlockSpec((tm, d), lambda j, i: (i, 0)),
                  pl.BlockSpec((tn, d), lambda j, i: (j, 0)),
                  pl.BlockSpec((tn, d), lambda j, i: (j + nj, 0)),
                  pl.BlockSpec((tn, d), lambda j, i: (j + 2 * nj, 0)),
                  state,
                  pl.BlockSpec((CONV_WIDTH, tn), lambda j, i: (0, j))],
        out_specs=[pl.BlockSpec((tm, tn), lambda j, i: (i, j)),
                   state],
        out_shape=[jax.ShapeDtypeStruct((m, D_CONV), BF16),
                   jax.ShapeDtypeStruct((nseq, 2 * shift, D_CONV), F32)],
        scratch_shapes=[wb, wb, wb, _conv_buf(tm, tn, shift)],
        compiler_params=_params("arbitrary", "arbitrary"),
        name="conv_proj",
    )(xn, w_in_t, w_in_t, w_in_t, prev, conv_w)


def _cumsum_rows(g):
    c = g.shape[0]
    row = lax.broadcasted_iota(jnp.int32, g.shape, 0)
    x = g
    s = 1
    while s < c:
        x = x + jnp.where(row >= s, pltpu.roll(x, s, 0), 0.0)
        s *= 2
    return x


def _bcast_block_row(x, s, k):
    c, lanes = x.shape
    if s == c:
        return jnp.broadcast_to(x[k:k + 1, :], x.shape)
    if s >= SUBLANES:
        y = x.reshape(c // s, s, lanes)
        return jnp.broadcast_to(y[:, k:k + 1, :], y.shape).reshape(c, lanes)
    y = x.reshape(c // SUBLANES, SUBLANES, lanes)
    sub = lax.broadcasted_iota(jnp.int32, y.shape, 1)
    out = None
    for blk in range(SUBLANES // s):
        src = jnp.broadcast_to(y[:, blk * s + k:blk * s + k + 1, :], y.shape)
        out = src if out is None else jnp.where(sub >= blk * s, src, out)
    return out.reshape(c, lanes)


def _gla_pair_masks(c):
    ri = lax.broadcasted_iota(jnp.int32, (c, c), 0)
    ci = lax.broadcasted_iota(jnp.int32, (c, c), 1)
    diff_bits = ri ^ ci
    masks = [diff_bits == 0]
    level = 0
    while (1 << level) < c:
        masks.append(((diff_bits >> level) == 1) & (((ri >> level) & 1) == 1))
        level += 1
    return masks


def _gla_chunk(q, k, v, g, s_prev, masks):
    c = q.shape[0]
    cum = _cumsum_rows(g)
    a = jnp.where(masks[0], _dot_nt(q.astype(BF16), k.astype(BF16)), 0.0)
    for level in range(len(masks) - 1):
        half = 1 << level
        ref = _bcast_block_row(cum, 2 * half, half - 1)
        d = cum - ref
        up = jnp.minimum(d, 0.0)
        qe = q * jnp.exp2(up)
        ke = k * jnp.exp2(up - d)
        a = a + jnp.where(masks[1 + level], _dot_nt(qe.astype(BF16), ke.astype(BF16)), 0.0)
    o = _dot(a.astype(BF16), v.astype(BF16)) + _dot((q * jnp.exp2(cum)).astype(BF16), s_prev.astype(BF16))
    last = cum[c - 1:c, :]
    kd = k * jnp.exp2(last - cum)
    dk = last.shape[1]
    decay_t = jnp.transpose(jnp.broadcast_to(jnp.exp2(last), (dk, dk)))
    decayed = jnp.concatenate([decay_t * s_prev[:, i:i + dk] for i in range(0, s_prev.shape[1], dk)], axis=1)
    s_new = decayed + _dot_tn(kd.astype(BF16), v.astype(BF16))
    return o, s_new


def _gla_kernel(q_ref, k_ref, v_ref, r_ref, g_ref, s0_ref, gn_ref, *refs, chunk, single_chunk, group, n_cast):
    cast_src, (o_ref, sn_ref), cast_dst = refs[:n_cast], refs[n_cast:n_cast + 2], refs[n_cast + 2:]
    bb, rows = q_ref.shape[0], q_ref.shape[1]
    nchunk = rows // chunk
    state_in = s0_ref if single_chunk else sn_ref

    if not single_chunk:
        @pl.when(pl.program_id(1) == 0)
        def _():
            sn_ref[...] = s0_ref[...]

    masks = _gla_pair_masks(chunk)

    def one(b, r):
        for h in range(GLA_HEADS):
            kc = slice(h * GLA_DK, (h + 1) * GLA_DK)
            vc = slice(h * GLA_DV, (h + 1) * GLA_DV)
            q = q_ref[b, r, kc] * (GLA_DK ** -0.5)
            o, s_new = _gla_chunk(q, k_ref[b, r, kc], v_ref[b, r, vc], g_ref[b, r, kc], state_in[b, h], masks)
            sn_ref[b, h] = s_new
            rr = r_ref[b, r, vc]
            o_ref[b, r, vc] = (_rms_rows(o, gn_ref[:, vc]) * (rr * jax.nn.sigmoid(rr))).astype(o_ref.dtype)

    def body(n, carry):
        r = pl.ds(pl.multiple_of((n % nchunk) * chunk, chunk), chunk)
        for u in range(group):
            one((n // nchunk) * group + u, r)
        return carry

    lax.fori_loop(0, (bb // group) * nchunk, body, 0)
    for src, dst in zip(cast_src, cast_dst):
        _cast_into(src, dst)


def _gla(p, logf, s0, gla_norm, *, q_blk, k_blk, v_blk, r_blk, bb, rows, chunk, group, cast=()):
    nb, t, _ = p.shape
    nk, nv = GLA_HEADS * GLA_DK, GLA_HEADS * GLA_DV
    nt = t // rows
    nsteps = (nb // bb) * nt
    kern = functools.partial(_gla_kernel, chunk=chunk, single_chunk=(t == chunk), group=group, n_cast=len(cast))
    state_spec = pl.BlockSpec((bb, GLA_HEADS, GLA_DK, GLA_DV), lambda b, c: (b, 0, 0, 0))
    cast_specs = [pl.BlockSpec((w.shape[0] // nsteps, w.shape[1]), lambda b, c: (b * nt + c, 0)) for w in cast]
    return pl.pallas_call(
        kern,
        grid=(nb // bb, nt),
        in_specs=[pl.BlockSpec((bb, rows, nk), lambda b, c: (b, c, q_blk)),
                  pl.BlockSpec((bb, rows, nk), lambda b, c: (b, c, k_blk)),
                  pl.BlockSpec((bb, rows, nv), lambda b, c: (b, c, v_blk)),
                  pl.BlockSpec((bb, rows, nv), lambda b, c: (b, c, r_blk)),
                  pl.BlockSpec((bb, rows, nk), lambda b, c: (b, c, 0)),
                  state_spec,
                  pl.BlockSpec((1, nv), lambda b, c: (0, 0))] + cast_specs,
        out_specs=[pl.BlockSpec((bb, rows, nv), lambda b, c: (b, c, 0)),
                   state_spec] + cast_specs,
        out_shape=[jax.ShapeDtypeStruct((nb, t, D_GLA), BF16),
                   jax.ShapeDtypeStruct((nb, GLA_HEADS, GLA_DK, GLA_DV), F32)]
                  + [jax.ShapeDtypeStruct(w.shape, BF16) for w in cast],
        compiler_params=_params("arbitrary", "arbitrary"),
        name="gla",
    )(p, p, p, p, logf, s0, gla_norm, *cast)


def _proj_res_norm_kernel(*refs, n_a, nk, final):
    a_refs, (w_ref, res_ref, g_ref) = refs[:n_a], refs[n_a:n_a + 3]
    out_refs, acc_ref = refs[n_a + 3:-1], refs[-1]
    k = pl.program_id(1)

    def product(rows):
        part, r0 = None, 0
        for a_ref in a_refs:
            kw = a_ref.shape[1]
            term = _dot(a_ref[rows, :], w_ref[r0:r0 + kw, :])
            part = term if part is None else part + term
            r0 += kw
        return part

    tm = res_ref.shape[0]
    if nk == 1:
        rb = min(tm, PROJ_ROW_BLOCK)
        ch = _row_chunk(rb, NORM_ROWS)
        for b0 in range(0, tm, rb):
            part = product(slice(b0, b0 + rb))
            for c0 in range(0, rb, ch):
                r = slice(b0 + c0, b0 + c0 + ch)
                h = res_ref[r, :] + part[c0:c0 + ch, :]
                hn = _rms_rows(h, g_ref[...])
                if final:
                    out_refs[0][r, :] = hn
                else:
                    out_refs[0][r, :] = h
                    out_refs[1][r, :] = hn.astype(out_refs[1].dtype)
        return

    part = product(slice(None))

    @pl.when(k == 0)
    def _():
        acc_ref[...] = part

    if nk > 1:
        @pl.when(k > 0)
        def _():
            acc_ref[...] += part

    @pl.when(k == nk - 1)
    def _():
        rows = acc_ref.shape[0]
        ch = _row_chunk(rows, NORM_ROWS)
        g = g_ref[...]

        def body(c, carry):
            r = pl.ds(pl.multiple_of(c * ch, ch), ch)
            h = res_ref[r, :] + acc_ref[r, :]
            hn = _rms_rows(h, g)
            if final:
                out_refs[0][r, :] = hn
            else:
                out_refs[0][r, :] = h
                out_refs[1][r, :] = hn.astype(out_refs[1].dtype)
            return carry

        lax.fori_loop(0, rows // ch, body, 0)


def _proj_res_norm(a_list, w, res, gain, *, tm, tk, final):
    m = a_list[0].shape[0]
    kdim, d = w.shape
    nk = kdim // tk
    assert len(a_list) == 1 or nk == 1
    kern = functools.partial(_proj_res_norm_kernel, n_a=len(a_list), nk=nk, final=final)
    a_specs = ([pl.BlockSpec((tm, tk), lambda i, k: (i, k))] if len(a_list) == 1 else
               [pl.BlockSpec((tm, a.shape[1]), lambda i, k: (i, 0)) for a in a_list])
    row_spec = pl.BlockSpec((tm, d), lambda i, k: (i, 0))
    if final:
        out_specs = row_spec
        out_shape = jax.ShapeDtypeStruct((m, d), F32)
    else:
        out_specs = [row_spec, row_spec]
        out_shape = [jax.ShapeDtypeStruct((m, d), F32), jax.ShapeDtypeStruct((m, d), BF16)]
    return pl.pallas_call(
        kern,
        grid=(m // tm, nk),
        in_specs=a_specs + [pl.BlockSpec((tk, d), lambda i, k: (k, 0),
                                         pipeline_mode=pl.Buffered(1 if nk == 1 else 2)),
                            row_spec,
                            pl.BlockSpec((1, d), lambda i, k: (0, 0))],
        out_specs=out_specs,
        out_shape=out_shape,
        scratch_shapes=[pltpu.VMEM((tm, d), F32)],
        compiler_params=_params("arbitrary", "arbitrary"),
        name="proj_res_norm",
    )(*a_list, w, res, gain)


def _xattn_kernel(hn_ref, wq_ref, k_ref, v_ref, o_ref, wb_ref):
    @pl.when(pl.program_id(1) == 0)
    def _():
        _cast_into(wq_ref, wb_ref)

    q = _dot(hn_ref[0], wb_ref[...]).astype(BF16)
    p = _softmax_rows(_dot_nt(q, k_ref[0].astype(BF16)) * (X_HD ** -0.5))
    o_ref[0] = _dot(p.astype(BF16), v_ref[0].astype(BF16)).astype(o_ref.dtype)


def _xattn(hn, w_xq, mk, mv):
    nb, t, d = hn.shape
    return pl.pallas_call(
        _xattn_kernel,
        grid=(X_HEADS, nb),
        in_specs=[pl.BlockSpec((1, t, d), lambda h, b: (b, 0, 0)),
                  pl.BlockSpec((d, X_HD), lambda h, b: (0, h)),
                  pl.BlockSpec((1, MEM_LEN, X_HD), lambda h, b: (b, 0, h)),
                  pl.BlockSpec((1, MEM_LEN, X_HD), lambda h, b: (b, 0, h))],
        out_specs=pl.BlockSpec((1, t, X_HD), lambda h, b: (b, 0, h)),
        out_shape=jax.ShapeDtypeStruct((nb, t, d), BF16),
        scratch_shapes=[pltpu.VMEM((d, X_HD), BF16)],
        compiler_params=_params("arbitrary", "arbitrary"),
        name="xattn",
    )(hn, w_xq, mk, mv)


def _softmax_rows(s):
    s = s - jnp.max(s, axis=-1, keepdims=True)
    e = jnp.exp(s)
    return e / jnp.sum(e, axis=-1, keepdims=True)


def _xattn_cache_kernel(q_ref, k_ref, v_ref, o_ref):
    bb, tq = q_ref.shape[0], q_ref.shape[1]
    nchunk = X_HD // LANES
    pitch = nchunk * X_HEADS

    def gather(ref, b, h):
        parts = [ref[b, pl.ds(c * X_HEADS + h, MEM_LEN, stride=pitch), :] for c in range(nchunk)]
        return jnp.concatenate(parts, axis=1).astype(BF16)

    pairs = [(b, h) for b in range(bb) for h in range(X_HEADS)]
    scores = [_dot_nt(q_ref[b, :, h * X_HD:(h + 1) * X_HD].astype(BF16), gather(k_ref, b, h)) for b, h in pairs]
    p = _softmax_rows(jnp.concatenate(scores, axis=0) * (X_HD ** -0.5)).astype(BF16)
    for n, (b, h) in enumerate(pairs):
        o_ref[b, :, h * X_HD:(h + 1) * X_HD] = _dot(p[n * tq:(n + 1) * tq], gather(v_ref, b, h)).astype(o_ref.dtype)


def _xattn_cache(q, ck, cv, *, bb):
    nb, tq, d = q.shape
    nchunk = X_HD // LANES

    def stored_order(c):
        c = c.reshape(nb, MEM_LEN, X_HEADS, nchunk, LANES).transpose(0, 1, 3, 2, 4)
        return c.reshape(nb, MEM_LEN * nchunk * X_HEADS, LANES)

    rows = MEM_LEN * nchunk * X_HEADS
    return pl.pallas_call(
        _xattn_cache_kernel,
        grid=(nb // bb,),
        in_specs=[pl.BlockSpec((bb, tq, d), lambda b: (b, 0, 0)),
                  pl.BlockSpec((bb, rows, LANES), lambda b: (b, 0, 0)),
                  pl.BlockSpec((bb, rows, LANES), lambda b: (b, 0, 0))],
        out_specs=pl.BlockSpec((bb, tq, d), lambda b: (b, 0, 0)),
        out_shape=jax.ShapeDtypeStruct((nb, tq, d), BF16),
        compiler_params=_params("arbitrary"),
        name="xattn_cache",
    )(q, stored_order(ck), stored_order(cv))


def _shift_rows(x, prev, steps, shift):
    n = steps * shift
    head = prev[2 * shift - n:, :]
    if n == x.shape[0]:
        return head
    if n % SUBLANES == 0:
        return jnp.concatenate([head, x[:-n, :]], axis=0)
    rolled = pltpu.roll(x, n, 0)
    row = lax.broadcasted_iota(jnp.int32, (SUBLANES, x.shape[1]), 0)
    first = rolled[:SUBLANES, :]
    for i in range(n):
        first = jnp.where(row == i, head[i:i + 1, :], first)
    return jnp.concatenate([first, rolled[SUBLANES:, :]], axis=0)


def _ffn_up_kernel(hn_ref, wg_ref, wu_ref, cw_ref, cb_ref, prev_ref, o_ref, new_ref, wgb_ref, wub_ref, *,
                   shift):
    rows = hn_ref.shape[0]

    @pl.when(pl.program_id(1) == 0)
    def _():
        _cast_into(wg_ref, wgb_ref)
        _cast_into(wu_ref, wub_ref)

    rb = max(FFN_ROW_BLOCK, 2 * shift)
    prev = prev_ref[0]
    for r0 in range(0, rows, rb):
        hn = hn_ref[r0:r0 + rb, :]
        gate = _dot(hn, wgb_ref[...])
        gc = (cw_ref[0:1, :] * _shift_rows(gate, prev, 2, shift)
              + cw_ref[1:2, :] * _shift_rows(gate, prev, 1, shift) + cw_ref[2:3, :] * gate + cb_ref[...])
        o_ref[r0:r0 + rb, :] = ((gc * jax.nn.sigmoid(gc)) * _dot(hn, wub_ref[...])).astype(o_ref.dtype)
        prev = gate[rb - 2 * shift:, :]
    new_ref[0] = prev


def _ffn_up(hn, wg, wu, cw, cb, prev, nseq, rows, shift, tn):
    d = hn.shape[1]
    kern = functools.partial(_ffn_up_kernel, shift=shift)
    return pl.pallas_call(
        kern,
        grid=(D_FF // tn, nseq),
        in_specs=[pl.BlockSpec((rows, d), lambda j, b: (b, 0)),
                  pl.BlockSpec((d, tn), lambda j, b: (0, j)),
                  pl.BlockSpec((d, tn), lambda j, b: (0, j)),
                  pl.BlockSpec((CONV_WIDTH, tn), lambda j, b: (0, j)),
                  pl.BlockSpec((1, tn), lambda j, b: (0, j)),
                  pl.BlockSpec((1, 2 * shift, tn), lambda j, b: (b, 0, j))],
        out_specs=[pl.BlockSpec((rows, tn), lambda j, b: (b, j)),
                   pl.BlockSpec((1, 2 * shift, tn), lambda j, b: (b, 0, j))],
        out_shape=[jax.ShapeDtypeStruct((nseq * rows, D_FF), BF16),
                   jax.ShapeDtypeStruct((nseq, 2 * shift, D_FF), F32)],
        scratch_shapes=[pltpu.VMEM((d, tn), BF16), pltpu.VMEM((d, tn), BF16)],
        compiler_params=_params("arbitrary", "arbitrary"),
        name="ffn_up",
    )(hn, wg, wu, cw, cb, prev)


def _layer(x, w, wb, *, nseq, rows, shift, prev_conv, s0, prev_ffn, mk, mv, time_major):
    m = x.shape[0]
    tm = min(m, 1024)
    xn, logf = _norm_gate(x, w["norm_mix"], w["w_g1_t"], w["w_g2"], w["b_gate"], tm)
    conv_out, conv_new = _conv_proj(xn, w["w_in_t"], prev_conv, w["conv_w"], nseq=nseq, shift=shift, tm=tm, tn=512)
    n_gla = N_MAIN - 3 * D_CONV
    p = _col_matmul(xn, w["w_in_t"], 3 * D_CONV, n_gla, F32, tm, 1024)

    if time_major:
        nt = m // shift
        pad = ((0, 0), (0, SUBLANES - nt), (0, 0))
        pg = jnp.pad(p.reshape(nt, shift, n_gla).transpose(1, 0, 2), pad)
        lg = jnp.pad(logf.reshape(nt, shift, -1).transpose(1, 0, 2), pad)
        gla_args = dict(bb=8, rows=SUBLANES, chunk=SUBLANES, group=8)
    else:
        pg, lg = p.reshape(nseq, rows, n_gla), logf.reshape(nseq, rows, -1)
        gla_args = dict(bb=4, rows=128, chunk=GLA_CHUNK, group=4)
    cast = () if wb is not None else (w["w_out"], w["w_xo"], w["w_fd"])
    o, s_new, *made = _gla(pg, lg, s0, w["gla_norm"], q_blk=0, k_blk=1, v_blk=1, r_blk=2, cast=cast, **gla_args)
    if wb is None:
        wb = dict(zip(("w_out", "w_xo", "w_fd"), made))
    if time_major:
        gla_out = o[:, :nt].transpose(1, 0, 2).reshape(m, D_GLA)
    else:
        gla_out = o.reshape(m, D_GLA)

    tm2 = min(m, 512)
    h, hn = _proj_res_norm([conv_out, gla_out], wb["w_out"], x, w["norm_x"], tm=tm2, tk=D_MODEL, final=False)
    if time_major:
        nt = m // shift
        qx = _matmul(hn, w["w_xq"], BF16, tm)
        qb = jnp.pad(qx.astype(F32).reshape(nt, shift, D_MODEL).transpose(1, 0, 2),
                     ((0, 0), (0, SUBLANES - nt), (0, 0)))
        ob = _xattn_cache(qb, mk, mv, bb=4)
        attn = ob[:, :nt].transpose(1, 0, 2).reshape(m, D_MODEL)
    else:
        attn = _xattn(hn.reshape(nseq, rows, D_MODEL), w["w_xq"], mk, mv).reshape(m, D_MODEL)

    h2, hn2 = _proj_res_norm([attn], wb["w_xo"], h, w["norm_ffn"], tm=tm2, tk=D_MODEL, final=False)
    act, ffn_new = _ffn_up(hn2, w["w_fg"], w["w_fu"], w["ffn_conv_w"], w["ffn_conv_b"], prev_ffn,
                           nseq, rows, shift, 512)
    return act, h2, conv_new, s_new, ffn_new, wb


def kernel(x_prompt, x_sample, mem_prompt, cache_conv, state_gla, cache_ffn, cache_mem_k, cache_mem_v,
           norm_mix, w_in, conv_w, w_gate2, b_gate, gla_norm, w_out, norm_x, norm_mem, w_xq, w_xk, w_xv,
           w_xo, norm_ffn, w_ffn_gate, w_ffn_up, ffn_conv_w, ffn_conv_b, w_ffn_down, norm_final):
    depth = w_in.shape[0]
    nb, seq, d = x_prompt.shape
    db, dseq, _ = x_sample.shape
    hp = x_prompt.reshape(nb * seq, d)
    hs = x_sample.transpose(1, 0, 2).reshape(dseq * db, d)
    outs = {k: [] for k in ("conv_p", "gla_p", "ffn_p", "mk", "mv", "conv_s", "gla_s", "ffn_s")}
    nfinal = norm_final.reshape(1, d)
    yp = ys = None
    for l in range(depth):
        w = {
            "norm_mix": norm_mix[l].reshape(1, d),
            "w_in_t": w_in[l].T,
            "w_g1_t": jnp.pad(w_in[l].T[N_MAIN:], ((0, LANES - GLA_RANK), (0, 0))).astype(BF16),
            "w_g2": jnp.pad(w_gate2[l], ((0, LANES - GLA_RANK), (0, 0))).astype(BF16),
            "b_gate": b_gate[l].reshape(1, -1),
            "conv_w": conv_w[l],
            "gla_norm": gla_norm[l].reshape(1, -1),
            "w_out": w_out[l],
            "norm_x": norm_x[l].reshape(1, d),
            "w_xq": w_xq[l],
            "w_xo": w_xo[l],
            "w_fd": w_ffn_down[l],
            "norm_ffn": norm_ffn[l].reshape(1, d),
            "w_fg": w_ffn_gate[l],
            "w_fu": w_ffn_up[l],
            "ffn_conv_w": ffn_conv_w[l],
            "ffn_conv_b": ffn_conv_b[l].reshape(1, -1),
        }
        last = l == depth - 1
        gain_next = nfinal if last else None

        mem = mem_prompt.reshape(nb * MEM_LEN, d)
        nmem = norm_mem[l].reshape(1, d)
        mk, mk_cache = _mem_proj(mem, nmem, w_xk[l], 1024)
        mv, mv_cache = _mem_proj(mem, nmem, w_xv[l], 1024)
        act, h2, c1, s1, f1, wb = _layer(
            hp, w, None, nseq=nb, rows=seq, shift=1,
            prev_conv=jnp.zeros((nb, CONV_WIDTH - 1, D_CONV), F32),
            s0=jnp.zeros((nb, GLA_HEADS, GLA_DK, GLA_DV), F32),
            prev_ffn=jnp.zeros((nb, CONV_WIDTH - 1, D_FF), F32),
            mk=mk.reshape(nb, MEM_LEN, d), mv=mv.reshape(nb, MEM_LEN, d), time_major=False)
        assert last, "only the final layer's epilogue (final rmsnorm) is implemented"
        yp = _proj_res_norm([act], wb["w_fd"], h2, gain_next, tm=256, tk=D_FF, final=True)
        outs["conv_p"].append(c1)
        outs["gla_p"].append(s1)
        outs["ffn_p"].append(f1)
        outs["mk"].append(mk_cache)
        outs["mv"].append(mv_cache)

        def tmajor(c):
            return c.transpose(1, 0, 2).reshape(1, (CONV_WIDTH - 1) * db, c.shape[-1])

        act, h2, c2, s2, f2, _ = _layer(
            hs, w, wb, nseq=1, rows=dseq * db, shift=db,
            prev_conv=tmajor(cache_conv[l]), s0=state_gla[l], prev_ffn=tmajor(cache_ffn[l]),
            mk=cache_mem_k[l], mv=cache_mem_v[l], time_major=True)
        ys = _proj_res_norm([act], wb["w_fd"], h2, gain_next, tm=256, tk=D_FF, final=True)
        outs["conv_s"].append(c2.reshape(CONV_WIDTH - 1, db, D_CONV).transpose(1, 0, 2))
        outs["gla_s"].append(s2)
        outs["ffn_s"].append(f2.reshape(CONV_WIDTH - 1, db, D_FF).transpose(1, 0, 2))

    y_prompt = yp.reshape(nb, seq, d)
    y_sample = ys.reshape(dseq, db, d).transpose(1, 0, 2)
    st = lambda k: jnp.stack(outs[k])
    return (y_prompt, y_sample, st("conv_p"), st("gla_p"), st("ffn_p"), st("mk"), st("mv"),
            st("conv_s"), st("gla_s"), st("ffn_s"))
```

```python
import functools

import jax
import jax.numpy as jnp
from jax import lax
from jax.experimental import pallas as pl
from jax.experimental.pallas import tpu as pltpu

F32 = jnp.float32
BF16 = jnp.bfloat16

D_MODEL = 2048
EPS = 1e-6
CONV_WIDTH = 3
D_CONV = 1024
D_GLA = 1024
GLA_HEADS = 4
GLA_DV = 256
GLA_DK = 128
GLA_RANK = 16
GLA_TAU = 16.0
LOG2_E = 1.4426950408889634
GLA_CHUNK = 64
X_HEADS = 4
X_HD = 512
MEM_LEN = 256
D_FF = 5632
N_MAIN = 3 * D_CONV + 2 * GLA_HEADS * GLA_DK + 2 * GLA_HEADS * GLA_DV

LANES = 128
SUBLANES = 8
VMEM_LIMIT_BYTES = 56 * 1024 * 1024


def _params(*sem):
    return pltpu.CompilerParams(dimension_semantics=sem, vmem_limit_bytes=VMEM_LIMIT_BYTES)


def _dot(a, b):
    return jnp.dot(a, b, preferred_element_type=F32)


def _dot_nt(a, b):
    return lax.dot_general(a, b, (((1,), (1,)), ((), ())), preferred_element_type=F32)


def _dot_tn(a, b):
    return lax.dot_general(a, b, (((0,), (0,)), ((), ())), preferred_element_type=F32)


def _rms_rows(x, g):
    ms = jnp.mean(x * x, axis=-1, keepdims=True)
    return (x * lax.rsqrt(ms + EPS)) * g


def _row_chunk(rows, limit=256):
    for c in (256, 128, 64, 32, 16, 8):
        if c <= limit and rows % c == 0:
            return c
    return rows


NORM_ROWS = 128
FFN_ROW_BLOCK = 512
PROJ_ROW_BLOCK = 256


def _norm_into(x_ref, g_ref, xn_ref):
    rows = x_ref.shape[0]
    ch = _row_chunk(rows)
    g = g_ref[...]

    def body(c, carry):
        r = pl.ds(pl.multiple_of(c * ch, ch), ch)
        xn_ref[r, :] = _rms_rows(x_ref[r, :], g).astype(xn_ref.dtype)
        return carry

    lax.fori_loop(0, rows // ch, body, 0)


def _norm_matmul_kernel(x_ref, g_ref, w_ref, o_ref, oc_ref, xn_ref):
    j = pl.program_id(1)

    @pl.when(j == 0)
    def _():
        _norm_into(x_ref, g_ref, xn_ref)

    o_ref[...] = _dot(xn_ref[...], w_ref[...].astype(BF16)).astype(o_ref.dtype)
    nchunk = X_HD // LANES
    pitch = nchunk * X_HEADS
    heads_per_tile = o_ref.shape[1] // X_HD
    for b in range(oc_ref.shape[0]):
        for hh in range(heads_per_tile):
            for c in range(nchunk):
                col = hh * X_HD + c * LANES
                row = c * X_HEADS + j * heads_per_tile + hh
                oc_ref[b, pl.ds(row, MEM_LEN, stride=pitch), :] = (
                    o_ref[b * MEM_LEN:(b + 1) * MEM_LEN, col:col + LANES])


def _norm_gate_kernel(x_ref, g_ref, wg1_ref, wg2_ref, bg_ref, xn_ref, lf_ref):
    _norm_into(x_ref, g_ref, xn_ref)
    g1 = _dot_nt(xn_ref[...], wg1_ref[...])
    z = _dot(g1.astype(BF16), wg2_ref[...]) + bg_ref[...]
    lf_ref[...] = (jnp.minimum(z, 0.0) - jnp.log1p(jnp.exp(-jnp.abs(z)))) * (LOG2_E / GLA_TAU)


def _mem_proj(mem, gain, w, tn):
    m, d = mem.shape
    n = w.shape[1]
    nb = m // MEM_LEN
    nchunk = X_HD // LANES
    rows = MEM_LEN * nchunk * X_HEADS
    flat, stored = pl.pallas_call(
        _norm_matmul_kernel,
        grid=(1, n // tn),
        in_specs=[pl.BlockSpec((m, d), lambda i, j: (0, 0)),
                  pl.BlockSpec((1, d), lambda i, j: (0, 0)),
                  pl.BlockSpec((d, tn), lambda i, j: (0, j))],
        out_specs=[pl.BlockSpec((m, tn), lambda i, j: (0, j)),
                   pl.BlockSpec((nb, rows, LANES), lambda i, j: (0, 0, 0))],
        out_shape=[jax.ShapeDtypeStruct((m, n), F32),
                   jax.ShapeDtypeStruct((nb, rows, LANES), F32)],
        scratch_shapes=[pltpu.VMEM((m, d), BF16)],
        compiler_params=_params("arbitrary", "arbitrary"),
        name="norm_matmul",
    )(mem, gain, w)
    cache = stored.reshape(nb, MEM_LEN, nchunk, X_HEADS, LANES).transpose(0, 1, 3, 2, 4)
    return flat, cache.reshape(nb, MEM_LEN, X_HEADS, X_HD)


def _norm_gate(x, gain, wg1, wg2, b_gate, tm):
    m, d = x.shape
    ng = wg2.shape[1]
    return pl.pallas_call(
        _norm_gate_kernel,
        grid=(m // tm,),
        in_specs=[pl.BlockSpec((tm, d), lambda i: (i, 0)),
                  pl.BlockSpec((1, d), lambda i: (0, 0)),
                  pl.BlockSpec((LANES, d), lambda i: (0, 0)),
                  pl.BlockSpec((LANES, ng), lambda i: (0, 0)),
                  pl.BlockSpec((1, ng), lambda i: (0, 0))],
        out_specs=[pl.BlockSpec((tm, d), lambda i: (i, 0)),
                   pl.BlockSpec((tm, ng), lambda i: (i, 0))],
        out_shape=[jax.ShapeDtypeStruct((m, d), BF16),
                   jax.ShapeDtypeStruct((m, ng), F32)],
        compiler_params=_params("arbitrary"),
        name="norm_gate",
    )(x, gain, wg1, wg2, b_gate)


def _cast_into(src_ref, dst_ref):
    rows = src_ref.shape[0]
    ch = _row_chunk(rows)

    def body(c, carry):
        r = pl.ds(pl.multiple_of(c * ch, ch), ch)
        dst_ref[r, :] = src_ref[r, :].astype(dst_ref.dtype)
        return carry

    lax.fori_loop(0, rows // ch, body, 0)


def _matmul_kernel(a_ref, w_ref, o_ref, wb_ref):
    @pl.when(pl.program_id(0) == 0)
    def _():
        _cast_into(w_ref, wb_ref)

    o_ref[...] = _dot(a_ref[...], wb_ref[...]).astype(o_ref.dtype)


def _matmul(a, w, out_dtype, tm):
    m, k = a.shape
    n = w.shape[1]
    return pl.pallas_call(
        _matmul_kernel,
        grid=(m // tm,),
        in_specs=[pl.BlockSpec((tm, k), lambda i: (i, 0)),
                  pl.BlockSpec((k, n), lambda i: (0, 0), pipeline_mode=pl.Buffered(1))],
        out_specs=pl.BlockSpec((tm, n), lambda i: (i, 0)),
        out_shape=jax.ShapeDtypeStruct((m, n), out_dtype),
        scratch_shapes=[pltpu.VMEM((k, n), BF16)],
        compiler_params=_params("arbitrary"),
        name="matmul",
    )(a, w)


def _col_matmul_kernel(a_ref, wt_ref, o_ref, wb_ref):
    @pl.when(pl.program_id(1) == 0)
    def _():
        _cast_into(wt_ref, wb_ref)

    o_ref[...] = _dot_nt(a_ref[...], wb_ref[...]).astype(o_ref.dtype)


def _col_matmul(a, wt, col0, ncols, out_dtype, tm, tn):
    m, k = a.shape
    j0 = col0 // tn
    return pl.pallas_call(
        _col_matmul_kernel,
        grid=(ncols // tn, m // tm),
        in_specs=[pl.BlockSpec((tm, k), lambda j, i: (i, 0)),
                  pl.BlockSpec((tn, k), lambda j, i: (j0 + j, 0))],
        out_specs=pl.BlockSpec((tm, tn), lambda j, i: (i, j)),
        out_shape=jax.ShapeDtypeStruct((m, ncols), out_dtype),
        scratch_shapes=[pltpu.VMEM((tn, k), BF16)],
        compiler_params=_params("arbitrary", "arbitrary"),
        name="col_matmul",
    )(a, wt)


def _halo_rows(shift):
    return max(2 * shift, SUBLANES)


def _conv_buf(rows, tn, shift):
    return pltpu.VMEM((tn // LANES, _halo_rows(shift) + rows, LANES), F32)


def _slab(s):
    return slice(s * LANES, (s + 1) * LANES)


def _conv_stage(buf_ref, row0, x):
    for s in range(buf_ref.shape[0]):
        buf_ref[s, row0:row0 + x.shape[0], :] = x[:, _slab(s)]


def _conv_taps(buf_ref, s, r0, rows, shift):
    halo = _halo_rows(shift)

    def back(steps):
        start = r0 + (halo - steps * shift)
        if (steps * shift) % SUBLANES == 0:
            return pl.ds(start if isinstance(start, int) else pl.multiple_of(start, SUBLANES), rows)
        return pl.ds(start, rows, stride=1)

    return buf_ref[s, back(2), :], buf_ref[s, back(1), :], buf_ref[s, back(0), :]


def _conv_last(buf_ref, new_ref, rows, shift):
    halo = _halo_rows(shift)
    for s in range(buf_ref.shape[0]):
        new_ref[0, :, _slab(s)] = buf_ref[s, halo + rows - 2 * shift:halo + rows, :]


def _conv_proj_kernel(xn_ref, wbg_ref, wcg_ref, wvc_ref, prev_ref, cw_ref, o_ref, new_ref,
                      bgb_ref, cgb_ref, vcb_ref, buf_ref, *, shift, tiles_per_seq):
    rows = xn_ref.shape[0]
    halo = _halo_rows(shift)
    i = pl.program_id(1)

    @pl.when(i == 0)
    def _():
        _cast_into(wbg_ref, bgb_ref)
        _cast_into(wcg_ref, cgb_ref)
        _cast_into(wvc_ref, vcb_ref)

    def from_cache():
        _conv_stage(buf_ref, halo - 2 * shift, prev_ref[0])

    if tiles_per_seq == 1:
        from_cache()
    else:
        pl.when(i % tiles_per_seq == 0)(from_cache)

        @pl.when(i % tiles_per_seq != 0)
        def _():
            for s in range(buf_ref.shape[0]):
                buf_ref[s, halo - 2 * shift:halo, :] = buf_ref[s, halo + rows - 2 * shift:halo + rows, :]

    xn = xn_ref[...]
    _conv_stage(buf_ref, halo, _dot_nt(xn, cgb_ref[...]) * _dot_nt(xn, vcb_ref[...]))
    bg = _dot_nt(xn, bgb_ref[...])
    for s in range(buf_ref.shape[0]):
        u2, u1, u0 = _conv_taps(buf_ref, s, 0, rows, shift)
        y = cw_ref[0:1, _slab(s)] * u2 + cw_ref[1:2, _slab(s)] * u1 + cw_ref[2:3, _slab(s)] * u0
        o_ref[:, _slab(s)] = (bg[:, _slab(s)] * y).astype(o_ref.dtype)
    _conv_last(buf_ref, new_ref, rows, shift)


def _conv_proj(xn, w_in_t, prev, conv_w, *, nseq, shift, tm, tn):
    m, d = xn.shape
    nj = D_CONV // tn
    tiles_per_seq = m // (nseq * tm)
    kern = functools.partial(_conv_proj_kernel, shift=shift, tiles_per_seq=tiles_per_seq)
    state = pl.BlockSpec((1, 2 * shift, tn), lambda j, i: (i // tiles_per_seq, 0, j))
    wb = pltpu.VMEM((tn, d), BF16)
    return pl.pallas_call(
        kern,
        grid=(nj, m // tm),
        in_specs=[pl.BlockSpec((tm, d), lambda j, i: (i, 0)),
                  pl.BlockSpec((tn, d), lambda j, i: (j, 0)),
                  pl.BlockSpec((tn, d), lambda j, i: (j + nj, 0)),
                  pl.BlockSpec((tn, d), lambda j, i: (j + 2 * nj, 0)),
                  state,
                  pl.BlockSpec((CONV_WIDTH, tn), lambda j, i: (0, j))],
        out_specs=[pl.BlockSpec((tm, tn), lambda j, i: (i, j)),
                   state],
        out_shape=[jax.ShapeDtypeStruct((m, D_CONV), BF16),
                   jax.ShapeDtypeStruct((nseq, 2 * shift, D_CONV), F32)],
        scratch_shapes=[wb, wb, wb, _conv_buf(tm, tn, shift)],
        compiler_params=_params("arbitrary", "arbitrary"),
        name="conv_proj",
    )(xn, w_in_t, w_in_t, w_in_t, prev, conv_w)


def _cumsum_rows(g):
    c = g.shape[0]
    row = lax.broadcasted_iota(jnp.int32, g.shape, 0)
    x = g
    s = 1
    while s < c:
        x = x + jnp.where(row >= s, pltpu.roll(x, s, 0), 0.0)
        s *= 2
    return x


def _bcast_block_row(x, s, k):
    c, lanes = x.shape
    if s == c:
        return jnp.broadcast_to(x[k:k + 1, :], x.shape)
    if s >= SUBLANES:
        y = x.reshape(c // s, s, lanes)
        return jnp.broadcast_to(y[:, k:k + 1, :], y.shape).reshape(c, lanes)
    y = x.reshape(c // SUBLANES, SUBLANES, lanes)
    sub = lax.broadcasted_iota(jnp.int32, y.shape, 1)
    out = None
    for blk in range(SUBLANES // s):
        src = jnp.broadcast_to(y[:, blk * s + k:blk * s + k + 1, :], y.shape)
        out = src if out is None else jnp.where(sub >= blk * s, src, out)
    return out.reshape(c, lanes)


def _gla_pair_masks(c):
    ri = lax.broadcasted_iota(jnp.int32, (c, c), 0)
    ci = lax.broadcasted_iota(jnp.int32, (c, c), 1)
    diff_bits = ri ^ ci
    masks = [diff_bits == 0]
    level = 0
    while (1 << level) < c:
        masks.append(((diff_bits >> level) == 1) & (((ri >> level) & 1) == 1))
        level += 1
    return masks


def _gla_chunk(q, k, v, g, s_prev, masks):
    c = q.shape[0]
    cum = _cumsum_rows(g)
    a = jnp.where(masks[0], _dot_nt(q.astype(BF16), k.astype(BF16)), 0.0)
    for level in range(len(masks) - 1):
        half = 1 << level
        ref = _bcast_block_row(cum, 2 * half, half - 1)
        d = cum - ref
        up = jnp.minimum(d, 0.0)
        qe = q * jnp.exp2(up)
        ke = k * jnp.exp2(up - d)
        a = a + jnp.where(masks[1 + level], _dot_nt(qe.astype(BF16), ke.astype(BF16)), 0.0)
    o = _dot(a.astype(BF16), v.astype(BF16)) + _dot((q * jnp.exp2(cum)).astype(BF16), s_prev.astype(BF16))
    last = cum[c - 1:c, :]
    kd = k * jnp.exp2(last - cum)
    dk = last.shape[1]
    decay_t = jnp.transpose(jnp.broadcast_to(jnp.exp2(last), (dk, dk)))
    decayed = jnp.concatenate([decay_t * s_prev[:, i:i + dk] for i in range(0, s_prev.shape[1], dk)], axis=1)
    s_new = decayed + _dot_tn(kd.astype(BF16), v.astype(BF16))
    return o, s_new


def _gla_kernel(q_ref, k_ref, v_ref, r_ref, g_ref, s0_ref, gn_ref, *refs, chunk, single_chunk, group, n_cast):
    cast_src, (o_ref, sn_ref), cast_dst = refs[:n_cast], refs[n_cast:n_cast + 2], refs[n_cast + 2:]
    bb, rows = q_ref.shape[0], q_ref.shape[1]
    nchunk = rows // chunk
    state_in = s0_ref if single_chunk else sn_ref

    if not single_chunk:
        @pl.when(pl.program_id(1) == 0)
        def _():
            sn_ref[...] = s0_ref[...]

    masks = _gla_pair_masks(chunk)

    def one(b, r):
        for h in range(GLA_HEADS):
            kc = slice(h * GLA_DK, (h + 1) * GLA_DK)
            vc = slice(h * GLA_DV, (h + 1) * GLA_DV)
            q = q_ref[b, r, kc] * (GLA_DK ** -0.5)
            o, s_new = _gla_chunk(q, k_ref[b, r, kc], v_ref[b, r, vc], g_ref[b, r, kc], state_in[b, h], masks)
            sn_ref[b, h] = s_new
            rr = r_ref[b, r, vc]
            o_ref[b, r, vc] = (_rms_rows(o, gn_ref[:, vc]) * (rr * jax.nn.sigmoid(rr))).astype(o_ref.dtype)

    def body(n, carry):
        r = pl.ds(pl.multiple_of((n % nchunk) * chunk, chunk), chunk)
        for u in range(group):
            one((n // nchunk) * group + u, r)
        return carry

    lax.fori_loop(0, (bb // group) * nchunk, body, 0)
    for src, dst in zip(cast_src, cast_dst):
        _cast_into(src, dst)


def _gla(p, logf, s0, gla_norm, *, q_blk, k_blk, v_blk, r_blk, bb, rows, chunk, group, cast=()):
    nb, t, _ = p.shape
    nk, nv = GLA_HEADS * GLA_DK, GLA_HEADS * GLA_DV
    nt = t // rows
    nsteps = (nb // bb) * nt
    kern = functools.partial(_gla_kernel, chunk=chunk, single_chunk=(t == chunk), group=group, n_cast=len(cast))
    state_spec = pl.BlockSpec((bb, GLA_HEADS, GLA_DK, GLA_DV), lambda b, c: (b, 0, 0, 0))
    cast_specs = [pl.BlockSpec((w.shape[0] // nsteps, w.shape[1]), lambda b, c: (b * nt + c, 0)) for w in cast]
    return pl.pallas_call(
        kern,
        grid=(nb // bb, nt),
        in_specs=[pl.BlockSpec((bb, rows, nk), lambda b, c: (b, c, q_blk)),
                  pl.BlockSpec((bb, rows, nk), lambda b, c: (b, c, k_blk)),
                  pl.BlockSpec((bb, rows, nv), lambda b, c: (b, c, v_blk)),
                  pl.BlockSpec((bb, rows, nv), lambda b, c: (b, c, r_blk)),
                  pl.BlockSpec((bb, rows, nk), lambda b, c: (b, c, 0)),
                  state_spec,
                  pl.BlockSpec((1, nv), lambda b, c: (0, 0))] + cast_specs,
        out_specs=[pl.BlockSpec((bb, rows, nv), lambda b, c: (b, c, 0)),
                   state_spec] + cast_specs,
        out_shape=[jax.ShapeDtypeStruct((nb, t, D_GLA), BF16),
                   jax.ShapeDtypeStruct((nb, GLA_HEADS, GLA_DK, GLA_DV), F32)]
                  + [jax.ShapeDtypeStruct(w.shape, BF16) for w in cast],
        compiler_params=_params("arbitrary", "arbitrary"),
        name="gla",
    )(p, p, p, p, logf, s0, gla_norm, *cast)


def _proj_res_norm_kernel(*refs, n_a, nk, final):
    a_refs, (w_ref, res_ref, g_ref) = refs[:n_a], refs[n_a:n_a + 3]
    out_refs, acc_ref = (refs[n_a + 3:], None) if nk == 1 else (refs[n_a + 3:-1], refs[-1])
    k = pl.program_id(1)

    def product(rows):
        part, r0 = None, 0
        for a_ref in a_refs:
            kw = a_ref.shape[1]
            term = _dot(a_ref[rows, :], w_ref[r0:r0 + kw, :])
            part = term if part is None else part + term
            r0 += kw
        return part

    tm = res_ref.shape[0]
    if nk == 1:
        rb = min(tm, PROJ_ROW_BLOCK)
        ch = _row_chunk(rb, NORM_ROWS)
        for b0 in range(0, tm, rb):
            part = product(slice(b0, b0 + rb))
            for c0 in range(0, rb, ch):
                r = slice(b0 + c0, b0 + c0 + ch)
                h = res_ref[r, :] + part[c0:c0 + ch, :]
                hn = _rms_rows(h, g_ref[...])
                if final:
                    out_refs[0][r, :] = hn
                else:
                    out_refs[0][r, :] = h
                    out_refs[1][r, :] = hn.astype(out_refs[1].dtype)
        return

    part = product(slice(None))

    @pl.when(k == 0)
    def _():
        acc_ref[...] = part

    if nk > 1:
        @pl.when(k > 0)
        def _():
            acc_ref[...] += part

    @pl.when(k == nk - 1)
    def _():
        rows = acc_ref.shape[0]
        ch = _row_chunk(rows, NORM_ROWS)
        g = g_ref[...]

        def body(c, carry):
            r = pl.ds(pl.multiple_of(c * ch, ch), ch)
            h = res_ref[r, :] + acc_ref[r, :]
            hn = _rms_rows(h, g)
            if final:
                out_refs[0][r, :] = hn
            else:
                out_refs[0][r, :] = h
                out_refs[1][r, :] = hn.astype(out_refs[1].dtype)
            return carry

        lax.fori_loop(0, rows // ch, body, 0)


def _proj_res_norm(a_list, w, res, gain, *, tm, tk, final):
    m = a_list[0].shape[0]
    kdim, d = w.shape
    nk = kdim // tk
    assert len(a_list) == 1 or nk == 1
    kern = functools.partial(_proj_res_norm_kernel, n_a=len(a_list), nk=nk, final=final)
    a_specs = ([pl.BlockSpec((tm, tk), lambda i, k: (i, k))] if len(a_list) == 1 else
               [pl.BlockSpec((tm, a.shape[1]), lambda i, k: (i, 0)) for a in a_list])
    row_spec = pl.BlockSpec((tm, d), lambda i, k: (i, 0))
    if final:
        out_specs = row_spec
        out_shape = jax.ShapeDtypeStruct((m, d), F32)
    else:
        out_specs = [row_spec, row_spec]
        out_shape = [jax.ShapeDtypeStruct((m, d), F32), jax.ShapeDtypeStruct((m, d), BF16)]
    return pl.pallas_call(
        kern,
        grid=(m // tm, nk),
        in_specs=a_specs + [pl.BlockSpec((tk, d), lambda i, k: (k, 0),
                                         pipeline_mode=pl.Buffered(1 if nk == 1 else 2)),
                            row_spec,
                            pl.BlockSpec((1, d), lambda i, k: (0, 0))],
        out_specs=out_specs,
        out_shape=out_shape,
        scratch_shapes=[] if nk == 1 else [pltpu.VMEM((tm, d), F32)],
        compiler_params=_params("arbitrary", "arbitrary"),
        name="proj_res_norm",
    )(*a_list, w, res, gain)


def _xattn_kernel(hn_ref, wq_ref, k_ref, v_ref, o_ref, wb_ref):
    @pl.when(pl.program_id(1) == 0)
    def _():
        _cast_into(wq_ref, wb_ref)

    q = _dot(hn_ref[0], wb_ref[...]).astype(BF16)
    p = _softmax_rows(_dot_nt(q, k_ref[0].astype(BF16)) * (X_HD ** -0.5))
    o_ref[0] = _dot(p.astype(BF16), v_ref[0].astype(BF16)).astype(o_ref.dtype)


def _xattn(hn, w_xq, mk, mv):
    nb, t, d = hn.shape
    return pl.pallas_call(
        _xattn_kernel,
        grid=(X_HEADS, nb),
        in_specs=[pl.BlockSpec((1, t, d), lambda h, b: (b, 0, 0)),
                  pl.BlockSpec((d, X_HD), lambda h, b: (0, h)),
                  pl.BlockSpec((1, MEM_LEN, X_HD), lambda h, b: (b, 0, h)),
                  pl.BlockSpec((1, MEM_LEN, X_HD), lambda h, b: (b, 0, h))],
        out_specs=pl.BlockSpec((1, t, X_HD), lambda h, b: (b, 0, h)),
        out_shape=jax.ShapeDtypeStruct((nb, t, d), BF16),
        scratch_shapes=[pltpu.VMEM((d, X_HD), BF16)],
        compiler_params=_params("arbitrary", "arbitrary"),
        name="xattn",
    )(hn, w_xq, mk, mv)


def _softmax_rows(s):
    s = s - jnp.max(s, axis=-1, keepdims=True)
    e = jnp.exp(s)
    return e / jnp.sum(e, axis=-1, keepdims=True)


def _xattn_cache_kernel(q_ref, k_ref, v_ref, o_ref):
    bb, tq = q_ref.shape[0], q_ref.shape[1]
    nchunk = X_HD // LANES
    pitch = nchunk * X_HEADS

    def gather(ref, b, h):
        parts = [ref[b, pl.ds(c * X_HEADS + h, MEM_LEN, stride=pitch), :] for c in range(nchunk)]
        return jnp.concatenate(parts, axis=1).astype(BF16)

    pairs = [(b, h) for b in range(bb) for h in range(X_HEADS)]
    scores = [_dot_nt(q_ref[b, :, h * X_HD:(h + 1) * X_HD].astype(BF16), gather(k_ref, b, h)) for b, h in pairs]
    p = _softmax_rows(jnp.concatenate(scores, axis=0) * (X_HD ** -0.5)).astype(BF16)
    for n, (b, h) in enumerate(pairs):
        o_ref[b, :, h * X_HD:(h + 1) * X_HD] = _dot(p[n * tq:(n + 1) * tq], gather(v_ref, b, h)).astype(o_ref.dtype)


def _xattn_cache(q, ck, cv, *, bb):
    nb, tq, d = q.shape
    nchunk = X_HD // LANES

    def stored_order(c):
        c = c.reshape(nb, MEM_LEN, X_HEADS, nchunk, LANES).transpose(0, 1, 3, 2, 4)
        return c.reshape(nb, MEM_LEN * nchunk * X_HEADS, LANES)

    rows = MEM_LEN * nchunk * X_HEADS
    return pl.pallas_call(
        _xattn_cache_kernel,
        grid=(nb // bb,),
        in_specs=[pl.BlockSpec((bb, tq, d), lambda b: (b, 0, 0)),
                  pl.BlockSpec((bb, rows, LANES), lambda b: (b, 0, 0)),
                  pl.BlockSpec((bb, rows, LANES), lambda b: (b, 0, 0))],
        out_specs=pl.BlockSpec((bb, tq, d), lambda b: (b, 0, 0)),
        out_shape=jax.ShapeDtypeStruct((nb, tq, d), BF16),
        compiler_params=_params("arbitrary"),
        name="xattn_cache",
    )(q, stored_order(ck), stored_order(cv))


def _shift_rows(x, prev, steps, shift):
    n = steps * shift
    head = prev[2 * shift - n:, :]
    if n == x.shape[0]:
        return head
    if n % SUBLANES == 0:
        return jnp.concatenate([head, x[:-n, :]], axis=0)
    rolled = pltpu.roll(x, n, 0)
    row = lax.broadcasted_iota(jnp.int32, (SUBLANES, x.shape[1]), 0)
    first = rolled[:SUBLANES, :]
    for i in range(n):
        first = jnp.where(row == i, head[i:i + 1, :], first)
    return jnp.concatenate([first, rolled[SUBLANES:, :]], axis=0)


def _ffn_up_kernel(hn_ref, wg_ref, wu_ref, cw_ref, cb_ref, prev_ref, o_ref, new_ref, wgb_ref, wub_ref, *,
                   shift):
    rows = hn_ref.shape[0]

    @pl.when(pl.program_id(1) == 0)
    def _():
        _cast_into(wg_ref, wgb_ref)
        _cast_into(wu_ref, wub_ref)

    rb = max(FFN_ROW_BLOCK, 2 * shift)
    prev = prev_ref[0]
    for r0 in range(0, rows, rb):
        hn = hn_ref[r0:r0 + rb, :]
        gate = _dot(hn, wgb_ref[...])
        gc = (cw_ref[0:1, :] * _shift_rows(gate, prev, 2, shift)
              + cw_ref[1:2, :] * _shift_rows(gate, prev, 1, shift) + cw_ref[2:3, :] * gate + cb_ref[...])
        o_ref[r0:r0 + rb, :] = ((gc * jax.nn.sigmoid(gc)) * _dot(hn, wub_ref[...])).astype(o_ref.dtype)
        prev = gate[rb - 2 * shift:, :]
    new_ref[0] = prev


def _ffn_up(hn, wg, wu, cw, cb, prev, nseq, rows, shift, tn):
    d = hn.shape[1]
    kern = functools.partial(_ffn_up_kernel, shift=shift)
    return pl.pallas_call(
        kern,
        grid=(D_FF // tn, nseq),
        in_specs=[pl.BlockSpec((rows, d), lambda j, b: (b, 0)),
                  pl.BlockSpec((d, tn), lambda j, b: (0, j)),
                  pl.BlockSpec((d, tn), lambda j, b: (0, j)),
                  pl.BlockSpec((CONV_WIDTH, tn), lambda j, b: (0, j)),
                  pl.BlockSpec((1, tn), lambda j, b: (0, j)),
                  pl.BlockSpec((1, 2 * shift, tn), lambda j, b: (b, 0, j))],
        out_specs=[pl.BlockSpec((rows, tn), lambda j, b: (b, j)),
                   pl.BlockSpec((1, 2 * shift, tn), lambda j, b: (b, 0, j))],
        out_shape=[jax.ShapeDtypeStruct((nseq * rows, D_FF), BF16),
                   jax.ShapeDtypeStruct((nseq, 2 * shift, D_FF), F32)],
        scratch_shapes=[pltpu.VMEM((d, tn), BF16), pltpu.VMEM((d, tn), BF16)],
        compiler_params=_params("arbitrary", "arbitrary"),
        name="ffn_up",
    )(hn, wg, wu, cw, cb, prev)


def _layer(x, w, wb, *, nseq, rows, shift, prev_conv, s0, prev_ffn, mk, mv, time_major):
    m = x.shape[0]
    tm = min(m, 1024)
    xn, logf = _norm_gate(x, w["norm_mix"], w["w_g1_t"], w["w_g2"], w["b_gate"], tm)
    conv_out, conv_new = _conv_proj(xn, w["w_in_t"], prev_conv, w["conv_w"], nseq=nseq, shift=shift, tm=tm, tn=512)
    n_gla = N_MAIN - 3 * D_CONV
    p = _col_matmul(xn, w["w_in_t"], 3 * D_CONV, n_gla, F32, tm, 1024)

    if time_major:
        nt = m // shift
        pad = ((0, 0), (0, SUBLANES - nt), (0, 0))
        pg = jnp.pad(p.reshape(nt, shift, n_gla).transpose(1, 0, 2), pad)
        lg = jnp.pad(logf.reshape(nt, shift, -1).transpose(1, 0, 2), pad)
        gla_args = dict(bb=8, rows=SUBLANES, chunk=SUBLANES, group=8)
    else:
        pg, lg = p.reshape(nseq, rows, n_gla), logf.reshape(nseq, rows, -1)
        gla_args = dict(bb=4, rows=128, chunk=GLA_CHUNK, group=4)
    cast = () if wb is not None else (w["w_out"], w["w_xo"], w["w_fd"])
    o, s_new, *made = _gla(pg, lg, s0, w["gla_norm"], q_blk=0, k_blk=1, v_blk=1, r_blk=2, cast=cast, **gla_args)
    if wb is None:
        wb = dict(zip(("w_out", "w_xo", "w_fd"), made))
    if time_major:
        gla_out = o[:, :nt].transpose(1, 0, 2).reshape(m, D_GLA)
    else:
        gla_out = o.reshape(m, D_GLA)

    tm2 = min(m, 512)
    h, hn = _proj_res_norm([conv_out, gla_out], wb["w_out"], x, w["norm_x"], tm=tm2, tk=D_MODEL, final=False)
    if time_major:
        nt = m // shift
        qx = _matmul(hn, w["w_xq"], BF16, tm)
        qb = jnp.pad(qx.astype(F32).reshape(nt, shift, D_MODEL).transpose(1, 0, 2),
                     ((0, 0), (0, SUBLANES - nt), (0, 0)))
        ob = _xattn_cache(qb, mk, mv, bb=4)
        attn = ob[:, :nt].transpose(1, 0, 2).reshape(m, D_MODEL)
    else:
        attn = _xattn(hn.reshape(nseq, rows, D_MODEL), w["w_xq"], mk, mv).reshape(m, D_MODEL)

    h2, hn2 = _proj_res_norm([attn], wb["w_xo"], h, w["norm_ffn"], tm=tm2, tk=D_MODEL, final=False)
    act, ffn_new = _ffn_up(hn2, w["w_fg"], w["w_fu"], w["ffn_conv_w"], w["ffn_conv_b"], prev_ffn,
                           nseq, rows, shift, 512)
    return act, h2, conv_new, s_new, ffn_new, wb


def kernel(x_prompt, x_sample, mem_prompt, cache_conv, state_gla, cache_ffn, cache_mem_k, cache_mem_v,
           norm_mix, w_in, conv_w, w_gate2, b_gate, gla_norm, w_out, norm_x, norm_mem, w_xq, w_xk, w_xv,
           w_xo, norm_ffn, w_ffn_gate, w_ffn_up, ffn_conv_w, ffn_conv_b, w_ffn_down, norm_final):
    depth = w_in.shape[0]
    nb, seq, d = x_prompt.shape
    db, dseq, _ = x_sample.shape
    hp = x_prompt.reshape(nb * seq, d)
    hs = x_sample.transpose(1, 0, 2).reshape(dseq * db, d)
    outs = {k: [] for k in ("conv_p", "gla_p", "ffn_p", "mk", "mv", "conv_s", "gla_s", "ffn_s")}
    nfinal = norm_final.reshape(1, d)
    yp = ys = None
    for l in range(depth):
        w = {
            "norm_mix": norm_mix[l].reshape(1, d),
            "w_in_t": w_in[l].T,
            "w_g1_t": jnp.pad(w_in[l].T[N_MAIN:], ((0, LANES - GLA_RANK), (0, 0))).astype(BF16),
            "w_g2": jnp.pad(w_gate2[l], ((0, LANES - GLA_RANK), (0, 0))).astype(BF16),
            "b_gate": b_gate[l].reshape(1, -1),
            "conv_w": conv_w[l],
            "gla_norm": gla_norm[l].reshape(1, -1),
            "w_out": w_out[l],
            "norm_x": norm_x[l].reshape(1, d),
            "w_xq": w_xq[l],
            "w_xo": w_xo[l],
            "w_fd": w_ffn_down[l],
            "norm_ffn": norm_ffn[l].reshape(1, d),
            "w_fg": w_ffn_gate[l],
            "w_fu": w_ffn_up[l],
            "ffn_conv_w": ffn_conv_w[l],
            "ffn_conv_b": ffn_conv_b[l].reshape(1, -1),
        }
        last = l == depth - 1
        gain_next = nfinal if last else None

        mem = mem_prompt.reshape(nb * MEM_LEN, d)
        nmem = norm_mem[l].reshape(1, d)
        mk, mk_cache = _mem_proj(mem, nmem, w_xk[l], 1024)
        mv, mv_cache = _mem_proj(mem, nmem, w_xv[l], 1024)
        act, h2, c1, s1, f1, wb = _layer(
            hp, w, None, nseq=nb, rows=seq, shift=1,
            prev_conv=jnp.zeros((nb, CONV_WIDTH - 1, D_CONV), F32),
            s0=jnp.zeros((nb, GLA_HEADS, GLA_DK, GLA_DV), F32),
            prev_ffn=jnp.zeros((nb, CONV_WIDTH - 1, D_FF), F32),
            mk=mk.reshape(nb, MEM_LEN, d), mv=mv.reshape(nb, MEM_LEN, d), time_major=False)
        assert last, "only the final layer's epilogue (final rmsnorm) is implemented"
        yp = _proj_res_norm([act], wb["w_fd"], h2, gain_next, tm=512, tk=D_FF, final=True)
        outs["conv_p"].append(c1)
        outs["gla_p"].append(s1)
        outs["ffn_p"].append(f1)
        outs["mk"].append(mk_cache)
        outs["mv"].append(mv_cache)

        def tmajor(c):
            return c.transpose(1, 0, 2).reshape(1, (CONV_WIDTH - 1) * db, c.shape[-1])

        act, h2, c2, s2, f2, _ = _layer(
            hs, w, wb, nseq=1, rows=dseq * db, shift=db,
            prev_conv=tmajor(cache_conv[l]), s0=state_gla[l], prev_ffn=tmajor(cache_ffn[l]),
            mk=cache_mem_k[l], mv=cache_mem_v[l], time_major=True)
        ys = _proj_res_norm([act], wb["w_fd"], h2, gain_next, tm=512, tk=D_FF, final=True)
        outs["conv_s"].append(c2.reshape(CONV_WIDTH - 1, db, D_CONV).transpose(1, 0, 2))
        outs["gla_s"].append(s2)
        outs["ffn_s"].append(f2.reshape(CONV_WIDTH - 1, db, D_FF).transpose(1, 0, 2))

    y_prompt = yp.reshape(nb, seq, d)
    y_sample = ys.reshape(dseq, db, d).transpose(1, 0, 2)
    st = lambda k: jnp.stack(outs[k])
    return (y_prompt, y_sample, st("conv_p"), st("gla_p"), st("ffn_p"), st("mk"), st("mv"),
            st("conv_s"), st("gla_s"), st("ffn_s"))
```

```python
import functools

import jax
import jax.numpy as jnp
from jax import lax
from jax.experimental import pallas as pl
from jax.experimental.pallas import tpu as pltpu

F32 = jnp.float32
BF16 = jnp.bfloat16

D_MODEL = 2048
EPS = 1e-6
CONV_WIDTH = 3
D_CONV = 1024
D_GLA = 1024
GLA_HEADS = 4
GLA_DV = 256
GLA_DK = 128
GLA_RANK = 16
GLA_TAU = 16.0
LOG2_E = 1.4426950408889634
GLA_CHUNK = 64
X_HEADS = 4
X_HD = 512
MEM_LEN = 256
D_FF = 5632
N_MAIN = 3 * D_CONV + 2 * GLA_HEADS * GLA_DK + 2 * GLA_HEADS * GLA_DV

LANES = 128
SUBLANES = 8
VMEM_LIMIT_BYTES = 56 * 1024 * 1024


def _params(*sem):
    return pltpu.CompilerParams(dimension_semantics=sem, vmem_limit_bytes=VMEM_LIMIT_BYTES)


def _dot(a, b):
    return jnp.dot(a, b, preferred_element_type=F32)


def _dot_nt(a, b):
    return lax.dot_general(a, b, (((1,), (1,)), ((), ())), preferred_element_type=F32)


def _dot_tn(a, b):
    return lax.dot_general(a, b, (((0,), (0,)), ((), ())), preferred_element_type=F32)


def _rms_rows(x, g):
    ms = jnp.mean(x * x, axis=-1, keepdims=True)
    return (x * lax.rsqrt(ms + EPS)) * g


def _row_chunk(rows, limit=256):
    for c in (256, 128, 64, 32, 16, 8):
        if c <= limit and rows % c == 0:
            return c
    return rows


NORM_ROWS = 128
FFN_ROW_BLOCK = 512
PROJ_ROW_BLOCK = 256


def _norm_into(x_ref, g_ref, xn_ref):
    rows = x_ref.shape[0]
    ch = _row_chunk(rows)
    g = g_ref[...]

    def body(c, carry):
        r = pl.ds(pl.multiple_of(c * ch, ch), ch)
        xn_ref[r, :] = _rms_rows(x_ref[r, :], g).astype(xn_ref.dtype)
        return carry

    lax.fori_loop(0, rows // ch, body, 0)


def _norm_matmul_kernel(x_ref, g_ref, w_ref, o_ref, oc_ref, xn_ref):
    j = pl.program_id(1)

    @pl.when(j == 0)
    def _():
        _norm_into(x_ref, g_ref, xn_ref)

    o_ref[...] = _dot(xn_ref[...], w_ref[...].astype(BF16)).astype(o_ref.dtype)
    nchunk = X_HD // LANES
    pitch = nchunk * X_HEADS
    heads_per_tile = o_ref.shape[1] // X_HD
    for b in range(oc_ref.shape[0]):
        for hh in range(heads_per_tile):
            for c in range(nchunk):
                col = hh * X_HD + c * LANES
                row = c * X_HEADS + j * heads_per_tile + hh
                oc_ref[b, pl.ds(row, MEM_LEN, stride=pitch), :] = (
                    o_ref[b * MEM_LEN:(b + 1) * MEM_LEN, col:col + LANES])


def _norm_gate_kernel(x_ref, g_ref, wg1_ref, wg2_ref, bg_ref, xn_ref, lf_ref):
    _norm_into(x_ref, g_ref, xn_ref)
    g1 = _dot_nt(xn_ref[...], wg1_ref[...])
    z = _dot(g1.astype(BF16), wg2_ref[...]) + bg_ref[...]
    lf_ref[...] = (jnp.minimum(z, 0.0) - jnp.log1p(jnp.exp(-jnp.abs(z)))) * (LOG2_E / GLA_TAU)


def _mem_proj(mem, gain, w, tn):
    m, d = mem.shape
    n = w.shape[1]
    nb = m // MEM_LEN
    nchunk = X_HD // LANES
    rows = MEM_LEN * nchunk * X_HEADS
    flat, stored = pl.pallas_call(
        _norm_matmul_kernel,
        grid=(1, n // tn),
        in_specs=[pl.BlockSpec((m, d), lambda i, j: (0, 0)),
                  pl.BlockSpec((1, d), lambda i, j: (0, 0)),
                  pl.BlockSpec((d, tn), lambda i, j: (0, j))],
        out_specs=[pl.BlockSpec((m, tn), lambda i, j: (0, j)),
                   pl.BlockSpec((nb, rows, LANES), lambda i, j: (0, 0, 0))],
        out_shape=[jax.ShapeDtypeStruct((m, n), F32),
                   jax.ShapeDtypeStruct((nb, rows, LANES), F32)],
        scratch_shapes=[pltpu.VMEM((m, d), BF16)],
        compiler_params=_params("arbitrary", "arbitrary"),
        name="norm_matmul",
    )(mem, gain, w)
    cache = stored.reshape(nb, MEM_LEN, nchunk, X_HEADS, LANES).transpose(0, 1, 3, 2, 4)
    return flat, cache.reshape(nb, MEM_LEN, X_HEADS, X_HD)


def _norm_gate(x, gain, wg1, wg2, b_gate, tm):
    m, d = x.shape
    ng = wg2.shape[1]
    return pl.pallas_call(
        _norm_gate_kernel,
        grid=(m // tm,),
        in_specs=[pl.BlockSpec((tm, d), lambda i: (i, 0)),
                  pl.BlockSpec((1, d), lambda i: (0, 0)),
                  pl.BlockSpec((LANES, d), lambda i: (0, 0)),
                  pl.BlockSpec((LANES, ng), lambda i: (0, 0)),
                  pl.BlockSpec((1, ng), lambda i: (0, 0))],
        out_specs=[pl.BlockSpec((tm, d), lambda i: (i, 0)),
                   pl.BlockSpec((tm, ng), lambda i: (i, 0))],
        out_shape=[jax.ShapeDtypeStruct((m, d), BF16),
                   jax.ShapeDtypeStruct((m, ng), F32)],
        compiler_params=_params("arbitrary"),
        name="norm_gate",
    )(x, gain, wg1, wg2, b_gate)


def _cast_into(src_ref, dst_ref):
    rows = src_ref.shape[0]
    ch = _row_chunk(rows)

    def body(c, carry):
        r = pl.ds(pl.multiple_of(c * ch, ch), ch)
        dst_ref[r, :] = src_ref[r, :].astype(dst_ref.dtype)
        return carry

    lax.fori_loop(0, rows // ch, body, 0)


def _matmul_kernel(a_ref, w_ref, o_ref, wb_ref):
    @pl.when(pl.program_id(0) == 0)
    def _():
        _cast_into(w_ref, wb_ref)

    o_ref[...] = _dot(a_ref[...], wb_ref[...]).astype(o_ref.dtype)


def _matmul(a, w, out_dtype, tm):
    m, k = a.shape
    n = w.shape[1]
    return pl.pallas_call(
        _matmul_kernel,
        grid=(m // tm,),
        in_specs=[pl.BlockSpec((tm, k), lambda i: (i, 0)),
                  pl.BlockSpec((k, n), lambda i: (0, 0), pipeline_mode=pl.Buffered(1))],
        out_specs=pl.BlockSpec((tm, n), lambda i: (i, 0)),
        out_shape=jax.ShapeDtypeStruct((m, n), out_dtype),
        scratch_shapes=[pltpu.VMEM((k, n), BF16)],
        compiler_params=_params("arbitrary"),
        name="matmul",
    )(a, w)


def _col_matmul_kernel(a_ref, wt_ref, o_ref, wb_ref):
    @pl.when(pl.program_id(1) == 0)
    def _():
        _cast_into(wt_ref, wb_ref)

    o_ref[...] = _dot_nt(a_ref[...], wb_ref[...]).astype(o_ref.dtype)


def _col_matmul(a, wt, col0, ncols, out_dtype, tm, tn):
    m, k = a.shape
    j0 = col0 // tn
    return pl.pallas_call(
        _col_matmul_kernel,
        grid=(ncols // tn, m // tm),
        in_specs=[pl.BlockSpec((tm, k), lambda j, i: (i, 0)),
                  pl.BlockSpec((tn, k), lambda j, i: (j0 + j, 0))],
        out_specs=pl.BlockSpec((tm, tn), lambda j, i: (i, j)),
        out_shape=jax.ShapeDtypeStruct((m, ncols), out_dtype),
        scratch_shapes=[pltpu.VMEM((tn, k), BF16)],
        compiler_params=_params("arbitrary", "arbitrary"),
        name="col_matmul",
    )(a, wt)


def _halo_rows(shift):
    return max(2 * shift, SUBLANES)


def _conv_buf(rows, tn, shift):
    return pltpu.VMEM((tn // LANES, _halo_rows(shift) + rows, LANES), F32)


def _slab(s):
    return slice(s * LANES, (s + 1) * LANES)


def _conv_stage(buf_ref, row0, x):
    for s in range(buf_ref.shape[0]):
        buf_ref[s, row0:row0 + x.shape[0], :] = x[:, _slab(s)]


def _conv_taps(buf_ref, s, r0, rows, shift):
    halo = _halo_rows(shift)

    def back(steps):
        start = r0 + (halo - steps * shift)
        if (steps * shift) % SUBLANES == 0:
            return pl.ds(start if isinstance(start, int) else pl.multiple_of(start, SUBLANES), rows)
        return pl.ds(start, rows, stride=1)

    return buf_ref[s, back(2), :], buf_ref[s, back(1), :], buf_ref[s, back(0), :]


def _conv_last(buf_ref, new_ref, rows, shift):
    halo = _halo_rows(shift)
    for s in range(buf_ref.shape[0]):
        new_ref[0, :, _slab(s)] = buf_ref[s, halo + rows - 2 * shift:halo + rows, :]


def _conv_proj_kernel(xn_ref, wbg_ref, wcg_ref, wvc_ref, prev_ref, cw_ref, o_ref, new_ref,
                      bgb_ref, cgb_ref, vcb_ref, buf_ref, *, shift, tiles_per_seq):
    rows = xn_ref.shape[0]
    halo = _halo_rows(shift)
    i = pl.program_id(1)

    @pl.when(i == 0)
    def _():
        _cast_into(wbg_ref, bgb_ref)
        _cast_into(wcg_ref, cgb_ref)
        _cast_into(wvc_ref, vcb_ref)

    def from_cache():
        _conv_stage(buf_ref, halo - 2 * shift, prev_ref[0])

    if tiles_per_seq == 1:
        from_cache()
    else:
        pl.when(i % tiles_per_seq == 0)(from_cache)

        @pl.when(i % tiles_per_seq != 0)
        def _():
            for s in range(buf_ref.shape[0]):
                buf_ref[s, halo - 2 * shift:halo, :] = buf_ref[s, halo + rows - 2 * shift:halo + rows, :]

    xn = xn_ref[...]
    _conv_stage(buf_ref, halo, _dot_nt(xn, cgb_ref[...]) * _dot_nt(xn, vcb_ref[...]))
    bg = _dot_nt(xn, bgb_ref[...])
    for s in range(buf_ref.shape[0]):
        u2, u1, u0 = _conv_taps(buf_ref, s, 0, rows, shift)
        y = cw_ref[0:1, _slab(s)] * u2 + cw_ref[1:2, _slab(s)] * u1 + cw_ref[2:3, _slab(s)] * u0
        o_ref[:, _slab(s)] = (bg[:, _slab(s)] * y).astype(o_ref.dtype)
    _conv_last(buf_ref, new_ref, rows, shift)


def _conv_proj(xn, w_in_t, prev, conv_w, *, nseq, shift, tm, tn):
    m, d = xn.shape
    nj = D_CONV // tn
    tiles_per_seq = m // (nseq * tm)
    kern = functools.partial(_conv_proj_kernel, shift=shift, tiles_per_seq=tiles_per_seq)
    state = pl.BlockSpec((1, 2 * shift, tn), lambda j, i: (i // tiles_per_seq, 0, j))
    wb = pltpu.VMEM((tn, d), BF16)
    return pl.pallas_call(
        kern,
        grid=(nj, m // tm),
        in_specs=[pl.BlockSpec((tm, d), lambda j, i: (i, 0)),
                  pl.BlockSpec((tn, d), lambda j, i: (j, 0)),
                  pl.BlockSpec((tn, d), lambda j, i: (j + nj, 0)),
                  pl.BlockSpec((tn, d), lambda j, i: (j + 2 * nj, 0)),
                  state,
                  pl.BlockSpec((CONV_WIDTH, tn), lambda j, i: (0, j))],
        out_specs=[pl.BlockSpec((tm, tn), lambda j, i: (i, j)),
                   state],
        out_shape=[jax.ShapeDtypeStruct((m, D_CONV), BF16),
                   jax.ShapeDtypeStruct((nseq, 2 * shift, D_CONV), F32)],
        scratch_shapes=[wb, wb, wb, _conv_buf(tm, tn, shift)],
        compiler_params=_params("arbitrary", "arbitrary"),
        name="conv_proj",
    )(xn, w_in_t, w_in_t, w_in_t, prev, conv_w)


def _cumsum_rows(g):
    c = g.shape[0]
    row = lax.broadcasted_iota(jnp.int32, g.shape, 0)
    x = g
    s = 1
    while s < c:
        x = x + jnp.where(row >= s, pltpu.roll(x, s, 0), 0.0)
        s *= 2
    return x


def _bcast_block_row(x, s, k):
    c, lanes = x.shape
    if s == c:
        return jnp.broadcast_to(x[k:k + 1, :], x.shape)
    if s >= SUBLANES:
        y = x.reshape(c // s, s, lanes)
        return jnp.broadcast_to(y[:, k:k + 1, :], y.shape).reshape(c, lanes)
    y = x.reshape(c // SUBLANES, SUBLANES, lanes)
    sub = lax.broadcasted_iota(jnp.int32, y.shape, 1)
    out = None
    for blk in range(SUBLANES // s):
        src = jnp.broadcast_to(y[:, blk * s + k:blk * s + k + 1, :], y.shape)
        out = src if out is None else jnp.where(sub >= blk * s, src, out)
    return out.reshape(c, lanes)


def _gla_pair_masks(c):
    ri = lax.broadcasted_iota(jnp.int32, (c, c), 0)
    ci = lax.broadcasted_iota(jnp.int32, (c, c), 1)
    diff_bits = ri ^ ci
    masks = [diff_bits == 0]
    level = 0
    while (1 << level) < c:
        masks.append(((diff_bits >> level) == 1) & (((ri >> level) & 1) == 1))
        level += 1
    return masks


def _gla_chunk(q, k, v, g, s_prev, masks):
    c = q.shape[0]
    cum = _cumsum_rows(g)
    a = jnp.where(masks[0], _dot_nt(q.astype(BF16), k.astype(BF16)), 0.0)
    for level in range(len(masks) - 1):
        half = 1 << level
        ref = _bcast_block_row(cum, 2 * half, half - 1)
        d = cum - ref
        up = jnp.minimum(d, 0.0)
        qe = q * jnp.exp2(up)
        ke = k * jnp.exp2(up - d)
        a = a + jnp.where(masks[1 + level], _dot_nt(qe.astype(BF16), ke.astype(BF16)), 0.0)
    o = _dot(a.astype(BF16), v.astype(BF16)) + _dot((q * jnp.exp2(cum)).astype(BF16), s_prev.astype(BF16))
    last = cum[c - 1:c, :]
    kd = k * jnp.exp2(last - cum)
    dk = last.shape[1]
    decay_t = jnp.transpose(jnp.broadcast_to(jnp.exp2(last), (dk, dk)))
    decayed = jnp.concatenate([decay_t * s_prev[:, i:i + dk] for i in range(0, s_prev.shape[1], dk)], axis=1)
    s_new = decayed + _dot_tn(kd.astype(BF16), v.astype(BF16))
    return o, s_new


def _gla_kernel(q_ref, k_ref, v_ref, r_ref, g_ref, s0_ref, gn_ref, *refs, chunk, single_chunk, group, n_cast):
    cast_src, (o_ref, sn_ref), cast_dst = refs[:n_cast], refs[n_cast:n_cast + 2], refs[n_cast + 2:]
    bb, rows = q_ref.shape[0], q_ref.shape[1]
    nchunk = rows // chunk
    state_in = s0_ref if single_chunk else sn_ref

    if not single_chunk:
        @pl.when(pl.program_id(1) == 0)
        def _():
            sn_ref[...] = s0_ref[...]

    masks = _gla_pair_masks(chunk)

    def one(b, r):
        for h in range(GLA_HEADS):
            kc = slice(h * GLA_DK, (h + 1) * GLA_DK)
            vc = slice(h * GLA_DV, (h + 1) * GLA_DV)
            q = q_ref[b, r, kc] * (GLA_DK ** -0.5)
            o, s_new = _gla_chunk(q, k_ref[b, r, kc], v_ref[b, r, vc], g_ref[b, r, kc], state_in[b, h], masks)
            sn_ref[b, h] = s_new
            rr = r_ref[b, r, vc]
            o_ref[b, r, vc] = (_rms_rows(o, gn_ref[:, vc]) * (rr * jax.nn.sigmoid(rr))).astype(o_ref.dtype)

    def body(n, carry):
        r = pl.ds(pl.multiple_of((n % nchunk) * chunk, chunk), chunk)
        for u in range(group):
            one((n // nchunk) * group + u, r)
        return carry

    lax.fori_loop(0, (bb // group) * nchunk, body, 0)
    for src, dst in zip(cast_src, cast_dst):
        _cast_into(src, dst)


def _gla(p, logf, s0, gla_norm, *, q_blk, k_blk, v_blk, r_blk, bb, rows, chunk, group, cast=()):
    nb, t, _ = p.shape
    nk, nv = GLA_HEADS * GLA_DK, GLA_HEADS * GLA_DV
    nt = t // rows
    nsteps = (nb // bb) * nt
    kern = functools.partial(_gla_kernel, chunk=chunk, single_chunk=(t == chunk), group=group, n_cast=len(cast))
    state_spec = pl.BlockSpec((bb, GLA_HEADS, GLA_DK, GLA_DV), lambda b, c: (b, 0, 0, 0))
    cast_specs = [pl.BlockSpec((w.shape[0] // nsteps, w.shape[1]), lambda b, c: (b * nt + c, 0)) for w in cast]
    return pl.pallas_call(
        kern,
        grid=(nb // bb, nt),
        in_specs=[pl.BlockSpec((bb, rows, nk), lambda b, c: (b, c, q_blk)),
                  pl.BlockSpec((bb, rows, nk), lambda b, c: (b, c, k_blk)),
                  pl.BlockSpec((bb, rows, nv), lambda b, c: (b, c, v_blk)),
                  pl.BlockSpec((bb, rows, nv), lambda b, c: (b, c, r_blk)),
                  pl.BlockSpec((bb, rows, nk), lambda b, c: (b, c, 0)),
                  state_spec,
                  pl.BlockSpec((1, nv), lambda b, c: (0, 0))] + cast_specs,
        out_specs=[pl.BlockSpec((bb, rows, nv), lambda b, c: (b, c, 0)),
                   state_spec] + cast_specs,
        out_shape=[jax.ShapeDtypeStruct((nb, t, D_GLA), BF16),
                   jax.ShapeDtypeStruct((nb, GLA_HEADS, GLA_DK, GLA_DV), F32)]
                  + [jax.ShapeDtypeStruct(w.shape, BF16) for w in cast],
        compiler_params=_params("arbitrary", "arbitrary"),
        name="gla",
    )(p, p, p, p, logf, s0, gla_norm, *cast)


def _proj_res_norm_kernel(*refs, n_a, nk, final):
    a_refs, (w_ref, res_ref, g_ref) = refs[:n_a], refs[n_a:n_a + 3]
    out_refs, acc_ref = (refs[n_a + 3:], None) if nk == 1 else (refs[n_a + 3:-1], refs[-1])
    k = pl.program_id(1)

    def product(rows):
        part, r0 = None, 0
        for a_ref in a_refs:
            kw = a_ref.shape[1]
            term = _dot(a_ref[rows, :], w_ref[r0:r0 + kw, :])
            part = term if part is None else part + term
            r0 += kw
        return part

    tm = res_ref.shape[0]
    if nk == 1:
        rb = min(tm, PROJ_ROW_BLOCK)
        ch = _row_chunk(rb, NORM_ROWS)
        for b0 in range(0, tm, rb):
            part = product(slice(b0, b0 + rb))
            for c0 in range(0, rb, ch):
                r = slice(b0 + c0, b0 + c0 + ch)
                h = res_ref[r, :] + part[c0:c0 + ch, :]
                hn = _rms_rows(h, g_ref[...])
                if final:
                    out_refs[0][r, :] = hn
                else:
                    out_refs[0][r, :] = h
                    out_refs[1][r, :] = hn.astype(out_refs[1].dtype)
        return

    part = product(slice(None))

    @pl.when(k == 0)
    def _():
        acc_ref[...] = part

    if nk > 1:
        @pl.when(k > 0)
        def _():
            acc_ref[...] += part

    @pl.when(k == nk - 1)
    def _():
        rows = acc_ref.shape[0]
        ch = _row_chunk(rows, NORM_ROWS)
        g = g_ref[...]

        def body(c, carry):
            r = pl.ds(pl.multiple_of(c * ch, ch), ch)
            h = res_ref[r, :] + acc_ref[r, :]
            hn = _rms_rows(h, g)
            if final:
                out_refs[0][r, :] = hn
            else:
                out_refs[0][r, :] = h
                out_refs[1][r, :] = hn.astype(out_refs[1].dtype)
            return carry

        lax.fori_loop(0, rows // ch, body, 0)


def _proj_res_norm(a_list, w, res, gain, *, tm, tk, final):
    m = a_list[0].shape[0]
    kdim, d = w.shape
    nk = kdim // tk
    assert len(a_list) == 1 or nk == 1
    kern = functools.partial(_proj_res_norm_kernel, n_a=len(a_list), nk=nk, final=final)
    a_specs = ([pl.BlockSpec((tm, tk), lambda i, k: (i, k))] if len(a_list) == 1 else
               [pl.BlockSpec((tm, a.shape[1]), lambda i, k: (i, 0)) for a in a_list])
    row_spec = pl.BlockSpec((tm, d), lambda i, k: (i, 0))
    if final:
        out_specs = row_spec
        out_shape = jax.ShapeDtypeStruct((m, d), F32)
    else:
        out_specs = [row_spec, row_spec]
        out_shape = [jax.ShapeDtypeStruct((m, d), F32), jax.ShapeDtypeStruct((m, d), BF16)]
    return pl.pallas_call(
        kern,
        grid=(m // tm, nk),
        in_specs=a_specs + [pl.BlockSpec((tk, d), lambda i, k: (k, 0),
                                         pipeline_mode=pl.Buffered(1 if nk == 1 else 2)),
                            row_spec,
                            pl.BlockSpec((1, d), lambda i, k: (0, 0))],
        out_specs=out_specs,
        out_shape=out_shape,
        scratch_shapes=[] if nk == 1 else [pltpu.VMEM((tm, d), F32)],
        compiler_params=_params("arbitrary", "arbitrary"),
        name="proj_res_norm",
    )(*a_list, w, res, gain)


def _xattn_kernel(hn_ref, wq_ref, k_ref, v_ref, o_ref, wb_ref):
    @pl.when(pl.program_id(1) == 0)
    def _():
        _cast_into(wq_ref, wb_ref)

    q = _dot(hn_ref[0], wb_ref[...]).astype(BF16)
    p = _softmax_rows(_dot_nt(q, k_ref[0].astype(BF16)) * (X_HD ** -0.5))
    o_ref[0] = _dot(p.astype(BF16), v_ref[0].astype(BF16)).astype(o_ref.dtype)


def _xattn(hn, w_xq, mk, mv):
    nb, t, d = hn.shape
    return pl.pallas_call(
        _xattn_kernel,
        grid=(X_HEADS, nb),
        in_specs=[pl.BlockSpec((1, t, d), lambda h, b: (b, 0, 0)),
                  pl.BlockSpec((d, X_HD), lambda h, b: (0, h)),
                  pl.BlockSpec((1, MEM_LEN, X_HD), lambda h, b: (b, 0, h)),
                  pl.BlockSpec((1, MEM_LEN, X_HD), lambda h, b: (b, 0, h))],
        out_specs=pl.BlockSpec((1, t, X_HD), lambda h, b: (b, 0, h)),
        out_shape=jax.ShapeDtypeStruct((nb, t, d), BF16),
        scratch_shapes=[pltpu.VMEM((d, X_HD), BF16)],
        compiler_params=_params("arbitrary", "arbitrary"),
        name="xattn",
    )(hn, w_xq, mk, mv)


def _softmax_rows(s):
    s = s - jnp.max(s, axis=-1, keepdims=True)
    e = jnp.exp(s)
    return e / jnp.sum(e, axis=-1, keepdims=True)


def _xattn_cache_kernel(q_ref, k_ref, v_ref, o_ref):
    bb, tq = q_ref.shape[0], q_ref.shape[1]
    nchunk = X_HD // LANES
    pitch = nchunk * X_HEADS

    def gather(ref, b, h):
        parts = [ref[b, pl.ds(c * X_HEADS + h, MEM_LEN, stride=pitch), :] for c in range(nchunk)]
        return jnp.concatenate(parts, axis=1).astype(BF16)

    pairs = [(b, h) for b in range(bb) for h in range(X_HEADS)]
    scores = [_dot_nt(q_ref[b, :, h * X_HD:(h + 1) * X_HD].astype(BF16), gather(k_ref, b, h)) for b, h in pairs]
    p = _softmax_rows(jnp.concatenate(scores, axis=0) * (X_HD ** -0.5)).astype(BF16)
    for n, (b, h) in enumerate(pairs):
        o_ref[b, :, h * X_HD:(h + 1) * X_HD] = _dot(p[n * tq:(n + 1) * tq], gather(v_ref, b, h)).astype(o_ref.dtype)


def _xattn_cache(q, ck, cv, *, bb):
    nb, tq, d = q.shape
    nchunk = X_HD // LANES

    def stored_order(c):
        c = c.reshape(nb, MEM_LEN, X_HEADS, nchunk, LANES).transpose(0, 1, 3, 2, 4)
        return c.reshape(nb, MEM_LEN * nchunk * X_HEADS, LANES)

    rows = MEM_LEN * nchunk * X_HEADS
    return pl.pallas_call(
        _xattn_cache_kernel,
        grid=(nb // bb,),
        in_specs=[pl.BlockSpec((bb, tq, d), lambda b: (b, 0, 0)),
                  pl.BlockSpec((bb, rows, LANES), lambda b: (b, 0, 0)),
                  pl.BlockSpec((bb, rows, LANES), lambda b: (b, 0, 0))],
        out_specs=pl.BlockSpec((bb, tq, d), lambda b: (b, 0, 0)),
        out_shape=jax.ShapeDtypeStruct((nb, tq, d), BF16),
        compiler_params=_params("arbitrary"),
        name="xattn_cache",
    )(q, stored_order(ck), stored_order(cv))


def _shift_rows(x, prev, steps, shift):
    n = steps * shift
    head = prev[2 * shift - n:, :]
    if n == x.shape[0]:
        return head
    if n % SUBLANES == 0:
        return jnp.concatenate([head, x[:-n, :]], axis=0)
    rolled = pltpu.roll(x, n, 0)
    row = lax.broadcasted_iota(jnp.int32, (SUBLANES, x.shape[1]), 0)
    first = rolled[:SUBLANES, :]
    for i in range(n):
        first = jnp.where(row == i, head[i:i + 1, :], first)
    return jnp.concatenate([first, rolled[SUBLANES:, :]], axis=0)


def _ffn_up_kernel(hn_ref, wg_ref, wu_ref, cw_ref, cb_ref, prev_ref, o_ref, new_ref, wgb_ref, wub_ref, *,
                   shift):
    rows = hn_ref.shape[0]

    @pl.when(pl.program_id(1) == 0)
    def _():
        _cast_into(wg_ref, wgb_ref)
        _cast_into(wu_ref, wub_ref)

    rb = max(FFN_ROW_BLOCK, 2 * shift)
    prev = prev_ref[0]
    for r0 in range(0, rows, rb):
        hn = hn_ref[r0:r0 + rb, :]
        gate = _dot(hn, wgb_ref[...])
        gc = (cw_ref[0:1, :] * _shift_rows(gate, prev, 2, shift)
              + cw_ref[1:2, :] * _shift_rows(gate, prev, 1, shift) + cw_ref[2:3, :] * gate + cb_ref[...])
        o_ref[r0:r0 + rb, :] = ((gc * jax.nn.sigmoid(gc)) * _dot(hn, wub_ref[...])).astype(o_ref.dtype)
        prev = gate[rb - 2 * shift:, :]
    new_ref[0] = prev


def _ffn_up(hn, wg, wu, cw, cb, prev, nseq, rows, shift, tn):
    d = hn.shape[1]
    kern = functools.partial(_ffn_up_kernel, shift=shift)
    return pl.pallas_call(
        kern,
        grid=(D_FF // tn, nseq),
        in_specs=[pl.BlockSpec((rows, d), lambda j, b: (b, 0)),
                  pl.BlockSpec((d, tn), lambda j, b: (0, j)),
                  pl.BlockSpec((d, tn), lambda j, b: (0, j)),
                  pl.BlockSpec((CONV_WIDTH, tn), lambda j, b: (0, j)),
                  pl.BlockSpec((1, tn), lambda j, b: (0, j)),
                  pl.BlockSpec((1, 2 * shift, tn), lambda j, b: (b, 0, j))],
        out_specs=[pl.BlockSpec((rows, tn), lambda j, b: (b, j)),
                   pl.BlockSpec((1, 2 * shift, tn), lambda j, b: (b, 0, j))],
        out_shape=[jax.ShapeDtypeStruct((nseq * rows, D_FF), BF16),
                   jax.ShapeDtypeStruct((nseq, 2 * shift, D_FF), F32)],
        scratch_shapes=[pltpu.VMEM((d, tn), BF16), pltpu.VMEM((d, tn), BF16)],
        compiler_params=_params("arbitrary", "arbitrary"),
        name="ffn_up",
    )(hn, wg, wu, cw, cb, prev)


def _layer(x, w, wb, *, nseq, rows, shift, prev_conv, s0, prev_ffn, mk, mv, time_major):
    m = x.shape[0]
    tm = min(m, 1024)
    xn, logf = _norm_gate(x, w["norm_mix"], w["w_g1_t"], w["w_g2"], w["b_gate"], tm)
    conv_out, conv_new = _conv_proj(xn, w["w_in_t"], prev_conv, w["conv_w"], nseq=nseq, shift=shift, tm=tm, tn=512)
    n_gla = N_MAIN - 3 * D_CONV
    p = _col_matmul(xn, w["w_in_t"], 3 * D_CONV, n_gla, F32, tm, 1024)

    if time_major:
        nt = m // shift
        pad = ((0, 0), (0, SUBLANES - nt), (0, 0))
        pg = jnp.pad(p.reshape(nt, shift, n_gla).transpose(1, 0, 2), pad)
        lg = jnp.pad(logf.reshape(nt, shift, -1).transpose(1, 0, 2), pad)
        gla_args = dict(bb=8, rows=SUBLANES, chunk=SUBLANES, group=8)
    else:
        pg, lg = p.reshape(nseq, rows, n_gla), logf.reshape(nseq, rows, -1)
        gla_args = dict(bb=4, rows=128, chunk=2 * GLA_CHUNK, group=4)
    cast = () if wb is not None else (w["w_out"], w["w_xo"], w["w_fd"])
    o, s_new, *made = _gla(pg, lg, s0, w["gla_norm"], q_blk=0, k_blk=1, v_blk=1, r_blk=2, cast=cast, **gla_args)
    if wb is None:
        wb = dict(zip(("w_out", "w_xo", "w_fd"), made))
    if time_major:
        gla_out = o[:, :nt].transpose(1, 0, 2).reshape(m, D_GLA)
    else:
        gla_out = o.reshape(m, D_GLA)

    tm2 = min(m, 512)
    h, hn = _proj_res_norm([conv_out, gla_out], wb["w_out"], x, w["norm_x"], tm=tm2, tk=D_MODEL, final=False)
    if time_major:
        nt = m // shift
        qx = _matmul(hn, w["w_xq"], BF16, tm)
        qb = jnp.pad(qx.astype(F32).reshape(nt, shift, D_MODEL).transpose(1, 0, 2),
                     ((0, 0), (0, SUBLANES - nt), (0, 0)))
        ob = _xattn_cache(qb, mk, mv, bb=4)
        attn = ob[:, :nt].transpose(1, 0, 2).reshape(m, D_MODEL)
    else:
        attn = _xattn(hn.reshape(nseq, rows, D_MODEL), w["w_xq"], mk, mv).reshape(m, D_MODEL)

    h2, hn2 = _proj_res_norm([attn], wb["w_xo"], h, w["norm_ffn"], tm=tm2, tk=D_MODEL, final=False)
    act, ffn_new = _ffn_up(hn2, w["w_fg"], w["w_fu"], w["ffn_conv_w"], w["ffn_conv_b"], prev_ffn,
                           nseq, rows, shift, 512)
    return act, h2, conv_new, s_new, ffn_new, wb


def kernel(x_prompt, x_sample, mem_prompt, cache_conv, state_gla, cache_ffn, cache_mem_k, cache_mem_v,
           norm_mix, w_in, conv_w, w_gate2, b_gate, gla_norm, w_out, norm_x, norm_mem, w_xq, w_xk, w_xv,
           w_xo, norm_ffn, w_ffn_gate, w_ffn_up, ffn_conv_w, ffn_conv_b, w_ffn_down, norm_final):
    depth = w_in.shape[0]
    nb, seq, d = x_prompt.shape
    db, dseq, _ = x_sample.shape
    hp = x_prompt.reshape(nb * seq, d)
    hs = x_sample.transpose(1, 0, 2).reshape(dseq * db, d)
    outs = {k: [] for k in ("conv_p", "gla_p", "ffn_p", "mk", "mv", "conv_s", "gla_s", "ffn_s")}
    nfinal = norm_final.reshape(1, d)
    yp = ys = None
    for l in range(depth):
        w = {
            "norm_mix": norm_mix[l].reshape(1, d),
            "w_in_t": w_in[l].T,
            "w_g1_t": jnp.pad(w_in[l].T[N_MAIN:], ((0, LANES - GLA_RANK), (0, 0))).astype(BF16),
            "w_g2": jnp.pad(w_gate2[l], ((0, LANES - GLA_RANK), (0, 0))).astype(BF16),
            "b_gate": b_gate[l].reshape(1, -1),
            "conv_w": conv_w[l],
            "gla_norm": gla_norm[l].reshape(1, -1),
            "w_out": w_out[l],
            "norm_x": norm_x[l].reshape(1, d),
            "w_xq": w_xq[l],
            "w_xo": w_xo[l],
            "w_fd": w_ffn_down[l],
            "norm_ffn": norm_ffn[l].reshape(1, d),
            "w_fg": w_ffn_gate[l],
            "w_fu": w_ffn_up[l],
            "ffn_conv_w": ffn_conv_w[l],
            "ffn_conv_b": ffn_conv_b[l].reshape(1, -1),
        }
        last = l == depth - 1
        gain_next = nfinal if last else None

        mem = mem_prompt.reshape(nb * MEM_LEN, d)
        nmem = norm_mem[l].reshape(1, d)
        mk, mk_cache = _mem_proj(mem, nmem, w_xk[l], 1024)
        mv, mv_cache = _mem_proj(mem, nmem, w_xv[l], 1024)
        act, h2, c1, s1, f1, wb = _layer(
            hp, w, None, nseq=nb, rows=seq, shift=1,
            prev_conv=jnp.zeros((nb, CONV_WIDTH - 1, D_CONV), F32),
            s0=jnp.zeros((nb, GLA_HEADS, GLA_DK, GLA_DV), F32),
            prev_ffn=jnp.zeros((nb, CONV_WIDTH - 1, D_FF), F32),
            mk=mk.reshape(nb, MEM_LEN, d), mv=mv.reshape(nb, MEM_LEN, d), time_major=False)
        assert last, "only the final layer's epilogue (final rmsnorm) is implemented"
        yp = _proj_res_norm([act], wb["w_fd"], h2, gain_next, tm=512, tk=D_FF, final=True)
        outs["conv_p"].append(c1)
        outs["gla_p"].append(s1)
        outs["ffn_p"].append(f1)
        outs["mk"].append(mk_cache)
        outs["mv"].append(mv_cache)

        def tmajor(c):
            return c.transpose(1, 0, 2).reshape(1, (CONV_WIDTH - 1) * db, c.shape[-1])

        act, h2, c2, s2, f2, _ = _layer(
            hs, w, wb, nseq=1, rows=dseq * db, shift=db,
            prev_conv=tmajor(cache_conv[l]), s0=state_gla[l], prev_ffn=tmajor(cache_ffn[l]),
            mk=cache_mem_k[l], mv=cache_mem_v[l], time_major=True)
        ys = _proj_res_norm([act], wb["w_fd"], h2, gain_next, tm=512, tk=D_FF, final=True)
        outs["conv_s"].append(c2.reshape(CONV_WIDTH - 1, db, D_CONV).transpose(1, 0, 2))
        outs["gla_s"].append(s2)
        outs["ffn_s"].append(f2.reshape(CONV_WIDTH - 1, db, D_FF).transpose(1, 0, 2))

    y_prompt = yp.reshape(nb, seq, d)
    y_sample = ys.reshape(dseq, db, d).transpose(1, 0, 2)
    st = lambda k: jnp.stack(outs[k])
    return (y_prompt, y_sample, st("conv_p"), st("gla_p"), st("ffn_p"), st("mk"), st("mv"),
            st("conv_s"), st("gla_s"), st("ffn_s"))
```

```python
import functools

import jax
import jax.numpy as jnp
from jax import lax
from jax.experimental import pallas as pl
from jax.experimental.pallas import tpu as pltpu

F32 = jnp.float32
BF16 = jnp.bfloat16

D_MODEL = 2048
EPS = 1e-6
CONV_WIDTH = 3
D_CONV = 1024
D_GLA = 1024
GLA_HEADS = 4
GLA_DV = 256
GLA_DK = 128
GLA_RANK = 16
GLA_TAU = 16.0
LOG2_E = 1.4426950408889634
GLA_CHUNK = 64
X_HEADS = 4
X_HD = 512
MEM_LEN = 256
D_FF = 5632
N_MAIN = 3 * D_CONV + 2 * GLA_HEADS * GLA_DK + 2 * GLA_HEADS * GLA_DV

LANES = 128
SUBLANES = 8
VMEM_LIMIT_BYTES = 56 * 1024 * 1024


def _params(*sem):
    return pltpu.CompilerParams(dimension_semantics=sem, vmem_limit_bytes=VMEM_LIMIT_BYTES)


def _dot(a, b):
    return jnp.dot(a, b, preferred_element_type=F32)


def _dot_nt(a, b):
    return lax.dot_general(a, b, (((1,), (1,)), ((), ())), preferred_element_type=F32)


def _dot_tn(a, b):
    return lax.dot_general(a, b, (((0,), (0,)), ((), ())), preferred_element_type=F32)


def _rms_rows(x, g):
    ms = jnp.mean(x * x, axis=-1, keepdims=True)
    return (x * lax.rsqrt(ms + EPS)) * g


def _row_chunk(rows, limit=256):
    for c in (256, 128, 64, 32, 16, 8):
        if c <= limit and rows % c == 0:
            return c
    return rows


NORM_ROWS = 128
FFN_ROW_BLOCK = 512
PROJ_ROW_BLOCK = 256


def _norm_into(x_ref, g_ref, xn_ref):
    rows = x_ref.shape[0]
    ch = _row_chunk(rows)
    g = g_ref[...]

    def body(c, carry):
        r = pl.ds(pl.multiple_of(c * ch, ch), ch)
        xn_ref[r, :] = _rms_rows(x_ref[r, :], g).astype(xn_ref.dtype)
        return carry

    lax.fori_loop(0, rows // ch, body, 0)


def _norm_matmul_kernel(x_ref, g_ref, w_ref, o_ref, oc_ref, xn_ref):
    j = pl.program_id(1)

    @pl.when(j == 0)
    def _():
        _norm_into(x_ref, g_ref, xn_ref)

    o_ref[...] = _dot(xn_ref[...], w_ref[...].astype(BF16)).astype(o_ref.dtype)
    nchunk = X_HD // LANES
    pitch = nchunk * X_HEADS
    heads_per_tile = o_ref.shape[1] // X_HD
    for b in range(oc_ref.shape[0]):
        for hh in range(heads_per_tile):
            for c in range(nchunk):
                col = hh * X_HD + c * LANES
                row = c * X_HEADS + j * heads_per_tile + hh
                oc_ref[b, pl.ds(row, MEM_LEN, stride=pitch), :] = (
                    o_ref[b * MEM_LEN:(b + 1) * MEM_LEN, col:col + LANES])


def _norm_gate_kernel(x_ref, g_ref, wg1_ref, wg2_ref, bg_ref, xn_ref, lf_ref):
    _norm_into(x_ref, g_ref, xn_ref)
    g1 = _dot_nt(xn_ref[...], wg1_ref[...])
    z = _dot(g1.astype(BF16), wg2_ref[...]) + bg_ref[...]
    lf_ref[...] = (jnp.minimum(z, 0.0) - jnp.log1p(jnp.exp(-jnp.abs(z)))) * (LOG2_E / GLA_TAU)


def _mem_proj(mem, gain, w, tn):
    m, d = mem.shape
    n = w.shape[1]
    nb = m // MEM_LEN
    nchunk = X_HD // LANES
    rows = MEM_LEN * nchunk * X_HEADS
    flat, stored = pl.pallas_call(
        _norm_matmul_kernel,
        grid=(1, n // tn),
        in_specs=[pl.BlockSpec((m, d), lambda i, j: (0, 0)),
                  pl.BlockSpec((1, d), lambda i, j: (0, 0)),
                  pl.BlockSpec((d, tn), lambda i, j: (0, j))],
        out_specs=[pl.BlockSpec((m, tn), lambda i, j: (0, j)),
                   pl.BlockSpec((nb, rows, LANES), lambda i, j: (0, 0, 0))],
        out_shape=[jax.ShapeDtypeStruct((m, n), F32),
                   jax.ShapeDtypeStruct((nb, rows, LANES), F32)],
        scratch_shapes=[pltpu.VMEM((m, d), BF16)],
        compiler_params=_params("arbitrary", "arbitrary"),
        name="norm_matmul",
    )(mem, gain, w)
    cache = stored.reshape(nb, MEM_LEN, nchunk, X_HEADS, LANES).transpose(0, 1, 3, 2, 4)
    return flat, cache.reshape(nb, MEM_LEN, X_HEADS, X_HD)


def _norm_gate(x, gain, wg1, wg2, b_gate, tm):
    m, d = x.shape
    ng = wg2.shape[1]
    return pl.pallas_call(
        _norm_gate_kernel,
        grid=(m // tm,),
        in_specs=[pl.BlockSpec((tm, d), lambda i: (i, 0)),
                  pl.BlockSpec((1, d), lambda i: (0, 0)),
                  pl.BlockSpec((LANES, d), lambda i: (0, 0)),
                  pl.BlockSpec((LANES, ng), lambda i: (0, 0)),
                  pl.BlockSpec((1, ng), lambda i: (0, 0))],
        out_specs=[pl.BlockSpec((tm, d), lambda i: (i, 0)),
                   pl.BlockSpec((tm, ng), lambda i: (i, 0))],
        out_shape=[jax.ShapeDtypeStruct((m, d), BF16),
                   jax.ShapeDtypeStruct((m, ng), F32)],
        compiler_params=_params("arbitrary"),
        name="norm_gate",
    )(x, gain, wg1, wg2, b_gate)


def _cast_into(src_ref, dst_ref):
    rows = src_ref.shape[0]
    ch = _row_chunk(rows)

    def body(c, carry):
        r = pl.ds(pl.multiple_of(c * ch, ch), ch)
        dst_ref[r, :] = src_ref[r, :].astype(dst_ref.dtype)
        return carry

    lax.fori_loop(0, rows // ch, body, 0)


def _matmul_kernel(a_ref, w_ref, o_ref, wb_ref):
    @pl.when(pl.program_id(0) == 0)
    def _():
        _cast_into(w_ref, wb_ref)

    o_ref[...] = _dot(a_ref[...], wb_ref[...]).astype(o_ref.dtype)


def _matmul(a, w, out_dtype, tm):
    m, k = a.shape
    n = w.shape[1]
    return pl.pallas_call(
        _matmul_kernel,
        grid=(m // tm,),
        in_specs=[pl.BlockSpec((tm, k), lambda i: (i, 0)),
                  pl.BlockSpec((k, n), lambda i: (0, 0), pipeline_mode=pl.Buffered(1))],
        out_specs=pl.BlockSpec((tm, n), lambda i: (i, 0)),
        out_shape=jax.ShapeDtypeStruct((m, n), out_dtype),
        scratch_shapes=[pltpu.VMEM((k, n), BF16)],
        compiler_params=_params("arbitrary"),
        name="matmul",
    )(a, w)


def _col_matmul_kernel(a_ref, wt_ref, o_ref, wb_ref):
    @pl.when(pl.program_id(1) == 0)
    def _():
        _cast_into(wt_ref, wb_ref)

    o_ref[...] = _dot_nt(a_ref[...], wb_ref[...]).astype(o_ref.dtype)


def _col_matmul(a, wt, col0, ncols, out_dtype, tm, tn):
    m, k = a.shape
    j0 = col0 // tn
    return pl.pallas_call(
        _col_matmul_kernel,
        grid=(ncols // tn, m // tm),
        in_specs=[pl.BlockSpec((tm, k), lambda j, i: (i, 0)),
                  pl.BlockSpec((tn, k), lambda j, i: (j0 + j, 0))],
        out_specs=pl.BlockSpec((tm, tn), lambda j, i: (i, j)),
        out_shape=jax.ShapeDtypeStruct((m, ncols), out_dtype),
        scratch_shapes=[pltpu.VMEM((tn, k), BF16)],
        compiler_params=_params("arbitrary", "arbitrary"),
        name="col_matmul",
    )(a, wt)


def _halo_rows(shift):
    return max(2 * shift, SUBLANES)


def _conv_buf(rows, tn, shift):
    return pltpu.VMEM((tn // LANES, _halo_rows(shift) + rows, LANES), F32)


def _slab(s):
    return slice(s * LANES, (s + 1) * LANES)


def _conv_stage(buf_ref, row0, x):
    for s in range(buf_ref.shape[0]):
        buf_ref[s, row0:row0 + x.shape[0], :] = x[:, _slab(s)]


def _conv_taps(buf_ref, s, r0, rows, shift):
    halo = _halo_rows(shift)

    def back(steps):
        start = r0 + (halo - steps * shift)
        if (steps * shift) % SUBLANES == 0:
            return pl.ds(start if isinstance(start, int) else pl.multiple_of(start, SUBLANES), rows)
        return pl.ds(start, rows, stride=1)

    return buf_ref[s, back(2), :], buf_ref[s, back(1), :], buf_ref[s, back(0), :]


def _conv_last(buf_ref, new_ref, rows, shift):
    halo = _halo_rows(shift)
    for s in range(buf_ref.shape[0]):
        new_ref[0, :, _slab(s)] = buf_ref[s, halo + rows - 2 * shift:halo + rows, :]


def _conv_proj_kernel(xn_ref, wbg_ref, wcg_ref, wvc_ref, prev_ref, cw_ref, o_ref, new_ref,
                      bgb_ref, cgb_ref, vcb_ref, buf_ref, *, shift, tiles_per_seq):
    rows = xn_ref.shape[0]
    halo = _halo_rows(shift)
    i = pl.program_id(1)

    @pl.when(i == 0)
    def _():
        _cast_into(wbg_ref, bgb_ref)
        _cast_into(wcg_ref, cgb_ref)
        _cast_into(wvc_ref, vcb_ref)

    def from_cache():
        _conv_stage(buf_ref, halo - 2 * shift, prev_ref[0])

    if tiles_per_seq == 1:
        from_cache()
    else:
        pl.when(i % tiles_per_seq == 0)(from_cache)

        @pl.when(i % tiles_per_seq != 0)
        def _():
            for s in range(buf_ref.shape[0]):
                buf_ref[s, halo - 2 * shift:halo, :] = buf_ref[s, halo + rows - 2 * shift:halo + rows, :]

    xn = xn_ref[...]
    _conv_stage(buf_ref, halo, _dot_nt(xn, cgb_ref[...]) * _dot_nt(xn, vcb_ref[...]))
    bg = _dot_nt(xn, bgb_ref[...])
    for s in range(buf_ref.shape[0]):
        u2, u1, u0 = _conv_taps(buf_ref, s, 0, rows, shift)
        y = cw_ref[0:1, _slab(s)] * u2 + cw_ref[1:2, _slab(s)] * u1 + cw_ref[2:3, _slab(s)] * u0
        o_ref[:, _slab(s)] = (bg[:, _slab(s)] * y).astype(o_ref.dtype)
    _conv_last(buf_ref, new_ref, rows, shift)


def _conv_proj(xn, w_in_t, prev, conv_w, *, nseq, shift, tm, tn):
    m, d = xn.shape
    nj = D_CONV // tn
    tiles_per_seq = m // (nseq * tm)
    kern = functools.partial(_conv_proj_kernel, shift=shift, tiles_per_seq=tiles_per_seq)
    state = pl.BlockSpec((1, 2 * shift, tn), lambda j, i: (i // tiles_per_seq, 0, j))
    wb = pltpu.VMEM((tn, d), BF16)
    return pl.pallas_call(
        kern,
        grid=(nj, m // tm),
        in_specs=[pl.BlockSpec((tm, d), lambda j, i: (i, 0)),
                  pl.BlockSpec((tn, d), lambda j, i: (j, 0)),
                  pl.BlockSpec((tn, d), lambda j, i: (j + nj, 0)),
                  pl.BlockSpec((tn, d), lambda j, i: (j + 2 * nj, 0)),
                  state,
                  pl.BlockSpec((CONV_WIDTH, tn), lambda j, i: (0, j))],
        out_specs=[pl.BlockSpec((tm, tn), lambda j, i: (i, j)),
                   state],
        out_shape=[jax.ShapeDtypeStruct((m, D_CONV), BF16),
                   jax.ShapeDtypeStruct((nseq, 2 * shift, D_CONV), F32)],
        scratch_shapes=[wb, wb, wb, _conv_buf(tm, tn, shift)],
        compiler_params=_params("arbitrary", "arbitrary"),
        name="conv_proj",
    )(xn, w_in_t, w_in_t, w_in_t, prev, conv_w)


def _cumsum_rows(g):
    c = g.shape[0]
    row = lax.broadcasted_iota(jnp.int32, g.shape, 0)
    x = g
    s = 1
    while s < c:
        x = x + jnp.where(row >= s, pltpu.roll(x, s, 0), 0.0)
        s *= 2
    return x


def _bcast_block_row(x, s, k):
    c, lanes = x.shape
    if s == c:
        return jnp.broadcast_to(x[k:k + 1, :], x.shape)
    if s >= SUBLANES:
        y = x.reshape(c // s, s, lanes)
        return jnp.broadcast_to(y[:, k:k + 1, :], y.shape).reshape(c, lanes)
    y = x.reshape(c // SUBLANES, SUBLANES, lanes)
    sub = lax.broadcasted_iota(jnp.int32, y.shape, 1)
    out = None
    for blk in range(SUBLANES // s):
        src = jnp.broadcast_to(y[:, blk * s + k:blk * s + k + 1, :], y.shape)
        out = src if out is None else jnp.where(sub >= blk * s, src, out)
    return out.reshape(c, lanes)


def _gla_pair_masks(c):
    ri = lax.broadcasted_iota(jnp.int32, (c, c), 0)
    ci = lax.broadcasted_iota(jnp.int32, (c, c), 1)
    diff_bits = ri ^ ci
    masks = [diff_bits == 0]
    level = 0
    while (1 << level) < c:
        masks.append(((diff_bits >> level) == 1) & (((ri >> level) & 1) == 1))
        level += 1
    return masks


def _gla_chunk(q, k, v, g, s_prev, masks):
    c = q.shape[0]
    cum = _cumsum_rows(g)
    a = jnp.where(masks[0], _dot_nt(q.astype(BF16), k.astype(BF16)), 0.0)
    for level in range(len(masks) - 1):
        half = 1 << level
        ref = _bcast_block_row(cum, 2 * half, half - 1)
        d = cum - ref
        up = jnp.minimum(d, 0.0)
        qe = q * jnp.exp2(up)
        ke = k * jnp.exp2(up - d)
        a = a + jnp.where(masks[1 + level], _dot_nt(qe.astype(BF16), ke.astype(BF16)), 0.0)
    o = _dot(a.astype(BF16), v.astype(BF16)) + _dot((q * jnp.exp2(cum)).astype(BF16), s_prev.astype(BF16))
    last = cum[c - 1:c, :]
    kd = k * jnp.exp2(last - cum)
    dk = last.shape[1]
    decay_t = jnp.transpose(jnp.broadcast_to(jnp.exp2(last), (dk, dk)))
    decayed = jnp.concatenate([decay_t * s_prev[:, i:i + dk] for i in range(0, s_prev.shape[1], dk)], axis=1)
    s_new = decayed + _dot_tn(kd.astype(BF16), v.astype(BF16))
    return o, s_new


def _gla_kernel(q_ref, k_ref, v_ref, r_ref, g_ref, s0_ref, gn_ref, *refs, chunk, single_chunk, group, n_cast):
    cast_src, (o_ref, sn_ref), cast_dst = refs[:n_cast], refs[n_cast:n_cast + 2], refs[n_cast + 2:]
    bb, rows = q_ref.shape[0], q_ref.shape[1]
    nchunk = rows // chunk
    state_in = s0_ref if single_chunk else sn_ref

    if not single_chunk:
        @pl.when(pl.program_id(1) == 0)
        def _():
            sn_ref[...] = s0_ref[...]

    masks = _gla_pair_masks(chunk)

    def one(b, r):
        for h in range(GLA_HEADS):
            kc = slice(h * GLA_DK, (h + 1) * GLA_DK)
            vc = slice(h * GLA_DV, (h + 1) * GLA_DV)
            q = q_ref[b, r, kc] * (GLA_DK ** -0.5)
            o, s_new = _gla_chunk(q, k_ref[b, r, kc], v_ref[b, r, vc], g_ref[b, r, kc], state_in[b, h], masks)
            sn_ref[b, h] = s_new
            rr = r_ref[b, r, vc]
            o_ref[b, r, vc] = (_rms_rows(o, gn_ref[:, vc]) * (rr * jax.nn.sigmoid(rr))).astype(o_ref.dtype)

    def body(n, carry):
        r = pl.ds(pl.multiple_of((n % nchunk) * chunk, chunk), chunk)
        for u in range(group):
            one((n // nchunk) * group + u, r)
        return carry

    lax.fori_loop(0, (bb // group) * nchunk, body, 0)
    for src, dst in zip(cast_src, cast_dst):
        _cast_into(src, dst)


def _gla(p, logf, s0, gla_norm, *, q_blk, k_blk, v_blk, r_blk, bb, rows, chunk, group, cast=()):
    nb, t, _ = p.shape
    nk, nv = GLA_HEADS * GLA_DK, GLA_HEADS * GLA_DV
    nt = t // rows
    nsteps = (nb // bb) * nt
    kern = functools.partial(_gla_kernel, chunk=chunk, single_chunk=(t == chunk), group=group, n_cast=len(cast))
    state_spec = pl.BlockSpec((bb, GLA_HEADS, GLA_DK, GLA_DV), lambda b, c: (b, 0, 0, 0))
    cast_specs = [pl.BlockSpec((w.shape[0] // nsteps, w.shape[1]), lambda b, c: (b * nt + c, 0)) for w in cast]
    return pl.pallas_call(
        kern,
        grid=(nb // bb, nt),
        in_specs=[pl.BlockSpec((bb, rows, nk), lambda b, c: (b, c, q_blk)),
                  pl.BlockSpec((bb, rows, nk), lambda b, c: (b, c, k_blk)),
                  pl.BlockSpec((bb, rows, nv), lambda b, c: (b, c, v_blk)),
                  pl.BlockSpec((bb, rows, nv), lambda b, c: (b, c, r_blk)),
                  pl.BlockSpec((bb, rows, nk), lambda b, c: (b, c, 0)),
                  state_spec,
                  pl.BlockSpec((1, nv), lambda b, c: (0, 0))] + cast_specs,
        out_specs=[pl.BlockSpec((bb, rows, nv), lambda b, c: (b, c, 0)),
                   state_spec] + cast_specs,
        out_shape=[jax.ShapeDtypeStruct((nb, t, D_GLA), BF16),
                   jax.ShapeDtypeStruct((nb, GLA_HEADS, GLA_DK, GLA_DV), F32)]
                  + [jax.ShapeDtypeStruct(w.shape, BF16) for w in cast],
        compiler_params=_params("arbitrary", "arbitrary"),
        name="gla",
    )(p, p, p, p, logf, s0, gla_norm, *cast)


def _proj_res_norm_kernel(*refs, n_a, nk, final):
    a_refs, (w_ref, res_ref, g_ref) = refs[:n_a], refs[n_a:n_a + 3]
    out_refs, acc_ref = (refs[n_a + 3:], None) if nk == 1 else (refs[n_a + 3:-1], refs[-1])
    k = pl.program_id(1)

    def product(rows):
        part, r0 = None, 0
        for a_ref in a_refs:
            kw = a_ref.shape[1]
            term = _dot(a_ref[rows, :], w_ref[r0:r0 + kw, :])
            part = term if part is None else part + term
            r0 += kw
        return part

    tm = res_ref.shape[0]
    if nk == 1:
        rb = min(tm, PROJ_ROW_BLOCK)
        ch = _row_chunk(rb, NORM_ROWS)
        for b0 in range(0, tm, rb):
            part = product(slice(b0, b0 + rb))
            for c0 in range(0, rb, ch):
                r = slice(b0 + c0, b0 + c0 + ch)
                h = res_ref[r, :] + part[c0:c0 + ch, :]
                hn = _rms_rows(h, g_ref[...])
                if final:
                    out_refs[0][r, :] = hn
                else:
                    out_refs[0][r, :] = h
                    out_refs[1][r, :] = hn.astype(out_refs[1].dtype)
        return

    part = product(slice(None))

    @pl.when(k == 0)
    def _():
        acc_ref[...] = part

    if nk > 1:
        @pl.when(k > 0)
        def _():
            acc_ref[...] += part

    @pl.when(k == nk - 1)
    def _():
        rows = acc_ref.shape[0]
        ch = _row_chunk(rows, NORM_ROWS)
        g = g_ref[...]

        def body(c, carry):
            r = pl.ds(pl.multiple_of(c * ch, ch), ch)
            h = res_ref[r, :] + acc_ref[r, :]
            hn = _rms_rows(h, g)
            if final:
                out_refs[0][r, :] = hn
            else:
                out_refs[0][r, :] = h
                out_refs[1][r, :] = hn.astype(out_refs[1].dtype)
            return carry

        lax.fori_loop(0, rows // ch, body, 0)


def _proj_res_norm(a_list, w, res, gain, *, tm, tk, final):
    m = a_list[0].shape[0]
    kdim, d = w.shape
    nk = kdim // tk
    assert len(a_list) == 1 or nk == 1
    kern = functools.partial(_proj_res_norm_kernel, n_a=len(a_list), nk=nk, final=final)
    a_specs = ([pl.BlockSpec((tm, tk), lambda i, k: (i, k))] if len(a_list) == 1 else
               [pl.BlockSpec((tm, a.shape[1]), lambda i, k: (i, 0)) for a in a_list])
    row_spec = pl.BlockSpec((tm, d), lambda i, k: (i, 0))
    if final:
        out_specs = row_spec
        out_shape = jax.ShapeDtypeStruct((m, d), F32)
    else:
        out_specs = [row_spec, row_spec]
        out_shape = [jax.ShapeDtypeStruct((m, d), F32), jax.ShapeDtypeStruct((m, d), BF16)]
    return pl.pallas_call(
        kern,
        grid=(m // tm, nk),
        in_specs=a_specs + [pl.BlockSpec((tk, d), lambda i, k: (k, 0),
                                         pipeline_mode=pl.Buffered(1 if nk == 1 else 2)),
                            row_spec,
                            pl.BlockSpec((1, d), lambda i, k: (0, 0))],
        out_specs=out_specs,
        out_shape=out_shape,
        scratch_shapes=[] if nk == 1 else [pltpu.VMEM((tm, d), F32)],
        compiler_params=_params("arbitrary", "arbitrary"),
        name="proj_res_norm",
    )(*a_list, w, res, gain)


def _xattn_kernel(hn_ref, wq_ref, k_ref, v_ref, o_ref, wb_ref):
    @pl.when(pl.program_id(1) == 0)
    def _():
        _cast_into(wq_ref, wb_ref)

    q = _dot(hn_ref[0], wb_ref[...]).astype(BF16)
    p = _softmax_rows(_dot_nt(q, k_ref[0].astype(BF16)) * (X_HD ** -0.5))
    o_ref[0] = _dot(p.astype(BF16), v_ref[0].astype(BF16)).astype(o_ref.dtype)


def _xattn(hn, w_xq, mk, mv):
    nb, t, d = hn.shape
    return pl.pallas_call(
        _xattn_kernel,
        grid=(X_HEADS, nb),
        in_specs=[pl.BlockSpec((1, t, d), lambda h, b: (b, 0, 0)),
                  pl.BlockSpec((d, X_HD), lambda h, b: (0, h)),
                  pl.BlockSpec((1, MEM_LEN, X_HD), lambda h, b: (b, 0, h)),
                  pl.BlockSpec((1, MEM_LEN, X_HD), lambda h, b: (b, 0, h))],
        out_specs=pl.BlockSpec((1, t, X_HD), lambda h, b: (b, 0, h)),
        out_shape=jax.ShapeDtypeStruct((nb, t, d), BF16),
        scratch_shapes=[pltpu.VMEM((d, X_HD), BF16)],
        compiler_params=_params("arbitrary", "arbitrary"),
        name="xattn",
    )(hn, w_xq, mk, mv)


def _softmax_rows(s):
    s = s - jnp.max(s, axis=-1, keepdims=True)
    e = jnp.exp(s)
    return e / jnp.sum(e, axis=-1, keepdims=True)


CACHE_SLOTS = 3


def _xattn_cache_kernel(q_ref, k_hbm, v_hbm, o_ref, kbuf, vbuf, sem, *, nsteps):
    bb, tq = q_ref.shape[0], q_ref.shape[1]
    nchunk = X_HD // LANES
    pitch = nchunk * X_HEADS
    ahead = CACHE_SLOTS - 1
    i = pl.program_id(0)

    def copies(step, slot):
        return [pltpu.make_async_copy(src.at[pl.ds(step * bb, bb)], dst.at[pl.ds(slot * bb, bb)], sem.at[slot, t])
                for t, (src, dst) in enumerate(((k_hbm, kbuf), (v_hbm, vbuf)))]

    @pl.when(i == 0)
    def _():
        for s in range(min(ahead, nsteps)):
            for cp in copies(s, s):
                cp.start()

    @pl.when(i + ahead < nsteps)
    def _():
        for cp in copies(i + ahead, (i + ahead) % CACHE_SLOTS):
            cp.start()

    slot = i % CACHE_SLOTS
    for cp in copies(i, slot):
        cp.wait()

    def gather(ref, b, h):
        parts = [ref[slot * bb + b, pl.ds(c * X_HEADS + h, MEM_LEN, stride=pitch), :] for c in range(nchunk)]
        return jnp.concatenate(parts, axis=1).astype(BF16)

    pairs = [(b, h) for b in range(bb) for h in range(X_HEADS)]
    scores = [_dot_nt(q_ref[b, :, h * X_HD:(h + 1) * X_HD].astype(BF16), gather(kbuf, b, h)) for b, h in pairs]
    p = _softmax_rows(jnp.concatenate(scores, axis=0) * (X_HD ** -0.5)).astype(BF16)
    for n, (b, h) in enumerate(pairs):
        o_ref[b, :, h * X_HD:(h + 1) * X_HD] = _dot(p[n * tq:(n + 1) * tq], gather(vbuf, b, h)).astype(o_ref.dtype)


def _xattn_cache(q, ck, cv, *, bb):
    nb, tq, d = q.shape
    nchunk = X_HD // LANES

    def stored_order(c):
        c = c.reshape(nb, MEM_LEN, X_HEADS, nchunk, LANES).transpose(0, 1, 3, 2, 4)
        return c.reshape(nb, MEM_LEN * nchunk * X_HEADS, LANES)

    rows = MEM_LEN * nchunk * X_HEADS
    nsteps = nb // bb
    ring = pltpu.VMEM((CACHE_SLOTS * bb, rows, LANES), F32)
    return pl.pallas_call(
        functools.partial(_xattn_cache_kernel, nsteps=nsteps),
        grid=(nsteps,),
        in_specs=[pl.BlockSpec((bb, tq, d), lambda b: (b, 0, 0)),
                  pl.BlockSpec(memory_space=pl.ANY),
                  pl.BlockSpec(memory_space=pl.ANY)],
        out_specs=pl.BlockSpec((bb, tq, d), lambda b: (b, 0, 0)),
        out_shape=jax.ShapeDtypeStruct((nb, tq, d), BF16),
        scratch_shapes=[ring, ring, pltpu.SemaphoreType.DMA((CACHE_SLOTS, 2))],
        compiler_params=_params("arbitrary"),
        name="xattn_cache",
    )(q, stored_order(ck), stored_order(cv))


def _shift_rows(x, prev, steps, shift):
    n = steps * shift
    head = prev[2 * shift - n:, :]
    if n == x.shape[0]:
        return head
    if n % SUBLANES == 0:
        return jnp.concatenate([head, x[:-n, :]], axis=0)
    rolled = pltpu.roll(x, n, 0)
    row = lax.broadcasted_iota(jnp.int32, (SUBLANES, x.shape[1]), 0)
    first = rolled[:SUBLANES, :]
    for i in range(n):
        first = jnp.where(row == i, head[i:i + 1, :], first)
    return jnp.concatenate([first, rolled[SUBLANES:, :]], axis=0)


def _ffn_up_kernel(hn_ref, wg_ref, wu_ref, cw_ref, cb_ref, prev_ref, o_ref, new_ref, wgb_ref, wub_ref, *,
                   shift):
    rows = hn_ref.shape[0]

    @pl.when(pl.program_id(1) == 0)
    def _():
        _cast_into(wg_ref, wgb_ref)
        _cast_into(wu_ref, wub_ref)

    rb = max(FFN_ROW_BLOCK, 2 * shift)
    prev = prev_ref[0]
    for r0 in range(0, rows, rb):
        hn = hn_ref[r0:r0 + rb, :]
        gate = _dot(hn, wgb_ref[...])
        gc = (cw_ref[0:1, :] * _shift_rows(gate, prev, 2, shift)
              + cw_ref[1:2, :] * _shift_rows(gate, prev, 1, shift) + cw_ref[2:3, :] * gate + cb_ref[...])
        o_ref[r0:r0 + rb, :] = ((gc * jax.nn.sigmoid(gc)) * _dot(hn, wub_ref[...])).astype(o_ref.dtype)
        prev = gate[rb - 2 * shift:, :]
    new_ref[0] = prev


def _ffn_up(hn, wg, wu, cw, cb, prev, nseq, rows, shift, tn):
    d = hn.shape[1]
    kern = functools.partial(_ffn_up_kernel, shift=shift)
    return pl.pallas_call(
        kern,
        grid=(D_FF // tn, nseq),
        in_specs=[pl.BlockSpec((rows, d), lambda j, b: (b, 0)),
                  pl.BlockSpec((d, tn), lambda j, b: (0, j)),
                  pl.BlockSpec((d, tn), lambda j, b: (0, j)),
                  pl.BlockSpec((CONV_WIDTH, tn), lambda j, b: (0, j)),
                  pl.BlockSpec((1, tn), lambda j, b: (0, j)),
                  pl.BlockSpec((1, 2 * shift, tn), lambda j, b: (b, 0, j))],
        out_specs=[pl.BlockSpec((rows, tn), lambda j, b: (b, j)),
                   pl.BlockSpec((1, 2 * shift, tn), lambda j, b: (b, 0, j))],
        out_shape=[jax.ShapeDtypeStruct((nseq * rows, D_FF), BF16),
                   jax.ShapeDtypeStruct((nseq, 2 * shift, D_FF), F32)],
        scratch_shapes=[pltpu.VMEM((d, tn), BF16), pltpu.VMEM((d, tn), BF16)],
        compiler_params=_params("arbitrary", "arbitrary"),
        name="ffn_up",
    )(hn, wg, wu, cw, cb, prev)


def _layer(x, w, wb, *, nseq, rows, shift, prev_conv, s0, prev_ffn, mk, mv, time_major):
    m = x.shape[0]
    tm = min(m, 1024)
    xn, logf = _norm_gate(x, w["norm_mix"], w["w_g1_t"], w["w_g2"], w["b_gate"], tm)
    conv_out, conv_new = _conv_proj(xn, w["w_in_t"], prev_conv, w["conv_w"], nseq=nseq, shift=shift, tm=tm, tn=512)
    n_gla = N_MAIN - 3 * D_CONV
    p = _col_matmul(xn, w["w_in_t"], 3 * D_CONV, n_gla, F32, tm, 1024)

    if time_major:
        nt = m // shift
        pad = ((0, 0), (0, SUBLANES - nt), (0, 0))
        pg = jnp.pad(p.reshape(nt, shift, n_gla).transpose(1, 0, 2), pad)
        lg = jnp.pad(logf.reshape(nt, shift, -1).transpose(1, 0, 2), pad)
        gla_args = dict(bb=8, rows=SUBLANES, chunk=SUBLANES, group=8)
    else:
        pg, lg = p.reshape(nseq, rows, n_gla), logf.reshape(nseq, rows, -1)
        gla_args = dict(bb=4, rows=128, chunk=2 * GLA_CHUNK, group=4)
    cast = () if wb is not None else (w["w_out"], w["w_xo"], w["w_fd"])
    o, s_new, *made = _gla(pg, lg, s0, w["gla_norm"], q_blk=0, k_blk=1, v_blk=1, r_blk=2, cast=cast, **gla_args)
    if wb is None:
        wb = dict(zip(("w_out", "w_xo", "w_fd"), made))
    if time_major:
        gla_out = o[:, :nt].transpose(1, 0, 2).reshape(m, D_GLA)
    else:
        gla_out = o.reshape(m, D_GLA)

    tm2 = min(m, 512)
    h, hn = _proj_res_norm([conv_out, gla_out], wb["w_out"], x, w["norm_x"], tm=tm2, tk=D_MODEL, final=False)
    if time_major:
        nt = m // shift
        qx = _matmul(hn, w["w_xq"], BF16, tm)
        qb = jnp.pad(qx.astype(F32).reshape(nt, shift, D_MODEL).transpose(1, 0, 2),
                     ((0, 0), (0, SUBLANES - nt), (0, 0)))
        ob = _xattn_cache(qb, mk, mv, bb=2)
        attn = ob[:, :nt].transpose(1, 0, 2).reshape(m, D_MODEL)
    else:
        attn = _xattn(hn.reshape(nseq, rows, D_MODEL), w["w_xq"], mk, mv).reshape(m, D_MODEL)

    h2, hn2 = _proj_res_norm([attn], wb["w_xo"], h, w["norm_ffn"], tm=tm2, tk=D_MODEL, final=False)
    act, ffn_new = _ffn_up(hn2, w["w_fg"], w["w_fu"], w["ffn_conv_w"], w["ffn_conv_b"], prev_ffn,
                           nseq, rows, shift, 512)
    return act, h2, conv_new, s_new, ffn_new, wb


def kernel(x_prompt, x_sample, mem_prompt, cache_conv, state_gla, cache_ffn, cache_mem_k, cache_mem_v,
           norm_mix, w_in, conv_w, w_gate2, b_gate, gla_norm, w_out, norm_x, norm_mem, w_xq, w_xk, w_xv,
           w_xo, norm_ffn, w_ffn_gate, w_ffn_up, ffn_conv_w, ffn_conv_b, w_ffn_down, norm_final):
    depth = w_in.shape[0]
    nb, seq, d = x_prompt.shape
    db, dseq, _ = x_sample.shape
    hp = x_prompt.reshape(nb * seq, d)
    hs = x_sample.transpose(1, 0, 2).reshape(dseq * db, d)
    outs = {k: [] for k in ("conv_p", "gla_p", "ffn_p", "mk", "mv", "conv_s", "gla_s", "ffn_s")}
    nfinal = norm_final.reshape(1, d)
    yp = ys = None
    for l in range(depth):
        w = {
            "norm_mix": norm_mix[l].reshape(1, d),
            "w_in_t": w_in[l].T,
            "w_g1_t": jnp.pad(w_in[l].T[N_MAIN:], ((0, LANES - GLA_RANK), (0, 0))).astype(BF16),
            "w_g2": jnp.pad(w_gate2[l], ((0, LANES - GLA_RANK), (0, 0))).astype(BF16),
            "b_gate": b_gate[l].reshape(1, -1),
            "conv_w": conv_w[l],
            "gla_norm": gla_norm[l].reshape(1, -1),
            "w_out": w_out[l],
            "norm_x": norm_x[l].reshape(1, d),
            "w_xq": w_xq[l],
            "w_xo": w_xo[l],
            "w_fd": w_ffn_down[l],
            "norm_ffn": norm_ffn[l].reshape(1, d),
            "w_fg": w_ffn_gate[l],
            "w_fu": w_ffn_up[l],
            "ffn_conv_w": ffn_conv_w[l],
            "ffn_conv_b": ffn_conv_b[l].reshape(1, -1),
        }
        last = l == depth - 1
        gain_next = nfinal if last else None

        mem = mem_prompt.reshape(nb * MEM_LEN, d)
        nmem = norm_mem[l].reshape(1, d)
        mk, mk_cache = _mem_proj(mem, nmem, w_xk[l], 1024)
        mv, mv_cache = _mem_proj(mem, nmem, w_xv[l], 1024)
        act, h2, c1, s1, f1, wb = _layer(
            hp, w, None, nseq=nb, rows=seq, shift=1,
            prev_conv=jnp.zeros((nb, CONV_WIDTH - 1, D_CONV), F32),
            s0=jnp.zeros((nb, GLA_HEADS, GLA_DK, GLA_DV), F32),
            prev_ffn=jnp.zeros((nb, CONV_WIDTH - 1, D_FF), F32),
            mk=mk.reshape(nb, MEM_LEN, d), mv=mv.reshape(nb, MEM_LEN, d), time_major=False)
        assert last, "only the final layer's epilogue (final rmsnorm) is implemented"
        yp = _proj_res_norm([act], wb["w_fd"], h2, gain_next, tm=512, tk=D_FF, final=True)
        outs["conv_p"].append(c1)
        outs["gla_p"].append(s1)
        outs["ffn_p"].append(f1)
        outs["mk"].append(mk_cache)
        outs["mv"].append(mv_cache)

        def tmajor(c):
            return c.transpose(1, 0, 2).reshape(1, (CONV_WIDTH - 1) * db, c.shape[-1])

        act, h2, c2, s2, f2, _ = _layer(
            hs, w, wb, nseq=1, rows=dseq * db, shift=db,
            prev_conv=tmajor(cache_conv[l]), s0=state_gla[l], prev_ffn=tmajor(cache_ffn[l]),
            mk=cache_mem_k[l], mv=cache_mem_v[l], time_major=True)
        ys = _proj_res_norm([act], wb["w_fd"], h2, gain_next, tm=512, tk=D_FF, final=True)
        outs["conv_s"].append(c2.reshape(CONV_WIDTH - 1, db, D_CONV).transpose(1, 0, 2))
        outs["gla_s"].append(s2)
        outs["ffn_s"].append(f2.reshape(CONV_WIDTH - 1, db, D_FF).transpose(1, 0, 2))

    y_prompt = yp.reshape(nb, seq, d)
    y_sample = ys.reshape(dseq, db, d).transpose(1, 0, 2)
    st = lambda k: jnp.stack(outs[k])
    return (y_prompt, y_sample, st("conv_p"), st("gla_p"), st("ffn_p"), st("mk"), st("mv"),
            st("conv_s"), st("gla_s"), st("ffn_s"))
```

```python
import functools

import jax
import jax.numpy as jnp
from jax import lax
from jax.experimental import pallas as pl
from jax.experimental.pallas import tpu as pltpu

F32 = jnp.float32
BF16 = jnp.bfloat16

D_MODEL = 2048
EPS = 1e-6
CONV_WIDTH = 3
D_CONV = 1024
D_GLA = 1024
GLA_HEADS = 4
GLA_DV = 256
GLA_DK = 128
GLA_RANK = 16
GLA_TAU = 16.0
LOG2_E = 1.4426950408889634
GLA_CHUNK = 64
X_HEADS = 4
X_HD = 512
MEM_LEN = 256
D_FF = 5632
N_MAIN = 3 * D_CONV + 2 * GLA_HEADS * GLA_DK + 2 * GLA_HEADS * GLA_DV

LANES = 128
SUBLANES = 8
VMEM_LIMIT_BYTES = 56 * 1024 * 1024


def _params(*sem):
    return pltpu.CompilerParams(dimension_semantics=sem, vmem_limit_bytes=VMEM_LIMIT_BYTES)


def _dot(a, b):
    return jnp.dot(a, b, preferred_element_type=F32)


def _dot_nt(a, b):
    return lax.dot_general(a, b, (((1,), (1,)), ((), ())), preferred_element_type=F32)


def _dot_tn(a, b):
    return lax.dot_general(a, b, (((0,), (0,)), ((), ())), preferred_element_type=F32)


def _rms_rows(x, g):
    ms = jnp.mean(x * x, axis=-1, keepdims=True)
    return (x * lax.rsqrt(ms + EPS)) * g


def _row_chunk(rows, limit=256):
    for c in (256, 128, 64, 32, 16, 8):
        if c <= limit and rows % c == 0:
            return c
    return rows


NORM_ROWS = 128
FFN_ROW_BLOCK = 512
PROJ_ROW_BLOCK = 256


def _norm_into(x_ref, g_ref, xn_ref):
    rows = x_ref.shape[0]
    ch = _row_chunk(rows)
    g = g_ref[...]

    def body(c, carry):
        r = pl.ds(pl.multiple_of(c * ch, ch), ch)
        xn_ref[r, :] = _rms_rows(x_ref[r, :], g).astype(xn_ref.dtype)
        return carry

    lax.fori_loop(0, rows // ch, body, 0)


def _norm_matmul_kernel(x_ref, g_ref, w_ref, o_ref, oc_ref, xn_ref):
    j = pl.program_id(1)

    @pl.when(j == 0)
    def _():
        _norm_into(x_ref, g_ref, xn_ref)

    o_ref[...] = _dot(xn_ref[...], w_ref[...].astype(BF16)).astype(o_ref.dtype)
    nchunk = X_HD // LANES
    pitch = nchunk * X_HEADS
    heads_per_tile = o_ref.shape[1] // X_HD
    for b in range(oc_ref.shape[0]):
        for hh in range(heads_per_tile):
            for c in range(nchunk):
                col = hh * X_HD + c * LANES
                row = c * X_HEADS + j * heads_per_tile + hh
                oc_ref[b, pl.ds(row, MEM_LEN, stride=pitch), :] = (
                    o_ref[b * MEM_LEN:(b + 1) * MEM_LEN, col:col + LANES])


def _norm_gate_kernel(x_ref, g_ref, wg1_ref, wg2_ref, bg_ref, xn_ref, lf_ref):
    _norm_into(x_ref, g_ref, xn_ref)
    g1 = _dot_nt(xn_ref[...], wg1_ref[...])
    z = _dot(g1.astype(BF16), wg2_ref[...]) + bg_ref[...]
    lf_ref[...] = (jnp.minimum(z, 0.0) - jnp.log1p(jnp.exp(-jnp.abs(z)))) * (LOG2_E / GLA_TAU)


def _mem_proj(mem, gain, w, tn):
    m, d = mem.shape
    n = w.shape[1]
    nb = m // MEM_LEN
    nchunk = X_HD // LANES
    rows = MEM_LEN * nchunk * X_HEADS
    flat, stored = pl.pallas_call(
        _norm_matmul_kernel,
        grid=(1, n // tn),
        in_specs=[pl.BlockSpec((m, d), lambda i, j: (0, 0)),
                  pl.BlockSpec((1, d), lambda i, j: (0, 0)),
                  pl.BlockSpec((d, tn), lambda i, j: (0, j))],
        out_specs=[pl.BlockSpec((m, tn), lambda i, j: (0, j)),
                   pl.BlockSpec((nb, rows, LANES), lambda i, j: (0, 0, 0))],
        out_shape=[jax.ShapeDtypeStruct((m, n), F32),
                   jax.ShapeDtypeStruct((nb, rows, LANES), F32)],
        scratch_shapes=[pltpu.VMEM((m, d), BF16)],
        compiler_params=_params("arbitrary", "arbitrary"),
        name="norm_matmul",
    )(mem, gain, w)
    cache = stored.reshape(nb, MEM_LEN, nchunk, X_HEADS, LANES).transpose(0, 1, 3, 2, 4)
    return flat, cache.reshape(nb, MEM_LEN, X_HEADS, X_HD)


def _norm_gate(x, gain, wg1, wg2, b_gate, tm):
    m, d = x.shape
    ng = wg2.shape[1]
    return pl.pallas_call(
        _norm_gate_kernel,
        grid=(m // tm,),
        in_specs=[pl.BlockSpec((tm, d), lambda i: (i, 0)),
                  pl.BlockSpec((1, d), lambda i: (0, 0)),
                  pl.BlockSpec((LANES, d), lambda i: (0, 0)),
                  pl.BlockSpec((LANES, ng), lambda i: (0, 0)),
                  pl.BlockSpec((1, ng), lambda i: (0, 0))],
        out_specs=[pl.BlockSpec((tm, d), lambda i: (i, 0)),
                   pl.BlockSpec((tm, ng), lambda i: (i, 0))],
        out_shape=[jax.ShapeDtypeStruct((m, d), BF16),
                   jax.ShapeDtypeStruct((m, ng), F32)],
        compiler_params=_params("arbitrary"),
        name="norm_gate",
    )(x, gain, wg1, wg2, b_gate)


def _cast_into(src_ref, dst_ref):
    rows = src_ref.shape[0]
    ch = _row_chunk(rows)

    def body(c, carry):
        r = pl.ds(pl.multiple_of(c * ch, ch), ch)
        dst_ref[r, :] = src_ref[r, :].astype(dst_ref.dtype)
        return carry

    lax.fori_loop(0, rows // ch, body, 0)


def _matmul_kernel(a_ref, w_ref, o_ref, wb_ref):
    @pl.when(pl.program_id(0) == 0)
    def _():
        _cast_into(w_ref, wb_ref)

    o_ref[...] = _dot(a_ref[...], wb_ref[...]).astype(o_ref.dtype)


def _matmul(a, w, out_dtype, tm):
    m, k = a.shape
    n = w.shape[1]
    return pl.pallas_call(
        _matmul_kernel,
        grid=(m // tm,),
        in_specs=[pl.BlockSpec((tm, k), lambda i: (i, 0)),
                  pl.BlockSpec((k, n), lambda i: (0, 0), pipeline_mode=pl.Buffered(1))],
        out_specs=pl.BlockSpec((tm, n), lambda i: (i, 0)),
        out_shape=jax.ShapeDtypeStruct((m, n), out_dtype),
        scratch_shapes=[pltpu.VMEM((k, n), BF16)],
        compiler_params=_params("arbitrary"),
        name="matmul",
    )(a, w)


def _col_matmul_kernel(a_ref, wt_ref, o_ref, wb_ref):
    @pl.when(pl.program_id(1) == 0)
    def _():
        _cast_into(wt_ref, wb_ref)

    o_ref[...] = _dot_nt(a_ref[...], wb_ref[...]).astype(o_ref.dtype)


def _col_matmul(a, wt, col0, ncols, out_dtype, tm, tn):
    m, k = a.shape
    j0 = col0 // tn
    return pl.pallas_call(
        _col_matmul_kernel,
        grid=(ncols // tn, m // tm),
        in_specs=[pl.BlockSpec((tm, k), lambda j, i: (i, 0)),
                  pl.BlockSpec((tn, k), lambda j, i: (j0 + j, 0))],
        out_specs=pl.BlockSpec((tm, tn), lambda j, i: (i, j)),
        out_shape=jax.ShapeDtypeStruct((m, ncols), out_dtype),
        scratch_shapes=[pltpu.VMEM((tn, k), BF16)],
        compiler_params=_params("arbitrary", "arbitrary"),
        name="col_matmul",
    )(a, wt)


def _halo_rows(shift):
    return max(2 * shift, SUBLANES)


def _conv_buf(rows, tn, shift):
    return pltpu.VMEM((tn // LANES, _halo_rows(shift) + rows, LANES), F32)


def _slab(s):
    return slice(s * LANES, (s + 1) * LANES)


def _conv_stage(buf_ref, row0, x):
    for s in range(buf_ref.shape[0]):
        buf_ref[s, row0:row0 + x.shape[0], :] = x[:, _slab(s)]


def _conv_taps(buf_ref, s, r0, rows, shift):
    halo = _halo_rows(shift)

    def back(steps):
        start = r0 + (halo - steps * shift)
        if (steps * shift) % SUBLANES == 0:
            return pl.ds(start if isinstance(start, int) else pl.multiple_of(start, SUBLANES), rows)
        return pl.ds(start, rows, stride=1)

    return buf_ref[s, back(2), :], buf_ref[s, back(1), :], buf_ref[s, back(0), :]


def _conv_last(buf_ref, new_ref, rows, shift):
    halo = _halo_rows(shift)
    for s in range(buf_ref.shape[0]):
        new_ref[0, :, _slab(s)] = buf_ref[s, halo + rows - 2 * shift:halo + rows, :]


def _conv_proj_kernel(xn_ref, wbg_ref, wcg_ref, wvc_ref, prev_ref, cw_ref, o_ref, new_ref,
                      bgb_ref, cgb_ref, vcb_ref, buf_ref, *, shift, tiles_per_seq):
    rows = xn_ref.shape[0]
    halo = _halo_rows(shift)
    i = pl.program_id(1)

    @pl.when(i == 0)
    def _():
        _cast_into(wbg_ref, bgb_ref)
        _cast_into(wcg_ref, cgb_ref)
        _cast_into(wvc_ref, vcb_ref)

    def from_cache():
        _conv_stage(buf_ref, halo - 2 * shift, prev_ref[0])

    if tiles_per_seq == 1:
        from_cache()
    else:
        pl.when(i % tiles_per_seq == 0)(from_cache)

        @pl.when(i % tiles_per_seq != 0)
        def _():
            for s in range(buf_ref.shape[0]):
                buf_ref[s, halo - 2 * shift:halo, :] = buf_ref[s, halo + rows - 2 * shift:halo + rows, :]

    xn = xn_ref[...]
    _conv_stage(buf_ref, halo, _dot_nt(xn, cgb_ref[...]) * _dot_nt(xn, vcb_ref[...]))
    bg = _dot_nt(xn, bgb_ref[...])
    for s in range(buf_ref.shape[0]):
        u2, u1, u0 = _conv_taps(buf_ref, s, 0, rows, shift)
        y = cw_ref[0:1, _slab(s)] * u2 + cw_ref[1:2, _slab(s)] * u1 + cw_ref[2:3, _slab(s)] * u0
        o_ref[:, _slab(s)] = (bg[:, _slab(s)] * y).astype(o_ref.dtype)
    _conv_last(buf_ref, new_ref, rows, shift)


def _conv_proj(xn, w_in_t, prev, conv_w, *, nseq, shift, tm, tn):
    m, d = xn.shape
    nj = D_CONV // tn
    tiles_per_seq = m // (nseq * tm)
    kern = functools.partial(_conv_proj_kernel, shift=shift, tiles_per_seq=tiles_per_seq)
    state = pl.BlockSpec((1, 2 * shift, tn), lambda j, i: (i // tiles_per_seq, 0, j))
    wb = pltpu.VMEM((tn, d), BF16)
    return pl.pallas_call(
        kern,
        grid=(nj, m // tm),
        in_specs=[pl.BlockSpec((tm, d), lambda j, i: (i, 0)),
                  pl.BlockSpec((tn, d), lambda j, i: (j, 0)),
                  pl.BlockSpec((tn, d), lambda j, i: (j + nj, 0)),
                  pl.BlockSpec((tn, d), lambda j, i: (j + 2 * nj, 0)),
                  state,
                  pl.BlockSpec((CONV_WIDTH, tn), lambda j, i: (0, j))],
        out_specs=[pl.BlockSpec((tm, tn), lambda j, i: (i, j)),
                   state],
        out_shape=[jax.ShapeDtypeStruct((m, D_CONV), BF16),
                   jax.ShapeDtypeStruct((nseq, 2 * shift, D_CONV), F32)],
        scratch_shapes=[wb, wb, wb, _conv_buf(tm, tn, shift)],
        compiler_params=_params("arbitrary", "arbitrary"),
        name="conv_proj",
    )(xn, w_in_t, w_in_t, w_in_t, prev, conv_w)


def _cumsum_rows(g):
    c = g.shape[0]
    row = lax.broadcasted_iota(jnp.int32, g.shape, 0)
    x = g
    s = 1
    while s < c:
        x = x + jnp.where(row >= s, pltpu.roll(x, s, 0), 0.0)
        s *= 2
    return x


def _bcast_block_row(x, s, k):
    c, lanes = x.shape
    if s == c:
        return jnp.broadcast_to(x[k:k + 1, :], x.shape)
    if s >= SUBLANES:
        y = x.reshape(c // s, s, lanes)
        return jnp.broadcast_to(y[:, k:k + 1, :], y.shape).reshape(c, lanes)
    y = x.reshape(c // SUBLANES, SUBLANES, lanes)
    sub = lax.broadcasted_iota(jnp.int32, y.shape, 1)
    out = None
    for blk in range(SUBLANES // s):
        src = jnp.broadcast_to(y[:, blk * s + k:blk * s + k + 1, :], y.shape)
        out = src if out is None else jnp.where(sub >= blk * s, src, out)
    return out.reshape(c, lanes)


def _gla_pair_masks(c):
    ri = lax.broadcasted_iota(jnp.int32, (c, c), 0)
    ci = lax.broadcasted_iota(jnp.int32, (c, c), 1)
    diff_bits = ri ^ ci
    masks = [diff_bits == 0]
    level = 0
    while (1 << level) < c:
        masks.append(((diff_bits >> level) == 1) & (((ri >> level) & 1) == 1))
        level += 1
    return masks


def _gla_chunk(q, k, v, g, s_prev, masks):
    c = q.shape[0]
    cum = _cumsum_rows(g)
    a = jnp.where(masks[0], _dot_nt(q.astype(BF16), k.astype(BF16)), 0.0)
    for level in range(len(masks) - 1):
        half = 1 << level
        ref = _bcast_block_row(cum, 2 * half, half - 1)
        d = cum - ref
        up = jnp.minimum(d, 0.0)
        qe = q * jnp.exp2(up)
        ke = k * jnp.exp2(up - d)
        a = a + jnp.where(masks[1 + level], _dot_nt(qe.astype(BF16), ke.astype(BF16)), 0.0)
    o = _dot(a.astype(BF16), v.astype(BF16)) + _dot((q * jnp.exp2(cum)).astype(BF16), s_prev.astype(BF16))
    last = cum[c - 1:c, :]
    kd = k * jnp.exp2(last - cum)
    dk = last.shape[1]
    decay_t = jnp.transpose(jnp.broadcast_to(jnp.exp2(last), (dk, dk)))
    decayed = jnp.concatenate([decay_t * s_prev[:, i:i + dk] for i in range(0, s_prev.shape[1], dk)], axis=1)
    s_new = decayed + _dot_tn(kd.astype(BF16), v.astype(BF16))
    return o, s_new


def _gla_kernel(q_ref, k_ref, v_ref, r_ref, g_ref, s0_ref, gn_ref, *refs, chunk, single_chunk, group, n_cast):
    cast_src, (o_ref, sn_ref), cast_dst = refs[:n_cast], refs[n_cast:n_cast + 2], refs[n_cast + 2:]
    bb, rows = q_ref.shape[0], q_ref.shape[1]
    nchunk = rows // chunk
    state_in = s0_ref if single_chunk else sn_ref

    if not single_chunk:
        @pl.when(pl.program_id(1) == 0)
        def _():
            sn_ref[...] = s0_ref[...]

    masks = _gla_pair_masks(chunk)

    def one(b, r):
        for h in range(GLA_HEADS):
            kc = slice(h * GLA_DK, (h + 1) * GLA_DK)
            vc = slice(h * GLA_DV, (h + 1) * GLA_DV)
            q = q_ref[b, r, kc] * (GLA_DK ** -0.5)
            o, s_new = _gla_chunk(q, k_ref[b, r, kc], v_ref[b, r, vc], g_ref[b, r, kc], state_in[b, h], masks)
            sn_ref[b, h] = s_new
            rr = r_ref[b, r, vc]
            o_ref[b, r, vc] = (_rms_rows(o, gn_ref[:, vc]) * (rr * jax.nn.sigmoid(rr))).astype(o_ref.dtype)

    def body(n, carry):
        r = pl.ds(pl.multiple_of((n % nchunk) * chunk, chunk), chunk)
        for u in range(group):
            one((n // nchunk) * group + u, r)
        return carry

    lax.fori_loop(0, (bb // group) * nchunk, body, 0)
    for src, dst in zip(cast_src, cast_dst):
        _cast_into(src, dst)


def _gla(p, logf, s0, gla_norm, *, q_blk, k_blk, v_blk, r_blk, bb, rows, chunk, group, cast=()):
    nb, t, _ = p.shape
    nk, nv = GLA_HEADS * GLA_DK, GLA_HEADS * GLA_DV
    nt = t // rows
    nsteps = (nb // bb) * nt
    kern = functools.partial(_gla_kernel, chunk=chunk, single_chunk=(t == chunk), group=group, n_cast=len(cast))
    state_spec = pl.BlockSpec((bb, GLA_HEADS, GLA_DK, GLA_DV), lambda b, c: (b, 0, 0, 0))
    cast_specs = [pl.BlockSpec((w.shape[0] // nsteps, w.shape[1]), lambda b, c: (b * nt + c, 0)) for w in cast]
    return pl.pallas_call(
        kern,
        grid=(nb // bb, nt),
        in_specs=[pl.BlockSpec((bb, rows, nk), lambda b, c: (b, c, q_blk)),
                  pl.BlockSpec((bb, rows, nk), lambda b, c: (b, c, k_blk)),
                  pl.BlockSpec((bb, rows, nv), lambda b, c: (b, c, v_blk)),
                  pl.BlockSpec((bb, rows, nv), lambda b, c: (b, c, r_blk)),
                  pl.BlockSpec((bb, rows, nk), lambda b, c: (b, c, 0)),
                  state_spec,
                  pl.BlockSpec((1, nv), lambda b, c: (0, 0))] + cast_specs,
        out_specs=[pl.BlockSpec((bb, rows, nv), lambda b, c: (b, c, 0)),
                   state_spec] + cast_specs,
        out_shape=[jax.ShapeDtypeStruct((nb, t, D_GLA), BF16),
                   jax.ShapeDtypeStruct((nb, GLA_HEADS, GLA_DK, GLA_DV), F32)]
                  + [jax.ShapeDtypeStruct(w.shape, BF16) for w in cast],
        compiler_params=_params("arbitrary", "arbitrary"),
        name="gla",
    )(p, p, p, p, logf, s0, gla_norm, *cast)


def _proj_res_norm_kernel(*refs, n_a, nk, final):
    a_refs, (w_ref, res_ref, g_ref) = refs[:n_a], refs[n_a:n_a + 3]
    out_refs, acc_ref = (refs[n_a + 3:], None) if nk == 1 else (refs[n_a + 3:-1], refs[-1])
    k = pl.program_id(1)

    def product(rows):
        part, r0 = None, 0
        for a_ref in a_refs:
            kw = a_ref.shape[1]
            term = _dot(a_ref[rows, :], w_ref[r0:r0 + kw, :])
            part = term if part is None else part + term
            r0 += kw
        return part

    tm = res_ref.shape[0]
    if nk == 1:
        rb = min(tm, PROJ_ROW_BLOCK)
        ch = _row_chunk(rb, NORM_ROWS)
        for b0 in range(0, tm, rb):
            part = product(slice(b0, b0 + rb))
            for c0 in range(0, rb, ch):
                r = slice(b0 + c0, b0 + c0 + ch)
                h = res_ref[r, :] + part[c0:c0 + ch, :]
                hn = _rms_rows(h, g_ref[...])
                if final:
                    out_refs[0][r, :] = hn
                else:
                    out_refs[0][r, :] = h
                    out_refs[1][r, :] = hn.astype(out_refs[1].dtype)
        return

    part = product(slice(None))

    @pl.when(k == 0)
    def _():
        acc_ref[...] = part

    if nk > 1:
        @pl.when(k > 0)
        def _():
            acc_ref[...] += part

    @pl.when(k == nk - 1)
    def _():
        rows = acc_ref.shape[0]
        ch = _row_chunk(rows, NORM_ROWS)
        g = g_ref[...]

        def body(c, carry):
            r = pl.ds(pl.multiple_of(c * ch, ch), ch)
            h = res_ref[r, :] + acc_ref[r, :]
            hn = _rms_rows(h, g)
            if final:
                out_refs[0][r, :] = hn
            else:
                out_refs[0][r, :] = h
                out_refs[1][r, :] = hn.astype(out_refs[1].dtype)
            return carry

        lax.fori_loop(0, rows // ch, body, 0)


def _proj_res_norm(a_list, w, res, gain, *, tm, tk, final):
    m = a_list[0].shape[0]
    kdim, d = w.shape
    nk = kdim // tk
    assert len(a_list) == 1 or nk == 1
    kern = functools.partial(_proj_res_norm_kernel, n_a=len(a_list), nk=nk, final=final)
    a_specs = ([pl.BlockSpec((tm, tk), lambda i, k: (i, k))] if len(a_list) == 1 else
               [pl.BlockSpec((tm, a.shape[1]), lambda i, k: (i, 0)) for a in a_list])
    row_spec = pl.BlockSpec((tm, d), lambda i, k: (i, 0))
    if final:
        out_specs = row_spec
        out_shape = jax.ShapeDtypeStruct((m, d), F32)
    else:
        out_specs = [row_spec, row_spec]
        out_shape = [jax.ShapeDtypeStruct((m, d), F32), jax.ShapeDtypeStruct((m, d), BF16)]
    return pl.pallas_call(
        kern,
        grid=(m // tm, nk),
        in_specs=a_specs + [pl.BlockSpec((tk, d), lambda i, k: (k, 0),
                                         pipeline_mode=pl.Buffered(1 if nk == 1 else 2)),
                            row_spec,
                            pl.BlockSpec((1, d), lambda i, k: (0, 0))],
        out_specs=out_specs,
        out_shape=out_shape,
        scratch_shapes=[] if nk == 1 else [pltpu.VMEM((tm, d), F32)],
        compiler_params=_params("arbitrary", "arbitrary"),
        name="proj_res_norm",
    )(*a_list, w, res, gain)


def _xattn_kernel(hn_ref, wq_ref, k_ref, v_ref, o_ref, wb_ref):
    @pl.when(pl.program_id(1) == 0)
    def _():
        _cast_into(wq_ref, wb_ref)

    q = _dot(hn_ref[0], wb_ref[...]).astype(BF16)
    p = _softmax_rows(_dot_nt(q, k_ref[0].astype(BF16)) * (X_HD ** -0.5))
    o_ref[0] = _dot(p.astype(BF16), v_ref[0].astype(BF16)).astype(o_ref.dtype)


def _xattn(hn, w_xq, mk, mv):
    nb, t, d = hn.shape
    return pl.pallas_call(
        _xattn_kernel,
        grid=(X_HEADS, nb),
        in_specs=[pl.BlockSpec((1, t, d), lambda h, b: (b, 0, 0)),
                  pl.BlockSpec((d, X_HD), lambda h, b: (0, h)),
                  pl.BlockSpec((1, MEM_LEN, X_HD), lambda h, b: (b, 0, h)),
                  pl.BlockSpec((1, MEM_LEN, X_HD), lambda h, b: (b, 0, h))],
        out_specs=pl.BlockSpec((1, t, X_HD), lambda h, b: (b, 0, h)),
        out_shape=jax.ShapeDtypeStruct((nb, t, d), BF16),
        scratch_shapes=[pltpu.VMEM((d, X_HD), BF16)],
        compiler_params=_params("arbitrary", "arbitrary"),
        name="xattn",
    )(hn, w_xq, mk, mv)


def _softmax_rows(s):
    s = s - jnp.max(s, axis=-1, keepdims=True)
    e = jnp.exp(s)
    return e / jnp.sum(e, axis=-1, keepdims=True)


CACHE_SLOTS = 3


def _xattn_cache_kernel(q_ref, k_hbm, v_hbm, o_ref, kbuf, vbuf, sem, *, nsteps):
    bb, tq = q_ref.shape[0], q_ref.shape[1]
    nchunk = X_HD // LANES
    pitch = nchunk * X_HEADS
    ahead = CACHE_SLOTS - 1
    i = pl.program_id(0)

    def copies(step, slot):
        return [pltpu.make_async_copy(src.at[pl.ds(step * bb, bb)], dst.at[pl.ds(slot * bb, bb)], sem.at[slot, t])
                for t, (src, dst) in enumerate(((k_hbm, kbuf), (v_hbm, vbuf)))]

    @pl.when(i == 0)
    def _():
        for s in range(min(ahead, nsteps)):
            for cp in copies(s, s):
                cp.start()

    @pl.when(i + ahead < nsteps)
    def _():
        for cp in copies(i + ahead, (i + ahead) % CACHE_SLOTS):
            cp.start()

    slot = i % CACHE_SLOTS
    for cp in copies(i, slot):
        cp.wait()

    def gather(ref, b, h):
        parts = [ref[slot * bb + b, pl.ds(c * X_HEADS + h, MEM_LEN, stride=pitch), :] for c in range(nchunk)]
        return jnp.concatenate(parts, axis=1).astype(BF16)

    pairs = [(b, h) for b in range(bb) for h in range(X_HEADS)]
    scores = [_dot_nt(q_ref[b, :, h * X_HD:(h + 1) * X_HD].astype(BF16), gather(kbuf, b, h)) for b, h in pairs]
    p = _softmax_rows(jnp.concatenate(scores, axis=0) * (X_HD ** -0.5)).astype(BF16)
    for n, (b, h) in enumerate(pairs):
        o_ref[b, :, h * X_HD:(h + 1) * X_HD] = _dot(p[n * tq:(n + 1) * tq], gather(vbuf, b, h)).astype(o_ref.dtype)


def _xattn_cache(q, ck, cv, *, bb):
    nb, tq, d = q.shape
    nchunk = X_HD // LANES

    def stored_order(c):
        c = c.reshape(nb, MEM_LEN, X_HEADS, nchunk, LANES).transpose(0, 1, 3, 2, 4)
        return c.reshape(nb, MEM_LEN * nchunk * X_HEADS, LANES)

    rows = MEM_LEN * nchunk * X_HEADS
    nsteps = nb // bb
    ring = pltpu.VMEM((CACHE_SLOTS * bb, rows, LANES), F32)
    return pl.pallas_call(
        functools.partial(_xattn_cache_kernel, nsteps=nsteps),
        grid=(nsteps,),
        in_specs=[pl.BlockSpec((bb, tq, d), lambda b: (b, 0, 0)),
                  pl.BlockSpec(memory_space=pl.ANY),
                  pl.BlockSpec(memory_space=pl.ANY)],
        out_specs=pl.BlockSpec((bb, tq, d), lambda b: (b, 0, 0)),
        out_shape=jax.ShapeDtypeStruct((nb, tq, d), BF16),
        scratch_shapes=[ring, ring, pltpu.SemaphoreType.DMA((CACHE_SLOTS, 2))],
        compiler_params=_params("arbitrary"),
        name="xattn_cache",
    )(q, stored_order(ck), stored_order(cv))


def _shift_rows(x, prev, steps, shift):
    n = steps * shift
    head = prev[2 * shift - n:, :]
    if n == x.shape[0]:
        return head
    if n % SUBLANES == 0:
        return jnp.concatenate([head, x[:-n, :]], axis=0)
    rolled = pltpu.roll(x, n, 0)
    row = lax.broadcasted_iota(jnp.int32, (SUBLANES, x.shape[1]), 0)
    first = rolled[:SUBLANES, :]
    for i in range(n):
        first = jnp.where(row == i, head[i:i + 1, :], first)
    return jnp.concatenate([first, rolled[SUBLANES:, :]], axis=0)


def _ffn_up_kernel(hn_ref, wg_ref, wu_ref, cw_ref, cb_ref, prev_ref, o_ref, new_ref, wgb_ref, wub_ref, *,
                   shift):
    rows = hn_ref.shape[0]

    @pl.when(pl.program_id(1) == 0)
    def _():
        _cast_into(wg_ref, wgb_ref)
        _cast_into(wu_ref, wub_ref)

    rb = max(FFN_ROW_BLOCK, 2 * shift)
    prev = prev_ref[0]
    for r0 in range(0, rows, rb):
        hn = hn_ref[r0:r0 + rb, :]
        gate = _dot(hn, wgb_ref[...])
        gc = (cw_ref[0:1, :] * _shift_rows(gate, prev, 2, shift)
              + cw_ref[1:2, :] * _shift_rows(gate, prev, 1, shift) + cw_ref[2:3, :] * gate + cb_ref[...])
        o_ref[r0:r0 + rb, :] = ((gc * jax.nn.sigmoid(gc)) * _dot(hn, wub_ref[...])).astype(o_ref.dtype)
        prev = gate[rb - 2 * shift:, :]
    new_ref[0] = prev


def _ffn_up(hn, wg, wu, cw, cb, prev, nseq, rows, shift, tn):
    d = hn.shape[1]
    kern = functools.partial(_ffn_up_kernel, shift=shift)
    return pl.pallas_call(
        kern,
        grid=(D_FF // tn, nseq),
        in_specs=[pl.BlockSpec((rows, d), lambda j, b: (b, 0)),
                  pl.BlockSpec((d, tn), lambda j, b: (0, j)),
                  pl.BlockSpec((d, tn), lambda j, b: (0, j)),
                  pl.BlockSpec((CONV_WIDTH, tn), lambda j, b: (0, j)),
                  pl.BlockSpec((1, tn), lambda j, b: (0, j)),
                  pl.BlockSpec((1, 2 * shift, tn), lambda j, b: (b, 0, j))],
        out_specs=[pl.BlockSpec((rows, tn), lambda j, b: (b, j)),
                   pl.BlockSpec((1, 2 * shift, tn), lambda j, b: (b, 0, j))],
        out_shape=[jax.ShapeDtypeStruct((nseq * rows, D_FF), BF16),
                   jax.ShapeDtypeStruct((nseq, 2 * shift, D_FF), F32)],
        scratch_shapes=[pltpu.VMEM((d, tn), BF16), pltpu.VMEM((d, tn), BF16)],
        compiler_params=_params("arbitrary", "arbitrary"),
        name="ffn_up",
    )(hn, wg, wu, cw, cb, prev)


def _layer(x, w, wb, *, nseq, rows, shift, prev_conv, s0, prev_ffn, mk, mv, time_major):
    m = x.shape[0]
    tm = min(m, 1024)
    xn, logf = _norm_gate(x, w["norm_mix"], w["w_g1_t"], w["w_g2"], w["b_gate"], tm)
    conv_out, conv_new = _conv_proj(xn, w["w_in_t"], prev_conv, w["conv_w"], nseq=nseq, shift=shift, tm=tm, tn=512)
    n_gla = N_MAIN - 3 * D_CONV
    p = _col_matmul(xn, w["w_in_t"], 3 * D_CONV, n_gla, F32, tm, 1024)

    if time_major:
        nt = m // shift
        pad = ((0, 0), (0, SUBLANES - nt), (0, 0))
        pg = jnp.pad(p.reshape(nt, shift, n_gla).transpose(1, 0, 2), pad)
        lg = jnp.pad(logf.reshape(nt, shift, -1).transpose(1, 0, 2), pad)
        gla_args = dict(bb=8, rows=SUBLANES, chunk=SUBLANES, group=8)
    else:
        pg, lg = p.reshape(nseq, rows, n_gla), logf.reshape(nseq, rows, -1)
        gla_args = dict(bb=4, rows=128, chunk=2 * GLA_CHUNK, group=4)
    cast = () if wb is not None else (w["w_out"], w["w_xo"], w["w_fd"])
    o, s_new, *made = _gla(pg, lg, s0, w["gla_norm"], q_blk=0, k_blk=1, v_blk=1, r_blk=2, cast=cast, **gla_args)
    if wb is None:
        wb = dict(zip(("w_out", "w_xo", "w_fd"), made))
    if time_major:
        gla_out = o[:, :nt].transpose(1, 0, 2).reshape(m, D_GLA)
    else:
        gla_out = o.reshape(m, D_GLA)

    tm2 = min(m, 512)
    h, hn = _proj_res_norm([conv_out, gla_out], wb["w_out"], x, w["norm_x"], tm=tm2, tk=D_MODEL, final=False)
    if time_major:
        nt = m // shift
        qx = _matmul(hn, w["w_xq"], BF16, tm)
        qb = jnp.pad(qx.astype(F32).reshape(nt, shift, D_MODEL).transpose(1, 0, 2),
                     ((0, 0), (0, SUBLANES - nt), (0, 0)))
        ob = _xattn_cache(qb, mk, mv, bb=4)
        attn = ob[:, :nt].transpose(1, 0, 2).reshape(m, D_MODEL)
    else:
        attn = _xattn(hn.reshape(nseq, rows, D_MODEL), w["w_xq"], mk, mv).reshape(m, D_MODEL)

    h2, hn2 = _proj_res_norm([attn], wb["w_xo"], h, w["norm_ffn"], tm=tm2, tk=D_MODEL, final=False)
    act, ffn_new = _ffn_up(hn2, w["w_fg"], w["w_fu"], w["ffn_conv_w"], w["ffn_conv_b"], prev_ffn,
                           nseq, rows, shift, 512)
    return act, h2, conv_new, s_new, ffn_new, wb


def kernel(x_prompt, x_sample, mem_prompt, cache_conv, state_gla, cache_ffn, cache_mem_k, cache_mem_v,
           norm_mix, w_in, conv_w, w_gate2, b_gate, gla_norm, w_out, norm_x, norm_mem, w_xq, w_xk, w_xv,
           w_xo, norm_ffn, w_ffn_gate, w_ffn_up, ffn_conv_w, ffn_conv_b, w_ffn_down, norm_final):
    depth = w_in.shape[0]
    nb, seq, d = x_prompt.shape
    db, dseq, _ = x_sample.shape
    hp = x_prompt.reshape(nb * seq, d)
    hs = x_sample.transpose(1, 0, 2).reshape(dseq * db, d)
    outs = {k: [] for k in ("conv_p", "gla_p", "ffn_p", "mk", "mv", "conv_s", "gla_s", "ffn_s")}
    nfinal = norm_final.reshape(1, d)
    yp = ys = None
    for l in range(depth):
        w = {
            "norm_mix": norm_mix[l].reshape(1, d),
            "w_in_t": w_in[l].T,
            "w_g1_t": jnp.pad(w_in[l].T[N_MAIN:], ((0, LANES - GLA_RANK), (0, 0))).astype(BF16),
            "w_g2": jnp.pad(w_gate2[l], ((0, LANES - GLA_RANK), (0, 0))).astype(BF16),
            "b_gate": b_gate[l].reshape(1, -1),
            "conv_w": conv_w[l],
            "gla_norm": gla_norm[l].reshape(1, -1),
            "w_out": w_out[l],
            "norm_x": norm_x[l].reshape(1, d),
            "w_xq": w_xq[l],
            "w_xo": w_xo[l],
            "w_fd": w_ffn_down[l],
            "norm_ffn": norm_ffn[l].reshape(1, d),
            "w_fg": w_ffn_gate[l],
            "w_fu": w_ffn_up[l],
            "ffn_conv_w": ffn_conv_w[l],
            "ffn_conv_b": ffn_conv_b[l].reshape(1, -1),
        }
        last = l == depth - 1
        gain_next = nfinal if last else None

        mem = mem_prompt.reshape(nb * MEM_LEN, d)
        nmem = norm_mem[l].reshape(1, d)
        mk, mk_cache = _mem_proj(mem, nmem, w_xk[l], 1024)
        mv, mv_cache = _mem_proj(mem, nmem, w_xv[l], 1024)
        act, h2, c1, s1, f1, wb = _layer(
            hp, w, None, nseq=nb, rows=seq, shift=1,
            prev_conv=jnp.zeros((nb, CONV_WIDTH - 1, D_CONV), F32),
            s0=jnp.zeros((nb, GLA_HEADS, GLA_DK, GLA_DV), F32),
            prev_ffn=jnp.zeros((nb, CONV_WIDTH - 1, D_FF), F32),
            mk=mk.reshape(nb, MEM_LEN, d), mv=mv.reshape(nb, MEM_LEN, d), time_major=False)
        assert last, "only the final layer's epilogue (final rmsnorm) is implemented"
        yp = _proj_res_norm([act], wb["w_fd"], h2, gain_next, tm=512, tk=D_FF, final=True)
        outs["conv_p"].append(c1)
        outs["gla_p"].append(s1)
        outs["ffn_p"].append(f1)
        outs["mk"].append(mk_cache)
        outs["mv"].append(mv_cache)

        def tmajor(c):
            return c.transpose(1, 0, 2).reshape(1, (CONV_WIDTH - 1) * db, c.shape[-1])

        act, h2, c2, s2, f2, _ = _layer(
            hs, w, wb, nseq=1, rows=dseq * db, shift=db,
            prev_conv=tmajor(cache_conv[l]), s0=state_gla[l], prev_ffn=tmajor(cache_ffn[l]),
            mk=cache_mem_k[l], mv=cache_mem_v[l], time_major=True)
        ys = _proj_res_norm([act], wb["w_fd"], h2, gain_next, tm=512, tk=D_FF, final=True)
        outs["conv_s"].append(c2.reshape(CONV_WIDTH - 1, db, D_CONV).transpose(1, 0, 2))
        outs["gla_s"].append(s2)
        outs["ffn_s"].append(f2.reshape(CONV_WIDTH - 1, db, D_FF).transpose(1, 0, 2))

    y_prompt = yp.reshape(nb, seq, d)
    y_sample = ys.reshape(dseq, db, d).transpose(1, 0, 2)
    st = lambda k: jnp.stack(outs[k])
    return (y_prompt, y_sample, st("conv_p"), st("gla_p"), st("ffn_p"), st("mk"), st("mv"),
            st("conv_s"), st("gla_s"), st("ffn_s"))
```
